```python
import math
import jax, jax.numpy as jnp
from jax import lax
import numpy as np

D_MODEL = 2048
BATCH = 8
SEQ = 8192
DEPTH = 1

HEAD_DIM = 128
N_Q_HEADS = 8
N_KV_HEADS = 2
GQA_GROUP = N_Q_HEADS // N_KV_HEADS
ATTN_WIDTH = N_Q_HEADS * HEAD_DIM
KV_WIDTH = N_KV_HEADS * HEAD_DIM
WINDOW = 128
BLOCK = 128
SPAN = BLOCK + 2 * WINDOW
N_BUCKETS = 32
MAX_DISTANCE = 128
POOL_SIZES = (2, 4, 8, 16)
N_POOL_GROUPS = len(POOL_SIZES)
POOL_WIDTH = D_MODEL // 2
POOL_GROUP_WIDTH = POOL_WIDTH // N_POOL_GROUPS
SPLIT_WIDTHS = (ATTN_WIDTH, KV_WIDTH, KV_WIDTH, ATTN_WIDTH, POOL_WIDTH, POOL_WIDTH)
SPLIT_POINTS = tuple(int(s) for s in np.cumsum(SPLIT_WIDTHS)[:-1])
IN_WIDTH = int(sum(SPLIT_WIDTHS))
N_BRANCHES = 2
EPS = 1e-6
NEG_INF = -1e30

kernel_name = "hybrid_swa_pool_gated_encoder"


def rmsnorm(x, g):
    xf = x.astype(jnp.float32)
    xf = xf * lax.rsqrt(jnp.mean(xf * xf, axis=-1, keepdims=True) + EPS)
    return xf.astype(x.dtype) * g


def t5_bucket(rel):
    half = N_BUCKETS // 2
    max_exact = half // 2
    ret = jnp.where(rel > 0, half, 0)
    n = jnp.abs(rel)
    nf = jnp.maximum(n, 1).astype(jnp.float32)
    large = max_exact + (jnp.log(nf / max_exact) / math.log(MAX_DISTANCE / max_exact)
                         * (half - max_exact)).astype(jnp.int32)
    large = jnp.minimum(large, half - 1)
    return ret + jnp.where(n < max_exact, n, large)


def windowed_gqa(q, k, v, rel_table, sink):
    B, S, _ = q.shape
    nblk = S // BLOCK
    qb = q.reshape(B, nblk, BLOCK, N_KV_HEADS, GQA_GROUP, HEAD_DIM)
    pad = ((0, 0), (WINDOW, WINDOW), (0, 0), (0, 0))
    kp = jnp.pad(k.reshape(B, S, N_KV_HEADS, HEAD_DIM), pad)
    vp = jnp.pad(v.reshape(B, S, N_KV_HEADS, HEAD_DIM), pad)
    idx = jnp.arange(nblk)[:, None] * BLOCK + jnp.arange(SPAN)[None, :]
    kw = kp[:, idx]
    vw = vp[:, idx]
    logits = jnp.einsum('bnqkgd,bntkd->bnkgqt', qb, kw).astype(jnp.float32) * (HEAD_DIM ** -0.5)
    rel = jnp.arange(SPAN)[None, :] - WINDOW - jnp.arange(BLOCK)[:, None]
    bias = rel_table[t5_bucket(rel)]
    bias = jnp.transpose(bias, (2, 0, 1)).reshape(N_KV_HEADS, GQA_GROUP, BLOCK, SPAN)
    key_pos = idx - WINDOW
    key_ok = (key_pos >= 0) & (key_pos < S)
    valid = (jnp.abs(rel) <= WINDOW)[None] & key_ok[:, None, :]
    logits = jnp.where(valid[None, :, None, None], logits + bias.astype(jnp.float32), NEG_INF)
    sink_l = jnp.broadcast_to(sink.astype(jnp.float32).reshape(N_KV_HEADS, GQA_GROUP, 1, 1),
                              logits.shape[:-1] + (1,))
    probs = jax.nn.softmax(jnp.concatenate([logits, sink_l], axis=-1), axis=-1)[..., :-1]
    out = jnp.einsum('bnkgqt,bntkd->bnqkgd', probs.astype(vw.dtype), vw)
    return out.reshape(B, S, ATTN_WIDTH)


def multiscale_pool(u, w_grp, scale):
    B, S, _ = u.shape
    ug = u.reshape(B, S, N_POOL_GROUPS, POOL_GROUP_WIDTH).astype(jnp.float32)
    cs = jnp.pad(jnp.cumsum(ug, axis=1), ((0, 0), (1, 0), (0, 0), (0, 0)))
    pos = jnp.arange(S)
    outs = []
    for gi, w in enumerate(POOL_SIZES):
        lo = jnp.clip(pos - w // 2, 0, S)
        hi = jnp.clip(pos + w // 2, 0, S)
        cnt = (hi - lo).astype(jnp.float32)[None, :, None]
        outs.append((cs[:, hi, gi] - cs[:, lo, gi]) / cnt - ug[:, :, gi])
    pooled = jnp.stack(outs, axis=2).astype(u.dtype)
    mixed = jnp.einsum('bsgc,gcd->bsgd', pooled, w_grp).reshape(B, S, POOL_WIDTH)
    return mixed * scale


def hybrid_layer(x, c, rel_table, w_ada, b_ada, pre_g, post_g, w_in, sink,
                 w_pool_grp, pool_scale, w_br_attn, w_br_pool, w_merge, b_merge, w_out):
    mod = jax.nn.silu(c) @ w_ada + b_ada
    shift, scale, gate = jnp.split(mod, 3, axis=-1)
    h = rmsnorm(x, pre_g) * (1.0 + scale[:, None]) + shift[:, None]
    proj = h @ w_in
    q, k, v, gate_a, u, gate_p = jnp.split(proj, SPLIT_POINTS, axis=-1)
    y_a = windowed_gqa(q, k, v, rel_table, sink) * jax.nn.silu(gate_a)
    y_p = multiscale_pool(u, w_pool_grp, pool_scale) * jax.nn.silu(gate_p)
    g = jax.nn.sigmoid(h @ w_merge + b_merge)
    g_a, g_p = jnp.split(g, N_BRANCHES, axis=-1)
    merged = g_a * (y_a @ w_br_attn) + g_p * (y_p @ w_br_pool)
    o = merged @ w_out
    return x + gate[:, None] * rmsnorm(o, post_g)


def _fwd_setup_inputs(seed: int = 0) -> dict:
    key = jax.random.key(seed)
    ks = jax.random.split(key, 18)
    D = D_MODEL
    nrm = lambda k, shape, s: jax.random.normal(k, shape, jnp.float32) * s
    return {
        "x": nrm(ks[0], (BATCH, SEQ, D), 1.0),
        "c": nrm(ks[1], (BATCH, D), 1.0),
        "rel_bias_table": nrm(ks[2], (N_BUCKETS, N_Q_HEADS), 0.5),
        "w_ada": nrm(ks[3], (DEPTH, D, 3 * D), 0.1 * D ** -0.5),
        "b_ada": nrm(ks[4], (DEPTH, 3 * D), 0.02),
        "pre_norm_g": 1.0 + nrm(ks[5], (DEPTH, D), 0.05),
        "post_norm_g": 1.0 + nrm(ks[6], (DEPTH, D), 0.05),
        "w_in": nrm(ks[7], (DEPTH, D, IN_WIDTH), D ** -0.5),
        "attn_sink": nrm(ks[8], (DEPTH, N_Q_HEADS), 0.5),
        "w_pool_group": nrm(ks[9], (DEPTH, N_POOL_GROUPS, POOL_GROUP_WIDTH, POOL_GROUP_WIDTH),
                            POOL_GROUP_WIDTH ** -0.5),
        "pool_scale": 1.0 + nrm(ks[10], (DEPTH, POOL_WIDTH), 0.1),
        "w_branch_attn": nrm(ks[11], (DEPTH, ATTN_WIDTH, D), ATTN_WIDTH ** -0.5),
        "w_branch_pool": nrm(ks[12], (DEPTH, POOL_WIDTH, D), POOL_WIDTH ** -0.5),
        "w_merge": nrm(ks[13], (DEPTH, D, N_BRANCHES * D), D ** -0.5),
        "b_merge": nrm(ks[14], (DEPTH, N_BRANCHES * D), 0.02),
        "w_out": nrm(ks[15], (DEPTH, D, D), D ** -0.5),
    }


def _fwd_reference(x, c, rel_bias_table, w_ada, b_ada, pre_norm_g, post_norm_g, w_in, attn_sink,
              w_pool_group, pool_scale, w_branch_attn, w_branch_pool, w_merge, b_merge, w_out):
    for l in range(DEPTH):
        x = hybrid_layer(x, c, rel_bias_table, w_ada[l], b_ada[l], pre_norm_g[l], post_norm_g[l],
                         w_in[l], attn_sink[l], w_pool_group[l], pool_scale[l],
                         w_branch_attn[l], w_branch_pool[l], w_merge[l], b_merge[l], w_out[l])
    return x


import jax as _jax
import jax.numpy as _jnp

TWIN_FORMAT = 'train_step'
FWD_PARAMS = ['x', 'c', 'rel_bias_table', 'w_ada', 'b_ada', 'pre_norm_g', 'post_norm_g', 'w_in', 'attn_sink', 'w_pool_group', 'pool_scale', 'w_branch_attn', 'w_branch_pool', 'w_merge', 'b_merge', 'w_out']
TWIN_WEIGHTS = ['rel_bias_table', 'w_ada', 'b_ada', 'pre_norm_g', 'post_norm_g', 'w_in', 'attn_sink', 'w_pool_group', 'pool_scale', 'w_branch_attn', 'w_branch_pool', 'w_merge', 'b_merge', 'w_out']
TWIN_DIFF_INPUT = 'x'
TWIN_INPUTS = ['x', 'c', 'rel_bias_table', 'w_ada', 'b_ada', 'pre_norm_g', 'post_norm_g', 'w_in', 'attn_sink', 'w_pool_group', 'pool_scale', 'w_branch_attn', 'w_branch_pool', 'w_merge', 'b_merge', 'w_out', 'loss_target', 'm_rel_bias_table', 'm_w_ada', 'm_b_ada', 'm_pre_norm_g', 'm_post_norm_g', 'm_w_in', 'm_attn_sink', 'm_w_pool_group', 'm_pool_scale', 'm_w_branch_attn', 'm_w_branch_pool', 'm_w_merge', 'm_b_merge', 'm_w_out', 'v_rel_bias_table', 'v_w_ada', 'v_b_ada', 'v_pre_norm_g', 'v_post_norm_g', 'v_w_in', 'v_attn_sink', 'v_w_pool_group', 'v_pool_scale', 'v_w_branch_attn', 'v_w_branch_pool', 'v_w_merge', 'v_b_merge', 'v_w_out']
TWIN_OUTPUTS = ['loss', 'grad_x', 'grad_rel_bias_table', 'grad_w_ada', 'grad_b_ada', 'grad_pre_norm_g', 'grad_post_norm_g', 'grad_w_in', 'grad_attn_sink', 'grad_w_pool_group', 'grad_pool_scale', 'grad_w_branch_attn', 'grad_w_branch_pool', 'grad_w_merge', 'grad_b_merge', 'grad_w_out', 'delta_rel_bias_table', 'delta_w_ada', 'delta_b_ada', 'delta_pre_norm_g', 'delta_post_norm_g', 'delta_w_in', 'delta_attn_sink', 'delta_w_pool_group', 'delta_pool_scale', 'delta_w_branch_attn', 'delta_w_branch_pool', 'delta_w_merge', 'delta_b_merge', 'delta_w_out', 'new_m_rel_bias_table', 'new_m_w_ada', 'new_m_b_ada', 'new_m_pre_norm_g', 'new_m_post_norm_g', 'new_m_w_in', 'new_m_attn_sink', 'new_m_w_pool_group', 'new_m_pool_scale', 'new_m_w_branch_attn', 'new_m_w_branch_pool', 'new_m_w_merge', 'new_m_b_merge', 'new_m_w_out', 'new_v_rel_bias_table', 'new_v_w_ada', 'new_v_b_ada', 'new_v_pre_norm_g', 'new_v_post_norm_g', 'new_v_w_in', 'new_v_attn_sink', 'new_v_w_pool_group', 'new_v_pool_scale', 'new_v_w_branch_attn', 'new_v_w_branch_pool', 'new_v_w_merge', 'new_v_b_merge', 'new_v_w_out']
TWIN_LEAF_KINDS = {'loss': 'loss', 'grad_x': 'grad_x', 'grad_rel_bias_table': 'grad_w', 'grad_w_ada': 'grad_w', 'grad_b_ada': 'grad_w', 'grad_pre_norm_g': 'grad_w', 'grad_post_norm_g': 'grad_w', 'grad_w_in': 'grad_w', 'grad_attn_sink': 'grad_w', 'grad_w_pool_group': 'grad_w', 'grad_pool_scale': 'grad_w', 'grad_w_branch_attn': 'grad_w', 'grad_w_branch_pool': 'grad_w', 'grad_w_merge': 'grad_w', 'grad_b_merge': 'grad_w', 'grad_w_out': 'grad_w', 'delta_rel_bias_table': 'delta_w', 'delta_w_ada': 'delta_w', 'delta_b_ada': 'delta_w', 'delta_pre_norm_g': 'delta_w', 'delta_post_norm_g': 'delta_w', 'delta_w_in': 'delta_w', 'delta_attn_sink': 'delta_w', 'delta_w_pool_group': 'delta_w', 'delta_pool_scale': 'delta_w', 'delta_w_branch_attn': 'delta_w', 'delta_w_branch_pool': 'delta_w', 'delta_w_merge': 'delta_w', 'delta_b_merge': 'delta_w', 'delta_w_out': 'delta_w', 'new_m_rel_bias_table': 'new_m', 'new_m_w_ada': 'new_m', 'new_m_b_ada': 'new_m', 'new_m_pre_norm_g': 'new_m', 'new_m_post_norm_g': 'new_m', 'new_m_w_in': 'new_m', 'new_m_attn_sink': 'new_m', 'new_m_w_pool_group': 'new_m', 'new_m_pool_scale': 'new_m', 'new_m_w_branch_attn': 'new_m', 'new_m_w_branch_pool': 'new_m', 'new_m_w_merge': 'new_m', 'new_m_b_merge': 'new_m', 'new_m_w_out': 'new_m', 'new_v_rel_bias_table': 'new_v', 'new_v_w_ada': 'new_v', 'new_v_b_ada': 'new_v', 'new_v_pre_norm_g': 'new_v', 'new_v_post_norm_g': 'new_v', 'new_v_w_in': 'new_v', 'new_v_attn_sink': 'new_v', 'new_v_w_pool_group': 'new_v', 'new_v_pool_scale': 'new_v', 'new_v_w_branch_attn': 'new_v', 'new_v_w_branch_pool': 'new_v', 'new_v_w_merge': 'new_v', 'new_v_b_merge': 'new_v', 'new_v_w_out': 'new_v'}


def _forward(args):
    return _fwd_reference(*[args[k] for k in FWD_PARAMS])


def _output_shape():
    def fwd():
        inp = _fwd_setup_inputs(0)
        return _fwd_reference(*[inp[k] for k in FWD_PARAMS])
    out = _jax.eval_shape(fwd)
    return out.shape, out.dtype

N_MICROBATCH = 1
ADAM_LR = 0.001
ADAM_B1 = 0.9
ADAM_B2 = 0.999
ADAM_EPS = 1e-08
ADAM_WD = 0.01
ADAM_STEP = 10
PER_EXAMPLE_BATCH_AXIS = {'x': 0, 'c': 0, 'loss_target': 0}
SHARED_INPUTS = []
_WEIGHT_DTYPES = {'rel_bias_table': _jnp.float32, 'w_ada': _jnp.float32, 'b_ada': _jnp.float32, 'pre_norm_g': _jnp.float32, 'post_norm_g': _jnp.float32, 'w_in': _jnp.float32, 'attn_sink': _jnp.float32, 'w_pool_group': _jnp.float32, 'pool_scale': _jnp.float32, 'w_branch_attn': _jnp.float32, 'w_branch_pool': _jnp.float32, 'w_merge': _jnp.float32, 'b_merge': _jnp.float32, 'w_out': _jnp.float32}
MOMENT_SCALE = {'rel_bias_table': 4.602580e-03, 'w_ada': 2.076695e-01, 'b_ada': 5.561763e-01, 'pre_norm_g': 1.802790e-02, 'post_norm_g': 1.435015e-01, 'w_in': 1.121125e-02, 'attn_sink': 5.281330e-05, 'w_pool_group': 1.686482e-02, 'pool_scale': 1.559081e-02, 'w_branch_attn': 1.922537e-03, 'w_branch_pool': 1.190181e-02, 'w_merge': 3.138160e-03, 'b_merge': 3.171110e-03, 'w_out': 1.198383e-02}


def _to_microbatches(a, axis):
    t = _jnp.moveaxis(a, axis, 0)
    t = t.reshape((N_MICROBATCH, t.shape[0] // N_MICROBATCH) + t.shape[1:])
    return _jnp.moveaxis(t, 1, axis + 1)


def setup_inputs(seed: int = 0) -> dict:
    inp = _fwd_setup_inputs(seed)
    key = _jax.random.fold_in(_jax.random.key(seed), 7919)
    shape, _ = _output_shape()
    out = dict(inp)
    out["loss_target"] = _jax.random.normal(_jax.random.fold_in(key, 0), shape, _jnp.float32)
    for i, name in enumerate(TWIN_WEIGHTS):
        w = inp[name].astype(_jnp.float32)
        if MOMENT_SCALE is None:
            s = _jnp.sqrt(_jnp.mean(_jnp.square(w)) + 1e-30)
        else:
            s = MOMENT_SCALE[name]
        km, kv = _jax.random.split(_jax.random.fold_in(key, i + 1))
        out[name] = w
        out["m_" + name] = s * _jax.random.normal(km, w.shape, _jnp.float32)
        out["v_" + name] = (s * s) * _jax.random.uniform(kv, w.shape, _jnp.float32, 0.5, 1.5)
    if N_MICROBATCH > 1:
        for name, axis in PER_EXAMPLE_BATCH_AXIS.items():
            out[name] = _to_microbatches(out[name], axis)
    return {'x': out['x'], 'c': out['c'], 'rel_bias_table': out['rel_bias_table'], 'w_ada': out['w_ada'], 'b_ada': out['b_ada'], 'pre_norm_g': out['pre_norm_g'], 'post_norm_g': out['post_norm_g'], 'w_in': out['w_in'], 'attn_sink': out['attn_sink'], 'w_pool_group': out['w_pool_group'], 'pool_scale': out['pool_scale'], 'w_branch_attn': out['w_branch_attn'], 'w_branch_pool': out['w_branch_pool'], 'w_merge': out['w_merge'], 'b_merge': out['b_merge'], 'w_out': out['w_out'], 'loss_target': out['loss_target'], 'm_rel_bias_table': out['m_rel_bias_table'], 'm_w_ada': out['m_w_ada'], 'm_b_ada': out['m_b_ada'], 'm_pre_norm_g': out['m_pre_norm_g'], 'm_post_norm_g': out['m_post_norm_g'], 'm_w_in': out['m_w_in'], 'm_attn_sink': out['m_attn_sink'], 'm_w_pool_group': out['m_w_pool_group'], 'm_pool_scale': out['m_pool_scale'], 'm_w_branch_attn': out['m_w_branch_attn'], 'm_w_branch_pool': out['m_w_branch_pool'], 'm_w_merge': out['m_w_merge'], 'm_b_merge': out['m_b_merge'], 'm_w_out': out['m_w_out'], 'v_rel_bias_table': out['v_rel_bias_table'], 'v_w_ada': out['v_w_ada'], 'v_b_ada': out['v_b_ada'], 'v_pre_norm_g': out['v_pre_norm_g'], 'v_post_norm_g': out['v_post_norm_g'], 'v_w_in': out['v_w_in'], 'v_attn_sink': out['v_attn_sink'], 'v_w_pool_group': out['v_w_pool_group'], 'v_pool_scale': out['v_pool_scale'], 'v_w_branch_attn': out['v_w_branch_attn'], 'v_w_branch_pool': out['v_w_branch_pool'], 'v_w_merge': out['v_w_merge'], 'v_b_merge': out['v_b_merge'], 'v_w_out': out['v_w_out']}


def _loss(weights, diff, rest, loss_target):
    with _jax.named_scope("forward"):
        args = {**rest, TWIN_DIFF_INPUT: diff, **{k: w.astype(_WEIGHT_DTYPES[k]) for k, w in weights.items()}}
        y = _forward(args)
    with _jax.named_scope("loss_head"):
        err = _jnp.square(y.astype(_jnp.float32) - loss_target)
        return 0.5 * _jnp.sum(_jnp.mean(err, axis=-1)) if err.ndim else 0.5 * err


def _adamw(w, g, m, v):
    m = ADAM_B1 * m + (1.0 - ADAM_B1) * g
    v = ADAM_B2 * v + (1.0 - ADAM_B2) * _jnp.square(g)
    m_hat = m / (1.0 - ADAM_B1 ** ADAM_STEP)
    v_hat = v / (1.0 - ADAM_B2 ** ADAM_STEP)
    delta = -ADAM_LR * (m_hat / (_jnp.sqrt(v_hat) + ADAM_EPS) + ADAM_WD * w)
    return delta, m, v


def reference(x, c, rel_bias_table, w_ada, b_ada, pre_norm_g, post_norm_g, w_in, attn_sink, w_pool_group, pool_scale, w_branch_attn, w_branch_pool, w_merge, b_merge, w_out, loss_target, m_rel_bias_table, m_w_ada, m_b_ada, m_pre_norm_g, m_post_norm_g, m_w_in, m_attn_sink, m_w_pool_group, m_pool_scale, m_w_branch_attn, m_w_branch_pool, m_w_merge, m_b_merge, m_w_out, v_rel_bias_table, v_w_ada, v_b_ada, v_pre_norm_g, v_post_norm_g, v_w_in, v_attn_sink, v_w_pool_group, v_pool_scale, v_w_branch_attn, v_w_branch_pool, v_w_merge, v_b_merge, v_w_out):
    given = dict(x=x, c=c, rel_bias_table=rel_bias_table, w_ada=w_ada, b_ada=b_ada, pre_norm_g=pre_norm_g, post_norm_g=post_norm_g, w_in=w_in, attn_sink=attn_sink, w_pool_group=w_pool_group, pool_scale=pool_scale, w_branch_attn=w_branch_attn, w_branch_pool=w_branch_pool, w_merge=w_merge, b_merge=b_merge, w_out=w_out, loss_target=loss_target, m_rel_bias_table=m_rel_bias_table, m_w_ada=m_w_ada, m_b_ada=m_b_ada, m_pre_norm_g=m_pre_norm_g, m_post_norm_g=m_post_norm_g, m_w_in=m_w_in, m_attn_sink=m_attn_sink, m_w_pool_group=m_w_pool_group, m_pool_scale=m_pool_scale, m_w_branch_attn=m_w_branch_attn, m_w_branch_pool=m_w_branch_pool, m_w_merge=m_w_merge, m_b_merge=m_b_merge, m_w_out=m_w_out, v_rel_bias_table=v_rel_bias_table, v_w_ada=v_w_ada, v_b_ada=v_b_ada, v_pre_norm_g=v_pre_norm_g, v_post_norm_g=v_post_norm_g, v_w_in=v_w_in, v_attn_sink=v_attn_sink, v_w_pool_group=v_w_pool_group, v_pool_scale=v_pool_scale, v_w_branch_attn=v_w_branch_attn, v_w_branch_pool=v_w_branch_pool, v_w_merge=v_w_merge, v_b_merge=v_b_merge, v_w_out=v_w_out)
    weights = {n: given[n] for n in TWIN_WEIGHTS}
    shared = {n: given[n] for n in SHARED_INPUTS}
    per_example = {n: given[n] for n in ['x', 'c']}
    grad_fn = _jax.value_and_grad(_loss, argnums=(0, 1))

    def one_microbatch(ex, loss_target):
        ex = dict(ex)
        diff = ex.pop(TWIN_DIFF_INPUT)
        return grad_fn(weights, diff, {**shared, **ex}, loss_target)

    if N_MICROBATCH == 1:
        loss, (grad_w, grad_x) = one_microbatch(per_example, given["loss_target"])
    else:
        def body(carry, xs):
            loss_sum, grad_sum = carry
            l_k, (gw_k, gx_k) = one_microbatch(xs[0], xs[1])
            with _jax.named_scope("update"):
                return (loss_sum + l_k, _jax.tree.map(_jnp.add, grad_sum, gw_k)), gx_k

        init = (_jnp.zeros((), _jnp.float32), _jax.tree.map(_jnp.zeros_like, weights))
        (loss, grad_w), grad_x = _jax.lax.scan(body, init, (per_example, given["loss_target"]))
    with _jax.named_scope("update"):
        delta_w, new_m, new_v = {}, {}, {}
        for n in TWIN_WEIGHTS:
            delta_w[n], new_m[n], new_v[n] = _adamw(weights[n], grad_w[n], given["m_" + n], given["v_" + n])
    return (loss, grad_x, *[grad_w[n] for n in TWIN_WEIGHTS], *[delta_w[n] for n in TWIN_WEIGHTS],
            *[new_m[n] for n in TWIN_WEIGHTS], *[new_v[n] for n in TWIN_WEIGHTS])
```

```python
import functools
import math

import numpy as np
import jax
import jax.numpy as jnp
from jax import lax
from jax.experimental import pallas as pl
from jax.experimental.pallas import tpu as pltpu

F32 = jnp.float32
BF16 = jnp.bfloat16
MESH = pl.DeviceIdType.MESH

HEAD_DIM = 128
N_Q_HEADS = 8
N_KV_HEADS = 2
GQA_GROUP = N_Q_HEADS // N_KV_HEADS
ATTN_WIDTH = N_Q_HEADS * HEAD_DIM
KV_WIDTH = N_KV_HEADS * HEAD_DIM
WINDOW = 128
BLOCK = 128
SPAN = BLOCK + 2 * WINDOW
N_BUCKETS = 32
MAX_DISTANCE = 128
POOL_SIZES = (2, 4, 8, 16)
N_POOL_GROUPS = len(POOL_SIZES)
HALO = 16
EPS = 1e-6
NEG_INF = -1e30
ADAM_LR = 0.001
ADAM_B1 = 0.9
ADAM_B2 = 0.999
ADAM_EPS = 1e-08
ADAM_WD = 0.01
ADAM_STEP = 10

N_DEV = 8
N_CHIPS = 4
LANE = 128
COL_TILE = 512
VMEM_CAP = 60000 * 1024
ROW_TILE = 256


def _cparams(sem, est_bytes):
    limit = int(min(max(est_bytes * 5 // 4 + (4 << 20), 16 << 20), VMEM_CAP))
    return pltpu.CompilerParams(dimension_semantics=sem, vmem_limit_bytes=limit)


def _sigmoid(x):
    return jax.nn.sigmoid(x)


def _silu(x):
    return x * _sigmoid(x)


def _dsilu(x):
    s = _sigmoid(x)
    return s * (1.0 + x * (1.0 - s))


def _place():
    x, y, c = lax.axis_index("x"), lax.axis_index("y"), lax.axis_index("c")
    return x, y, c


def _flip(v, bit):
    return (1 - v) if bit else v


def _xor_peer(k):
    x, y, c = _place()
    return (_flip(x, (k >> 2) & 1), _flip(y, (k >> 1) & 1), _flip(c, k & 1))


def _resident(shape):
    nd = len(shape)
    return pl.BlockSpec(shape, lambda *_: (0,) * nd, pipeline_mode=pl.Buffered(1))


def _const_spec(shape):
    nd = len(shape)
    return pl.BlockSpec(shape, lambda *_: (0,) * nd)


def _all_gather8(name, x, pre=None):
    r, n = x.shape

    def body(x_ref, out_ref, stage, send_sems, recv_sems):
        px, py, pc = _place()
        me = 4 * px + 2 * py + pc
        v = x_ref[...]
        if pre is not None:
            v = pre(v)
        stage[...] = v
        out_ref[me] = v
        copies = []
        for k in range(1, N_DEV):
            cp = pltpu.make_async_remote_copy(
                src_ref=stage, dst_ref=out_ref.at[me], send_sem=send_sems.at[k - 1], recv_sem=recv_sems.at[k - 1],
                device_id=_xor_peer(k), device_id_type=MESH)
            cp.start()
            copies.append(cp)
        for cp in copies:
            cp.wait()

    return pl.pallas_call(
        body, name=name,
        out_shape=jax.ShapeDtypeStruct((N_DEV, r, n), F32),
        in_specs=[pl.BlockSpec(memory_space=pltpu.VMEM)],
        out_specs=pl.BlockSpec(memory_space=pltpu.VMEM),
        scratch_shapes=[pltpu.VMEM((r, n), F32), pltpu.SemaphoreType.DMA((N_DEV - 1,)),
                        pltpu.SemaphoreType.DMA((N_DEV - 1,))],
    )(x)


class _Sharded:
    def __init__(self, kind, full_shape):
        self.kind = kind
        self.full_shape = tuple(full_shape)
        if kind == "col":
            r, c = full_shape
            self.shard_shape = (r, c // N_CHIPS)
        elif kind == "row":
            r, c = full_shape
            self.shard_shape = (r // N_CHIPS, c)
        else:
            g, r, c = full_shape
            self.shard_shape = (g, r // N_CHIPS, c)
        self.piece_shape = self.half_of_shard_shape()

    def half_of_shard_shape(self):
        s = self.shard_shape
        if self.kind == "col" or self.kind == "row":
            return (s[0] // 2, s[1])
        return (s[0], s[1] // 2, s[2])

    def shard_half(self, ref, hc):
        s = self.shard_shape
        if self.kind == "grp":
            h = s[1] // 2
            return ref.at[:, pl.ds(pl.multiple_of(hc * h, 16), h), :]
        h = s[0] // 2
        return ref.at[pl.ds(pl.multiple_of(hc * h, 16), h), :]

    def window(self, ref, chip, hc=None):
        s = self.shard_shape
        if self.kind == "col":
            cols = pl.ds(pl.multiple_of(chip * s[1], LANE), s[1])
            if hc is None:
                return ref.at[:, cols]
            h = s[0] // 2
            return ref.at[pl.ds(pl.multiple_of(hc * h, 16), h), cols]
        if self.kind == "row":
            if hc is None:
                return ref.at[pl.ds(pl.multiple_of(chip * s[0], 16), s[0]), :]
            h = s[0] // 2
            return ref.at[pl.ds(pl.multiple_of(chip * s[0] + hc * h, 16), h), :]
        if hc is None:
            return ref.at[:, pl.ds(pl.multiple_of(chip * s[1], 16), s[1]), :]
        h = s[1] // 2
        return ref.at[:, pl.ds(pl.multiple_of(chip * s[1] + hc * h, 16), h), :]


def _gather_weights(specs, shards):
    nw = len(specs)

    def body(*refs):
        shard_refs = refs[:nw]
        full_refs = refs[nw:2 * nw]
        send_sems, recv_sems, local_sems = refs[2 * nw:]
        x, y, c = _place()
        my_chip = 2 * x + y
        sibling = (x, y, 1 - c)
        chips = [(1 - x, y), (x, 1 - y), (1 - x, 1 - y)]

        def remote(sp, k, src, dst, to):
            return pltpu.make_async_remote_copy(src_ref=src, dst_ref=dst, send_sem=send_sems.at[k],
                                                recv_sem=recv_sems.at[k], device_id=to, device_id_type=MESH)

        local, first, passed = [], [], []
        for w, sp in enumerate(specs):
            cp = pltpu.make_async_copy(shard_refs[w], sp.window(full_refs[w], my_chip), local_sems.at[w])
            cp.start()
            local.append(cp)
            for t, (cx, cy) in enumerate(chips):
                cp = remote(sp, 6 * w + t, sp.shard_half(shard_refs[w], c), sp.window(full_refs[w], my_chip, c),
                            (cx, cy, c))
                cp.start()
                first.append(cp)
        for w, sp in enumerate(specs):
            for t, (cx, cy) in enumerate(chips):
                landed = sp.window(full_refs[w], 2 * cx + cy, c)
                remote(sp, 6 * w + t, landed, landed, (cx, cy, c)).wait_recv()
                cp = remote(sp, 6 * w + 3 + t, landed, landed, sibling)
                cp.start()
                passed.append(cp)
        for w, sp in enumerate(specs):
            for t, (cx, cy) in enumerate(chips):
                landed = sp.window(full_refs[w], 2 * cx + cy, 1 - c)
                remote(sp, 6 * w + 3 + t, landed, landed, sibling).wait_recv()
        for cp in first + passed:
            cp.wait_send()
        for cp in local:
            cp.wait()

    any_spec = pl.BlockSpec(memory_space=pl.ANY)
    return pl.pallas_call(
        body, name="gather_weights",
        out_shape=tuple(jax.ShapeDtypeStruct(sp.full_shape, BF16) for sp in specs),
        in_specs=[any_spec] * nw, out_specs=tuple([any_spec] * nw),
        scratch_shapes=[pltpu.SemaphoreType.DMA((6 * nw,)), pltpu.SemaphoreType.DMA((6 * nw,)),
                        pltpu.SemaphoreType.DMA((nw,))],
    )(*shards)


def _scatter_grads(specs, partials):
    nw = len(specs)

    def body(*refs):
        part_refs = refs[:nw]
        recv_refs = refs[nw:2 * nw]
        send_sems, recv_sems, local_sems = refs[2 * nw:]
        x, y, c = _place()
        my_chip = 2 * x + y
        copies = []
        for w, sp in enumerate(specs):
            cp = pltpu.make_async_copy(sp.window(part_refs[w], my_chip, c), recv_refs[w].at[0], local_sems.at[w])
            cp.start()
            copies.append(cp)
            for k in range(1, N_DEV):
                px, py, pc = _xor_peer(k)
                cp = pltpu.make_async_remote_copy(
                    src_ref=sp.window(part_refs[w], 2 * px + py, pc), dst_ref=recv_refs[w].at[k],
                    send_sem=send_sems.at[7 * w + k - 1], recv_sem=recv_sems.at[7 * w + k - 1],
                    device_id=(px, py, pc), device_id_type=MESH)
                cp.start()
                copies.append(cp)
        for cp in copies:
            cp.wait()

    any_spec = pl.BlockSpec(memory_space=pl.ANY)
    return pl.pallas_call(
        body, name="scatter_grads",
        out_shape=tuple(jax.ShapeDtypeStruct((N_DEV,) + sp.piece_shape, BF16) for sp in specs),
        in_specs=[any_spec] * nw, out_specs=tuple([any_spec] * nw),
        scratch_shapes=[pltpu.SemaphoreType.DMA((7 * nw,)), pltpu.SemaphoreType.DMA((7 * nw,)),
                        pltpu.SemaphoreType.DMA((nw,))],
    )(*partials)


def _exchange_halves(specs, halves):
    nw = len(specs)

    def body(*refs):
        half_refs = refs[:nw]
        full_refs = refs[nw:2 * nw]
        send_sems, recv_sems, local_sems = refs[2 * nw:]
        x, y, c = _place()
        sibling = (x, y, 1 - c)
        copies = []
        for w, sp in enumerate(specs):
            cp = pltpu.make_async_copy(half_refs[w], sp.shard_half(full_refs[w], c), local_sems.at[w])
            cp.start()
            copies.append(cp)
            cp = pltpu.make_async_remote_copy(
                src_ref=half_refs[w], dst_ref=sp.shard_half(full_refs[w], c), send_sem=send_sems.at[w],
                recv_sem=recv_sems.at[w], device_id=sibling, device_id_type=MESH)
            cp.start()
            copies.append(cp)
        for cp in copies:
            cp.wait()

    any_spec = pl.BlockSpec(memory_space=pl.ANY)
    return pl.pallas_call(
        body, name="exchange_halves",
        out_shape=tuple(jax.ShapeDtypeStruct(sp.shard_shape, F32) for sp in specs),
        in_specs=[any_spec] * nw, out_specs=tuple([any_spec] * nw),
        scratch_shapes=[pltpu.SemaphoreType.DMA((nw,)), pltpu.SemaphoreType.DMA((nw,)),
                        pltpu.SemaphoreType.DMA((nw,))],
    )(*halves)


def _row_tile(rows, cap):
    for t in range(min(rows, cap), 0, -1):
        if rows % t == 0 and (t % 16 == 0 or t == rows):
            return t
    return rows


def _cast_bf16(name, x):
    r, c = x.shape
    tr = _row_tile(r, 512)

    def body(x_ref, o_ref):
        o_ref[...] = x_ref[...].astype(BF16)

    return pl.pallas_call(
        body, name=name, grid=(r // tr,), out_shape=jax.ShapeDtypeStruct((r, c), BF16),
        in_specs=[pl.BlockSpec((tr, c), lambda i: (i, 0))], out_specs=pl.BlockSpec((tr, c), lambda i: (i, 0)),
        compiler_params=_cparams(("parallel",), 2 * tr * c * 6),
    )(x)


def _adam_math(w, g, m, v):
    m = ADAM_B1 * m + (1.0 - ADAM_B1) * g
    v = ADAM_B2 * v + (1.0 - ADAM_B2) * (g * g)
    m_hat = m / (1.0 - ADAM_B1 ** ADAM_STEP)
    v_hat = v / (1.0 - ADAM_B2 ** ADAM_STEP)
    delta = -ADAM_LR * (m_hat / (jnp.sqrt(v_hat) + ADAM_EPS) + ADAM_WD * w)
    return delta, m, v


def _adamw(name, w, g, m, v):
    r, c = w.shape
    tr = _row_tile(r, max(8, (1 << 18) // c))

    def body(w_ref, g_ref, m_ref, v_ref, d_ref, nm_ref, nv_ref):
        d, nm, nv = _adam_math(w_ref[...], g_ref[...], m_ref[...], v_ref[...])
        d_ref[...] = d
        nm_ref[...] = nm
        nv_ref[...] = nv

    spec = pl.BlockSpec((tr, c), lambda i: (i, 0))
    out = jax.ShapeDtypeStruct((r, c), F32)
    return pl.pallas_call(
        body, name=name, grid=(r // tr,), out_shape=(out, out, out), in_specs=[spec] * 4, out_specs=(spec,) * 3,
        compiler_params=_cparams(("parallel",), 2 * 7 * tr * c * 4),
    )(w, g, m, v)


def _reduce8(name, x):
    _, r, c = x.shape
    tr = _row_tile(r, max(16, (1 << 18) // c))

    def body(x_ref, o_ref):
        acc = x_ref[0].astype(F32)
        for k in range(1, N_DEV):
            acc = acc + x_ref[k].astype(F32)
        o_ref[...] = acc

    return pl.pallas_call(
        body, name=name, grid=(r // tr,), out_shape=jax.ShapeDtypeStruct((r, c), F32),
        in_specs=[pl.BlockSpec((N_DEV, tr, c), lambda i: (0, i, 0))], out_specs=pl.BlockSpec((tr, c), lambda i: (i, 0)),
        compiler_params=_cparams(("parallel",), 2 * (N_DEV * 2 + 4) * tr * c),
    )(x)


def _t5_buckets():
    rel = jnp.arange(SPAN)[None, :] - WINDOW - jnp.arange(BLOCK)[:, None]
    half = N_BUCKETS // 2
    max_exact = half // 2
    ret = jnp.where(rel > 0, half, 0)
    n = jnp.abs(rel)
    nf = jnp.maximum(n, 1).astype(F32)
    large = max_exact + (jnp.log(nf / max_exact) / math.log(MAX_DISTANCE / max_exact)
                         * (half - max_exact)).astype(jnp.int32)
    large = jnp.minimum(large, half - 1)
    return (ret + jnp.where(n < max_exact, n, large)).astype(jnp.int32)


def _bias_table(table, buckets):
    def body(t_ref, b_ref, o_ref):
        bk = b_ref[...]
        for h in range(N_Q_HEADS):
            acc = jnp.zeros((BLOCK, SPAN), F32)
            for b in range(N_BUCKETS):
                acc = jnp.where(bk == b, t_ref[b, h], acc)
            o_ref[h] = acc

    return pl.pallas_call(
        body, name="bias_table", out_shape=jax.ShapeDtypeStruct((N_Q_HEADS, BLOCK, SPAN), F32),
        in_specs=[pl.BlockSpec(memory_space=pltpu.SMEM), pl.BlockSpec(memory_space=pltpu.VMEM)],
        out_specs=pl.BlockSpec(memory_space=pltpu.VMEM),
    )(table, buckets)


def _bias_table_grad(dbias, buckets):
    def body(d_ref, b_ref, o_ref):
        bk = b_ref[...]
        row = lax.broadcasted_iota(jnp.int32, (N_BUCKETS, LANE), 0)
        lane = lax.broadcasted_iota(jnp.int32, (N_BUCKETS, LANE), 1)
        acc = jnp.zeros((N_BUCKETS, LANE), F32)
        for h in range(N_Q_HEADS):
            d = d_ref[h]
            for b in range(N_BUCKETS):
                s = jnp.sum(jnp.where(bk == b, d, 0.0))
                acc = jnp.where((row == b) & (lane == h), s, acc)
        o_ref[...] = acc

    return pl.pallas_call(
        body, name="bias_table_grad", out_shape=jax.ShapeDtypeStruct((N_BUCKETS, LANE), F32),
        in_specs=[pl.BlockSpec(memory_space=pltpu.VMEM), pl.BlockSpec(memory_space=pltpu.VMEM)],
        out_specs=pl.BlockSpec(memory_space=pltpu.VMEM),
    )(dbias, buckets)


def _ada_forward(sc_all, w_ada):
    d, n = w_ada.shape
    tn = _pick_tile(n, COL_TILE)

    def body(sc_ref, w_ref, o_ref):
        row = lax.broadcasted_iota(jnp.int32, (N_DEV, d), 0)
        sc = jnp.zeros((N_DEV, d), F32)
        for k in range(N_DEV):
            sc = jnp.where(row == k, sc_ref[k], sc)
        o_ref[...] = jnp.dot(sc, w_ref[...], preferred_element_type=F32, precision=lax.Precision.HIGHEST)

    return pl.pallas_call(
        body, name="ada_forward", grid=(n // tn,), out_shape=jax.ShapeDtypeStruct((N_DEV, n), F32),
        in_specs=[_const_spec((N_DEV, N_DEV, d)), pl.BlockSpec((d, tn), lambda j: (0, j))],
        out_specs=pl.BlockSpec((N_DEV, tn), lambda j: (0, j)),
        compiler_params=_cparams(("parallel",), 2 * d * tn * 4 + N_DEV * N_DEV * d * 8),
    )(sc_all, w_ada)


def _mod_finish(m_all, b_ada):
    _, _, n = m_all.shape

    def body(m_ref, b_ref, o_ref):
        x, y, c = _place()
        me = 4 * x + 2 * y + c
        row = lax.broadcasted_iota(jnp.int32, (N_DEV, n), 0)
        for j in range(N_CHIPS):
            blk = m_ref[2 * j]
            mine = jnp.sum(jnp.where(row == me, blk, 0.0), axis=0, keepdims=True)
            o_ref[:, j * n:(j + 1) * n] = mine + b_ref[:, j * n:(j + 1) * n]

    return pl.pallas_call(
        body, name="mod_finish", out_shape=jax.ShapeDtypeStruct((1, N_CHIPS * n), F32),
        in_specs=[pl.BlockSpec(memory_space=pltpu.VMEM), pl.BlockSpec(memory_space=pltpu.VMEM)],
        out_specs=pl.BlockSpec(memory_space=pltpu.VMEM),
    )(m_all, b_ada)


def _ada_backward(sc_t, dmod_cols, w, m, v):
    d, n = w.shape
    tr, tn = _row_tile(d, 512), _pick_tile(n, COL_TILE)

    def body(s_ref, dm_ref, w_ref, m_ref, v_ref, g_ref, d_ref, nm_ref, nv_ref):
        g = jnp.dot(s_ref[...], dm_ref[...], preferred_element_type=F32, precision=lax.Precision.HIGHEST)
        dl, nm, nv = _adam_math(w_ref[...], g, m_ref[...], v_ref[...])
        g_ref[...] = g
        d_ref[...] = dl
        nm_ref[...] = nm
        nv_ref[...] = nv

    tile = pl.BlockSpec((tr, tn), lambda i, j: (i, j))
    out = jax.ShapeDtypeStruct((d, n), F32)
    return pl.pallas_call(
        body, name="ada_backward", grid=(d // tr, n // tn), out_shape=(out,) * 4,
        in_specs=[pl.BlockSpec((tr, N_DEV), lambda i, j: (i, 0)), pl.BlockSpec((N_DEV, tn), lambda i, j: (0, j)),
                  tile, tile, tile],
        out_specs=(tile,) * 4,
        compiler_params=_cparams(("parallel", "parallel"), 2 * 8 * tr * tn * 4),
    )(sc_t, dmod_cols, w, m, v)


def _prenorm(x, mod, pre_g):
    s, d = x.shape
    tm = _row_tile(s, 512)

    def body(x_ref, mod_ref, g_ref, h_ref):
        xv = x_ref[...]
        r = lax.rsqrt(jnp.mean(xv * xv, axis=1, keepdims=True) + EPS)
        xn = xv * r * g_ref[...]
        h_ref[...] = (xn * (1.0 + mod_ref[:, d:2 * d]) + mod_ref[:, 0:d]).astype(BF16)

    return pl.pallas_call(
        body, name="prenorm", grid=(s // tm,), out_shape=jax.ShapeDtypeStruct((s, d), BF16),
        in_specs=[pl.BlockSpec((tm, d), lambda i: (i, 0)), _const_spec((1, 3 * d)), _const_spec((1, d))],
        out_specs=pl.BlockSpec((tm, d), lambda i: (i, 0)),
        compiler_params=_cparams(("parallel",), 2 * tm * d * 6 + 4 * tm * d * 4),
    )(x, mod, pre_g)


def _matmul(name, a, b, bias=None):
    s, k = a.shape
    _, n = b.shape
    tm, tn = _row_tile(s, 1024), COL_TILE

    def body(*refs):
        if bias is None:
            a_ref, b_ref, o_ref = refs
        else:
            a_ref, b_ref, bias_ref, o_ref = refs
        acc = jnp.dot(a_ref[...], b_ref[...], preferred_element_type=F32)
        if bias is not None:
            acc = _sigmoid(acc + bias_ref[...])
        o_ref[...] = acc.astype(BF16)

    in_specs = [pl.BlockSpec((tm, k), lambda i, j: (i, 0)), pl.BlockSpec((k, tn), lambda i, j: (0, j))]
    args = [a, b]
    if bias is not None:
        in_specs.append(pl.BlockSpec((1, tn), lambda i, j: (0, j)))
        args.append(bias)
    return pl.pallas_call(
        body, name=name, grid=(s // tm, n // tn), out_shape=jax.ShapeDtypeStruct((s, n), BF16),
        in_specs=in_specs, out_specs=pl.BlockSpec((tm, tn), lambda i, j: (i, j)),
        compiler_params=_cparams(("parallel", "arbitrary"), 2 * (tm * k + k * tn + tm * tn) * 2 + 2 * tm * tn * 4),
    )(*args)


def _col_specs(off, width, rows, row_index):
    assert off % COL_TILE == 0 and width % COL_TILE == 0
    return [pl.BlockSpec((rows, COL_TILE), functools.partial(lambda p, *ids: (row_index(*ids), p), off // COL_TILE + p))
            for p in range(width // COL_TILE)]


def _cat(refs):
    vals = [r[...] for r in refs]
    return vals[0] if len(vals) == 1 else jnp.concatenate(vals, axis=1)


def _attn_mask(n, s):
    rows = GQA_GROUP * BLOCK
    qi = lax.broadcasted_iota(jnp.int32, (rows, SPAN), 0) & (BLOCK - 1)
    t = lax.broadcasted_iota(jnp.int32, (rows, SPAN), 1)
    rel = t - WINDOW - qi
    kpos = (n - 1) * BLOCK + t
    return (jnp.abs(rel) <= WINDOW) & (kpos >= 0) & (kpos < s)


def _sink_column(sink_ref, kh):
    rows = GQA_GROUP * BLOCK
    grp = lax.broadcasted_iota(jnp.int32, (rows, 1), 0) // BLOCK
    col = jnp.zeros((rows, 1), F32)
    for g in range(GQA_GROUP):
        col = jnp.where(grp == g, sink_ref[0, kh * GQA_GROUP + g], col)
    return col


def _stack_heads(x, kh):
    base = kh * GQA_GROUP * HEAD_DIM
    return jnp.concatenate([x[:, base + g * HEAD_DIM: base + (g + 1) * HEAD_DIM] for g in range(GQA_GROUP)], axis=0)


def _softmax_parts(qs, k, bias, valid, sink_col):
    sc = lax.dot_general(qs, k, (((1,), (1,)), ((), ())), preferred_element_type=F32)
    sc = sc * (HEAD_DIM ** -0.5) + bias
    sc = jnp.where(valid, sc, NEG_INF)
    mx = jnp.maximum(jnp.max(sc, axis=1, keepdims=True), sink_col)
    e = jnp.exp(sc - mx)
    es = jnp.exp(sink_col - mx)
    inv = 1.0 / (jnp.sum(e, axis=1, keepdims=True) + es)
    return e * inv, es * inv


def _attn_forward(proj, bias2, sink, s):
    nblk = s // BLOCK
    nq = ATTN_WIDTH // COL_TILE
    kv_col = ATTN_WIDTH // COL_TILE
    assert 2 * KV_WIDTH == COL_TILE

    def body(*refs):
        q_refs = refs[:nq]
        kvp, kvc, kvn, bias_ref, sink_ref, o_ref = refs[nq:]
        n = pl.program_id(0)
        q = _cat(q_refs)
        kv = jnp.concatenate([kvp[...], kvc[...], kvn[...]], axis=0)
        valid = _attn_mask(n, s)
        for kh in range(N_KV_HEADS):
            qs = _stack_heads(q, kh)
            k = kv[:, kh * HEAD_DIM:(kh + 1) * HEAD_DIM]
            v = kv[:, KV_WIDTH + kh * HEAD_DIM: KV_WIDTH + (kh + 1) * HEAD_DIM]
            p, _ = _softmax_parts(qs, k, bias_ref[kh], valid, _sink_column(sink_ref, kh))
            o = jnp.dot(p.astype(BF16), v, preferred_element_type=F32)
            for g in range(GQA_GROUP):
                h = kh * GQA_GROUP + g
                o_ref[:, h * HEAD_DIM:(h + 1) * HEAD_DIM] = o[g * BLOCK:(g + 1) * BLOCK].astype(BF16)

    in_specs = _col_specs(0, ATTN_WIDTH, BLOCK, lambda n: n)
    in_specs += [pl.BlockSpec((BLOCK, COL_TILE), lambda n: (jnp.maximum(n - 1, 0), kv_col)),
                 pl.BlockSpec((BLOCK, COL_TILE), lambda n: (n, kv_col)),
                 pl.BlockSpec((BLOCK, COL_TILE), lambda n: (jnp.minimum(n + 1, nblk - 1), kv_col)),
                 _const_spec((N_KV_HEADS, GQA_GROUP * BLOCK, SPAN)),
                 pl.BlockSpec(memory_space=pltpu.SMEM)]
    return pl.pallas_call(
        body, name="attn_forward", grid=(nblk,), out_shape=jax.ShapeDtypeStruct((s, ATTN_WIDTH), BF16),
        in_specs=in_specs, out_specs=pl.BlockSpec((BLOCK, ATTN_WIDTH), lambda n: (n, 0)),
        compiler_params=_cparams(("parallel",), 16 << 20),
    )(*([proj] * (nq + 3)), bias2, sink)


def _pool_positions(i, tm, s, width):
    pos = i * tm - HALO + lax.broadcasted_iota(jnp.int32, (tm + 2 * HALO, width), 0)
    return pos, (pos >= 0) & (pos < s)


def _pool_count(pos, w, s):
    return (jnp.minimum(pos + w // 2, s) - jnp.maximum(pos - w // 2, 0)).astype(F32)


def _halo_specs_cols(off, width, tm, s):
    per = tm // HALO
    last = s // HALO - 1
    prev = _col_specs(off, width, HALO, lambda i: jnp.maximum(i * per - 1, 0))
    nxt = _col_specs(off, width, HALO, lambda i: jnp.minimum((i + 1) * per, last))
    return prev, nxt


def _branches(proj, attn, g, w_bra, w_brp, w_grp, pool_scale, s, d):
    a_w, p_w = ATTN_WIDTH, pool_scale.shape[1]
    cg = p_w // N_POOL_GROUPS
    tm = _row_tile(s, ROW_TILE)
    off_ga = ATTN_WIDTH + 2 * KV_WIDTH
    off_u = off_ga + a_w
    off_gp = off_u + p_w
    n_ga, n_u, n_gp = a_w // COL_TILE, p_w // COL_TILE, p_w // COL_TILE

    def body(*refs):
        it = iter(refs)
        attn_ref = next(it)
        ga_refs = [next(it) for _ in range(n_ga)]
        u_refs = [next(it) for _ in range(n_u)]
        up_refs = [next(it) for _ in range(n_u)]
        un_refs = [next(it) for _ in range(n_u)]
        gp_refs = [next(it) for _ in range(n_gp)]
        g_ref, wa_ref, wp_ref, wg_ref, ps_ref = (next(it) for _ in range(5))
        ya_ref, yp_ref, za_ref, zp_ref, mg_ref, pooled_ref, mixed_ref = (next(it) for _ in range(7))
        i = pl.program_id(0)
        ya = (attn_ref[...].astype(F32) * _silu(_cat(ga_refs).astype(F32))).astype(BF16)
        ya_ref[...] = ya
        za = jnp.dot(ya, wa_ref[...], preferred_element_type=F32)
        za_ref[...] = za.astype(BF16)

        u = _cat(u_refs).astype(F32)
        ext = jnp.concatenate([_cat(up_refs).astype(F32), u, _cat(un_refs).astype(F32)], axis=0)
        pos, ok = _pool_positions(i, tm, s, cg)
        mixed = []
        for gi, w in enumerate(POOL_SIZES):
            e = jnp.where(ok, ext[:, gi * cg:(gi + 1) * cg], 0.0)
            acc = e[HALO - w // 2: HALO - w // 2 + tm]
            for dd in range(-w // 2 + 1, w // 2):
                acc = acc + e[HALO + dd: HALO + dd + tm]
            cnt = _pool_count(pos[HALO:HALO + tm], w, s)
            pooled = (acc / cnt - u[:, gi * cg:(gi + 1) * cg]).astype(BF16)
            pooled_ref[:, gi * cg:(gi + 1) * cg] = pooled
            mixed.append(jnp.dot(pooled, wg_ref[gi], preferred_element_type=F32))
        mixed = jnp.concatenate(mixed, axis=1)
        mixed_ref[...] = mixed.astype(BF16)
        yp = (mixed * ps_ref[...] * _silu(_cat(gp_refs).astype(F32))).astype(BF16)
        yp_ref[...] = yp
        zp = jnp.dot(yp, wp_ref[...], preferred_element_type=F32)
        zp_ref[...] = zp.astype(BF16)
        gate = g_ref[...].astype(F32)
        mg_ref[...] = (gate[:, :d] * za + gate[:, d:] * zp).astype(BF16)

    row = lambda i: i
    u_prev, u_next = _halo_specs_cols(off_u, p_w, tm, s)
    in_specs = [pl.BlockSpec((tm, a_w), lambda i: (i, 0))]
    in_specs += _col_specs(off_ga, a_w, tm, row) + _col_specs(off_u, p_w, tm, row) + u_prev + u_next
    in_specs += _col_specs(off_gp, p_w, tm, row)
    in_specs += [pl.BlockSpec((tm, 2 * d), lambda i: (i, 0)), _resident((a_w, d)), _resident((p_w, d)),
                 _resident((N_POOL_GROUPS, cg, cg)), _const_spec((1, p_w))]
    n_proj = n_ga + 3 * n_u + n_gp
    tile = lambda w: pl.BlockSpec((tm, w), lambda i: (i, 0))
    out_widths = (a_w, p_w, d, d, d, p_w, p_w)
    est = 2 * tm * (a_w + a_w + 2 * p_w + 2 * d + sum(out_widths)) * 2 + (a_w + p_w) * d * 2 + 6 * tm * d * 4
    return pl.pallas_call(
        body, name="branches", grid=(s // tm,),
        out_shape=tuple(jax.ShapeDtypeStruct((s, w), BF16) for w in out_widths),
        in_specs=in_specs, out_specs=tuple(tile(w) for w in out_widths),
        compiler_params=_cparams(("parallel",), est),
    )(attn, *([proj] * n_proj), g, w_bra, w_brp, w_grp, pool_scale)


def _out_loss(merged, x, target, w_out, post_g, mod, s, d):
    tm = _row_tile(s, ROW_TILE)
    nsteps = s // tm

    def body(mg_ref, x_ref, t_ref, w_ref, pg_ref, mod_ref, dout_ref, do_ref, loss_ref, dgate_ref, dpg_ref, lacc):
        i = pl.program_id(0)

        @pl.when(i == 0)
        def _():
            lacc[...] = jnp.zeros_like(lacc)
            dgate_ref[...] = jnp.zeros_like(dgate_ref)
            dpg_ref[...] = jnp.zeros_like(dpg_ref)

        o = jnp.dot(mg_ref[...], w_ref[...], preferred_element_type=F32)
        r = lax.rsqrt(jnp.mean(o * o, axis=1, keepdims=True) + EPS)
        ohat = o * r
        pg = pg_ref[...]
        gate = mod_ref[:, 2 * d:3 * d]
        y = ohat * pg
        e = x_ref[...] + gate * y - t_ref[...]
        lacc[...] += jnp.sum(e * e, axis=0, keepdims=True)
        dout = e * (1.0 / d)
        dout_ref[...] = dout
        dgate_ref[...] += jnp.sum(dout * y, axis=0, keepdims=True)
        dy = dout * gate
        dpg_ref[...] += jnp.sum(dy * ohat, axis=0, keepdims=True)
        dohat = dy * pg
        do = r * (dohat - ohat * jnp.mean(dohat * ohat, axis=1, keepdims=True))
        do_ref[...] = do.astype(BF16)

        @pl.when(i == nsteps - 1)
        def _():
            loss_ref[...] = (0.5 / d) * jnp.sum(lacc[...], axis=1, keepdims=True)

    tile = pl.BlockSpec((tm, d), lambda i: (i, 0))
    vec = _const_spec((1, d))
    return pl.pallas_call(
        body, name="out_loss", grid=(nsteps,),
        out_shape=(jax.ShapeDtypeStruct((s, d), F32), jax.ShapeDtypeStruct((s, d), BF16),
                   jax.ShapeDtypeStruct((1, 1), F32), jax.ShapeDtypeStruct((1, d), F32),
                   jax.ShapeDtypeStruct((1, d), F32)),
        in_specs=[tile, tile, tile, _resident((d, d)), vec, _const_spec((1, 3 * d))],
        out_specs=(tile, tile, _const_spec((1, 1)), vec, vec),
        scratch_shapes=[pltpu.VMEM((1, d), F32)],
        compiler_params=_cparams(("arbitrary",), 2 * tm * d * (2 + 4 + 4 + 4 + 2) + d * d * 2 + 8 * tm * d * 4),
    )(merged, x, target, w_out, post_g, mod)


def _bwd_out(d_o, g, za, zp, w_out, s, d):
    tm = _row_tile(s, ROW_TILE)

    def body(do_ref, g_ref, za_ref, zp_ref, w_ref, dza_ref, dzp_ref, dgl_ref, dbm_ref):
        i = pl.program_id(0)

        @pl.when(i == 0)
        def _():
            dbm_ref[...] = jnp.zeros_like(dbm_ref)

        dm = lax.dot_general(do_ref[...], w_ref[...], (((1,), (1,)), ((), ())), preferred_element_type=F32)
        gate = g_ref[...].astype(F32)
        ga, gp = gate[:, :d], gate[:, d:]
        dza_ref[...] = (dm * ga).astype(BF16)
        dzp_ref[...] = (dm * gp).astype(BF16)
        dla = dm * za_ref[...].astype(F32) * ga * (1.0 - ga)
        dlp = dm * zp_ref[...].astype(F32) * gp * (1.0 - gp)
        dgl_ref[:, :d] = dla.astype(BF16)
        dgl_ref[:, d:] = dlp.astype(BF16)
        dbm_ref[:, :d] += jnp.sum(dla, axis=0, keepdims=True)
        dbm_ref[:, d:] += jnp.sum(dlp, axis=0, keepdims=True)

    tile = pl.BlockSpec((tm, d), lambda i: (i, 0))
    wide = pl.BlockSpec((tm, 2 * d), lambda i: (i, 0))
    return pl.pallas_call(
        body, name="bwd_out", grid=(s // tm,),
        out_shape=(jax.ShapeDtypeStruct((s, d), BF16), jax.ShapeDtypeStruct((s, d), BF16),
                   jax.ShapeDtypeStruct((s, 2 * d), BF16), jax.ShapeDtypeStruct((1, 2 * d), F32)),
        in_specs=[tile, wide, tile, tile, _resident((d, d))],
        out_specs=(tile, tile, wide, _const_spec((1, 2 * d))),
        compiler_params=_cparams(("arbitrary",), 2 * tm * d * 2 * 9 + d * d * 2 + 8 * tm * d * 4),
    )(d_o, g, za, zp, w_out)


def _bwd_branches(dza, dzp, attn, proj, mixed, w_bra, w_brp, w_grp, pool_scale, s, d):
    a_w, p_w = ATTN_WIDTH, pool_scale.shape[1]
    cg = p_w // N_POOL_GROUPS
    tm = _row_tile(s, ROW_TILE)
    off_ga = ATTN_WIDTH + 2 * KV_WIDTH
    off_gp = off_ga + a_w + p_w
    n_ga, n_gp = a_w // COL_TILE, p_w // COL_TILE

    def body(*refs):
        it = iter(refs)
        dza_ref, dzp_ref, attn_ref = next(it), next(it), next(it)
        ga_refs = [next(it) for _ in range(n_ga)]
        gp_refs = [next(it) for _ in range(n_gp)]
        mixed_ref, wa_ref, wp_ref, wg_ref, ps_ref = (next(it) for _ in range(5))
        dattn_ref, dga_ref, dgp_ref, dmix_ref, dpool_ref, dps_ref = (next(it) for _ in range(6))
        i = pl.program_id(0)

        @pl.when(i == 0)
        def _():
            dps_ref[...] = jnp.zeros_like(dps_ref)

        dya = lax.dot_general(dza_ref[...], wa_ref[...], (((1,), (1,)), ((), ())), preferred_element_type=F32)
        ga = _cat(ga_refs).astype(F32)
        dattn_ref[...] = (dya * _silu(ga)).astype(BF16)
        dga_ref[...] = (dya * attn_ref[...].astype(F32) * _dsilu(ga)).astype(BF16)

        dyp = lax.dot_general(dzp_ref[...], wp_ref[...], (((1,), (1,)), ((), ())), preferred_element_type=F32)
        gp = _cat(gp_refs).astype(F32)
        mixed = mixed_ref[...].astype(F32)
        ps = ps_ref[...]
        sg = _silu(gp)
        dgp_ref[...] = (dyp * mixed * ps * _dsilu(gp)).astype(BF16)
        dps_ref[...] += jnp.sum(dyp * sg * mixed, axis=0, keepdims=True)
        dmix = (dyp * sg * ps).astype(BF16)
        dmix_ref[...] = dmix
        for gi in range(N_POOL_GROUPS):
            dp = lax.dot_general(dmix[:, gi * cg:(gi + 1) * cg], wg_ref[gi], (((1,), (1,)), ((), ())),
                                 preferred_element_type=F32)
            dpool_ref[:, gi * cg:(gi + 1) * cg] = dp.astype(BF16)

    row = lambda i: i
    tile = lambda w: pl.BlockSpec((tm, w), lambda i: (i, 0))
    in_specs = [tile(d), tile(d), tile(a_w)] + _col_specs(off_ga, a_w, tm, row) + _col_specs(off_gp, p_w, tm, row)
    in_specs += [tile(p_w), _resident((a_w, d)), _resident((p_w, d)), _resident((N_POOL_GROUPS, cg, cg)),
                 _const_spec((1, p_w))]
    out_widths = (a_w, a_w, p_w, p_w, p_w)
    est = 2 * tm * (2 * d + 2 * a_w + 2 * p_w + sum(out_widths)) * 2 + (a_w + p_w) * d * 2 + 8 * tm * a_w * 4
    return pl.pallas_call(
        body, name="bwd_branches", grid=(s // tm,),
        out_shape=tuple(jax.ShapeDtypeStruct((s, w), BF16) for w in out_widths)
        + (jax.ShapeDtypeStruct((1, p_w), F32),),
        in_specs=in_specs, out_specs=tuple(tile(w) for w in out_widths) + (_const_spec((1, p_w)),),
        compiler_params=_cparams(("arbitrary",), est),
    )(dza, dzp, attn, *([proj] * (n_ga + n_gp)), mixed, w_bra, w_brp, w_grp, pool_scale)


def _pool_backward(dpooled, s):
    _, p_w = dpooled.shape
    cg = p_w // N_POOL_GROUPS
    tm = _row_tile(s, ROW_TILE)
    per, last = tm // HALO, s // HALO - 1

    def body(dp_ref, prev_ref, next_ref, du_ref):
        i = pl.program_id(0)
        dp = dp_ref[...].astype(F32)
        ext = jnp.concatenate([prev_ref[...].astype(F32), dp, next_ref[...].astype(F32)], axis=0)
        pos, ok = _pool_positions(i, tm, s, cg)
        for gi, w in enumerate(POOL_SIZES):
            t = jnp.where(ok, ext[:, gi * cg:(gi + 1) * cg], 0.0) / _pool_count(pos, w, s)
            acc = t[HALO - w // 2 + 1: HALO - w // 2 + 1 + tm]
            for dd in range(-w // 2 + 2, w // 2 + 1):
                acc = acc + t[HALO + dd: HALO + dd + tm]
            du_ref[:, gi * cg:(gi + 1) * cg] = (acc - dp[:, gi * cg:(gi + 1) * cg]).astype(BF16)

    return pl.pallas_call(
        body, name="pool_backward", grid=(s // tm,), out_shape=jax.ShapeDtypeStruct((s, p_w), BF16),
        in_specs=[pl.BlockSpec((tm, p_w), lambda i: (i, 0)),
                  pl.BlockSpec((HALO, p_w), lambda i: (jnp.maximum(i * per - 1, 0), 0)),
                  pl.BlockSpec((HALO, p_w), lambda i: (jnp.minimum((i + 1) * per, last), 0))],
        out_specs=pl.BlockSpec((tm, p_w), lambda i: (i, 0)),
        compiler_params=_cparams(("parallel",), 4 * tm * p_w * 2 + 8 * tm * p_w * 4),
    )(dpooled, dpooled, dpooled)


def _attn_backward(proj, dattn, bias2, sink, s):
    nblk = s // BLOCK
    nq = ATTN_WIDTH // COL_TILE
    kv_col = ATTN_WIDTH // COL_TILE
    rows = GQA_GROUP * BLOCK
    scale = HEAD_DIM ** -0.5

    def body(*refs):
        q_refs = refs[:nq]
        kvp, kvc, kvn, do_ref, bias_ref, sink_ref, dq_ref, dkv_ref, dbias_ref, dsink_ref, acc, sacc = refs[nq:]
        n = pl.program_id(0)

        @pl.when(n == 0)
        def _():
            acc[...] = jnp.zeros_like(acc)
            sacc[...] = jnp.zeros_like(sacc)
            dbias_ref[...] = jnp.zeros_like(dbias_ref)
            dsink_ref[...] = jnp.zeros_like(dsink_ref)

        @pl.when(jnp.logical_and(n >= 1, n < nblk))
        def _():
            acc[(n + 1) % 3] = jnp.zeros((BLOCK, 2 * KV_WIDTH), F32)

        @pl.when(n < nblk)
        def _():
            q = _cat(q_refs)
            do = do_ref[...]
            kv = jnp.concatenate([kvp[...], kvc[...], kvn[...]], axis=0)
            valid = _attn_mask(n, s)
            for kh in range(N_KV_HEADS):
                qs = _stack_heads(q, kh)
                dos = _stack_heads(do, kh)
                k = kv[:, kh * HEAD_DIM:(kh + 1) * HEAD_DIM]
                v = kv[:, KV_WIDTH + kh * HEAD_DIM: KV_WIDTH + (kh + 1) * HEAD_DIM]
                p, ps = _softmax_parts(qs, k, bias_ref[kh], valid, _sink_column(sink_ref, kh))
                dp = lax.dot_general(dos, v, (((1,), (1,)), ((), ())), preferred_element_type=F32)
                delta = jnp.sum(p * dp, axis=1, keepdims=True)
                ds = p * (dp - delta)
                dbias_ref[kh] += ds
                sacc[kh] += -ps * delta
                dsb = ds.astype(BF16)
                dq = jnp.dot(dsb, k, preferred_element_type=F32) * scale
                for g in range(GQA_GROUP):
                    h = kh * GQA_GROUP + g
                    dq_ref[:, h * HEAD_DIM:(h + 1) * HEAD_DIM] = dq[g * BLOCK:(g + 1) * BLOCK].astype(BF16)
                dk = lax.dot_general(dsb, qs, (((0,), (0,)), ((), ())), preferred_element_type=F32) * scale
                dv = lax.dot_general(p.astype(BF16), dos, (((0,), (0,)), ((), ())), preferred_element_type=F32)
                for j in range(3):
                    slot = (n + 2 + j) % 3
                    acc[slot, :, kh * HEAD_DIM:(kh + 1) * HEAD_DIM] += dk[j * BLOCK:(j + 1) * BLOCK]
                    acc[slot, :, KV_WIDTH + kh * HEAD_DIM: KV_WIDTH + (kh + 1) * HEAD_DIM] += dv[j * BLOCK:(j + 1) * BLOCK]

        dkv_ref[...] = acc[(n + 2) % 3].astype(BF16)

        @pl.when(n == nblk)
        def _():
            lane = lax.broadcasted_iota(jnp.int32, (1, LANE), 1)
            out = jnp.zeros((1, LANE), F32)
            for kh in range(N_KV_HEADS):
                col = sacc[kh]
                for g in range(GQA_GROUP):
                    out = jnp.where(lane == kh * GQA_GROUP + g, jnp.sum(col[g * BLOCK:(g + 1) * BLOCK]), out)
            dsink_ref[...] = out

    qi = lambda n: jnp.minimum(n, nblk - 1)
    in_specs = _col_specs(0, ATTN_WIDTH, BLOCK, qi)
    in_specs += [pl.BlockSpec((BLOCK, COL_TILE), lambda n: (jnp.maximum(qi(n) - 1, 0), kv_col)),
                 pl.BlockSpec((BLOCK, COL_TILE), lambda n: (qi(n), kv_col)),
                 pl.BlockSpec((BLOCK, COL_TILE), lambda n: (jnp.minimum(qi(n) + 1, nblk - 1), kv_col)),
                 pl.BlockSpec((BLOCK, ATTN_WIDTH), lambda n: (qi(n), 0)),
                 _const_spec((N_KV_HEADS, rows, SPAN)),
                 pl.BlockSpec(memory_space=pltpu.SMEM)]
    return pl.pallas_call(
        body, name="attn_backward", grid=(nblk + 1,),
        out_shape=(jax.ShapeDtypeStruct((s, ATTN_WIDTH), BF16), jax.ShapeDtypeStruct((s, 2 * KV_WIDTH), BF16),
                   jax.ShapeDtypeStruct((N_KV_HEADS, rows, SPAN), F32), jax.ShapeDtypeStruct((1, LANE), F32)),
        in_specs=in_specs,
        out_specs=(pl.BlockSpec((BLOCK, ATTN_WIDTH), lambda n: (qi(n), 0)),
                   pl.BlockSpec((BLOCK, 2 * KV_WIDTH), lambda n: (jnp.clip(n - 1, 0, nblk - 1), 0)),
                   _const_spec((N_KV_HEADS, rows, SPAN)), _const_spec((1, LANE))),
        scratch_shapes=[pltpu.VMEM((3, BLOCK, 2 * KV_WIDTH), F32), pltpu.VMEM((N_KV_HEADS, rows, 1), F32)],
        compiler_params=_cparams(("arbitrary",), 24 << 20),
    )(*([proj] * (nq + 3)), dattn, bias2, sink)


def _pick_tile(n, cap):
    t = cap - cap % LANE
    while n % t:
        t -= LANE
    return t


def _matmul_tn(name, a, b):
    s, m = a.shape
    _, n = b.shape
    tk = _row_tile(s, 512)
    tm, tn = _pick_tile(m, 1024), _pick_tile(n, 1152)
    nk = s // tk

    def body(a_ref, b_ref, o_ref, acc):
        k = pl.program_id(2)

        @pl.when(k == 0)
        def _():
            acc[...] = jnp.zeros_like(acc)

        acc[...] += lax.dot_general(a_ref[...], b_ref[...], (((0,), (0,)), ((), ())), preferred_element_type=F32)

        @pl.when(k == nk - 1)
        def _():
            o_ref[...] = acc[...].astype(BF16)

    return pl.pallas_call(
        body, name=name, grid=(m // tm, n // tn, nk), out_shape=jax.ShapeDtypeStruct((m, n), BF16),
        in_specs=[pl.BlockSpec((tk, tm), lambda i, j, k: (k, i)), pl.BlockSpec((tk, tn), lambda i, j, k: (k, j))],
        out_specs=pl.BlockSpec((tm, tn), lambda i, j, k: (i, j)),
        scratch_shapes=[pltpu.VMEM((tm, tn), F32)],
        compiler_params=_cparams(("parallel", "parallel", "arbitrary"),
                                 2 * tk * (tm + tn) * 2 + tm * tn * (4 + 4 + 4)),
    )(a, b)


def _pool_weight_grad(pooled, dmix, s):
    _, p_w = pooled.shape
    cg = p_w // N_POOL_GROUPS
    tk = _row_tile(s, 512)
    nk = s // tk

    def body(a_ref, b_ref, o_ref, acc):
        k = pl.program_id(1)

        @pl.when(k == 0)
        def _():
            acc[...] = jnp.zeros_like(acc)

        acc[...] += lax.dot_general(a_ref[...], b_ref[...], (((0,), (0,)), ((), ())), preferred_element_type=F32)

        @pl.when(k == nk - 1)
        def _():
            o_ref[0] = acc[...].astype(BF16)

    return pl.pallas_call(
        body, name="pool_weight_grad", grid=(N_POOL_GROUPS, nk),
        out_shape=jax.ShapeDtypeStruct((N_POOL_GROUPS, cg, cg), BF16),
        in_specs=[pl.BlockSpec((tk, cg), lambda g, k: (k, g)), pl.BlockSpec((tk, cg), lambda g, k: (k, g))],
        out_specs=pl.BlockSpec((1, cg, cg), lambda g, k: (g, 0, 0)),
        scratch_shapes=[pltpu.VMEM((cg, cg), F32)],
        compiler_params=_cparams(("parallel", "arbitrary"), 16 << 20),
    )(pooled, dmix)


def _bwd_input(dproj, dgl, w_in, w_merge, x, dout, mod, pre_g, s, d):
    tm = _row_tile(s, 512)
    n_in = dproj.shape[1] // COL_TILE
    n_mg = dgl.shape[1] // COL_TILE
    nk = n_in + n_mg
    ni = s // tm

    def body(dp_ref, dg_ref, wi_ref, wm_ref, x_ref, dout_ref, mod_ref, pg_ref, gx_ref, dsh_ref, dsc_ref, dpg_ref,
             acc):
        i, k = pl.program_id(0), pl.program_id(1)

        @pl.when(jnp.logical_and(i == 0, k == 0))
        def _():
            dsh_ref[...] = jnp.zeros_like(dsh_ref)
            dsc_ref[...] = jnp.zeros_like(dsc_ref)
            dpg_ref[...] = jnp.zeros_like(dpg_ref)

        @pl.when(k == 0)
        def _():
            acc[...] = jnp.zeros_like(acc)

        @pl.when(k < n_in)
        def _():
            acc[...] += lax.dot_general(dp_ref[...], wi_ref[...], (((1,), (1,)), ((), ())),
                                        preferred_element_type=F32)

        @pl.when(k >= n_in)
        def _():
            acc[...] += lax.dot_general(dg_ref[...], wm_ref[...], (((1,), (1,)), ((), ())),
                                        preferred_element_type=F32)

        @pl.when(k == nk - 1)
        def _():
            dh = acc[...]
            xv = x_ref[...]
            r = lax.rsqrt(jnp.mean(xv * xv, axis=1, keepdims=True) + EPS)
            xhat = xv * r
            pg = pg_ref[...]
            one_scale = 1.0 + mod_ref[:, d:2 * d]
            dsh_ref[...] += jnp.sum(dh, axis=0, keepdims=True)
            dsc_ref[...] += jnp.sum(dh * xhat, axis=0, keepdims=True) * pg
            dpg_ref[...] += jnp.sum(dh * xhat, axis=0, keepdims=True) * one_scale
            dxh = dh * (one_scale * pg)
            dx = r * (dxh - xhat * jnp.mean(dxh * xhat, axis=1, keepdims=True))
            gx_ref[...] = dout_ref[...] + dx

    tile = pl.BlockSpec((tm, d), lambda i, k: (i, 0))
    vec = _const_spec((1, d))
    in_specs = [pl.BlockSpec((tm, COL_TILE), lambda i, k: (i, jnp.minimum(k, n_in - 1))),
                pl.BlockSpec((tm, COL_TILE), lambda i, k: (i, jnp.maximum(k - n_in, 0))),
                pl.BlockSpec((d, COL_TILE), lambda i, k: (0, jnp.minimum(k, n_in - 1))),
                pl.BlockSpec((d, COL_TILE), lambda i, k: (0, jnp.maximum(k - n_in, 0))),
                tile, tile, _const_spec((1, 3 * d)), vec]
    est = 2 * 2 * (tm + d) * COL_TILE * 2 + 2 * 3 * tm * d * 4 + tm * d * 4 + 8 * tm * d * 4
    return pl.pallas_call(
        body, name="bwd_input", grid=(ni, nk),
        out_shape=(jax.ShapeDtypeStruct((s, d), F32),) + (jax.ShapeDtypeStruct((1, d), F32),) * 3,
        in_specs=in_specs, out_specs=(tile, vec, vec, vec),
        scratch_shapes=[pltpu.VMEM((tm, d), F32)],
        compiler_params=_cparams(("arbitrary", "arbitrary"), est),
    )(dproj, dgl, w_in, w_merge, x, dout, mod, pre_g)


def _pad_lanes(v, width):
    return jnp.pad(v, ((0, 0), (0, width - v.shape[1])))


def kernel(x, c, rel_bias_table, w_ada, b_ada, pre_norm_g, post_norm_g, w_in, attn_sink, w_pool_group, pool_scale, w_branch_attn, w_branch_pool, w_merge, b_merge, w_out, loss_target, m_rel_bias_table, m_w_ada, m_b_ada, m_pre_norm_g, m_post_norm_g, m_w_in, m_attn_sink, m_w_pool_group, m_pool_scale, m_w_branch_attn, m_w_branch_pool, m_w_merge, m_b_merge, m_w_out, v_rel_bias_table, v_w_ada, v_b_ada, v_pre_norm_g, v_post_norm_g, v_w_in, v_attn_sink, v_w_pool_group, v_pool_scale, v_w_branch_attn, v_w_branch_pool, v_w_merge, v_b_merge, v_w_out):
    _, s, d = x.shape
    p_w = pool_scale.shape[-1]
    cg = p_w // N_POOL_GROUPS
    in_w = 2 * ATTN_WIDTH + 2 * KV_WIDTH + 2 * p_w
    x2, t2 = x[0], loss_target[0]
    chip = 2 * lax.axis_index("x") + lax.axis_index("y")

    sc_all = _all_gather8("gather_cond", jnp.broadcast_to(c, (N_DEV, d)), pre=_silu)
    m_all = _all_gather8("gather_mod", _ada_forward(sc_all, w_ada[0]))
    mod = _mod_finish(m_all, b_ada)

    specs = [_Sharded("col", (d, in_w)), _Sharded("col", (d, 2 * d)), _Sharded("col", (ATTN_WIDTH, d)),
             _Sharded("col", (p_w, d)), _Sharded("row", (d, d)), _Sharded("grp", (N_POOL_GROUPS, cg, cg))]
    shards32 = [w_in[0], w_merge[0], w_branch_attn[0], w_branch_pool[0], w_out[0],
                w_pool_group[0].reshape(N_POOL_GROUPS * cg // N_CHIPS, cg)]
    names = ["w_in", "w_merge", "w_branch_attn", "w_branch_pool", "w_out", "w_pool_group"]
    shards16 = [_cast_bf16("cast_" + nm, w) for nm, w in zip(names, shards32)]
    shards16[5] = shards16[5].reshape(N_POOL_GROUPS, cg // N_CHIPS, cg)
    wf_in, wf_merge, wf_bra, wf_brp, wf_out, wf_grp = _gather_weights(specs, shards16)

    h = _prenorm(x2, mod, pre_norm_g)
    proj = _matmul("proj", h, wf_in)
    gates = _matmul("merge_gates", h, wf_merge, bias=b_merge)
    buckets = _t5_buckets()
    bias2 = _bias_table(rel_bias_table, buckets).reshape(N_KV_HEADS, GQA_GROUP * BLOCK, SPAN)
    attn = _attn_forward(proj, bias2, attn_sink, s)
    ya, yp, za, zp, merged, pooled, mixed = _branches(proj, attn, gates, wf_bra, wf_brp, wf_grp, pool_scale, s, d)
    dout, d_o, loss_part, dgate, dpostg = _out_loss(merged, x2, t2, wf_out, post_norm_g, mod, s, d)

    dza, dzp, dgl, dbm = _bwd_out(d_o, gates, za, zp, wf_out, s, d)
    dattn, dga, dgp, dmix, dpooled, dps = _bwd_branches(dza, dzp, attn, proj, mixed, wf_bra, wf_brp, wf_grp,
                                                         pool_scale, s, d)
    du = _pool_backward(dpooled, s)
    dq, dkv, dbias, dsink = _attn_backward(proj, dattn, bias2, attn_sink, s)
    dproj = jnp.concatenate([dq, dkv, dga, du, dgp], axis=1)
    gx, dshift, dscale, dpreg = _bwd_input(dproj, dgl, wf_in, wf_merge, x2, dout, mod, pre_norm_g, s, d)

    partials = [_matmul_tn("grad_w_in", h, dproj), _matmul_tn("grad_w_merge", h, dgl),
                _matmul_tn("grad_w_branch_attn", ya, dza), _matmul_tn("grad_w_branch_pool", yp, dzp),
                _matmul_tn("grad_w_out", merged, d_o), _pool_weight_grad(pooled, dmix, s)]

    pieces = _scatter_grads(specs, partials)
    halves = []
    for nm, sp, pc in zip(names, specs, pieces):
        flat = pc.reshape(N_DEV, -1, pc.shape[-1])
        halves.append(_reduce8("reduce_" + nm, flat).reshape(sp.piece_shape))
    grads = list(_exchange_halves(specs, halves))
    weights = [w_in, w_merge, w_branch_attn, w_branch_pool, w_out, w_pool_group]
    moms = [m_w_in, m_w_merge, m_w_branch_attn, m_w_branch_pool, m_w_out, m_w_pool_group]
    vars_ = [v_w_in, v_w_merge, v_w_branch_attn, v_w_branch_pool, v_w_out, v_w_pool_group]
    big = {}
    for nm, gr, w, m, v in zip(names, grads, weights, moms, vars_):
        shape2 = (-1, w.shape[-1])
        dl, nm_, nv_ = _adamw("adamw_" + nm, w.reshape(shape2), gr.reshape(shape2), m.reshape(shape2),
                              v.reshape(shape2))
        big[nm] = tuple(a.reshape(w.shape) for a in (gr, dl, nm_, nv_))

    dtable = _bias_table_grad(dbias.reshape(N_Q_HEADS, BLOCK, SPAN), buckets)[:, :N_Q_HEADS]
    segs = [("b_ada", jnp.concatenate([dshift, dscale, dgate], axis=1), 3 * d),
            ("pre_norm_g", dpreg, d), ("post_norm_g", dpostg, d), ("attn_sink", dsink, LANE),
            ("pool_scale", dps, p_w), ("b_merge", dbm, 2 * d), ("rel_bias_table", dtable.reshape(1, -1), 2 * LANE)]
    packed = jnp.concatenate([_pad_lanes(v, w) for _, v, w in segs], axis=1)
    n_small = packed.shape[1]
    all_small = _all_gather8("gather_small", jnp.pad(packed, ((0, N_DEV - 1), (0, 0))))
    rows = all_small[:, 0, :]

    def pack(vals):
        return jnp.concatenate([_pad_lanes(v.reshape(1, -1), w) for v, (_, _, w) in zip(vals, segs)], axis=1)

    small_w = [b_ada, pre_norm_g, post_norm_g, attn_sink, pool_scale, b_merge, rel_bias_table]
    small_m = [m_b_ada, m_pre_norm_g, m_post_norm_g, m_attn_sink, m_pool_scale, m_b_merge, m_rel_bias_table]
    small_v = [v_b_ada, v_pre_norm_g, v_post_norm_g, v_attn_sink, v_pool_scale, v_b_merge, v_rel_bias_table]
    g_small, d_small, nm_small, nv_small = _small_update(rows, pack(small_w), pack(small_m), pack(small_v))
    small = {}
    off = 0
    for (nm, _, w), ref in zip(segs, small_w):
        cut = lambda a: a[:, off:off + ref.size].reshape(ref.shape)
        small[nm] = (cut(g_small), cut(d_small), cut(nm_small), cut(nv_small))
        off += w

    dmod_cols = lax.dynamic_slice_in_dim(rows[:, :3 * d], chip * (3 * d // N_CHIPS), 3 * d // N_CHIPS, axis=1)
    sc_t = sc_all[:, 0, :].T
    g_ada, d_ada, nm_ada, nv_ada = _ada_backward(sc_t, dmod_cols, w_ada[0], m_w_ada[0], v_w_ada[0])
    big["w_ada"] = tuple(a.reshape(w_ada.shape) for a in (g_ada, d_ada, nm_ada, nv_ada))

    loss = lax.psum(loss_part[0, 0], ("x", "y", "c"))
    order = ["rel_bias_table", "w_ada", "b_ada", "pre_norm_g", "post_norm_g", "w_in", "attn_sink", "w_pool_group",
             "pool_scale", "w_branch_attn", "w_branch_pool", "w_merge", "b_merge", "w_out"]
    res = {**big, **small}
    outs = [loss, gx.reshape(x.shape)]
    for part in range(4):
        outs += [res[nm][part] for nm in order]
    return tuple(outs)


def _small_update(rows, w, m, v):
    _, n = rows.shape

    def body(r_ref, w_ref, m_ref, v_ref, g_ref, d_ref, nm_ref, nv_ref):
        g = r_ref[0:1, :]
        for k in range(1, N_DEV):
            g = g + r_ref[k:k + 1, :]
        dl, nm, nv = _adam_math(w_ref[...], g, m_ref[...], v_ref[...])
        g_ref[...] = g
        d_ref[...] = dl
        nm_ref[...] = nm
        nv_ref[...] = nv

    vm = pl.BlockSpec(memory_space=pltpu.VMEM)
    out = jax.ShapeDtypeStruct((1, n), F32)
    return pl.pallas_call(
        body, name="small_update", out_shape=(out,) * 4, in_specs=[vm] * 4, out_specs=(vm,) * 4,
    )(rows, w, m, v)
```

```python
import functools
import math

import numpy as np
import jax
import jax.numpy as jnp
from jax import lax
from jax.experimental import pallas as pl
from jax.experimental.pallas import tpu as pltpu

F32 = jnp.float32
BF16 = jnp.bfloat16
MESH = pl.DeviceIdType.MESH

HEAD_DIM = 128
N_Q_HEADS = 8
N_KV_HEADS = 2
GQA_GROUP = N_Q_HEADS // N_KV_HEADS
ATTN_WIDTH = N_Q_HEADS * HEAD_DIM
KV_WIDTH = N_KV_HEADS * HEAD_DIM
WINDOW = 128
BLOCK = 128
SPAN = BLOCK + 2 * WINDOW
N_BUCKETS = 32
MAX_DISTANCE = 128
POOL_SIZES = (2, 4, 8, 16)
N_POOL_GROUPS = len(POOL_SIZES)
HALO = 16
EPS = 1e-6
NEG_INF = -1e30
ADAM_LR = 0.001
ADAM_B1 = 0.9
ADAM_B2 = 0.999
ADAM_EPS = 1e-08
ADAM_WD = 0.01
ADAM_STEP = 10

N_DEV = 8
N_CHIPS = 4
LANE = 128
COL_TILE = 512
VMEM_CAP = 60000 * 1024
ROW_TILE = 256


def _cparams(sem, est_bytes):
    limit = int(min(max(est_bytes * 5 // 4 + (4 << 20), 16 << 20), VMEM_CAP))
    return pltpu.CompilerParams(dimension_semantics=sem, vmem_limit_bytes=limit)


def _sigmoid(x):
    return jax.nn.sigmoid(x)


def _silu(x):
    return x * _sigmoid(x)


def _dsilu(x):
    s = _sigmoid(x)
    return s * (1.0 + x * (1.0 - s))


def _place():
    x, y, c = lax.axis_index("x"), lax.axis_index("y"), lax.axis_index("c")
    return x, y, c


def _flip(v, bit):
    return (1 - v) if bit else v


def _xor_peer(k):
    x, y, c = _place()
    return (_flip(x, (k >> 2) & 1), _flip(y, (k >> 1) & 1), _flip(c, k & 1))


def _resident(shape):
    nd = len(shape)
    return pl.BlockSpec(shape, lambda *_: (0,) * nd, pipeline_mode=pl.Buffered(1))


def _const_spec(shape):
    nd = len(shape)
    return pl.BlockSpec(shape, lambda *_: (0,) * nd)


CHUNK_BYTES = 256 << 10
MAX_CHUNKS = 32


def _all_gather8(name, x, nrows, pre=None):
    r, n = x.shape

    def body(x_ref, out_ref, stage, send_sems, recv_sems):
        px, py, pc = _place()
        me = 4 * px + 2 * py + pc
        v = x_ref[...]
        if pre is not None:
            v = pre(v)
        stage[...] = v[0:nrows]
        out_ref[me] = v[0:nrows]
        copies = []
        for k in range(1, N_DEV):
            cp = pltpu.make_async_remote_copy(
                src_ref=stage, dst_ref=out_ref.at[me], send_sem=send_sems.at[k - 1], recv_sem=recv_sems.at[k - 1],
                device_id=_xor_peer(k), device_id_type=MESH)
            cp.start()
            copies.append(cp)
        for cp in copies:
            cp.wait()

    return pl.pallas_call(
        body, name=name,
        out_shape=jax.ShapeDtypeStruct((N_DEV, nrows, n), F32),
        in_specs=[pl.BlockSpec(memory_space=pltpu.VMEM)],
        out_specs=pl.BlockSpec(memory_space=pltpu.VMEM),
        scratch_shapes=[pltpu.VMEM((nrows, n), F32), pltpu.SemaphoreType.DMA((N_DEV - 1,)),
                        pltpu.SemaphoreType.DMA((N_DEV - 1,))],
    )(x)


class _Sharded:
    def __init__(self, kind, full_shape):
        self.kind = kind
        self.full_shape = tuple(full_shape)
        if kind == "col":
            r, c = full_shape
            self.shard_shape = (r, c // N_CHIPS)
        elif kind == "row":
            r, c = full_shape
            self.shard_shape = (r // N_CHIPS, c)
        else:
            g, r, c = full_shape
            self.shard_shape = (g, r // N_CHIPS, c)
        self.axis = 1 if kind == "grp" else 0
        s = list(self.shard_shape)
        s[self.axis] //= 2
        self.piece_shape = tuple(s)

    def _rows(self, ref, start, size):
        idx = (slice(None),) * self.axis + (pl.ds(pl.multiple_of(start, 16), size),)
        return ref.at[idx]

    def shard_half(self, ref, hc):
        h = self.piece_shape[self.axis]
        return self._rows(ref, hc * h, h)

    def window(self, ref, chip, hc=None):
        s = self.shard_shape
        if self.kind == "col":
            cols = pl.ds(pl.multiple_of(chip * s[1], LANE), s[1])
            if hc is None:
                return ref.at[:, cols]
            h = s[0] // 2
            return ref.at[pl.ds(pl.multiple_of(hc * h, 16), h), cols]
        n = s[self.axis]
        if hc is None:
            return self._rows(ref, chip * n, n)
        return self._rows(ref, chip * n + hc * (n // 2), n // 2)

    def chunks(self, view, shape, itemsize):
        rows = shape[self.axis]
        nbytes = math.prod(shape) * itemsize
        n = 1
        while 2 * n <= MAX_CHUNKS and nbytes // (2 * n) >= CHUNK_BYTES and rows % (2 * n * 16) == 0:
            n *= 2
        h = rows // n
        return [view.at[(slice(None),) * self.axis + (pl.ds(j * h, h),)] for j in range(n)]


def _remote(src, dst, send_sem, recv_sem, to):
    return pltpu.make_async_remote_copy(src_ref=src, dst_ref=dst, send_sem=send_sem, recv_sem=recv_sem,
                                        device_id=to, device_id_type=MESH)


def _start_remote(sp, src, dst, shape, itemsize, send_sem, recv_sem, to):
    for s_part, d_part in zip(sp.chunks(src, shape, itemsize), sp.chunks(dst, shape, itemsize)):
        _remote(s_part, d_part, send_sem, recv_sem, to).start()


def _start_local(sp, src, dst, shape, itemsize, sem):
    for s_part, d_part in zip(sp.chunks(src, shape, itemsize), sp.chunks(dst, shape, itemsize)):
        pltpu.make_async_copy(s_part, d_part, sem).start()


class _GatherWeights:
    def __init__(self, specs, shards):
        self.specs = specs
        self.inputs = list(shards)
        self.out_shapes = [jax.ShapeDtypeStruct(sp.full_shape, BF16) for sp in specs]
        nw = len(specs)
        self.scratch = [pltpu.SemaphoreType.DMA((6 * nw,)), pltpu.SemaphoreType.DMA((6 * nw,)),
                        pltpu.SemaphoreType.DMA((nw,))]

    def phases(self, nsteps):
        return [(0, self.start), (max(1, (7 * nsteps) // 10) if nsteps > 1 else 0, self.middle),
                (nsteps - 1, self.end)]

    def _ctx(self):
        x, y, c = _place()
        return x, y, c, 2 * x + y, (x, y, 1 - c), [(1 - x, y), (x, 1 - y), (1 - x, 1 - y)]

    def start(self, shard_refs, full_refs, sems):
        send_sems, recv_sems, local_sems = sems
        x, y, c, my_chip, sibling, chips = self._ctx()
        for w, sp in enumerate(self.specs):
            _start_local(sp, shard_refs[w], sp.window(full_refs[w], my_chip), sp.shard_shape, 2, local_sems.at[w])
            for t, (cx, cy) in enumerate(chips):
                _start_remote(sp, sp.shard_half(shard_refs[w], c), sp.window(full_refs[w], my_chip, c),
                              sp.piece_shape, 2, send_sems.at[6 * w + t], recv_sems.at[6 * w + t], (cx, cy, c))

    def middle(self, shard_refs, full_refs, sems):
        send_sems, recv_sems, local_sems = sems
        x, y, c, my_chip, sibling, chips = self._ctx()
        for w, sp in enumerate(self.specs):
            for t, (cx, cy) in enumerate(chips):
                landed = sp.window(full_refs[w], 2 * cx + cy, c)
                _remote(landed, landed, send_sems.at[6 * w + t], recv_sems.at[6 * w + t], (cx, cy, c)).wait_recv()
                _start_remote(sp, landed, landed, sp.piece_shape, 2, send_sems.at[6 * w + 3 + t],
                              recv_sems.at[6 * w + 3 + t], sibling)

    def end(self, shard_refs, full_refs, sems):
        send_sems, recv_sems, local_sems = sems
        x, y, c, my_chip, sibling, chips = self._ctx()
        for w, sp in enumerate(self.specs):
            for t, (cx, cy) in enumerate(chips):
                other = sp.window(full_refs[w], 2 * cx + cy, 1 - c)
                _remote(other, other, send_sems.at[6 * w + 3 + t], recv_sems.at[6 * w + 3 + t], sibling).wait_recv()
        for w, sp in enumerate(self.specs):
            for t, (cx, cy) in enumerate(chips):
                mine = sp.shard_half(shard_refs[w], c)
                _remote(mine, mine, send_sems.at[6 * w + t], recv_sems.at[6 * w + t], (cx, cy, c)).wait_send()
                landed = sp.window(full_refs[w], 2 * cx + cy, c)
                _remote(landed, landed, send_sems.at[6 * w + 3 + t], recv_sems.at[6 * w + 3 + t], sibling).wait_send()
            pltpu.make_async_copy(shard_refs[w], sp.window(full_refs[w], my_chip), local_sems.at[w]).wait()


class _ScatterGrads:
    def __init__(self, specs, partials):
        self.specs = specs
        self.inputs = list(partials)
        self.out_shapes = [jax.ShapeDtypeStruct((N_DEV,) + sp.piece_shape, BF16) for sp in specs]
        nw = len(specs)
        self.scratch = [pltpu.SemaphoreType.DMA((7 * nw,)), pltpu.SemaphoreType.DMA((7 * nw,)),
                        pltpu.SemaphoreType.DMA((nw,))]

    def phases(self, nsteps):
        return [(0, self.start), (nsteps - 1, self.end)]

    def start(self, part_refs, recv_refs, sems):
        send_sems, recv_sems, local_sems = sems
        x, y, c = _place()
        for w, sp in enumerate(self.specs):
            _start_local(sp, sp.window(part_refs[w], 2 * x + y, c), recv_refs[w].at[0], sp.piece_shape, 2,
                         local_sems.at[w])
            for k in range(1, N_DEV):
                px, py, pc = _xor_peer(k)
                _start_remote(sp, sp.window(part_refs[w], 2 * px + py, pc), recv_refs[w].at[k], sp.piece_shape, 2,
                              send_sems.at[7 * w + k - 1], recv_sems.at[7 * w + k - 1], (px, py, pc))

    def end(self, part_refs, recv_refs, sems):
        send_sems, recv_sems, local_sems = sems
        x, y, c = _place()
        for w, sp in enumerate(self.specs):
            pltpu.make_async_copy(sp.window(part_refs[w], 2 * x + y, c), recv_refs[w].at[0], local_sems.at[w]).wait()
            for k in range(1, N_DEV):
                px, py, pc = _xor_peer(k)
                _remote(sp.window(part_refs[w], 2 * px + py, pc), recv_refs[w].at[k], send_sems.at[7 * w + k - 1],
                        recv_sems.at[7 * w + k - 1], (px, py, pc)).wait()


def _call(name, body, grid, in_specs, args, out_shape, out_specs, scratch, semantics, est_bytes, comm=None):
    out_shape, out_specs = tuple(out_shape), tuple(out_specs)
    if comm is None:
        res = pl.pallas_call(body, name=name, grid=grid, out_shape=out_shape, in_specs=list(in_specs),
                             out_specs=out_specs, scratch_shapes=list(scratch),
                             compiler_params=_cparams(semantics, est_bytes))(*args)
        return tuple(res), ()
    n_in, n_out, n_sc = len(in_specs), len(out_shape), len(scratch)
    c_in, c_out = len(comm.inputs), len(comm.out_shapes)
    nsteps = math.prod(grid)
    phases = comm.phases(nsteps)

    def hosted(*refs):
        pos = [0]

        def take(n):
            part = refs[pos[0]:pos[0] + n]
            pos[0] += n
            return part

        ins, cins, outs, couts, scr, sems = take(n_in), take(c_in), take(n_out), take(c_out), take(n_sc), take(3)
        step = 0
        for ax, extent in enumerate(grid):
            step = step * extent + pl.program_id(ax)
        for at, fn in phases:
            if at == 0:
                pl.when(step == 0)(functools.partial(fn, cins, couts, sems))
        body(*ins, *outs, *scr)
        for at, fn in phases:
            if at > 0:
                pl.when(step == at)(functools.partial(fn, cins, couts, sems))

    any_spec = pl.BlockSpec(memory_space=pl.ANY)
    res = pl.pallas_call(
        hosted, name=name, grid=grid, out_shape=out_shape + tuple(comm.out_shapes),
        in_specs=list(in_specs) + [any_spec] * c_in, out_specs=out_specs + (any_spec,) * c_out,
        scratch_shapes=list(scratch) + list(comm.scratch),
        compiler_params=_cparams(("arbitrary",) * len(grid), est_bytes))(*args, *comm.inputs)
    return tuple(res[:n_out]), tuple(res[n_out:])


def _comm_only(name, comm):
    _, outs = _call(name, lambda: None, (3,), [], [], (), (), [], ("arbitrary",), 1 << 20, comm=comm)
    return outs


def _exchange_halves(specs, halves):
    nw = len(specs)

    def body(*refs):
        half_refs = refs[:nw]
        full_refs = refs[nw:2 * nw]
        send_sems, recv_sems, local_sems = refs[2 * nw:]
        x, y, c = _place()
        sibling = (x, y, 1 - c)
        for w, sp in enumerate(specs):
            mine = sp.shard_half(full_refs[w], c)
            _start_local(sp, half_refs[w], mine, sp.piece_shape, 4, local_sems.at[w])
            _start_remote(sp, half_refs[w], mine, sp.piece_shape, 4, send_sems.at[w], recv_sems.at[w], sibling)
        for w, sp in enumerate(specs):
            mine = sp.shard_half(full_refs[w], c)
            pltpu.make_async_copy(half_refs[w], mine, local_sems.at[w]).wait()
            _remote(half_refs[w], mine, send_sems.at[w], recv_sems.at[w], sibling).wait()

    any_spec = pl.BlockSpec(memory_space=pl.ANY)
    return pl.pallas_call(
        body, name="exchange_halves",
        out_shape=tuple(jax.ShapeDtypeStruct(sp.shard_shape, F32) for sp in specs),
        in_specs=[any_spec] * nw, out_specs=tuple([any_spec] * nw),
        scratch_shapes=[pltpu.SemaphoreType.DMA((nw,)), pltpu.SemaphoreType.DMA((nw,)),
                        pltpu.SemaphoreType.DMA((nw,))],
    )(*halves)


def _row_tile(rows, cap):
    for t in range(min(rows, cap), 0, -1):
        if rows % t == 0 and (t % 16 == 0 or t == rows):
            return t
    return rows


def _cast_bf16(name, x):
    r, c = x.shape
    tr = _row_tile(r, 512)

    def body(x_ref, o_ref):
        o_ref[...] = x_ref[...].astype(BF16)

    return pl.pallas_call(
        body, name=name, grid=(r // tr,), out_shape=jax.ShapeDtypeStruct((r, c), BF16),
        in_specs=[pl.BlockSpec((tr, c), lambda i: (i, 0))], out_specs=pl.BlockSpec((tr, c), lambda i: (i, 0)),
        compiler_params=_cparams(("parallel",), 2 * tr * c * 6),
    )(x)


def _adam_math(w, g, m, v):
    m = ADAM_B1 * m + (1.0 - ADAM_B1) * g
    v = ADAM_B2 * v + (1.0 - ADAM_B2) * (g * g)
    m_hat = m / (1.0 - ADAM_B1 ** ADAM_STEP)
    v_hat = v / (1.0 - ADAM_B2 ** ADAM_STEP)
    delta = -ADAM_LR * (m_hat / (jnp.sqrt(v_hat) + ADAM_EPS) + ADAM_WD * w)
    return delta, m, v


def _adamw(name, w, g, m, v):
    r, c = w.shape
    tr = _row_tile(r, max(8, (1 << 18) // c))

    def body(w_ref, g_ref, m_ref, v_ref, d_ref, nm_ref, nv_ref):
        d, nm, nv = _adam_math(w_ref[...], g_ref[...], m_ref[...], v_ref[...])
        d_ref[...] = d
        nm_ref[...] = nm
        nv_ref[...] = nv

    spec = pl.BlockSpec((tr, c), lambda i: (i, 0))
    out = jax.ShapeDtypeStruct((r, c), F32)
    return pl.pallas_call(
        body, name=name, grid=(r // tr,), out_shape=(out, out, out), in_specs=[spec] * 4, out_specs=(spec,) * 3,
        compiler_params=_cparams(("parallel",), 2 * 7 * tr * c * 4),
    )(w, g, m, v)


def _reduce8(name, x):
    _, r, c = x.shape
    tr = _row_tile(r, max(16, (1 << 18) // c))

    def body(x_ref, o_ref):
        acc = x_ref[0].astype(F32)
        for k in range(1, N_DEV):
            acc = acc + x_ref[k].astype(F32)
        o_ref[...] = acc

    return pl.pallas_call(
        body, name=name, grid=(r // tr,), out_shape=jax.ShapeDtypeStruct((r, c), F32),
        in_specs=[pl.BlockSpec((N_DEV, tr, c), lambda i: (0, i, 0))], out_specs=pl.BlockSpec((tr, c), lambda i: (i, 0)),
        compiler_params=_cparams(("parallel",), 2 * (N_DEV * 2 + 4) * tr * c),
    )(x)


def _t5_buckets():
    rel = jnp.arange(SPAN)[None, :] - WINDOW - jnp.arange(BLOCK)[:, None]
    half = N_BUCKETS // 2
    max_exact = half // 2
    ret = jnp.where(rel > 0, half, 0)
    n = jnp.abs(rel)
    nf = jnp.maximum(n, 1).astype(F32)
    large = max_exact + (jnp.log(nf / max_exact) / math.log(MAX_DISTANCE / max_exact)
                         * (half - max_exact)).astype(jnp.int32)
    large = jnp.minimum(large, half - 1)
    return (ret + jnp.where(n < max_exact, n, large)).astype(jnp.int32)


def _bias_table(table, buckets):
    def body(t_ref, b_ref, o_ref):
        bk = b_ref[...]
        for h in range(N_Q_HEADS):
            acc = jnp.zeros((BLOCK, SPAN), F32)
            for b in range(N_BUCKETS):
                acc = jnp.where(bk == b, t_ref[b, h], acc)
            o_ref[h] = acc

    return pl.pallas_call(
        body, name="bias_table", out_shape=jax.ShapeDtypeStruct((N_Q_HEADS, BLOCK, SPAN), F32),
        in_specs=[pl.BlockSpec(memory_space=pltpu.SMEM), pl.BlockSpec(memory_space=pltpu.VMEM)],
        out_specs=pl.BlockSpec(memory_space=pltpu.VMEM),
    )(table, buckets)


def _bias_table_grad(dbias, buckets):
    def body(d_ref, b_ref, o_ref):
        bk = b_ref[...]
        row = lax.broadcasted_iota(jnp.int32, (N_BUCKETS, LANE), 0)
        lane = lax.broadcasted_iota(jnp.int32, (N_BUCKETS, LANE), 1)
        acc = jnp.zeros((N_BUCKETS, LANE), F32)
        for h in range(N_Q_HEADS):
            d = d_ref[h]
            for b in range(N_BUCKETS):
                s = jnp.sum(jnp.where(bk == b, d, 0.0))
                acc = jnp.where((row == b) & (lane == h), s, acc)
        o_ref[...] = acc

    return pl.pallas_call(
        body, name="bias_table_grad", out_shape=jax.ShapeDtypeStruct((N_BUCKETS, LANE), F32),
        in_specs=[pl.BlockSpec(memory_space=pltpu.VMEM), pl.BlockSpec(memory_space=pltpu.VMEM)],
        out_specs=pl.BlockSpec(memory_space=pltpu.VMEM),
    )(dbias, buckets)


def _ada_forward(sc_all, w_ada):
    d, n = w_ada.shape
    tn = _pick_tile(n, COL_TILE)

    def body(sc_ref, w_ref, o_ref):
        row = lax.broadcasted_iota(jnp.int32, (N_DEV, d), 0)
        sc = jnp.zeros((N_DEV, d), F32)
        for k in range(N_DEV):
            sc = jnp.where(row == k, sc_ref[k], sc)
        o_ref[...] = jnp.dot(sc, w_ref[...], preferred_element_type=F32, precision=lax.Precision.HIGHEST)

    return pl.pallas_call(
        body, name="ada_forward", grid=(n // tn,), out_shape=jax.ShapeDtypeStruct((N_DEV, n), F32),
        in_specs=[_const_spec((N_DEV, 1, d)), pl.BlockSpec((d, tn), lambda j: (0, j))],
        out_specs=pl.BlockSpec((N_DEV, tn), lambda j: (0, j)),
        compiler_params=_cparams(("parallel",), 2 * d * tn * 4 + N_DEV * N_DEV * d * 8),
    )(sc_all, w_ada)


def _mod_finish(m_all, b_ada):
    _, _, n = m_all.shape

    def body(m_ref, b_ref, o_ref):
        x, y, c = _place()
        me = 4 * x + 2 * y + c
        row = lax.broadcasted_iota(jnp.int32, (N_DEV, n), 0)
        for j in range(N_CHIPS):
            blk = m_ref[2 * j]
            mine = jnp.sum(jnp.where(row == me, blk, 0.0), axis=0, keepdims=True)
            o_ref[:, j * n:(j + 1) * n] = mine + b_ref[:, j * n:(j + 1) * n]

    return pl.pallas_call(
        body, name="mod_finish", out_shape=jax.ShapeDtypeStruct((1, N_CHIPS * n), F32),
        in_specs=[pl.BlockSpec(memory_space=pltpu.VMEM), pl.BlockSpec(memory_space=pltpu.VMEM)],
        out_specs=pl.BlockSpec(memory_space=pltpu.VMEM),
    )(m_all, b_ada)


def _ada_backward(sc_t, dmod_cols, w, m, v):
    d, n = w.shape
    tr, tn = _row_tile(d, 512), _pick_tile(n, COL_TILE)

    def body(s_ref, dm_ref, w_ref, m_ref, v_ref, g_ref, d_ref, nm_ref, nv_ref):
        g = jnp.dot(s_ref[...], dm_ref[...], preferred_element_type=F32, precision=lax.Precision.HIGHEST)
        dl, nm, nv = _adam_math(w_ref[...], g, m_ref[...], v_ref[...])
        g_ref[...] = g
        d_ref[...] = dl
        nm_ref[...] = nm
        nv_ref[...] = nv

    tile = pl.BlockSpec((tr, tn), lambda i, j: (i, j))
    out = jax.ShapeDtypeStruct((d, n), F32)
    return pl.pallas_call(
        body, name="ada_backward", grid=(d // tr, n // tn), out_shape=(out,) * 4,
        in_specs=[pl.BlockSpec((tr, N_DEV), lambda i, j: (i, 0)), pl.BlockSpec((N_DEV, tn), lambda i, j: (0, j)),
                  tile, tile, tile],
        out_specs=(tile,) * 4,
        compiler_params=_cparams(("parallel", "parallel"), 2 * 8 * tr * tn * 4),
    )(sc_t, dmod_cols, w, m, v)


def _prenorm(x, mod, pre_g):
    s, d = x.shape
    tm = _row_tile(s, 512)

    def body(x_ref, mod_ref, g_ref, h_ref):
        xv = x_ref[...]
        r = lax.rsqrt(jnp.mean(xv * xv, axis=1, keepdims=True) + EPS)
        xn = xv * r * g_ref[...]
        h_ref[...] = (xn * (1.0 + mod_ref[:, d:2 * d]) + mod_ref[:, 0:d]).astype(BF16)

    return pl.pallas_call(
        body, name="prenorm", grid=(s // tm,), out_shape=jax.ShapeDtypeStruct((s, d), BF16),
        in_specs=[pl.BlockSpec((tm, d), lambda i: (i, 0)), _const_spec((1, 3 * d)), _const_spec((1, d))],
        out_specs=pl.BlockSpec((tm, d), lambda i: (i, 0)),
        compiler_params=_cparams(("parallel",), 2 * tm * d * 6 + 4 * tm * d * 4),
    )(x, mod, pre_g)


def _matmul(name, a, b, bias=None, comm=None):
    s, k = a.shape
    _, n = b.shape
    tm, tn = _row_tile(s, 1024), COL_TILE

    def body(*refs):
        if bias is None:
            a_ref, b_ref, o_ref = refs
        else:
            a_ref, b_ref, bias_ref, o_ref = refs
        acc = jnp.dot(a_ref[...], b_ref[...], preferred_element_type=F32)
        if bias is not None:
            acc = _sigmoid(acc + bias_ref[...])
        o_ref[...] = acc.astype(BF16)

    in_specs = [pl.BlockSpec((tm, k), lambda i, j: (i, 0)), pl.BlockSpec((k, tn), lambda i, j: (0, j))]
    args = [a, b]
    if bias is not None:
        in_specs.append(pl.BlockSpec((1, tn), lambda i, j: (0, j)))
        args.append(bias)
    (out,), extra = _call(name, body, (s // tm, n // tn), in_specs, args, [jax.ShapeDtypeStruct((s, n), BF16)],
                          [pl.BlockSpec((tm, tn), lambda i, j: (i, j))], [], ("parallel", "arbitrary"),
                          2 * (tm * k + k * tn + tm * tn) * 2 + 2 * tm * tn * 4, comm=comm)
    return out, extra


def _col_specs(off, width, rows, row_index):
    assert off % COL_TILE == 0 and width % COL_TILE == 0
    return [pl.BlockSpec((rows, COL_TILE), functools.partial(lambda p, *ids: (row_index(*ids), p), off // COL_TILE + p))
            for p in range(width // COL_TILE)]


def _cat(refs):
    vals = [r[...] for r in refs]
    return vals[0] if len(vals) == 1 else jnp.concatenate(vals, axis=1)


def _attn_mask(n, s):
    rows = GQA_GROUP * BLOCK
    qi = lax.broadcasted_iota(jnp.int32, (rows, SPAN), 0) & (BLOCK - 1)
    t = lax.broadcasted_iota(jnp.int32, (rows, SPAN), 1)
    rel = t - WINDOW - qi
    kpos = (n - 1) * BLOCK + t
    return (jnp.abs(rel) <= WINDOW) & (kpos >= 0) & (kpos < s)


def _sink_column(sink_ref, kh):
    rows = GQA_GROUP * BLOCK
    grp = lax.broadcasted_iota(jnp.int32, (rows, 1), 0) // BLOCK
    col = jnp.zeros((rows, 1), F32)
    for g in range(GQA_GROUP):
        col = jnp.where(grp == g, sink_ref[0, kh * GQA_GROUP + g], col)
    return col


def _stack_heads(x, kh):
    base = kh * GQA_GROUP * HEAD_DIM
    return jnp.concatenate([x[:, base + g * HEAD_DIM: base + (g + 1) * HEAD_DIM] for g in range(GQA_GROUP)], axis=0)


def _softmax_parts(qs, k, bias, valid, sink_col):
    sc = lax.dot_general(qs, k, (((1,), (1,)), ((), ())), preferred_element_type=F32)
    sc = sc * (HEAD_DIM ** -0.5) + bias
    sc = jnp.where(valid, sc, NEG_INF)
    mx = jnp.maximum(jnp.max(sc, axis=1, keepdims=True), sink_col)
    e = jnp.exp(sc - mx)
    es = jnp.exp(sink_col - mx)
    inv = 1.0 / (jnp.sum(e, axis=1, keepdims=True) + es)
    return e * inv, es * inv


def _attn_forward(proj, bias2, sink, s):
    nblk = s // BLOCK
    nq = ATTN_WIDTH // COL_TILE
    kv_col = ATTN_WIDTH // COL_TILE
    assert 2 * KV_WIDTH == COL_TILE

    def body(*refs):
        q_refs = refs[:nq]
        kvp, kvc, kvn, bias_ref, sink_ref, o_ref = refs[nq:]
        n = pl.program_id(0)
        q = _cat(q_refs)
        kv = jnp.concatenate([kvp[...], kvc[...], kvn[...]], axis=0)
        valid = _attn_mask(n, s)
        for kh in range(N_KV_HEADS):
            qs = _stack_heads(q, kh)
            k = kv[:, kh * HEAD_DIM:(kh + 1) * HEAD_DIM]
            v = kv[:, KV_WIDTH + kh * HEAD_DIM: KV_WIDTH + (kh + 1) * HEAD_DIM]
            p, _ = _softmax_parts(qs, k, bias_ref[kh], valid, _sink_column(sink_ref, kh))
            o = jnp.dot(p.astype(BF16), v, preferred_element_type=F32)
            for g in range(GQA_GROUP):
                h = kh * GQA_GROUP + g
                o_ref[:, h * HEAD_DIM:(h + 1) * HEAD_DIM] = o[g * BLOCK:(g + 1) * BLOCK].astype(BF16)

    in_specs = _col_specs(0, ATTN_WIDTH, BLOCK, lambda n: n)
    in_specs += [pl.BlockSpec((BLOCK, COL_TILE), lambda n: (jnp.maximum(n - 1, 0), kv_col)),
                 pl.BlockSpec((BLOCK, COL_TILE), lambda n: (n, kv_col)),
                 pl.BlockSpec((BLOCK, COL_TILE), lambda n: (jnp.minimum(n + 1, nblk - 1), kv_col)),
                 _const_spec((N_KV_HEADS, GQA_GROUP * BLOCK, SPAN)),
                 pl.BlockSpec(memory_space=pltpu.SMEM)]
    return pl.pallas_call(
        body, name="attn_forward", grid=(nblk,), out_shape=jax.ShapeDtypeStruct((s, ATTN_WIDTH), BF16),
        in_specs=in_specs, out_specs=pl.BlockSpec((BLOCK, ATTN_WIDTH), lambda n: (n, 0)),
        compiler_params=_cparams(("parallel",), 16 << 20),
    )(*([proj] * (nq + 3)), bias2, sink)


def _pool_positions(i, tm, s, width):
    pos = i * tm - HALO + lax.broadcasted_iota(jnp.int32, (tm + 2 * HALO, width), 0)
    return pos, (pos >= 0) & (pos < s)


def _pool_count(pos, w, s):
    return (jnp.minimum(pos + w // 2, s) - jnp.maximum(pos - w // 2, 0)).astype(F32)


def _halo_specs_cols(off, width, tm, s):
    per = tm // HALO
    last = s // HALO - 1
    prev = _col_specs(off, width, HALO, lambda i: jnp.maximum(i * per - 1, 0))
    nxt = _col_specs(off, width, HALO, lambda i: jnp.minimum((i + 1) * per, last))
    return prev, nxt


def _branches(proj, attn, g, w_bra, w_brp, w_grp, pool_scale, s, d):
    a_w, p_w = ATTN_WIDTH, pool_scale.shape[1]
    cg = p_w // N_POOL_GROUPS
    tm = _row_tile(s, ROW_TILE)
    off_ga = ATTN_WIDTH + 2 * KV_WIDTH
    off_u = off_ga + a_w
    off_gp = off_u + p_w
    n_ga, n_u, n_gp = a_w // COL_TILE, p_w // COL_TILE, p_w // COL_TILE

    def body(*refs):
        it = iter(refs)
        attn_ref = next(it)
        ga_refs = [next(it) for _ in range(n_ga)]
        u_refs = [next(it) for _ in range(n_u)]
        up_refs = [next(it) for _ in range(n_u)]
        un_refs = [next(it) for _ in range(n_u)]
        gp_refs = [next(it) for _ in range(n_gp)]
        g_ref, wa_ref, wp_ref, wg_ref, ps_ref = (next(it) for _ in range(5))
        ya_ref, yp_ref, za_ref, zp_ref, mg_ref, pooled_ref, mixed_ref = (next(it) for _ in range(7))
        i = pl.program_id(0)
        ya = (attn_ref[...].astype(F32) * _silu(_cat(ga_refs).astype(F32))).astype(BF16)
        ya_ref[...] = ya
        za = jnp.dot(ya, wa_ref[...], preferred_element_type=F32)
        za_ref[...] = za.astype(BF16)

        u = _cat(u_refs).astype(F32)
        ext = jnp.concatenate([_cat(up_refs).astype(F32), u, _cat(un_refs).astype(F32)], axis=0)
        pos, ok = _pool_positions(i, tm, s, cg)
        mixed = []
        for gi, w in enumerate(POOL_SIZES):
            e = jnp.where(ok, ext[:, gi * cg:(gi + 1) * cg], 0.0)
            acc = e[HALO - w // 2: HALO - w // 2 + tm]
            for dd in range(-w // 2 + 1, w // 2):
                acc = acc + e[HALO + dd: HALO + dd + tm]
            cnt = _pool_count(pos[HALO:HALO + tm], w, s)
            pooled = (acc / cnt - u[:, gi * cg:(gi + 1) * cg]).astype(BF16)
            pooled_ref[:, gi * cg:(gi + 1) * cg] = pooled
            mixed.append(jnp.dot(pooled, wg_ref[gi], preferred_element_type=F32))
        mixed = jnp.concatenate(mixed, axis=1)
        mixed_ref[...] = mixed.astype(BF16)
        yp = (mixed * ps_ref[...] * _silu(_cat(gp_refs).astype(F32))).astype(BF16)
        yp_ref[...] = yp
        zp = jnp.dot(yp, wp_ref[...], preferred_element_type=F32)
        zp_ref[...] = zp.astype(BF16)
        gate = g_ref[...].astype(F32)
        mg_ref[...] = (gate[:, :d] * za + gate[:, d:] * zp).astype(BF16)

    row = lambda i: i
    u_prev, u_next = _halo_specs_cols(off_u, p_w, tm, s)
    in_specs = [pl.BlockSpec((tm, a_w), lambda i: (i, 0))]
    in_specs += _col_specs(off_ga, a_w, tm, row) + _col_specs(off_u, p_w, tm, row) + u_prev + u_next
    in_specs += _col_specs(off_gp, p_w, tm, row)
    in_specs += [pl.BlockSpec((tm, 2 * d), lambda i: (i, 0)), _resident((a_w, d)), _resident((p_w, d)),
                 _resident((N_POOL_GROUPS, cg, cg)), _const_spec((1, p_w))]
    n_proj = n_ga + 3 * n_u + n_gp
    tile = lambda w: pl.BlockSpec((tm, w), lambda i: (i, 0))
    out_widths = (a_w, p_w, d, d, d, p_w, p_w)
    est = 2 * tm * (a_w + a_w + 2 * p_w + 2 * d + sum(out_widths)) * 2 + (a_w + p_w) * d * 2 + 6 * tm * d * 4
    return pl.pallas_call(
        body, name="branches", grid=(s // tm,),
        out_shape=tuple(jax.ShapeDtypeStruct((s, w), BF16) for w in out_widths),
        in_specs=in_specs, out_specs=tuple(tile(w) for w in out_widths),
        compiler_params=_cparams(("parallel",), est),
    )(attn, *([proj] * n_proj), g, w_bra, w_brp, w_grp, pool_scale)


def _out_loss(merged, x, target, w_out, post_g, mod, s, d):
    tm = _row_tile(s, ROW_TILE)
    nsteps = s // tm

    def body(mg_ref, x_ref, t_ref, w_ref, pg_ref, mod_ref, dout_ref, do_ref, loss_ref, dgate_ref, dpg_ref, lacc):
        i = pl.program_id(0)

        @pl.when(i == 0)
        def _():
            lacc[...] = jnp.zeros_like(lacc)
            dgate_ref[...] = jnp.zeros_like(dgate_ref)
            dpg_ref[...] = jnp.zeros_like(dpg_ref)

        o = jnp.dot(mg_ref[...], w_ref[...], preferred_element_type=F32)
        r = lax.rsqrt(jnp.mean(o * o, axis=1, keepdims=True) + EPS)
        ohat = o * r
        pg = pg_ref[...]
        gate = mod_ref[:, 2 * d:3 * d]
        y = ohat * pg
        e = x_ref[...] + gate * y - t_ref[...]
        lacc[...] += jnp.sum(e * e, axis=0, keepdims=True)
        dout = e * (1.0 / d)
        dout_ref[...] = dout
        dgate_ref[...] += jnp.sum(dout * y, axis=0, keepdims=True)
        dy = dout * gate
        dpg_ref[...] += jnp.sum(dy * ohat, axis=0, keepdims=True)
        dohat = dy * pg
        do = r * (dohat - ohat * jnp.mean(dohat * ohat, axis=1, keepdims=True))
        do_ref[...] = do.astype(BF16)

        @pl.when(i == nsteps - 1)
        def _():
            loss_ref[...] = (0.5 / d) * jnp.sum(lacc[...], axis=1, keepdims=True)

    tile = pl.BlockSpec((tm, d), lambda i: (i, 0))
    vec = _const_spec((1, d))
    return pl.pallas_call(
        body, name="out_loss", grid=(nsteps,),
        out_shape=(jax.ShapeDtypeStruct((s, d), F32), jax.ShapeDtypeStruct((s, d), BF16),
                   jax.ShapeDtypeStruct((1, 1), F32), jax.ShapeDtypeStruct((1, d), F32),
                   jax.ShapeDtypeStruct((1, d), F32)),
        in_specs=[tile, tile, tile, _resident((d, d)), vec, _const_spec((1, 3 * d))],
        out_specs=(tile, tile, _const_spec((1, 1)), vec, vec),
        scratch_shapes=[pltpu.VMEM((1, d), F32)],
        compiler_params=_cparams(("arbitrary",), 2 * tm * d * (2 + 4 + 4 + 4 + 2) + d * d * 2 + 8 * tm * d * 4),
    )(merged, x, target, w_out, post_g, mod)


def _bwd_out(d_o, g, za, zp, w_out, s, d, comm=None):
    tm = _row_tile(s, ROW_TILE)

    def body(do_ref, g_ref, za_ref, zp_ref, w_ref, dza_ref, dzp_ref, dgl_ref, dbm_ref):
        i = pl.program_id(0)

        @pl.when(i == 0)
        def _():
            dbm_ref[...] = jnp.zeros_like(dbm_ref)

        dm = lax.dot_general(do_ref[...], w_ref[...], (((1,), (1,)), ((), ())), preferred_element_type=F32)
        gate = g_ref[...].astype(F32)
        ga, gp = gate[:, :d], gate[:, d:]
        dza_ref[...] = (dm * ga).astype(BF16)
        dzp_ref[...] = (dm * gp).astype(BF16)
        dla = dm * za_ref[...].astype(F32) * ga * (1.0 - ga)
        dlp = dm * zp_ref[...].astype(F32) * gp * (1.0 - gp)
        dgl_ref[:, :d] = dla.astype(BF16)
        dgl_ref[:, d:] = dlp.astype(BF16)
        dbm_ref[:, :d] += jnp.sum(dla, axis=0, keepdims=True)
        dbm_ref[:, d:] += jnp.sum(dlp, axis=0, keepdims=True)

    tile = pl.BlockSpec((tm, d), lambda i: (i, 0))
    wide = pl.BlockSpec((tm, 2 * d), lambda i: (i, 0))
    return _call("bwd_out", body, (s // tm,), [tile, wide, tile, tile, _resident((d, d))], (d_o, g, za, zp, w_out),
                 (jax.ShapeDtypeStruct((s, d), BF16), jax.ShapeDtypeStruct((s, d), BF16),
                  jax.ShapeDtypeStruct((s, 2 * d), BF16), jax.ShapeDtypeStruct((1, 2 * d), F32)),
                 (tile, tile, wide, _const_spec((1, 2 * d))), [], ("arbitrary",),
                 2 * tm * d * 2 * 9 + d * d * 2 + 8 * tm * d * 4, comm=comm)


def _bwd_branches(dza, dzp, attn, proj, mixed, w_bra, w_brp, w_grp, pool_scale, s, d, comm=None):
    a_w, p_w = ATTN_WIDTH, pool_scale.shape[1]
    cg = p_w // N_POOL_GROUPS
    tm = _row_tile(s, ROW_TILE)
    off_ga = ATTN_WIDTH + 2 * KV_WIDTH
    off_gp = off_ga + a_w + p_w
    n_ga, n_gp = a_w // COL_TILE, p_w // COL_TILE

    def body(*refs):
        it = iter(refs)
        dza_ref, dzp_ref, attn_ref = next(it), next(it), next(it)
        ga_refs = [next(it) for _ in range(n_ga)]
        gp_refs = [next(it) for _ in range(n_gp)]
        mixed_ref, wa_ref, wp_ref, wg_ref, ps_ref = (next(it) for _ in range(5))
        dattn_ref, dga_ref, dgp_ref, dmix_ref, dpool_ref, dps_ref = (next(it) for _ in range(6))
        i = pl.program_id(0)

        @pl.when(i == 0)
        def _():
            dps_ref[...] = jnp.zeros_like(dps_ref)

        dya = lax.dot_general(dza_ref[...], wa_ref[...], (((1,), (1,)), ((), ())), preferred_element_type=F32)
        ga = _cat(ga_refs).astype(F32)
        dattn_ref[...] = (dya * _silu(ga)).astype(BF16)
        dga_ref[...] = (dya * attn_ref[...].astype(F32) * _dsilu(ga)).astype(BF16)

        dyp = lax.dot_general(dzp_ref[...], wp_ref[...], (((1,), (1,)), ((), ())), preferred_element_type=F32)
        gp = _cat(gp_refs).astype(F32)
        mixed = mixed_ref[...].astype(F32)
        ps = ps_ref[...]
        sg = _silu(gp)
        dgp_ref[...] = (dyp * mixed * ps * _dsilu(gp)).astype(BF16)
        dps_ref[...] += jnp.sum(dyp * sg * mixed, axis=0, keepdims=True)
        dmix = (dyp * sg * ps).astype(BF16)
        dmix_ref[...] = dmix
        for gi in range(N_POOL_GROUPS):
            dp = lax.dot_general(dmix[:, gi * cg:(gi + 1) * cg], wg_ref[gi], (((1,), (1,)), ((), ())),
                                 preferred_element_type=F32)
            dpool_ref[:, gi * cg:(gi + 1) * cg] = dp.astype(BF16)

    row = lambda i: i
    tile = lambda w: pl.BlockSpec((tm, w), lambda i: (i, 0))
    in_specs = [tile(d), tile(d), tile(a_w)] + _col_specs(off_ga, a_w, tm, row) + _col_specs(off_gp, p_w, tm, row)
    in_specs += [tile(p_w), _resident((a_w, d)), _resident((p_w, d)), _resident((N_POOL_GROUPS, cg, cg)),
                 _const_spec((1, p_w))]
    out_widths = (a_w, a_w, p_w, p_w, p_w)
    est = 2 * tm * (2 * d + 2 * a_w + 2 * p_w + sum(out_widths)) * 2 + (a_w + p_w) * d * 2 + 8 * tm * a_w * 4
    return _call("bwd_branches", body, (s // tm,), in_specs,
                 (dza, dzp, attn, *([proj] * (n_ga + n_gp)), mixed, w_bra, w_brp, w_grp, pool_scale),
                 tuple(jax.ShapeDtypeStruct((s, w), BF16) for w in out_widths) + (jax.ShapeDtypeStruct((1, p_w), F32),),
                 tuple(tile(w) for w in out_widths) + (_const_spec((1, p_w)),), [], ("arbitrary",), est, comm=comm)


def _pool_backward(dpooled, s):
    _, p_w = dpooled.shape
    cg = p_w // N_POOL_GROUPS
    tm = _row_tile(s, ROW_TILE)
    per, last = tm // HALO, s // HALO - 1

    def body(dp_ref, prev_ref, next_ref, du_ref):
        i = pl.program_id(0)
        dp = dp_ref[...].astype(F32)
        ext = jnp.concatenate([prev_ref[...].astype(F32), dp, next_ref[...].astype(F32)], axis=0)
        pos, ok = _pool_positions(i, tm, s, cg)
        for gi, w in enumerate(POOL_SIZES):
            t = jnp.where(ok, ext[:, gi * cg:(gi + 1) * cg], 0.0) / _pool_count(pos, w, s)
            acc = t[HALO - w // 2 + 1: HALO - w // 2 + 1 + tm]
            for dd in range(-w // 2 + 2, w // 2 + 1):
                acc = acc + t[HALO + dd: HALO + dd + tm]
            du_ref[:, gi * cg:(gi + 1) * cg] = (acc - dp[:, gi * cg:(gi + 1) * cg]).astype(BF16)

    return pl.pallas_call(
        body, name="pool_backward", grid=(s // tm,), out_shape=jax.ShapeDtypeStruct((s, p_w), BF16),
        in_specs=[pl.BlockSpec((tm, p_w), lambda i: (i, 0)),
                  pl.BlockSpec((HALO, p_w), lambda i: (jnp.maximum(i * per - 1, 0), 0)),
                  pl.BlockSpec((HALO, p_w), lambda i: (jnp.minimum((i + 1) * per, last), 0))],
        out_specs=pl.BlockSpec((tm, p_w), lambda i: (i, 0)),
        compiler_params=_cparams(("parallel",), 4 * tm * p_w * 2 + 8 * tm * p_w * 4),
    )(dpooled, dpooled, dpooled)


def _attn_backward(proj, dattn, bias2, sink, s):
    nblk = s // BLOCK
    nq = ATTN_WIDTH // COL_TILE
    kv_col = ATTN_WIDTH // COL_TILE
    rows = GQA_GROUP * BLOCK
    scale = HEAD_DIM ** -0.5

    def body(*refs):
        q_refs = refs[:nq]
        kvp, kvc, kvn, do_ref, bias_ref, sink_ref, dq_ref, dkv_ref, dbias_ref, dsink_ref, acc, sacc = refs[nq:]
        n = pl.program_id(0)

        @pl.when(n == 0)
        def _():
            acc[...] = jnp.zeros_like(acc)
            sacc[...] = jnp.zeros_like(sacc)
            dbias_ref[...] = jnp.zeros_like(dbias_ref)
            dsink_ref[...] = jnp.zeros_like(dsink_ref)

        @pl.when(jnp.logical_and(n >= 1, n < nblk))
        def _():
            acc[(n + 1) % 3] = jnp.zeros((BLOCK, 2 * KV_WIDTH), F32)

        @pl.when(n < nblk)
        def _():
            q = _cat(q_refs)
            do = do_ref[...]
            kv = jnp.concatenate([kvp[...], kvc[...], kvn[...]], axis=0)
            valid = _attn_mask(n, s)
            for kh in range(N_KV_HEADS):
                qs = _stack_heads(q, kh)
                dos = _stack_heads(do, kh)
                k = kv[:, kh * HEAD_DIM:(kh + 1) * HEAD_DIM]
                v = kv[:, KV_WIDTH + kh * HEAD_DIM: KV_WIDTH + (kh + 1) * HEAD_DIM]
                p, ps = _softmax_parts(qs, k, bias_ref[kh], valid, _sink_column(sink_ref, kh))
                dp = lax.dot_general(dos, v, (((1,), (1,)), ((), ())), preferred_element_type=F32)
                delta = jnp.sum(p * dp, axis=1, keepdims=True)
                ds = p * (dp - delta)
                dbias_ref[kh] += ds
                sacc[kh] += -ps * delta
                dsb = ds.astype(BF16)
                dq = jnp.dot(dsb, k, preferred_element_type=F32) * scale
                for g in range(GQA_GROUP):
                    h = kh * GQA_GROUP + g
                    dq_ref[:, h * HEAD_DIM:(h + 1) * HEAD_DIM] = dq[g * BLOCK:(g + 1) * BLOCK].astype(BF16)
                dk = lax.dot_general(dsb, qs, (((0,), (0,)), ((), ())), preferred_element_type=F32) * scale
                dv = lax.dot_general(p.astype(BF16), dos, (((0,), (0,)), ((), ())), preferred_element_type=F32)
                for j in range(3):
                    slot = (n + 2 + j) % 3
                    acc[slot, :, kh * HEAD_DIM:(kh + 1) * HEAD_DIM] += dk[j * BLOCK:(j + 1) * BLOCK]
                    acc[slot, :, KV_WIDTH + kh * HEAD_DIM: KV_WIDTH + (kh + 1) * HEAD_DIM] += dv[j * BLOCK:(j + 1) * BLOCK]

        dkv_ref[...] = acc[(n + 2) % 3].astype(BF16)

        @pl.when(n == nblk)
        def _():
            lane = lax.broadcasted_iota(jnp.int32, (1, LANE), 1)
            out = jnp.zeros((1, LANE), F32)
            for kh in range(N_KV_HEADS):
                col = sacc[kh]
                for g in range(GQA_GROUP):
                    out = jnp.where(lane == kh * GQA_GROUP + g, jnp.sum(col[g * BLOCK:(g + 1) * BLOCK]), out)
            dsink_ref[...] = out

    qi = lambda n: jnp.minimum(n, nblk - 1)
    in_specs = _col_specs(0, ATTN_WIDTH, BLOCK, qi)
    in_specs += [pl.BlockSpec((BLOCK, COL_TILE), lambda n: (jnp.maximum(qi(n) - 1, 0), kv_col)),
                 pl.BlockSpec((BLOCK, COL_TILE), lambda n: (qi(n), kv_col)),
                 pl.BlockSpec((BLOCK, COL_TILE), lambda n: (jnp.minimum(qi(n) + 1, nblk - 1), kv_col)),
                 pl.BlockSpec((BLOCK, ATTN_WIDTH), lambda n: (qi(n), 0)),
                 _const_spec((N_KV_HEADS, rows, SPAN)),
                 pl.BlockSpec(memory_space=pltpu.SMEM)]
    return pl.pallas_call(
        body, name="attn_backward", grid=(nblk + 1,),
        out_shape=(jax.ShapeDtypeStruct((s, ATTN_WIDTH), BF16), jax.ShapeDtypeStruct((s, 2 * KV_WIDTH), BF16),
                   jax.ShapeDtypeStruct((N_KV_HEADS, rows, SPAN), F32), jax.ShapeDtypeStruct((1, LANE), F32)),
        in_specs=in_specs,
        out_specs=(pl.BlockSpec((BLOCK, ATTN_WIDTH), lambda n: (qi(n), 0)),
                   pl.BlockSpec((BLOCK, 2 * KV_WIDTH), lambda n: (jnp.clip(n - 1, 0, nblk - 1), 0)),
                   _const_spec((N_KV_HEADS, rows, SPAN)), _const_spec((1, LANE))),
        scratch_shapes=[pltpu.VMEM((3, BLOCK, 2 * KV_WIDTH), F32), pltpu.VMEM((N_KV_HEADS, rows, 1), F32)],
        compiler_params=_cparams(("arbitrary",), 24 << 20),
    )(*([proj] * (nq + 3)), dattn, bias2, sink)


def _pick_tile(n, cap):
    t = cap - cap % LANE
    while n % t:
        t -= LANE
    return t


def _matmul_tn(name, a, b, comm=None):
    s, m = a.shape
    _, n = b.shape
    tk = _row_tile(s, 512)
    tm, tn = _pick_tile(m, 1024), _pick_tile(n, 1152)
    nk = s // tk

    def body(a_ref, b_ref, o_ref, acc):
        k = pl.program_id(2)

        @pl.when(k == 0)
        def _():
            acc[...] = jnp.zeros_like(acc)

        acc[...] += lax.dot_general(a_ref[...], b_ref[...], (((0,), (0,)), ((), ())), preferred_element_type=F32)

        @pl.when(k == nk - 1)
        def _():
            o_ref[...] = acc[...].astype(BF16)

    (out,), extra = _call(
        name, body, (m // tm, n // tn, nk),
        [pl.BlockSpec((tk, tm), lambda i, j, k: (k, i)), pl.BlockSpec((tk, tn), lambda i, j, k: (k, j))], (a, b),
        [jax.ShapeDtypeStruct((m, n), BF16)], [pl.BlockSpec((tm, tn), lambda i, j, k: (i, j))],
        [pltpu.VMEM((tm, tn), F32)], ("parallel", "parallel", "arbitrary"),
        2 * tk * (tm + tn) * 2 + tm * tn * (4 + 4 + 4), comm=comm)
    return out, extra


def _pool_weight_grad(pooled, dmix, s):
    _, p_w = pooled.shape
    cg = p_w // N_POOL_GROUPS
    tk = _row_tile(s, 512)
    nk = s // tk

    def body(a_ref, b_ref, o_ref, acc):
        k = pl.program_id(1)

        @pl.when(k == 0)
        def _():
            acc[...] = jnp.zeros_like(acc)

        acc[...] += lax.dot_general(a_ref[...], b_ref[...], (((0,), (0,)), ((), ())), preferred_element_type=F32)

        @pl.when(k == nk - 1)
        def _():
            o_ref[0] = acc[...].astype(BF16)

    return pl.pallas_call(
        body, name="pool_weight_grad", grid=(N_POOL_GROUPS, nk),
        out_shape=jax.ShapeDtypeStruct((N_POOL_GROUPS, cg, cg), BF16),
        in_specs=[pl.BlockSpec((tk, cg), lambda g, k: (k, g)), pl.BlockSpec((tk, cg), lambda g, k: (k, g))],
        out_specs=pl.BlockSpec((1, cg, cg), lambda g, k: (g, 0, 0)),
        scratch_shapes=[pltpu.VMEM((cg, cg), F32)],
        compiler_params=_cparams(("parallel", "arbitrary"), 16 << 20),
    )(pooled, dmix)


def _bwd_input(dproj, dgl, w_in, w_merge, x, dout, mod, pre_g, s, d, comm=None):
    tm = _row_tile(s, 512)
    n_in = dproj.shape[1] // COL_TILE
    n_mg = dgl.shape[1] // COL_TILE
    nk = n_in + n_mg
    ni = s // tm

    def body(dp_ref, dg_ref, wi_ref, wm_ref, x_ref, dout_ref, mod_ref, pg_ref, gx_ref, dsh_ref, dsc_ref, dpg_ref,
             acc):
        i, k = pl.program_id(0), pl.program_id(1)

        @pl.when(jnp.logical_and(i == 0, k == 0))
        def _():
            dsh_ref[...] = jnp.zeros_like(dsh_ref)
            dsc_ref[...] = jnp.zeros_like(dsc_ref)
            dpg_ref[...] = jnp.zeros_like(dpg_ref)

        @pl.when(k == 0)
        def _():
            acc[...] = jnp.zeros_like(acc)

        @pl.when(k < n_in)
        def _():
            acc[...] += lax.dot_general(dp_ref[...], wi_ref[...], (((1,), (1,)), ((), ())),
                                        preferred_element_type=F32)

        @pl.when(k >= n_in)
        def _():
            acc[...] += lax.dot_general(dg_ref[...], wm_ref[...], (((1,), (1,)), ((), ())),
                                        preferred_element_type=F32)

        @pl.when(k == nk - 1)
        def _():
            dh = acc[...]
            xv = x_ref[...]
            r = lax.rsqrt(jnp.mean(xv * xv, axis=1, keepdims=True) + EPS)
            xhat = xv * r
            pg = pg_ref[...]
            one_scale = 1.0 + mod_ref[:, d:2 * d]
            dsh_ref[...] += jnp.sum(dh, axis=0, keepdims=True)
            dsc_ref[...] += jnp.sum(dh * xhat, axis=0, keepdims=True) * pg
            dpg_ref[...] += jnp.sum(dh * xhat, axis=0, keepdims=True) * one_scale
            dxh = dh * (one_scale * pg)
            dx = r * (dxh - xhat * jnp.mean(dxh * xhat, axis=1, keepdims=True))
            gx_ref[...] = dout_ref[...] + dx

    tile = pl.BlockSpec((tm, d), lambda i, k: (i, 0))
    vec = _const_spec((1, d))
    in_specs = [pl.BlockSpec((tm, COL_TILE), lambda i, k: (i, jnp.minimum(k, n_in - 1))),
                pl.BlockSpec((tm, COL_TILE), lambda i, k: (i, jnp.maximum(k - n_in, 0))),
                pl.BlockSpec((d, COL_TILE), lambda i, k: (0, jnp.minimum(k, n_in - 1))),
                pl.BlockSpec((d, COL_TILE), lambda i, k: (0, jnp.maximum(k - n_in, 0))),
                tile, tile, _const_spec((1, 3 * d)), vec]
    est = 2 * 2 * (tm + d) * COL_TILE * 2 + 2 * 3 * tm * d * 4 + tm * d * 4 + 8 * tm * d * 4
    return _call("bwd_input", body, (ni, nk), in_specs, (dproj, dgl, w_in, w_merge, x, dout, mod, pre_g),
                 (jax.ShapeDtypeStruct((s, d), F32),) + (jax.ShapeDtypeStruct((1, d), F32),) * 3,
                 (tile, vec, vec, vec), [pltpu.VMEM((tm, d), F32)], ("arbitrary", "arbitrary"), est, comm=comm)


def _pad_lanes(v, width):
    return jnp.pad(v, ((0, 0), (0, width - v.shape[1])))


def kernel(x, c, rel_bias_table, w_ada, b_ada, pre_norm_g, post_norm_g, w_in, attn_sink, w_pool_group, pool_scale, w_branch_attn, w_branch_pool, w_merge, b_merge, w_out, loss_target, m_rel_bias_table, m_w_ada, m_b_ada, m_pre_norm_g, m_post_norm_g, m_w_in, m_attn_sink, m_w_pool_group, m_pool_scale, m_w_branch_attn, m_w_branch_pool, m_w_merge, m_b_merge, m_w_out, v_rel_bias_table, v_w_ada, v_b_ada, v_pre_norm_g, v_post_norm_g, v_w_in, v_attn_sink, v_w_pool_group, v_pool_scale, v_w_branch_attn, v_w_branch_pool, v_w_merge, v_b_merge, v_w_out):
    _, s, d = x.shape
    p_w = pool_scale.shape[-1]
    cg = p_w // N_POOL_GROUPS
    in_w = 2 * ATTN_WIDTH + 2 * KV_WIDTH + 2 * p_w
    x2, t2 = x[0], loss_target[0]
    chip = 2 * lax.axis_index("x") + lax.axis_index("y")

    specs = [_Sharded("col", (d, in_w)), _Sharded("col", (d, 2 * d)), _Sharded("col", (ATTN_WIDTH, d)),
             _Sharded("col", (p_w, d)), _Sharded("row", (d, d)), _Sharded("grp", (N_POOL_GROUPS, cg, cg))]
    shards32 = [w_in[0], w_merge[0], w_branch_attn[0], w_branch_pool[0], w_out[0],
                w_pool_group[0].reshape(N_POOL_GROUPS * cg // N_CHIPS, cg)]
    names = ["w_in", "w_merge", "w_branch_attn", "w_branch_pool", "w_out", "w_pool_group"]
    shards16 = [_cast_bf16("cast_" + nm, w) for nm, w in zip(names, shards32)]
    shards16[5] = shards16[5].reshape(N_POOL_GROUPS, cg // N_CHIPS, cg)
    (wf_in,) = _comm_only("gather_w_in", _GatherWeights(specs[:1], shards16[:1]))

    sc_all = _all_gather8("gather_cond", c, 1, pre=_silu)
    m_all = _all_gather8("gather_mod", _ada_forward(sc_all, w_ada[0]), N_DEV)
    mod = _mod_finish(m_all, b_ada)

    h = _prenorm(x2, mod, pre_norm_g)
    proj, (wf_merge,) = _matmul("proj", h, wf_in, comm=_GatherWeights(specs[1:2], shards16[1:2]))
    gates, (wf_bra, wf_brp, wf_out, wf_grp) = _matmul("merge_gates", h, wf_merge, bias=b_merge,
                                                      comm=_GatherWeights(specs[2:], shards16[2:]))
    buckets = _t5_buckets()
    bias2 = _bias_table(rel_bias_table, buckets).reshape(N_KV_HEADS, GQA_GROUP * BLOCK, SPAN)
    attn = _attn_forward(proj, bias2, attn_sink, s)
    ya, yp, za, zp, merged, pooled, mixed = _branches(proj, attn, gates, wf_bra, wf_brp, wf_grp, pool_scale, s, d)
    dout, d_o, loss_part, dgate, dpostg = _out_loss(merged, x2, t2, wf_out, post_norm_g, mod, s, d)

    pw_out, _ = _matmul_tn("grad_w_out", merged, d_o)
    (dza, dzp, dgl, dbm), (pc_out,) = _bwd_out(d_o, gates, za, zp, wf_out, s, d,
                                                comm=_ScatterGrads(specs[4:5], [pw_out]))
    pw_bra, _ = _matmul_tn("grad_w_branch_attn", ya, dza)
    pw_brp, _ = _matmul_tn("grad_w_branch_pool", yp, dzp)
    pw_merge, _ = _matmul_tn("grad_w_merge", h, dgl)
    (dattn, dga, dgp, dmix, dpooled, dps), (pc_bra, pc_brp) = _bwd_branches(
        dza, dzp, attn, proj, mixed, wf_bra, wf_brp, wf_grp, pool_scale, s, d,
        comm=_ScatterGrads(specs[2:4], [pw_bra, pw_brp]))
    pw_grp = _pool_weight_grad(pooled, dmix, s)
    du = _pool_backward(dpooled, s)
    dq, dkv, dbias, dsink = _attn_backward(proj, dattn, bias2, attn_sink, s)
    dproj = jnp.concatenate([dq, dkv, dga, du, dgp], axis=1)
    pw_in, (pc_merge,) = _matmul_tn("grad_w_in", h, dproj, comm=_ScatterGrads(specs[1:2], [pw_merge]))
    (gx, dshift, dscale, dpreg), (pc_in, pc_grp) = _bwd_input(
        dproj, dgl, wf_in, wf_merge, x2, dout, mod, pre_norm_g, s, d,
        comm=_ScatterGrads([specs[0], specs[5]], [pw_in, pw_grp]))

    pieces = [pc_in, pc_merge, pc_bra, pc_brp, pc_out, pc_grp]
    halves = []
    for nm, sp, pc in zip(names, specs, pieces):
        flat = pc.reshape(N_DEV, -1, pc.shape[-1])
        halves.append(_reduce8("reduce_" + nm, flat).reshape(sp.piece_shape))
    grads = list(_exchange_halves(specs, halves))
    weights = [w_in, w_merge, w_branch_attn, w_branch_pool, w_out, w_pool_group]
    moms = [m_w_in, m_w_merge, m_w_branch_attn, m_w_branch_pool, m_w_out, m_w_pool_group]
    vars_ = [v_w_in, v_w_merge, v_w_branch_attn, v_w_branch_pool, v_w_out, v_w_pool_group]
    big = {}
    for nm, gr, w, m, v in zip(names, grads, weights, moms, vars_):
        shape2 = (-1, w.shape[-1])
        dl, nm_, nv_ = _adamw("adamw_" + nm, w.reshape(shape2), gr.reshape(shape2), m.reshape(shape2),
                              v.reshape(shape2))
        big[nm] = tuple(a.reshape(w.shape) for a in (gr, dl, nm_, nv_))

    dtable = _bias_table_grad(dbias.reshape(N_Q_HEADS, BLOCK, SPAN), buckets)[:, :N_Q_HEADS]
    segs = [("b_ada", jnp.concatenate([dshift, dscale, dgate], axis=1), 3 * d),
            ("pre_norm_g", dpreg, d), ("post_norm_g", dpostg, d), ("attn_sink", dsink, LANE),
            ("pool_scale", dps, p_w), ("b_merge", dbm, 2 * d), ("rel_bias_table", dtable.reshape(1, -1), 2 * LANE)]
    packed = jnp.concatenate([_pad_lanes(v, w) for _, v, w in segs], axis=1)
    rows = _all_gather8("gather_small", packed, 1)[:, 0, :]

    def pack(vals):
        return jnp.concatenate([_pad_lanes(v.reshape(1, -1), w) for v, (_, _, w) in zip(vals, segs)], axis=1)

    small_w = [b_ada, pre_norm_g, post_norm_g, attn_sink, pool_scale, b_merge, rel_bias_table]
    small_m = [m_b_ada, m_pre_norm_g, m_post_norm_g, m_attn_sink, m_pool_scale, m_b_merge, m_rel_bias_table]
    small_v = [v_b_ada, v_pre_norm_g, v_post_norm_g, v_attn_sink, v_pool_scale, v_b_merge, v_rel_bias_table]
    g_small, d_small, nm_small, nv_small = _small_update(rows, pack(small_w), pack(small_m), pack(small_v))
    small = {}
    off = 0
    for (nm, _, w), ref in zip(segs, small_w):
        cut = lambda a: a[:, off:off + ref.size].reshape(ref.shape)
        small[nm] = (cut(g_small), cut(d_small), cut(nm_small), cut(nv_small))
        off += w

    dmod_cols = lax.dynamic_slice_in_dim(rows[:, :3 * d], chip * (3 * d // N_CHIPS), 3 * d // N_CHIPS, axis=1)
    sc_t = sc_all[:, 0, :].T
    g_ada, d_ada, nm_ada, nv_ada = _ada_backward(sc_t, dmod_cols, w_ada[0], m_w_ada[0], v_w_ada[0])
    big["w_ada"] = tuple(a.reshape(w_ada.shape) for a in (g_ada, d_ada, nm_ada, nv_ada))

    loss = lax.psum(loss_part[0, 0], ("x", "y", "c"))
    order = ["rel_bias_table", "w_ada", "b_ada", "pre_norm_g", "post_norm_g", "w_in", "attn_sink", "w_pool_group",
             "pool_scale", "w_branch_attn", "w_branch_pool", "w_merge", "b_merge", "w_out"]
    res = {**big, **small}
    outs = [loss, gx.reshape(x.shape)]
    for part in range(4):
        outs += [res[nm][part] for nm in order]
    return tuple(outs)


def _small_update(rows, w, m, v):
    _, n = rows.shape

    def body(r_ref, w_ref, m_ref, v_ref, g_ref, d_ref, nm_ref, nv_ref):
        g = r_ref[0:1, :]
        for k in range(1, N_DEV):
            g = g + r_ref[k:k + 1, :]
        dl, nm, nv = _adam_math(w_ref[...], g, m_ref[...], v_ref[...])
        g_ref[...] = g
        d_ref[...] = dl
        nm_ref[...] = nm
        nv_ref[...] = nv

    vm = pl.BlockSpec(memory_space=pltpu.VMEM)
    out = jax.ShapeDtypeStruct((1, n), F32)
    return pl.pallas_call(
        body, name="small_update", out_shape=(out,) * 4, in_specs=[vm] * 4, out_specs=(vm,) * 4,
    )(rows, w, m, v)
```

```python
import functools
import math

import numpy as np
import jax
import jax.numpy as jnp
from jax import lax
from jax.experimental import pallas as pl
from jax.experimental.pallas import tpu as pltpu

F32 = jnp.float32
BF16 = jnp.bfloat16
MESH = pl.DeviceIdType.MESH

HEAD_DIM = 128
N_Q_HEADS = 8
N_KV_HEADS = 2
GQA_GROUP = N_Q_HEADS // N_KV_HEADS
ATTN_WIDTH = N_Q_HEADS * HEAD_DIM
KV_WIDTH = N_KV_HEADS * HEAD_DIM
WINDOW = 128
BLOCK = 128
SPAN = BLOCK + 2 * WINDOW
N_BUCKETS = 32
MAX_DISTANCE = 128
POOL_SIZES = (2, 4, 8, 16)
N_POOL_GROUPS = len(POOL_SIZES)
HALO = 16
EPS = 1e-6
NEG_INF = -1e30
ADAM_LR = 0.001
ADAM_B1 = 0.9
ADAM_B2 = 0.999
ADAM_EPS = 1e-08
ADAM_WD = 0.01
ADAM_STEP = 10

N_DEV = 8
N_CHIPS = 4
LANE = 128
COL_TILE = 512
VMEM_CAP = 60000 * 1024
ROW_TILE = 256


def _cparams(sem, est_bytes):
    limit = int(min(max(est_bytes * 5 // 4 + (4 << 20), 16 << 20), VMEM_CAP))
    return pltpu.CompilerParams(dimension_semantics=sem, vmem_limit_bytes=limit)


def _sigmoid(x):
    return jax.nn.sigmoid(x)


def _silu(x):
    return x * _sigmoid(x)


def _dsilu(x):
    s = _sigmoid(x)
    return s * (1.0 + x * (1.0 - s))


def _place():
    x, y, c = lax.axis_index("x"), lax.axis_index("y"), lax.axis_index("c")
    return x, y, c


def _flip(v, bit):
    return (1 - v) if bit else v


def _xor_peer(k):
    x, y, c = _place()
    return (_flip(x, (k >> 2) & 1), _flip(y, (k >> 1) & 1), _flip(c, k & 1))


def _resident(shape):
    nd = len(shape)
    return pl.BlockSpec(shape, lambda *_: (0,) * nd, pipeline_mode=pl.Buffered(1))


def _const_spec(shape):
    nd = len(shape)
    return pl.BlockSpec(shape, lambda *_: (0,) * nd)


CHUNK_BYTES = 256 << 10
MAX_CHUNKS = 32


def _all_gather8(name, x, nrows, pre=None):
    r, n = x.shape

    def body(x_ref, out_ref, stage, send_sems, recv_sems):
        px, py, pc = _place()
        me = 4 * px + 2 * py + pc
        v = x_ref[...]
        if pre is not None:
            v = pre(v)
        stage[...] = v[0:nrows]
        out_ref[me] = v[0:nrows]
        copies = []
        for k in range(1, N_DEV):
            cp = pltpu.make_async_remote_copy(
                src_ref=stage, dst_ref=out_ref.at[me], send_sem=send_sems.at[k - 1], recv_sem=recv_sems.at[k - 1],
                device_id=_xor_peer(k), device_id_type=MESH)
            cp.start()
            copies.append(cp)
        for cp in copies:
            cp.wait()

    return pl.pallas_call(
        body, name=name,
        out_shape=jax.ShapeDtypeStruct((N_DEV, nrows, n), F32),
        in_specs=[pl.BlockSpec(memory_space=pltpu.VMEM)],
        out_specs=pl.BlockSpec(memory_space=pltpu.VMEM),
        scratch_shapes=[pltpu.VMEM((nrows, n), F32), pltpu.SemaphoreType.DMA((N_DEV - 1,)),
                        pltpu.SemaphoreType.DMA((N_DEV - 1,))],
    )(x)


class _Sharded:
    def __init__(self, kind, full_shape):
        self.kind = kind
        self.full_shape = tuple(full_shape)
        if kind == "col":
            r, c = full_shape
            self.shard_shape = (r, c // N_CHIPS)
        elif kind == "row":
            r, c = full_shape
            self.shard_shape = (r // N_CHIPS, c)
        else:
            g, r, c = full_shape
            self.shard_shape = (g, r // N_CHIPS, c)
        self.axis = 1 if kind == "grp" else 0
        s = list(self.shard_shape)
        s[self.axis] //= 2
        self.piece_shape = tuple(s)

    def _rows(self, ref, start, size):
        idx = (slice(None),) * self.axis + (pl.ds(pl.multiple_of(start, 16), size),)
        return ref.at[idx]

    def shard_half(self, ref, hc):
        h = self.piece_shape[self.axis]
        return self._rows(ref, hc * h, h)

    def window(self, ref, chip, hc=None):
        s = self.shard_shape
        if self.kind == "col":
            cols = pl.ds(pl.multiple_of(chip * s[1], LANE), s[1])
            if hc is None:
                return ref.at[:, cols]
            h = s[0] // 2
            return ref.at[pl.ds(pl.multiple_of(hc * h, 16), h), cols]
        n = s[self.axis]
        if hc is None:
            return self._rows(ref, chip * n, n)
        return self._rows(ref, chip * n + hc * (n // 2), n // 2)

    def chunks(self, view, shape, itemsize):
        rows = shape[self.axis]
        nbytes = math.prod(shape) * itemsize
        n = 1
        while 2 * n <= MAX_CHUNKS and nbytes // (2 * n) >= CHUNK_BYTES and rows % (2 * n * 16) == 0:
            n *= 2
        h = rows // n
        return [view.at[(slice(None),) * self.axis + (pl.ds(j * h, h),)] for j in range(n)]


def _remote(src, dst, send_sem, recv_sem, to):
    return pltpu.make_async_remote_copy(src_ref=src, dst_ref=dst, send_sem=send_sem, recv_sem=recv_sem,
                                        device_id=to, device_id_type=MESH)


def _start_remote(sp, src, dst, shape, itemsize, send_sem, recv_sem, to):
    for s_part, d_part in zip(sp.chunks(src, shape, itemsize), sp.chunks(dst, shape, itemsize)):
        _remote(s_part, d_part, send_sem, recv_sem, to).start()


def _start_local(sp, src, dst, shape, itemsize, sem):
    for s_part, d_part in zip(sp.chunks(src, shape, itemsize), sp.chunks(dst, shape, itemsize)):
        pltpu.make_async_copy(s_part, d_part, sem).start()


class _GatherWeights:
    def __init__(self, specs, shards):
        self.specs = specs
        self.inputs = list(shards)
        self.out_shapes = [jax.ShapeDtypeStruct(sp.full_shape, BF16) for sp in specs]
        nw = len(specs)
        self.scratch = [pltpu.SemaphoreType.DMA((6 * nw,)), pltpu.SemaphoreType.DMA((6 * nw,)),
                        pltpu.SemaphoreType.DMA((nw,))]

    def phases(self, nsteps):
        return [(0, self.start), (max(1, (7 * nsteps) // 10) if nsteps > 1 else 0, self.middle),
                (nsteps - 1, self.end)]

    def _ctx(self):
        x, y, c = _place()
        return x, y, c, 2 * x + y, (x, y, 1 - c), [(1 - x, y), (x, 1 - y), (1 - x, 1 - y)]

    def start(self, shard_refs, full_refs, sems):
        send_sems, recv_sems, local_sems = sems
        x, y, c, my_chip, sibling, chips = self._ctx()
        for w, sp in enumerate(self.specs):
            _start_local(sp, shard_refs[w], sp.window(full_refs[w], my_chip), sp.shard_shape, 2, local_sems.at[w])
            for t, (cx, cy) in enumerate(chips):
                _start_remote(sp, sp.shard_half(shard_refs[w], c), sp.window(full_refs[w], my_chip, c),
                              sp.piece_shape, 2, send_sems.at[6 * w + t], recv_sems.at[6 * w + t], (cx, cy, c))

    def middle(self, shard_refs, full_refs, sems):
        send_sems, recv_sems, local_sems = sems
        x, y, c, my_chip, sibling, chips = self._ctx()
        for w, sp in enumerate(self.specs):
            for t, (cx, cy) in enumerate(chips):
                landed = sp.window(full_refs[w], 2 * cx + cy, c)
                _remote(landed, landed, send_sems.at[6 * w + t], recv_sems.at[6 * w + t], (cx, cy, c)).wait_recv()
                _start_remote(sp, landed, landed, sp.piece_shape, 2, send_sems.at[6 * w + 3 + t],
                              recv_sems.at[6 * w + 3 + t], sibling)

    def end(self, shard_refs, full_refs, sems):
        send_sems, recv_sems, local_sems = sems
        x, y, c, my_chip, sibling, chips = self._ctx()
        for w, sp in enumerate(self.specs):
            for t, (cx, cy) in enumerate(chips):
                other = sp.window(full_refs[w], 2 * cx + cy, 1 - c)
                _remote(other, other, send_sems.at[6 * w + 3 + t], recv_sems.at[6 * w + 3 + t], sibling).wait_recv()
        for w, sp in enumerate(self.specs):
            for t, (cx, cy) in enumerate(chips):
                mine = sp.shard_half(shard_refs[w], c)
                _remote(mine, mine, send_sems.at[6 * w + t], recv_sems.at[6 * w + t], (cx, cy, c)).wait_send()
                landed = sp.window(full_refs[w], 2 * cx + cy, c)
                _remote(landed, landed, send_sems.at[6 * w + 3 + t], recv_sems.at[6 * w + 3 + t], sibling).wait_send()
            pltpu.make_async_copy(shard_refs[w], sp.window(full_refs[w], my_chip), local_sems.at[w]).wait()


class _ScatterGrads:
    def __init__(self, specs, partials):
        self.specs = specs
        self.inputs = list(partials)
        self.out_shapes = [jax.ShapeDtypeStruct((2, N_DEV) + sp.piece_shape, BF16) for sp in specs]
        nw = len(specs)
        self.scratch = [pltpu.SemaphoreType.DMA((15 * nw,)), pltpu.SemaphoreType.DMA((15 * nw,)),
                        pltpu.SemaphoreType.DMA((nw,))]

    def phases(self, nsteps):
        return [(0, self.start), (max(1, (7 * nsteps) // 10), self.middle), (nsteps - 1, self.end)]

    def _own(self, sp, part_ref, recv_ref, x, y, c, sem):
        return pltpu.make_async_copy(sp.window(part_ref, 2 * x + y, c), recv_ref.at[c, 0], sem)

    def start(self, part_refs, recv_refs, sems):
        send_sems, recv_sems, local_sems = sems
        x, y, c = _place()
        for w, sp in enumerate(self.specs):
            _start_local(sp, sp.window(part_refs[w], 2 * x + y, c), recv_refs[w].at[c, 0], sp.piece_shape, 2,
                         local_sems.at[w])
            for k in range(1, N_DEV):
                px, py, pc = _xor_peer(k)
                _start_remote(sp, sp.window(part_refs[w], 2 * px + py, pc), recv_refs[w].at[pc, k], sp.piece_shape,
                              2, send_sems.at[15 * w + k - 1], recv_sems.at[15 * w + k - 1], (px, py, pc))

    def middle(self, part_refs, recv_refs, sems):
        send_sems, recv_sems, local_sems = sems
        x, y, c = _place()
        sibling = (x, y, 1 - c)
        for w, sp in enumerate(self.specs):
            self._own(sp, part_refs[w], recv_refs[w], x, y, c, local_sems.at[w]).wait()
            for k in range(N_DEV):
                landed = recv_refs[w].at[c, k]
                if k:
                    _remote(landed, landed, send_sems.at[15 * w + k - 1], recv_sems.at[15 * w + k - 1],
                            sibling).wait_recv()
                _start_remote(sp, landed, landed, sp.piece_shape, 2, send_sems.at[15 * w + 7 + k],
                              recv_sems.at[15 * w + 7 + k], sibling)

    def end(self, part_refs, recv_refs, sems):
        send_sems, recv_sems, local_sems = sems
        x, y, c = _place()
        sibling = (x, y, 1 - c)
        for w, sp in enumerate(self.specs):
            for k in range(N_DEV):
                other = recv_refs[w].at[1 - c, k]
                _remote(other, other, send_sems.at[15 * w + 7 + k], recv_sems.at[15 * w + 7 + k], sibling).wait_recv()
            for k in range(N_DEV):
                landed = recv_refs[w].at[c, k]
                _remote(landed, landed, send_sems.at[15 * w + 7 + k], recv_sems.at[15 * w + 7 + k],
                        sibling).wait_send()
                if k:
                    px, py, pc = _xor_peer(k)
                    sent = sp.window(part_refs[w], 2 * px + py, pc)
                    _remote(sent, sent, send_sems.at[15 * w + k - 1], recv_sems.at[15 * w + k - 1],
                            (px, py, pc)).wait_send()


def _call(name, body, grid, in_specs, args, out_shape, out_specs, scratch, semantics, est_bytes, comm=None):
    out_shape, out_specs = tuple(out_shape), tuple(out_specs)
    if comm is None:
        res = pl.pallas_call(body, name=name, grid=grid, out_shape=out_shape, in_specs=list(in_specs),
                             out_specs=out_specs, scratch_shapes=list(scratch),
                             compiler_params=_cparams(semantics, est_bytes))(*args)
        return tuple(res), ()
    n_in, n_out, n_sc = len(in_specs), len(out_shape), len(scratch)
    c_in, c_out = len(comm.inputs), len(comm.out_shapes)
    nsteps = math.prod(grid)
    phases = comm.phases(nsteps)

    def hosted(*refs):
        pos = [0]

        def take(n):
            part = refs[pos[0]:pos[0] + n]
            pos[0] += n
            return part

        ins, cins, outs, couts, scr, sems = take(n_in), take(c_in), take(n_out), take(c_out), take(n_sc), take(3)
        step = 0
        for ax, extent in enumerate(grid):
            step = step * extent + pl.program_id(ax)
        for at, fn in phases:
            if at == 0:
                pl.when(step == 0)(functools.partial(fn, cins, couts, sems))
        body(*ins, *outs, *scr)
        for at, fn in phases:
            if at > 0:
                pl.when(step == at)(functools.partial(fn, cins, couts, sems))

    any_spec = pl.BlockSpec(memory_space=pl.ANY)
    res = pl.pallas_call(
        hosted, name=name, grid=grid, out_shape=out_shape + tuple(comm.out_shapes),
        in_specs=list(in_specs) + [any_spec] * c_in, out_specs=out_specs + (any_spec,) * c_out,
        scratch_shapes=list(scratch) + list(comm.scratch),
        compiler_params=_cparams(("arbitrary",) * len(grid), est_bytes))(*args, *comm.inputs)
    return tuple(res[:n_out]), tuple(res[n_out:])


def _row_tile(rows, cap):
    for t in range(min(rows, cap), 0, -1):
        if rows % t == 0 and (t % 16 == 0 or t == rows):
            return t
    return rows


def _cast_bf16(name, x):
    r, c = x.shape
    tr = _row_tile(r, 512)

    def body(x_ref, o_ref):
        o_ref[...] = x_ref[...].astype(BF16)

    return pl.pallas_call(
        body, name=name, grid=(r // tr,), out_shape=jax.ShapeDtypeStruct((r, c), BF16),
        in_specs=[pl.BlockSpec((tr, c), lambda i: (i, 0))], out_specs=pl.BlockSpec((tr, c), lambda i: (i, 0)),
        compiler_params=_cparams(("parallel",), 2 * tr * c * 6),
    )(x)


def _adam_math(w, g, m, v):
    m = ADAM_B1 * m + (1.0 - ADAM_B1) * g
    v = ADAM_B2 * v + (1.0 - ADAM_B2) * (g * g)
    m_hat = m / (1.0 - ADAM_B1 ** ADAM_STEP)
    v_hat = v / (1.0 - ADAM_B2 ** ADAM_STEP)
    delta = -ADAM_LR * (m_hat / (jnp.sqrt(v_hat) + ADAM_EPS) + ADAM_WD * w)
    return delta, m, v


def _adamw(name, w, g, m, v):
    r, c = w.shape
    tr = _row_tile(r, max(8, (1 << 18) // c))

    def body(w_ref, g_ref, m_ref, v_ref, d_ref, nm_ref, nv_ref):
        d, nm, nv = _adam_math(w_ref[...], g_ref[...], m_ref[...], v_ref[...])
        d_ref[...] = d
        nm_ref[...] = nm
        nv_ref[...] = nv

    spec = pl.BlockSpec((tr, c), lambda i: (i, 0))
    out = jax.ShapeDtypeStruct((r, c), F32)
    return pl.pallas_call(
        body, name=name, grid=(r // tr,), out_shape=(out, out, out), in_specs=[spec] * 4, out_specs=(spec,) * 3,
        compiler_params=_cparams(("parallel",), 2 * 7 * tr * c * 4),
    )(w, g, m, v)


def _sum_pieces(x_ref):
    acc = x_ref[0, 0].astype(F32)
    for k in range(1, N_DEV):
        acc = acc + x_ref[0, k].astype(F32)
    return acc


def _reduce16(name, x):
    _, _, r, c = x.shape
    tr = _row_tile(r, max(16, (1 << 17) // c))
    nt = r // tr

    def body(x_ref, o_ref):
        o_ref[...] = _sum_pieces(x_ref)

    return pl.pallas_call(
        body, name=name, grid=(2, nt), out_shape=jax.ShapeDtypeStruct((2 * r, c), F32),
        in_specs=[pl.BlockSpec((1, N_DEV, tr, c), lambda hf, i: (hf, 0, i, 0))],
        out_specs=pl.BlockSpec((tr, c), lambda hf, i: (hf * nt + i, 0)),
        compiler_params=_cparams(("parallel", "parallel"), 2 * (N_DEV * 2 + 4) * tr * c),
    )(x)


def _reduce_adamw(name, x, w, m, v):
    _, _, r, c = x.shape
    tr = _row_tile(r, max(16, (1 << 17) // c))
    nt = r // tr

    def body(x_ref, w_ref, m_ref, v_ref, g_ref, d_ref, nm_ref, nv_ref):
        g = _sum_pieces(x_ref)
        d, nm, nv = _adam_math(w_ref[...], g, m_ref[...], v_ref[...])
        g_ref[...] = g
        d_ref[...] = d
        nm_ref[...] = nm
        nv_ref[...] = nv

    tile = pl.BlockSpec((tr, c), lambda hf, i: (hf * nt + i, 0))
    out = jax.ShapeDtypeStruct((2 * r, c), F32)
    return pl.pallas_call(
        body, name=name, grid=(2, nt), out_shape=(out,) * 4,
        in_specs=[pl.BlockSpec((1, N_DEV, tr, c), lambda hf, i: (hf, 0, i, 0)), tile, tile, tile],
        out_specs=(tile,) * 4,
        compiler_params=_cparams(("parallel", "parallel"), 2 * (N_DEV * 2 + 7 * 4) * tr * c),
    )(x, w, m, v)


def _t5_buckets():
    rel = jnp.arange(SPAN)[None, :] - WINDOW - jnp.arange(BLOCK)[:, None]
    half = N_BUCKETS // 2
    max_exact = half // 2
    ret = jnp.where(rel > 0, half, 0)
    n = jnp.abs(rel)
    nf = jnp.maximum(n, 1).astype(F32)
    large = max_exact + (jnp.log(nf / max_exact) / math.log(MAX_DISTANCE / max_exact)
                         * (half - max_exact)).astype(jnp.int32)
    large = jnp.minimum(large, half - 1)
    return (ret + jnp.where(n < max_exact, n, large)).astype(jnp.int32)


def _bias_table(table, buckets):
    def body(t_ref, b_ref, o_ref):
        bk = b_ref[...]
        for h in range(N_Q_HEADS):
            acc = jnp.zeros((BLOCK, SPAN), F32)
            for b in range(N_BUCKETS):
                acc = jnp.where(bk == b, t_ref[b, h], acc)
            o_ref[h] = acc

    return pl.pallas_call(
        body, name="bias_table", out_shape=jax.ShapeDtypeStruct((N_Q_HEADS, BLOCK, SPAN), F32),
        in_specs=[pl.BlockSpec(memory_space=pltpu.SMEM), pl.BlockSpec(memory_space=pltpu.VMEM)],
        out_specs=pl.BlockSpec(memory_space=pltpu.VMEM),
    )(table, buckets)


def _bias_table_grad(dbias, buckets):
    def body(d_ref, b_ref, o_ref):
        bk = b_ref[...]
        row = lax.broadcasted_iota(jnp.int32, (N_BUCKETS, LANE), 0)
        lane = lax.broadcasted_iota(jnp.int32, (N_BUCKETS, LANE), 1)
        acc = jnp.zeros((N_BUCKETS, LANE), F32)
        for h in range(N_Q_HEADS):
            d = d_ref[h]
            for b in range(N_BUCKETS):
                s = jnp.sum(jnp.where(bk == b, d, 0.0))
                acc = jnp.where((row == b) & (lane == h), s, acc)
        o_ref[...] = acc

    return pl.pallas_call(
        body, name="bias_table_grad", out_shape=jax.ShapeDtypeStruct((N_BUCKETS, LANE), F32),
        in_specs=[pl.BlockSpec(memory_space=pltpu.VMEM), pl.BlockSpec(memory_space=pltpu.VMEM)],
        out_specs=pl.BlockSpec(memory_space=pltpu.VMEM),
    )(dbias, buckets)


def _ada_forward(sc_all, w_ada):
    d, n = w_ada.shape
    tn = _pick_tile(n, COL_TILE)

    def body(sc_ref, w_ref, o_ref):
        row = lax.broadcasted_iota(jnp.int32, (N_DEV, d), 0)
        sc = jnp.zeros((N_DEV, d), F32)
        for k in range(N_DEV):
            sc = jnp.where(row == k, sc_ref[k], sc)
        o_ref[...] = jnp.dot(sc, w_ref[...], preferred_element_type=F32, precision=lax.Precision.HIGHEST)

    return pl.pallas_call(
        body, name="ada_forward", grid=(n // tn,), out_shape=jax.ShapeDtypeStruct((N_DEV, n), F32),
        in_specs=[_const_spec((N_DEV, 1, d)), pl.BlockSpec((d, tn), lambda j: (0, j))],
        out_specs=pl.BlockSpec((N_DEV, tn), lambda j: (0, j)),
        compiler_params=_cparams(("parallel",), 2 * d * tn * 4 + N_DEV * N_DEV * d * 8),
    )(sc_all, w_ada)


def _mod_finish(m_all, b_ada):
    _, _, n = m_all.shape

    def body(m_ref, b_ref, o_ref):
        x, y, c = _place()
        me = 4 * x + 2 * y + c
        row = lax.broadcasted_iota(jnp.int32, (N_DEV, n), 0)
        for j in range(N_CHIPS):
            blk = m_ref[2 * j]
            mine = jnp.sum(jnp.where(row == me, blk, 0.0), axis=0, keepdims=True)
            o_ref[:, j * n:(j + 1) * n] = mine + b_ref[:, j * n:(j + 1) * n]

    return pl.pallas_call(
        body, name="mod_finish", out_shape=jax.ShapeDtypeStruct((1, N_CHIPS * n), F32),
        in_specs=[pl.BlockSpec(memory_space=pltpu.VMEM), pl.BlockSpec(memory_space=pltpu.VMEM)],
        out_specs=pl.BlockSpec(memory_space=pltpu.VMEM),
    )(m_all, b_ada)


def _ada_backward(sc_t, dmod_cols, w, m, v):
    d, n = w.shape
    tr, tn = _row_tile(d, 512), _pick_tile(n, COL_TILE)

    def body(s_ref, dm_ref, w_ref, m_ref, v_ref, g_ref, d_ref, nm_ref, nv_ref):
        g = jnp.dot(s_ref[...], dm_ref[...], preferred_element_type=F32, precision=lax.Precision.HIGHEST)
        dl, nm, nv = _adam_math(w_ref[...], g, m_ref[...], v_ref[...])
        g_ref[...] = g
        d_ref[...] = dl
        nm_ref[...] = nm
        nv_ref[...] = nv

    tile = pl.BlockSpec((tr, tn), lambda i, j: (i, j))
    out = jax.ShapeDtypeStruct((d, n), F32)
    return pl.pallas_call(
        body, name="ada_backward", grid=(d // tr, n // tn), out_shape=(out,) * 4,
        in_specs=[pl.BlockSpec((tr, N_DEV), lambda i, j: (i, 0)), pl.BlockSpec((N_DEV, tn), lambda i, j: (0, j)),
                  tile, tile, tile],
        out_specs=(tile,) * 4,
        compiler_params=_cparams(("parallel", "parallel"), 2 * 8 * tr * tn * 4),
    )(sc_t, dmod_cols, w, m, v)


def _prenorm(x, mod, pre_g, comm=None):
    s, d = x.shape
    tm = _row_tile(s, 512)

    def body(x_ref, mod_ref, g_ref, h_ref):
        xv = x_ref[...]
        r = lax.rsqrt(jnp.mean(xv * xv, axis=1, keepdims=True) + EPS)
        xn = xv * r * g_ref[...]
        h_ref[...] = (xn * (1.0 + mod_ref[:, d:2 * d]) + mod_ref[:, 0:d]).astype(BF16)

    (h,), extra = _call("prenorm", body, (s // tm,),
                        [pl.BlockSpec((tm, d), lambda i: (i, 0)), _const_spec((1, 3 * d)), _const_spec((1, d))],
                        (x, mod, pre_g), [jax.ShapeDtypeStruct((s, d), BF16)], [pl.BlockSpec((tm, d), lambda i: (i, 0))],
                        [], ("parallel",), 2 * tm * d * 6 + 4 * tm * d * 4, comm=comm)
    return h, extra


def _matmul(name, a, b, bias=None, comm=None):
    s, k = a.shape
    _, n = b.shape
    tm, tn = _row_tile(s, 1024), COL_TILE

    def body(*refs):
        if bias is None:
            a_ref, b_ref, o_ref = refs
        else:
            a_ref, b_ref, bias_ref, o_ref = refs
        acc = jnp.dot(a_ref[...], b_ref[...], preferred_element_type=F32)
        if bias is not None:
            acc = _sigmoid(acc + bias_ref[...])
        o_ref[...] = acc.astype(BF16)

    in_specs = [pl.BlockSpec((tm, k), lambda i, j: (i, 0)), pl.BlockSpec((k, tn), lambda i, j: (0, j))]
    args = [a, b]
    if bias is not None:
        in_specs.append(pl.BlockSpec((1, tn), lambda i, j: (0, j)))
        args.append(bias)
    (out,), extra = _call(name, body, (s // tm, n // tn), in_specs, args, [jax.ShapeDtypeStruct((s, n), BF16)],
                          [pl.BlockSpec((tm, tn), lambda i, j: (i, j))], [], ("parallel", "arbitrary"),
                          2 * (tm * k + k * tn + tm * tn) * 2 + 2 * tm * tn * 4, comm=comm)
    return out, extra


def _col_specs(off, width, rows, row_index):
    assert off % COL_TILE == 0 and width % COL_TILE == 0
    return [pl.BlockSpec((rows, COL_TILE), functools.partial(lambda p, *ids: (row_index(*ids), p), off // COL_TILE + p))
            for p in range(width // COL_TILE)]


def _cat(refs):
    vals = [r[...] for r in refs]
    return vals[0] if len(vals) == 1 else jnp.concatenate(vals, axis=1)


def _attn_mask(n, s):
    rows = GQA_GROUP * BLOCK
    qi = lax.broadcasted_iota(jnp.int32, (rows, SPAN), 0) & (BLOCK - 1)
    t = lax.broadcasted_iota(jnp.int32, (rows, SPAN), 1)
    rel = t - WINDOW - qi
    kpos = (n - 1) * BLOCK + t
    return (jnp.abs(rel) <= WINDOW) & (kpos >= 0) & (kpos < s)


def _sink_column(sink_ref, kh):
    rows = GQA_GROUP * BLOCK
    grp = lax.broadcasted_iota(jnp.int32, (rows, 1), 0) // BLOCK
    col = jnp.zeros((rows, 1), F32)
    for g in range(GQA_GROUP):
        col = jnp.where(grp == g, sink_ref[0, kh * GQA_GROUP + g], col)
    return col


def _stack_heads(x, kh):
    base = kh * GQA_GROUP * HEAD_DIM
    return jnp.concatenate([x[:, base + g * HEAD_DIM: base + (g + 1) * HEAD_DIM] for g in range(GQA_GROUP)], axis=0)


def _softmax_parts(qs, k, bias, valid, sink_col):
    sc = lax.dot_general(qs, k, (((1,), (1,)), ((), ())), preferred_element_type=F32)
    sc = sc * (HEAD_DIM ** -0.5) + bias
    sc = jnp.where(valid, sc, NEG_INF)
    mx = jnp.maximum(jnp.max(sc, axis=1, keepdims=True), sink_col)
    e = jnp.exp(sc - mx)
    es = jnp.exp(sink_col - mx)
    inv = 1.0 / (jnp.sum(e, axis=1, keepdims=True) + es)
    return e * inv, es * inv


def _attn_forward(proj, bias2, sink, s):
    nblk = s // BLOCK
    nq = ATTN_WIDTH // COL_TILE
    kv_col = ATTN_WIDTH // COL_TILE
    assert 2 * KV_WIDTH == COL_TILE

    def body(*refs):
        q_refs = refs[:nq]
        kvp, kvc, kvn, bias_ref, sink_ref, o_ref = refs[nq:]
        n = pl.program_id(0)
        q = _cat(q_refs)
        kv = jnp.concatenate([kvp[...], kvc[...], kvn[...]], axis=0)
        valid = _attn_mask(n, s)
        for kh in range(N_KV_HEADS):
            qs = _stack_heads(q, kh)
            k = kv[:, kh * HEAD_DIM:(kh + 1) * HEAD_DIM]
            v = kv[:, KV_WIDTH + kh * HEAD_DIM: KV_WIDTH + (kh + 1) * HEAD_DIM]
            p, _ = _softmax_parts(qs, k, bias_ref[kh], valid, _sink_column(sink_ref, kh))
            o = jnp.dot(p.astype(BF16), v, preferred_element_type=F32)
            for g in range(GQA_GROUP):
                h = kh * GQA_GROUP + g
                o_ref[:, h * HEAD_DIM:(h + 1) * HEAD_DIM] = o[g * BLOCK:(g + 1) * BLOCK].astype(BF16)

    in_specs = _col_specs(0, ATTN_WIDTH, BLOCK, lambda n: n)
    in_specs += [pl.BlockSpec((BLOCK, COL_TILE), lambda n: (jnp.maximum(n - 1, 0), kv_col)),
                 pl.BlockSpec((BLOCK, COL_TILE), lambda n: (n, kv_col)),
                 pl.BlockSpec((BLOCK, COL_TILE), lambda n: (jnp.minimum(n + 1, nblk - 1), kv_col)),
                 _const_spec((N_KV_HEADS, GQA_GROUP * BLOCK, SPAN)),
                 pl.BlockSpec(memory_space=pltpu.SMEM)]
    return pl.pallas_call(
        body, name="attn_forward", grid=(nblk,), out_shape=jax.ShapeDtypeStruct((s, ATTN_WIDTH), BF16),
        in_specs=in_specs, out_specs=pl.BlockSpec((BLOCK, ATTN_WIDTH), lambda n: (n, 0)),
        compiler_params=_cparams(("parallel",), 16 << 20),
    )(*([proj] * (nq + 3)), bias2, sink)


def _pool_positions(i, tm, s, width):
    pos = i * tm - HALO + lax.broadcasted_iota(jnp.int32, (tm + 2 * HALO, width), 0)
    return pos, (pos >= 0) & (pos < s)


def _pool_count(pos, w, s):
    return (jnp.minimum(pos + w // 2, s) - jnp.maximum(pos - w // 2, 0)).astype(F32)


def _halo_specs_cols(off, width, tm, s):
    per = tm // HALO
    last = s // HALO - 1
    prev = _col_specs(off, width, HALO, lambda i: jnp.maximum(i * per - 1, 0))
    nxt = _col_specs(off, width, HALO, lambda i: jnp.minimum((i + 1) * per, last))
    return prev, nxt


def _branches(proj, attn, g, w_bra, w_brp, w_grp, pool_scale, s, d):
    a_w, p_w = ATTN_WIDTH, pool_scale.shape[1]
    cg = p_w // N_POOL_GROUPS
    tm = _row_tile(s, ROW_TILE)
    off_ga = ATTN_WIDTH + 2 * KV_WIDTH
    off_u = off_ga + a_w
    off_gp = off_u + p_w
    n_ga, n_u, n_gp = a_w // COL_TILE, p_w // COL_TILE, p_w // COL_TILE

    def body(*refs):
        it = iter(refs)
        attn_ref = next(it)
        ga_refs = [next(it) for _ in range(n_ga)]
        u_refs = [next(it) for _ in range(n_u)]
        up_refs = [next(it) for _ in range(n_u)]
        un_refs = [next(it) for _ in range(n_u)]
        gp_refs = [next(it) for _ in range(n_gp)]
        g_ref, wa_ref, wp_ref, wg_ref, ps_ref = (next(it) for _ in range(5))
        ya_ref, yp_ref, za_ref, zp_ref, mg_ref, pooled_ref, mixed_ref = (next(it) for _ in range(7))
        i = pl.program_id(0)
        ya = (attn_ref[...].astype(F32) * _silu(_cat(ga_refs).astype(F32))).astype(BF16)
        ya_ref[...] = ya
        za = jnp.dot(ya, wa_ref[...], preferred_element_type=F32)
        za_ref[...] = za.astype(BF16)

        u = _cat(u_refs).astype(F32)
        ext = jnp.concatenate([_cat(up_refs).astype(F32), u, _cat(un_refs).astype(F32)], axis=0)
        pos, ok = _pool_positions(i, tm, s, cg)
        mixed = []
        for gi, w in enumerate(POOL_SIZES):
            e = jnp.where(ok, ext[:, gi * cg:(gi + 1) * cg], 0.0)
            acc = e[HALO - w // 2: HALO - w // 2 + tm]
            for dd in range(-w // 2 + 1, w // 2):
                acc = acc + e[HALO + dd: HALO + dd + tm]
            cnt = _pool_count(pos[HALO:HALO + tm], w, s)
            pooled = (acc / cnt - u[:, gi * cg:(gi + 1) * cg]).astype(BF16)
            pooled_ref[:, gi * cg:(gi + 1) * cg] = pooled
            mixed.append(jnp.dot(pooled, wg_ref[gi], preferred_element_type=F32))
        mixed = jnp.concatenate(mixed, axis=1)
        mixed_ref[...] = mixed.astype(BF16)
        yp = (mixed * ps_ref[...] * _silu(_cat(gp_refs).astype(F32))).astype(BF16)
        yp_ref[...] = yp
        zp = jnp.dot(yp, wp_ref[...], preferred_element_type=F32)
        zp_ref[...] = zp.astype(BF16)
        gate = g_ref[...].astype(F32)
        mg_ref[...] = (gate[:, :d] * za + gate[:, d:] * zp).astype(BF16)

    row = lambda i: i
    u_prev, u_next = _halo_specs_cols(off_u, p_w, tm, s)
    in_specs = [pl.BlockSpec((tm, a_w), lambda i: (i, 0))]
    in_specs += _col_specs(off_ga, a_w, tm, row) + _col_specs(off_u, p_w, tm, row) + u_prev + u_next
    in_specs += _col_specs(off_gp, p_w, tm, row)
    in_specs += [pl.BlockSpec((tm, 2 * d), lambda i: (i, 0)), _resident((a_w, d)), _resident((p_w, d)),
                 _resident((N_POOL_GROUPS, cg, cg)), _const_spec((1, p_w))]
    n_proj = n_ga + 3 * n_u + n_gp
    tile = lambda w: pl.BlockSpec((tm, w), lambda i: (i, 0))
    out_widths = (a_w, p_w, d, d, d, p_w, p_w)
    est = 2 * tm * (a_w + a_w + 2 * p_w + 2 * d + sum(out_widths)) * 2 + (a_w + p_w) * d * 2 + 6 * tm * d * 4
    return pl.pallas_call(
        body, name="branches", grid=(s // tm,),
        out_shape=tuple(jax.ShapeDtypeStruct((s, w), BF16) for w in out_widths),
        in_specs=in_specs, out_specs=tuple(tile(w) for w in out_widths),
        compiler_params=_cparams(("parallel",), est),
    )(attn, *([proj] * n_proj), g, w_bra, w_brp, w_grp, pool_scale)


def _out_loss(merged, x, target, w_out, post_g, mod, s, d):
    tm = _row_tile(s, ROW_TILE)
    nsteps = s // tm

    def body(mg_ref, x_ref, t_ref, w_ref, pg_ref, mod_ref, dout_ref, do_ref, loss_ref, dgate_ref, dpg_ref, lacc):
        i = pl.program_id(0)

        @pl.when(i == 0)
        def _():
            lacc[...] = jnp.zeros_like(lacc)
            dgate_ref[...] = jnp.zeros_like(dgate_ref)
            dpg_ref[...] = jnp.zeros_like(dpg_ref)

        o = jnp.dot(mg_ref[...], w_ref[...], preferred_element_type=F32)
        r = lax.rsqrt(jnp.mean(o * o, axis=1, keepdims=True) + EPS)
        ohat = o * r
        pg = pg_ref[...]
        gate = mod_ref[:, 2 * d:3 * d]
        y = ohat * pg
        e = x_ref[...] + gate * y - t_ref[...]
        lacc[...] += jnp.sum(e * e, axis=0, keepdims=True)
        dout = e * (1.0 / d)
        dout_ref[...] = dout
        dgate_ref[...] += jnp.sum(dout * y, axis=0, keepdims=True)
        dy = dout * gate
        dpg_ref[...] += jnp.sum(dy * ohat, axis=0, keepdims=True)
        dohat = dy * pg
        do = r * (dohat - ohat * jnp.mean(dohat * ohat, axis=1, keepdims=True))
        do_ref[...] = do.astype(BF16)

        @pl.when(i == nsteps - 1)
        def _():
            loss_ref[...] = (0.5 / d) * jnp.sum(lacc[...], axis=1, keepdims=True)

    tile = pl.BlockSpec((tm, d), lambda i: (i, 0))
    vec = _const_spec((1, d))
    return pl.pallas_call(
        body, name="out_loss", grid=(nsteps,),
        out_shape=(jax.ShapeDtypeStruct((s, d), F32), jax.ShapeDtypeStruct((s, d), BF16),
                   jax.ShapeDtypeStruct((1, 1), F32), jax.ShapeDtypeStruct((1, d), F32),
                   jax.ShapeDtypeStruct((1, d), F32)),
        in_specs=[tile, tile, tile, _resident((d, d)), vec, _const_spec((1, 3 * d))],
        out_specs=(tile, tile, _const_spec((1, 1)), vec, vec),
        scratch_shapes=[pltpu.VMEM((1, d), F32)],
        compiler_params=_cparams(("arbitrary",), 2 * tm * d * (2 + 4 + 4 + 4 + 2) + d * d * 2 + 8 * tm * d * 4),
    )(merged, x, target, w_out, post_g, mod)


def _bwd_out(d_o, g, za, zp, w_out, s, d, comm=None):
    tm = _row_tile(s, ROW_TILE)

    def body(do_ref, g_ref, za_ref, zp_ref, w_ref, dza_ref, dzp_ref, dgl_ref, dbm_ref):
        i = pl.program_id(0)

        @pl.when(i == 0)
        def _():
            dbm_ref[...] = jnp.zeros_like(dbm_ref)

        dm = lax.dot_general(do_ref[...], w_ref[...], (((1,), (1,)), ((), ())), preferred_element_type=F32)
        gate = g_ref[...].astype(F32)
        ga, gp = gate[:, :d], gate[:, d:]
        dza_ref[...] = (dm * ga).astype(BF16)
        dzp_ref[...] = (dm * gp).astype(BF16)
        dla = dm * za_ref[...].astype(F32) * ga * (1.0 - ga)
        dlp = dm * zp_ref[...].astype(F32) * gp * (1.0 - gp)
        dgl_ref[:, :d] = dla.astype(BF16)
        dgl_ref[:, d:] = dlp.astype(BF16)
        dbm_ref[:, :d] += jnp.sum(dla, axis=0, keepdims=True)
        dbm_ref[:, d:] += jnp.sum(dlp, axis=0, keepdims=True)

    tile = pl.BlockSpec((tm, d), lambda i: (i, 0))
    wide = pl.BlockSpec((tm, 2 * d), lambda i: (i, 0))
    return _call("bwd_out", body, (s // tm,), [tile, wide, tile, tile, _resident((d, d))], (d_o, g, za, zp, w_out),
                 (jax.ShapeDtypeStruct((s, d), BF16), jax.ShapeDtypeStruct((s, d), BF16),
                  jax.ShapeDtypeStruct((s, 2 * d), BF16), jax.ShapeDtypeStruct((1, 2 * d), F32)),
                 (tile, tile, wide, _const_spec((1, 2 * d))), [], ("arbitrary",),
                 2 * tm * d * 2 * 9 + d * d * 2 + 8 * tm * d * 4, comm=comm)


def _bwd_branches(dza, dzp, attn, proj, mixed, w_bra, w_brp, w_grp, pool_scale, s, d, comm=None):
    a_w, p_w = ATTN_WIDTH, pool_scale.shape[1]
    cg = p_w // N_POOL_GROUPS
    tm = _row_tile(s, ROW_TILE)
    off_ga = ATTN_WIDTH + 2 * KV_WIDTH
    off_gp = off_ga + a_w + p_w
    n_ga, n_gp = a_w // COL_TILE, p_w // COL_TILE

    def body(*refs):
        it = iter(refs)
        dza_ref, dzp_ref, attn_ref = next(it), next(it), next(it)
        ga_refs = [next(it) for _ in range(n_ga)]
        gp_refs = [next(it) for _ in range(n_gp)]
        mixed_ref, wa_ref, wp_ref, wg_ref, ps_ref = (next(it) for _ in range(5))
        dattn_ref, dga_ref, dgp_ref, dmix_ref, dpool_ref, dps_ref = (next(it) for _ in range(6))
        i = pl.program_id(0)

        @pl.when(i == 0)
        def _():
            dps_ref[...] = jnp.zeros_like(dps_ref)

        dya = lax.dot_general(dza_ref[...], wa_ref[...], (((1,), (1,)), ((), ())), preferred_element_type=F32)
        ga = _cat(ga_refs).astype(F32)
        dattn_ref[...] = (dya * _silu(ga)).astype(BF16)
        dga_ref[...] = (dya * attn_ref[...].astype(F32) * _dsilu(ga)).astype(BF16)

        dyp = lax.dot_general(dzp_ref[...], wp_ref[...], (((1,), (1,)), ((), ())), preferred_element_type=F32)
        gp = _cat(gp_refs).astype(F32)
        mixed = mixed_ref[...].astype(F32)
        ps = ps_ref[...]
        sg = _silu(gp)
        dgp_ref[...] = (dyp * mixed * ps * _dsilu(gp)).astype(BF16)
        dps_ref[...] += jnp.sum(dyp * sg * mixed, axis=0, keepdims=True)
        dmix = (dyp * sg * ps).astype(BF16)
        dmix_ref[...] = dmix
        for gi in range(N_POOL_GROUPS):
            dp = lax.dot_general(dmix[:, gi * cg:(gi + 1) * cg], wg_ref[gi], (((1,), (1,)), ((), ())),
                                 preferred_element_type=F32)
            dpool_ref[:, gi * cg:(gi + 1) * cg] = dp.astype(BF16)

    row = lambda i: i
    tile = lambda w: pl.BlockSpec((tm, w), lambda i: (i, 0))
    in_specs = [tile(d), tile(d), tile(a_w)] + _col_specs(off_ga, a_w, tm, row) + _col_specs(off_gp, p_w, tm, row)
    in_specs += [tile(p_w), _resident((a_w, d)), _resident((p_w, d)), _resident((N_POOL_GROUPS, cg, cg)),
                 _const_spec((1, p_w))]
    out_widths = (a_w, a_w, p_w, p_w, p_w)
    est = 2 * tm * (2 * d + 2 * a_w + 2 * p_w + sum(out_widths)) * 2 + (a_w + p_w) * d * 2 + 8 * tm * a_w * 4
    return _call("bwd_branches", body, (s // tm,), in_specs,
                 (dza, dzp, attn, *([proj] * (n_ga + n_gp)), mixed, w_bra, w_brp, w_grp, pool_scale),
                 tuple(jax.ShapeDtypeStruct((s, w), BF16) for w in out_widths) + (jax.ShapeDtypeStruct((1, p_w), F32),),
                 tuple(tile(w) for w in out_widths) + (_const_spec((1, p_w)),), [], ("arbitrary",), est, comm=comm)


def _pool_backward(dpooled, s):
    _, p_w = dpooled.shape
    cg = p_w // N_POOL_GROUPS
    tm = _row_tile(s, ROW_TILE)
    per, last = tm // HALO, s // HALO - 1

    def body(dp_ref, prev_ref, next_ref, du_ref):
        i = pl.program_id(0)
        dp = dp_ref[...].astype(F32)
        ext = jnp.concatenate([prev_ref[...].astype(F32), dp, next_ref[...].astype(F32)], axis=0)
        pos, ok = _pool_positions(i, tm, s, cg)
        for gi, w in enumerate(POOL_SIZES):
            t = jnp.where(ok, ext[:, gi * cg:(gi + 1) * cg], 0.0) / _pool_count(pos, w, s)
            acc = t[HALO - w // 2 + 1: HALO - w // 2 + 1 + tm]
            for dd in range(-w // 2 + 2, w // 2 + 1):
                acc = acc + t[HALO + dd: HALO + dd + tm]
            du_ref[:, gi * cg:(gi + 1) * cg] = (acc - dp[:, gi * cg:(gi + 1) * cg]).astype(BF16)

    return pl.pallas_call(
        body, name="pool_backward", grid=(s // tm,), out_shape=jax.ShapeDtypeStruct((s, p_w), BF16),
        in_specs=[pl.BlockSpec((tm, p_w), lambda i: (i, 0)),
                  pl.BlockSpec((HALO, p_w), lambda i: (jnp.maximum(i * per - 1, 0), 0)),
                  pl.BlockSpec((HALO, p_w), lambda i: (jnp.minimum((i + 1) * per, last), 0))],
        out_specs=pl.BlockSpec((tm, p_w), lambda i: (i, 0)),
        compiler_params=_cparams(("parallel",), 4 * tm * p_w * 2 + 8 * tm * p_w * 4),
    )(dpooled, dpooled, dpooled)


def _attn_backward(proj, dattn, bias2, sink, s):
    nblk = s // BLOCK
    nq = ATTN_WIDTH // COL_TILE
    kv_col = ATTN_WIDTH // COL_TILE
    rows = GQA_GROUP * BLOCK
    scale = HEAD_DIM ** -0.5

    def body(*refs):
        q_refs = refs[:nq]
        kvp, kvc, kvn, do_ref, bias_ref, sink_ref, dq_ref, dkv_ref, dbias_ref, dsink_ref, acc, sacc = refs[nq:]
        n = pl.program_id(0)

        @pl.when(n == 0)
        def _():
            acc[...] = jnp.zeros_like(acc)
            sacc[...] = jnp.zeros_like(sacc)
            dbias_ref[...] = jnp.zeros_like(dbias_ref)
            dsink_ref[...] = jnp.zeros_like(dsink_ref)

        @pl.when(jnp.logical_and(n >= 1, n < nblk))
        def _():
            acc[(n + 1) % 3] = jnp.zeros((BLOCK, 2 * KV_WIDTH), F32)

        @pl.when(n < nblk)
        def _():
            q = _cat(q_refs)
            do = do_ref[...]
            kv = jnp.concatenate([kvp[...], kvc[...], kvn[...]], axis=0)
            valid = _attn_mask(n, s)
            for kh in range(N_KV_HEADS):
                qs = _stack_heads(q, kh)
                dos = _stack_heads(do, kh)
                k = kv[:, kh * HEAD_DIM:(kh + 1) * HEAD_DIM]
                v = kv[:, KV_WIDTH + kh * HEAD_DIM: KV_WIDTH + (kh + 1) * HEAD_DIM]
                p, ps = _softmax_parts(qs, k, bias_ref[kh], valid, _sink_column(sink_ref, kh))
                dp = lax.dot_general(dos, v, (((1,), (1,)), ((), ())), preferred_element_type=F32)
                delta = jnp.sum(p * dp, axis=1, keepdims=True)
                ds = p * (dp - delta)
                dbias_ref[kh] += ds
                sacc[kh] += -ps * delta
                dsb = ds.astype(BF16)
                dq = jnp.dot(dsb, k, preferred_element_type=F32) * scale
                for g in range(GQA_GROUP):
                    h = kh * GQA_GROUP + g
                    dq_ref[:, h * HEAD_DIM:(h + 1) * HEAD_DIM] = dq[g * BLOCK:(g + 1) * BLOCK].astype(BF16)
                dk = lax.dot_general(dsb, qs, (((0,), (0,)), ((), ())), preferred_element_type=F32) * scale
                dv = lax.dot_general(p.astype(BF16), dos, (((0,), (0,)), ((), ())), preferred_element_type=F32)
                for j in range(3):
                    slot = (n + 2 + j) % 3
                    acc[slot, :, kh * HEAD_DIM:(kh + 1) * HEAD_DIM] += dk[j * BLOCK:(j + 1) * BLOCK]
                    acc[slot, :, KV_WIDTH + kh * HEAD_DIM: KV_WIDTH + (kh + 1) * HEAD_DIM] += dv[j * BLOCK:(j + 1) * BLOCK]

        dkv_ref[...] = acc[(n + 2) % 3].astype(BF16)

        @pl.when(n == nblk)
        def _():
            lane = lax.broadcasted_iota(jnp.int32, (1, LANE), 1)
            out = jnp.zeros((1, LANE), F32)
            for kh in range(N_KV_HEADS):
                col = sacc[kh]
                for g in range(GQA_GROUP):
                    out = jnp.where(lane == kh * GQA_GROUP + g, jnp.sum(col[g * BLOCK:(g + 1) * BLOCK]), out)
            dsink_ref[...] = out

    qi = lambda n: jnp.minimum(n, nblk - 1)
    in_specs = _col_specs(0, ATTN_WIDTH, BLOCK, qi)
    in_specs += [pl.BlockSpec((BLOCK, COL_TILE), lambda n: (jnp.maximum(qi(n) - 1, 0), kv_col)),
                 pl.BlockSpec((BLOCK, COL_TILE), lambda n: (qi(n), kv_col)),
                 pl.BlockSpec((BLOCK, COL_TILE), lambda n: (jnp.minimum(qi(n) + 1, nblk - 1), kv_col)),
                 pl.BlockSpec((BLOCK, ATTN_WIDTH), lambda n: (qi(n), 0)),
                 _const_spec((N_KV_HEADS, rows, SPAN)),
                 pl.BlockSpec(memory_space=pltpu.SMEM)]
    return pl.pallas_call(
        body, name="attn_backward", grid=(nblk + 1,),
        out_shape=(jax.ShapeDtypeStruct((s, ATTN_WIDTH), BF16), jax.ShapeDtypeStruct((s, 2 * KV_WIDTH), BF16),
                   jax.ShapeDtypeStruct((N_KV_HEADS, rows, SPAN), F32), jax.ShapeDtypeStruct((1, LANE), F32)),
        in_specs=in_specs,
        out_specs=(pl.BlockSpec((BLOCK, ATTN_WIDTH), lambda n: (qi(n), 0)),
                   pl.BlockSpec((BLOCK, 2 * KV_WIDTH), lambda n: (jnp.clip(n - 1, 0, nblk - 1), 0)),
                   _const_spec((N_KV_HEADS, rows, SPAN)), _const_spec((1, LANE))),
        scratch_shapes=[pltpu.VMEM((3, BLOCK, 2 * KV_WIDTH), F32), pltpu.VMEM((N_KV_HEADS, rows, 1), F32)],
        compiler_params=_cparams(("arbitrary",), 24 << 20),
    )(*([proj] * (nq + 3)), dattn, bias2, sink)


def _pick_tile(n, cap):
    t = cap - cap % LANE
    while n % t:
        t -= LANE
    return t


def _matmul_tn(name, a, b, comm=None):
    s, m = a.shape
    _, n = b.shape
    tk = _row_tile(s, 512)
    tm, tn = _pick_tile(m, 1024), _pick_tile(n, 1152)
    nk = s // tk

    def body(a_ref, b_ref, o_ref, acc):
        k = pl.program_id(2)

        @pl.when(k == 0)
        def _():
            acc[...] = jnp.zeros_like(acc)

        acc[...] += lax.dot_general(a_ref[...], b_ref[...], (((0,), (0,)), ((), ())), preferred_element_type=F32)

        @pl.when(k == nk - 1)
        def _():
            o_ref[...] = acc[...].astype(BF16)

    (out,), extra = _call(
        name, body, (m // tm, n // tn, nk),
        [pl.BlockSpec((tk, tm), lambda i, j, k: (k, i)), pl.BlockSpec((tk, tn), lambda i, j, k: (k, j))], (a, b),
        [jax.ShapeDtypeStruct((m, n), BF16)], [pl.BlockSpec((tm, tn), lambda i, j, k: (i, j))],
        [pltpu.VMEM((tm, tn), F32)], ("parallel", "parallel", "arbitrary"),
        2 * tk * (tm + tn) * 2 + tm * tn * (4 + 4 + 4), comm=comm)
    return out, extra


def _pool_weight_grad(pooled, dmix, s):
    _, p_w = pooled.shape
    cg = p_w // N_POOL_GROUPS
    tk = _row_tile(s, 512)
    nk = s // tk

    def body(a_ref, b_ref, o_ref, acc):
        k = pl.program_id(1)

        @pl.when(k == 0)
        def _():
            acc[...] = jnp.zeros_like(acc)

        acc[...] += lax.dot_general(a_ref[...], b_ref[...], (((0,), (0,)), ((), ())), preferred_element_type=F32)

        @pl.when(k == nk - 1)
        def _():
            o_ref[0] = acc[...].astype(BF16)

    return pl.pallas_call(
        body, name="pool_weight_grad", grid=(N_POOL_GROUPS, nk),
        out_shape=jax.ShapeDtypeStruct((N_POOL_GROUPS, cg, cg), BF16),
        in_specs=[pl.BlockSpec((tk, cg), lambda g, k: (k, g)), pl.BlockSpec((tk, cg), lambda g, k: (k, g))],
        out_specs=pl.BlockSpec((1, cg, cg), lambda g, k: (g, 0, 0)),
        scratch_shapes=[pltpu.VMEM((cg, cg), F32)],
        compiler_params=_cparams(("parallel", "arbitrary"), 16 << 20),
    )(pooled, dmix)


def _bwd_input(dproj, dgl, w_in, w_merge, x, dout, mod, pre_g, s, d, comm=None):
    tm = _row_tile(s, 512)
    n_in = dproj.shape[1] // COL_TILE
    n_mg = dgl.shape[1] // COL_TILE
    nk = n_in + n_mg
    ni = s // tm

    def body(dp_ref, dg_ref, wi_ref, wm_ref, x_ref, dout_ref, mod_ref, pg_ref, gx_ref, dsh_ref, dsc_ref, dpg_ref,
             acc):
        i, k = pl.program_id(0), pl.program_id(1)

        @pl.when(jnp.logical_and(i == 0, k == 0))
        def _():
            dsh_ref[...] = jnp.zeros_like(dsh_ref)
            dsc_ref[...] = jnp.zeros_like(dsc_ref)
            dpg_ref[...] = jnp.zeros_like(dpg_ref)

        @pl.when(k == 0)
        def _():
            acc[...] = jnp.zeros_like(acc)

        @pl.when(k < n_in)
        def _():
            acc[...] += lax.dot_general(dp_ref[...], wi_ref[...], (((1,), (1,)), ((), ())),
                                        preferred_element_type=F32)

        @pl.when(k >= n_in)
        def _():
            acc[...] += lax.dot_general(dg_ref[...], wm_ref[...], (((1,), (1,)), ((), ())),
                                        preferred_element_type=F32)

        @pl.when(k == nk - 1)
        def _():
            dh = acc[...]
            xv = x_ref[...]
            r = lax.rsqrt(jnp.mean(xv * xv, axis=1, keepdims=True) + EPS)
            xhat = xv * r
            pg = pg_ref[...]
            one_scale = 1.0 + mod_ref[:, d:2 * d]
            dsh_ref[...] += jnp.sum(dh, axis=0, keepdims=True)
            dsc_ref[...] += jnp.sum(dh * xhat, axis=0, keepdims=True) * pg
            dpg_ref[...] += jnp.sum(dh * xhat, axis=0, keepdims=True) * one_scale
            dxh = dh * (one_scale * pg)
            dx = r * (dxh - xhat * jnp.mean(dxh * xhat, axis=1, keepdims=True))
            gx_ref[...] = dout_ref[...] + dx

    tile = pl.BlockSpec((tm, d), lambda i, k: (i, 0))
    vec = _const_spec((1, d))
    in_specs = [pl.BlockSpec((tm, COL_TILE), lambda i, k: (i, jnp.minimum(k, n_in - 1))),
                pl.BlockSpec((tm, COL_TILE), lambda i, k: (i, jnp.maximum(k - n_in, 0))),
                pl.BlockSpec((d, COL_TILE), lambda i, k: (0, jnp.minimum(k, n_in - 1))),
                pl.BlockSpec((d, COL_TILE), lambda i, k: (0, jnp.maximum(k - n_in, 0))),
                tile, tile, _const_spec((1, 3 * d)), vec]
    est = 2 * 2 * (tm + d) * COL_TILE * 2 + 2 * 3 * tm * d * 4 + tm * d * 4 + 8 * tm * d * 4
    return _call("bwd_input", body, (ni, nk), in_specs, (dproj, dgl, w_in, w_merge, x, dout, mod, pre_g),
                 (jax.ShapeDtypeStruct((s, d), F32),) + (jax.ShapeDtypeStruct((1, d), F32),) * 3,
                 (tile, vec, vec, vec), [pltpu.VMEM((tm, d), F32)], ("arbitrary", "arbitrary"), est, comm=comm)


def _pad_lanes(v, width):
    return jnp.pad(v, ((0, 0), (0, width - v.shape[1])))


def kernel(x, c, rel_bias_table, w_ada, b_ada, pre_norm_g, post_norm_g, w_in, attn_sink, w_pool_group, pool_scale, w_branch_attn, w_branch_pool, w_merge, b_merge, w_out, loss_target, m_rel_bias_table, m_w_ada, m_b_ada, m_pre_norm_g, m_post_norm_g, m_w_in, m_attn_sink, m_w_pool_group, m_pool_scale, m_w_branch_attn, m_w_branch_pool, m_w_merge, m_b_merge, m_w_out, v_rel_bias_table, v_w_ada, v_b_ada, v_pre_norm_g, v_post_norm_g, v_w_in, v_attn_sink, v_w_pool_group, v_pool_scale, v_w_branch_attn, v_w_branch_pool, v_w_merge, v_b_merge, v_w_out):
    _, s, d = x.shape
    p_w = pool_scale.shape[-1]
    cg = p_w // N_POOL_GROUPS
    in_w = 2 * ATTN_WIDTH + 2 * KV_WIDTH + 2 * p_w
    x2, t2 = x[0], loss_target[0]
    chip = 2 * lax.axis_index("x") + lax.axis_index("y")

    specs = [_Sharded("col", (d, in_w)), _Sharded("col", (d, 2 * d)), _Sharded("col", (ATTN_WIDTH, d)),
             _Sharded("col", (p_w, d)), _Sharded("row", (d, d)), _Sharded("grp", (N_POOL_GROUPS, cg, cg))]
    shards32 = [w_in[0], w_merge[0], w_branch_attn[0], w_branch_pool[0], w_out[0],
                w_pool_group[0].reshape(N_POOL_GROUPS * cg // N_CHIPS, cg)]
    names = ["w_in", "w_merge", "w_branch_attn", "w_branch_pool", "w_out", "w_pool_group"]
    shards16 = [_cast_bf16("cast_" + nm, w) for nm, w in zip(names, shards32)]
    shards16[5] = shards16[5].reshape(N_POOL_GROUPS, cg // N_CHIPS, cg)

    sc_all = _all_gather8("gather_cond", c, 1, pre=_silu)
    m_all = _all_gather8("gather_mod", _ada_forward(sc_all, w_ada[0]), N_DEV)
    mod = _mod_finish(m_all, b_ada)

    h, (wf_in,) = _prenorm(x2, mod, pre_norm_g, comm=_GatherWeights(specs[:1], shards16[:1]))
    proj, (wf_merge,) = _matmul("proj", h, wf_in, comm=_GatherWeights(specs[1:2], shards16[1:2]))
    gates, (wf_bra, wf_brp, wf_out, wf_grp) = _matmul("merge_gates", h, wf_merge, bias=b_merge,
                                                      comm=_GatherWeights(specs[2:], shards16[2:]))
    buckets = _t5_buckets()
    bias2 = _bias_table(rel_bias_table, buckets).reshape(N_KV_HEADS, GQA_GROUP * BLOCK, SPAN)
    attn = _attn_forward(proj, bias2, attn_sink, s)
    ya, yp, za, zp, merged, pooled, mixed = _branches(proj, attn, gates, wf_bra, wf_brp, wf_grp, pool_scale, s, d)
    dout, d_o, loss_part, dgate, dpostg = _out_loss(merged, x2, t2, wf_out, post_norm_g, mod, s, d)

    pw_out, _ = _matmul_tn("grad_w_out", merged, d_o)
    (dza, dzp, dgl, dbm), (pc_out,) = _bwd_out(d_o, gates, za, zp, wf_out, s, d,
                                                comm=_ScatterGrads(specs[4:5], [pw_out]))
    pw_bra, _ = _matmul_tn("grad_w_branch_attn", ya, dza)
    pw_brp, _ = _matmul_tn("grad_w_branch_pool", yp, dzp)
    pw_merge, _ = _matmul_tn("grad_w_merge", h, dgl)
    (dattn, dga, dgp, dmix, dpooled, dps), (pc_bra, pc_brp) = _bwd_branches(
        dza, dzp, attn, proj, mixed, wf_bra, wf_brp, wf_grp, pool_scale, s, d,
        comm=_ScatterGrads(specs[2:4], [pw_bra, pw_brp]))
    pw_grp = _pool_weight_grad(pooled, dmix, s)
    du = _pool_backward(dpooled, s)
    dq, dkv, dbias, dsink = _attn_backward(proj, dattn, bias2, attn_sink, s)
    dproj = jnp.concatenate([dq, dkv, dga, du, dgp], axis=1)
    pw_in, (pc_merge,) = _matmul_tn("grad_w_in", h, dproj, comm=_ScatterGrads(specs[1:2], [pw_merge]))
    (gx, dshift, dscale, dpreg), (pc_in, pc_grp) = _bwd_input(
        dproj, dgl, wf_in, wf_merge, x2, dout, mod, pre_norm_g, s, d,
        comm=_ScatterGrads([specs[0], specs[5]], [pw_in, pw_grp]))

    pieces = [pc_in, pc_merge, pc_bra, pc_brp, pc_out]
    weights = [w_in, w_merge, w_branch_attn, w_branch_pool, w_out]
    moms = [m_w_in, m_w_merge, m_w_branch_attn, m_w_branch_pool, m_w_out]
    vars_ = [v_w_in, v_w_merge, v_w_branch_attn, v_w_branch_pool, v_w_out]
    big = {}
    for nm, pc, w, m, v in zip(names, pieces, weights, moms, vars_):
        shape2 = (-1, w.shape[-1])
        res4 = _reduce_adamw("update_" + nm, pc, w.reshape(shape2), m.reshape(shape2), v.reshape(shape2))
        big[nm] = tuple(a.reshape(w.shape) for a in res4)
    hq = cg // N_CHIPS // 2
    g_grp = _reduce16("reduce_w_pool_group", pc_grp.reshape(2, N_DEV, N_POOL_GROUPS * hq, cg))
    g_grp = g_grp.reshape(2, N_POOL_GROUPS, hq, cg).transpose(1, 0, 2, 3).reshape(N_POOL_GROUPS * 2 * hq, cg)
    res3 = _adamw("adamw_w_pool_group", w_pool_group.reshape(-1, cg), g_grp, m_w_pool_group.reshape(-1, cg),
                  v_w_pool_group.reshape(-1, cg))
    big["w_pool_group"] = tuple(a.reshape(w_pool_group.shape) for a in (g_grp,) + tuple(res3))

    dtable = _bias_table_grad(dbias.reshape(N_Q_HEADS, BLOCK, SPAN), buckets)[:, :N_Q_HEADS]
    segs = [("b_ada", jnp.concatenate([dshift, dscale, dgate], axis=1), 3 * d),
            ("pre_norm_g", dpreg, d), ("post_norm_g", dpostg, d), ("attn_sink", dsink, LANE),
            ("pool_scale", dps, p_w), ("b_merge", dbm, 2 * d), ("rel_bias_table", dtable.reshape(1, -1), 2 * LANE)]
    packed = jnp.concatenate([_pad_lanes(v, w) for _, v, w in segs], axis=1)
    rows = _all_gather8("gather_small", packed, 1)[:, 0, :]

    def pack(vals):
        return jnp.concatenate([_pad_lanes(v.reshape(1, -1), w) for v, (_, _, w) in zip(vals, segs)], axis=1)

    small_w = [b_ada, pre_norm_g, post_norm_g, attn_sink, pool_scale, b_merge, rel_bias_table]
    small_m = [m_b_ada, m_pre_norm_g, m_post_norm_g, m_attn_sink, m_pool_scale, m_b_merge, m_rel_bias_table]
    small_v = [v_b_ada, v_pre_norm_g, v_post_norm_g, v_attn_sink, v_pool_scale, v_b_merge, v_rel_bias_table]
    g_small, d_small, nm_small, nv_small = _small_update(rows, pack(small_w), pack(small_m), pack(small_v))
    small = {}
    off = 0
    for (nm, _, w), ref in zip(segs, small_w):
        cut = lambda a: a[:, off:off + ref.size].reshape(ref.shape)
        small[nm] = (cut(g_small), cut(d_small), cut(nm_small), cut(nv_small))
        off += w

    dmod_cols = lax.dynamic_slice_in_dim(rows[:, :3 * d], chip * (3 * d // N_CHIPS), 3 * d // N_CHIPS, axis=1)
    sc_t = sc_all[:, 0, :].T
    g_ada, d_ada, nm_ada, nv_ada = _ada_backward(sc_t, dmod_cols, w_ada[0], m_w_ada[0], v_w_ada[0])
    big["w_ada"] = tuple(a.reshape(w_ada.shape) for a in (g_ada, d_ada, nm_ada, nv_ada))

    loss = lax.psum(loss_part[0, 0], ("x", "y", "c"))
    order = ["rel_bias_table", "w_ada", "b_ada", "pre_norm_g", "post_norm_g", "w_in", "attn_sink", "w_pool_group",
             "pool_scale", "w_branch_attn", "w_branch_pool", "w_merge", "b_merge", "w_out"]
    res = {**big, **small}
    outs = [loss, gx.reshape(x.shape)]
    for part in range(4):
        outs += [res[nm][part] for nm in order]
    return tuple(outs)


def _small_update(rows, w, m, v):
    _, n = rows.shape

    def body(r_ref, w_ref, m_ref, v_ref, g_ref, d_ref, nm_ref, nv_ref):
        g = r_ref[0:1, :]
        for k in range(1, N_DEV):
            g = g + r_ref[k:k + 1, :]
        dl, nm, nv = _adam_math(w_ref[...], g, m_ref[...], v_ref[...])
        g_ref[...] = g
        d_ref[...] = dl
        nm_ref[...] = nm
        nv_ref[...] = nv

    vm = pl.BlockSpec(memory_space=pltpu.VMEM)
    out = jax.ShapeDtypeStruct((1, n), F32)
    return pl.pallas_call(
        body, name="small_update", out_shape=(out,) * 4, in_specs=[vm] * 4, out_specs=(vm,) * 4,
    )(rows, w, m, v)
```

```python
import functools
import math

import numpy as np
import jax
import jax.numpy as jnp
from jax import lax
from jax.experimental import pallas as pl
from jax.experimental.pallas import tpu as pltpu

F32 = jnp.float32
BF16 = jnp.bfloat16
MESH = pl.DeviceIdType.MESH

HEAD_DIM = 128
N_Q_HEADS = 8
N_KV_HEADS = 2
GQA_GROUP = N_Q_HEADS // N_KV_HEADS
ATTN_WIDTH = N_Q_HEADS * HEAD_DIM
KV_WIDTH = N_KV_HEADS * HEAD_DIM
WINDOW = 128
BLOCK = 128
SPAN = BLOCK + 2 * WINDOW
N_BUCKETS = 32
MAX_DISTANCE = 128
POOL_SIZES = (2, 4, 8, 16)
N_POOL_GROUPS = len(POOL_SIZES)
HALO = 16
EPS = 1e-6
NEG_INF = -1e30
ADAM_LR = 0.001
ADAM_B1 = 0.9
ADAM_B2 = 0.999
ADAM_EPS = 1e-08
ADAM_WD = 0.01
ADAM_STEP = 10

N_DEV = 8
N_CHIPS = 4
LANE = 128
COL_TILE = 512
VMEM_CAP = 60000 * 1024
ROW_TILE = 256


def _cparams(sem, est_bytes):
    limit = int(min(max(est_bytes * 5 // 4 + (4 << 20), 16 << 20), VMEM_CAP))
    return pltpu.CompilerParams(dimension_semantics=sem, vmem_limit_bytes=limit)


def _sigmoid(x):
    return jax.nn.sigmoid(x)


def _silu(x):
    return x * _sigmoid(x)


def _dsilu(x):
    s = _sigmoid(x)
    return s * (1.0 + x * (1.0 - s))


def _place():
    x, y, c = lax.axis_index("x"), lax.axis_index("y"), lax.axis_index("c")
    return x, y, c


def _flip(v, bit):
    return (1 - v) if bit else v


def _xor_peer(k):
    x, y, c = _place()
    return (_flip(x, (k >> 2) & 1), _flip(y, (k >> 1) & 1), _flip(c, k & 1))


def _resident(shape):
    nd = len(shape)
    return pl.BlockSpec(shape, lambda *_: (0,) * nd, pipeline_mode=pl.Buffered(1))


def _const_spec(shape):
    nd = len(shape)
    return pl.BlockSpec(shape, lambda *_: (0,) * nd)


CHUNK_BYTES = 256 << 10
MAX_CHUNKS = 32


def _all_gather8(name, x, nrows, pre=None):
    r, n = x.shape

    def body(x_ref, out_ref, stage, send_sems, recv_sems):
        px, py, pc = _place()
        me = 4 * px + 2 * py + pc
        v = x_ref[...]
        if pre is not None:
            v = pre(v)
        stage[...] = v[0:nrows]
        out_ref[me] = v[0:nrows]
        copies = []
        for k in range(1, N_DEV):
            cp = pltpu.make_async_remote_copy(
                src_ref=stage, dst_ref=out_ref.at[me], send_sem=send_sems.at[k - 1], recv_sem=recv_sems.at[k - 1],
                device_id=_xor_peer(k), device_id_type=MESH)
            cp.start()
            copies.append(cp)
        for cp in copies:
            cp.wait()

    return pl.pallas_call(
        body, name=name,
        out_shape=jax.ShapeDtypeStruct((N_DEV, nrows, n), F32),
        in_specs=[pl.BlockSpec(memory_space=pltpu.VMEM)],
        out_specs=pl.BlockSpec(memory_space=pltpu.VMEM),
        scratch_shapes=[pltpu.VMEM((nrows, n), F32), pltpu.SemaphoreType.DMA((N_DEV - 1,)),
                        pltpu.SemaphoreType.DMA((N_DEV - 1,))],
    )(x)


class _Sharded:
    def __init__(self, kind, full_shape):
        self.kind = kind
        self.full_shape = tuple(full_shape)
        if kind == "col":
            r, c = full_shape
            self.shard_shape = (r, c // N_CHIPS)
        elif kind == "row":
            r, c = full_shape
            self.shard_shape = (r // N_CHIPS, c)
        else:
            g, r, c = full_shape
            self.shard_shape = (g, r // N_CHIPS, c)
        self.axis = 1 if kind == "grp" else 0
        s = list(self.shard_shape)
        s[self.axis] //= 2
        self.piece_shape = tuple(s)

    def _rows(self, ref, start, size):
        idx = (slice(None),) * self.axis + (pl.ds(pl.multiple_of(start, 16), size),)
        return ref.at[idx]

    def shard_half(self, ref, hc):
        h = self.piece_shape[self.axis]
        return self._rows(ref, hc * h, h)

    def window(self, ref, chip, hc=None):
        s = self.shard_shape
        if self.kind == "col":
            cols = pl.ds(pl.multiple_of(chip * s[1], LANE), s[1])
            if hc is None:
                return ref.at[:, cols]
            h = s[0] // 2
            return ref.at[pl.ds(pl.multiple_of(hc * h, 16), h), cols]
        n = s[self.axis]
        if hc is None:
            return self._rows(ref, chip * n, n)
        return self._rows(ref, chip * n + hc * (n // 2), n // 2)

    def chunks(self, view, shape, itemsize):
        rows = shape[self.axis]
        nbytes = math.prod(shape) * itemsize
        n = 1
        while 2 * n <= MAX_CHUNKS and nbytes // (2 * n) >= CHUNK_BYTES and rows % (2 * n * 16) == 0:
            n *= 2
        h = rows // n
        return [view.at[(slice(None),) * self.axis + (pl.ds(j * h, h),)] for j in range(n)]


def _remote(src, dst, send_sem, recv_sem, to):
    return pltpu.make_async_remote_copy(src_ref=src, dst_ref=dst, send_sem=send_sem, recv_sem=recv_sem,
                                        device_id=to, device_id_type=MESH)


def _start_remote(sp, src, dst, shape, itemsize, send_sem, recv_sem, to):
    for s_part, d_part in zip(sp.chunks(src, shape, itemsize), sp.chunks(dst, shape, itemsize)):
        _remote(s_part, d_part, send_sem, recv_sem, to).start()


def _start_local(sp, src, dst, shape, itemsize, sem):
    for s_part, d_part in zip(sp.chunks(src, shape, itemsize), sp.chunks(dst, shape, itemsize)):
        pltpu.make_async_copy(s_part, d_part, sem).start()


class _GatherWeights:
    def __init__(self, specs, shards):
        self.specs = specs
        self.inputs = list(shards)
        self.out_shapes = [jax.ShapeDtypeStruct(sp.full_shape, BF16) for sp in specs]
        nw = len(specs)
        self.scratch = [pltpu.SemaphoreType.DMA((6 * nw,)), pltpu.SemaphoreType.DMA((6 * nw,)),
                        pltpu.SemaphoreType.DMA((nw,))]

    def phases(self, nsteps):
        return [(0, self.start), (max(1, (7 * nsteps) // 10) if nsteps > 1 else 0, self.middle),
                (nsteps - 1, self.end)]

    def _ctx(self):
        x, y, c = _place()
        return x, y, c, 2 * x + y, (x, y, 1 - c), [(1 - x, y), (x, 1 - y), (1 - x, 1 - y)]

    def start(self, shard_refs, full_refs, sems):
        send_sems, recv_sems, local_sems = sems
        x, y, c, my_chip, sibling, chips = self._ctx()
        for w, sp in enumerate(self.specs):
            _start_local(sp, shard_refs[w], sp.window(full_refs[w], my_chip), sp.shard_shape, 2, local_sems.at[w])
            for t, (cx, cy) in enumerate(chips):
                _start_remote(sp, sp.shard_half(shard_refs[w], c), sp.window(full_refs[w], my_chip, c),
                              sp.piece_shape, 2, send_sems.at[6 * w + t], recv_sems.at[6 * w + t], (cx, cy, c))

    def middle(self, shard_refs, full_refs, sems):
        send_sems, recv_sems, local_sems = sems
        x, y, c, my_chip, sibling, chips = self._ctx()
        for w, sp in enumerate(self.specs):
            for t, (cx, cy) in enumerate(chips):
                landed = sp.window(full_refs[w], 2 * cx + cy, c)
                _remote(landed, landed, send_sems.at[6 * w + t], recv_sems.at[6 * w + t], (cx, cy, c)).wait_recv()
                _start_remote(sp, landed, landed, sp.piece_shape, 2, send_sems.at[6 * w + 3 + t],
                              recv_sems.at[6 * w + 3 + t], sibling)

    def end(self, shard_refs, full_refs, sems):
        send_sems, recv_sems, local_sems = sems
        x, y, c, my_chip, sibling, chips = self._ctx()
        for w, sp in enumerate(self.specs):
            for t, (cx, cy) in enumerate(chips):
                other = sp.window(full_refs[w], 2 * cx + cy, 1 - c)
                _remote(other, other, send_sems.at[6 * w + 3 + t], recv_sems.at[6 * w + 3 + t], sibling).wait_recv()
        for w, sp in enumerate(self.specs):
            for t, (cx, cy) in enumerate(chips):
                mine = sp.shard_half(shard_refs[w], c)
                _remote(mine, mine, send_sems.at[6 * w + t], recv_sems.at[6 * w + t], (cx, cy, c)).wait_send()
                landed = sp.window(full_refs[w], 2 * cx + cy, c)
                _remote(landed, landed, send_sems.at[6 * w + 3 + t], recv_sems.at[6 * w + 3 + t], sibling).wait_send()
            pltpu.make_async_copy(shard_refs[w], sp.window(full_refs[w], my_chip), local_sems.at[w]).wait()


class _ScatterGrads:
    def __init__(self, specs, partials):
        self.specs = specs
        self.inputs = list(partials)
        self.out_shapes = [jax.ShapeDtypeStruct((2, N_DEV) + sp.piece_shape, BF16) for sp in specs]
        nw = len(specs)
        self.scratch = [pltpu.SemaphoreType.DMA((15 * nw,)), pltpu.SemaphoreType.DMA((15 * nw,)),
                        pltpu.SemaphoreType.DMA((nw,))]

    def phases(self, nsteps):
        return [(0, self.start), (max(1, (7 * nsteps) // 10), self.middle), (nsteps - 1, self.end)]

    def _own(self, sp, part_ref, recv_ref, x, y, c, sem):
        return pltpu.make_async_copy(sp.window(part_ref, 2 * x + y, c), recv_ref.at[c, 0], sem)

    def start(self, part_refs, recv_refs, sems):
        send_sems, recv_sems, local_sems = sems
        x, y, c = _place()
        for w, sp in enumerate(self.specs):
            _start_local(sp, sp.window(part_refs[w], 2 * x + y, c), recv_refs[w].at[c, 0], sp.piece_shape, 2,
                         local_sems.at[w])
            for k in range(1, N_DEV):
                px, py, pc = _xor_peer(k)
                _start_remote(sp, sp.window(part_refs[w], 2 * px + py, pc), recv_refs[w].at[pc, k], sp.piece_shape,
                              2, send_sems.at[15 * w + k - 1], recv_sems.at[15 * w + k - 1], (px, py, pc))

    def middle(self, part_refs, recv_refs, sems):
        send_sems, recv_sems, local_sems = sems
        x, y, c = _place()
        sibling = (x, y, 1 - c)
        for w, sp in enumerate(self.specs):
            self._own(sp, part_refs[w], recv_refs[w], x, y, c, local_sems.at[w]).wait()
            for k in range(N_DEV):
                landed = recv_refs[w].at[c, k]
                if k:
                    _remote(landed, landed, send_sems.at[15 * w + k - 1], recv_sems.at[15 * w + k - 1],
                            sibling).wait_recv()
                _start_remote(sp, landed, landed, sp.piece_shape, 2, send_sems.at[15 * w + 7 + k],
                              recv_sems.at[15 * w + 7 + k], sibling)

    def end(self, part_refs, recv_refs, sems):
        send_sems, recv_sems, local_sems = sems
        x, y, c = _place()
        sibling = (x, y, 1 - c)
        for w, sp in enumerate(self.specs):
            for k in range(N_DEV):
                other = recv_refs[w].at[1 - c, k]
                _remote(other, other, send_sems.at[15 * w + 7 + k], recv_sems.at[15 * w + 7 + k], sibling).wait_recv()
            for k in range(N_DEV):
                landed = recv_refs[w].at[c, k]
                _remote(landed, landed, send_sems.at[15 * w + 7 + k], recv_sems.at[15 * w + 7 + k],
                        sibling).wait_send()
                if k:
                    px, py, pc = _xor_peer(k)
                    sent = sp.window(part_refs[w], 2 * px + py, pc)
                    _remote(sent, sent, send_sems.at[15 * w + k - 1], recv_sems.at[15 * w + k - 1],
                            (px, py, pc)).wait_send()


def _call(name, body, grid, in_specs, args, out_shape, out_specs, scratch, semantics, est_bytes, comm=None):
    out_shape, out_specs = tuple(out_shape), tuple(out_specs)
    if comm is None:
        res = pl.pallas_call(body, name=name, grid=grid, out_shape=out_shape, in_specs=list(in_specs),
                             out_specs=out_specs, scratch_shapes=list(scratch),
                             compiler_params=_cparams(semantics, est_bytes))(*args)
        return tuple(res), ()
    n_in, n_out, n_sc = len(in_specs), len(out_shape), len(scratch)
    c_in, c_out = len(comm.inputs), len(comm.out_shapes)
    nsteps = math.prod(grid)
    phases = comm.phases(nsteps)

    def hosted(*refs):
        pos = [0]

        def take(n):
            part = refs[pos[0]:pos[0] + n]
            pos[0] += n
            return part

        ins, cins, outs, couts, scr, sems = take(n_in), take(c_in), take(n_out), take(c_out), take(n_sc), take(3)
        step = 0
        for ax, extent in enumerate(grid):
            step = step * extent + pl.program_id(ax)
        for at, fn in phases:
            if at == 0:
                pl.when(step == 0)(functools.partial(fn, cins, couts, sems))
        body(*ins, *outs, *scr)
        for at, fn in phases:
            if at > 0:
                pl.when(step == at)(functools.partial(fn, cins, couts, sems))

    any_spec = pl.BlockSpec(memory_space=pl.ANY)
    res = pl.pallas_call(
        hosted, name=name, grid=grid, out_shape=out_shape + tuple(comm.out_shapes),
        in_specs=list(in_specs) + [any_spec] * c_in, out_specs=out_specs + (any_spec,) * c_out,
        scratch_shapes=list(scratch) + list(comm.scratch),
        compiler_params=_cparams(("arbitrary",) * len(grid), est_bytes))(*args, *comm.inputs)
    return tuple(res[:n_out]), tuple(res[n_out:])


def _row_tile(rows, cap):
    for t in range(min(rows, cap), 0, -1):
        if rows % t == 0 and (t % 16 == 0 or t == rows):
            return t
    return rows


def _cast_bf16(name, x):
    r, c = x.shape
    tr = _row_tile(r, 512)

    def body(x_ref, o_ref):
        o_ref[...] = x_ref[...].astype(BF16)

    return pl.pallas_call(
        body, name=name, grid=(r // tr,), out_shape=jax.ShapeDtypeStruct((r, c), BF16),
        in_specs=[pl.BlockSpec((tr, c), lambda i: (i, 0))], out_specs=pl.BlockSpec((tr, c), lambda i: (i, 0)),
        compiler_params=_cparams(("parallel",), 2 * tr * c * 6),
    )(x)


def _adam_math(w, g, m, v):
    m = ADAM_B1 * m + (1.0 - ADAM_B1) * g
    v = ADAM_B2 * v + (1.0 - ADAM_B2) * (g * g)
    m_hat = m / (1.0 - ADAM_B1 ** ADAM_STEP)
    v_hat = v / (1.0 - ADAM_B2 ** ADAM_STEP)
    delta = -ADAM_LR * (m_hat / (jnp.sqrt(v_hat) + ADAM_EPS) + ADAM_WD * w)
    return delta, m, v


def _adamw(name, w, g, m, v):
    r, c = w.shape
    tr = _row_tile(r, max(8, (1 << 18) // c))

    def body(w_ref, g_ref, m_ref, v_ref, d_ref, nm_ref, nv_ref):
        d, nm, nv = _adam_math(w_ref[...], g_ref[...], m_ref[...], v_ref[...])
        d_ref[...] = d
        nm_ref[...] = nm
        nv_ref[...] = nv

    spec = pl.BlockSpec((tr, c), lambda i: (i, 0))
    out = jax.ShapeDtypeStruct((r, c), F32)
    return pl.pallas_call(
        body, name=name, grid=(r // tr,), out_shape=(out, out, out), in_specs=[spec] * 4, out_specs=(spec,) * 3,
        compiler_params=_cparams(("parallel",), 2 * 7 * tr * c * 4),
    )(w, g, m, v)


def _sum_pieces(x_ref):
    acc = x_ref[0, 0].astype(F32)
    for k in range(1, N_DEV):
        acc = acc + x_ref[0, k].astype(F32)
    return acc


def _reduce16(name, x):
    _, _, r, c = x.shape
    tr = _row_tile(r, max(16, (1 << 17) // c))
    nt = r // tr

    def body(x_ref, o_ref):
        o_ref[...] = _sum_pieces(x_ref)

    return pl.pallas_call(
        body, name=name, grid=(2, nt), out_shape=jax.ShapeDtypeStruct((2 * r, c), F32),
        in_specs=[pl.BlockSpec((1, N_DEV, tr, c), lambda hf, i: (hf, 0, i, 0))],
        out_specs=pl.BlockSpec((tr, c), lambda hf, i: (hf * nt + i, 0)),
        compiler_params=_cparams(("parallel", "parallel"), 2 * (N_DEV * 2 + 4) * tr * c),
    )(x)


def _reduce_adamw(name, x, w, m, v):
    _, _, r, c = x.shape
    tr = _row_tile(r, max(16, (1 << 17) // c))
    nt = r // tr

    def body(x_ref, w_ref, m_ref, v_ref, g_ref, d_ref, nm_ref, nv_ref):
        g = _sum_pieces(x_ref)
        d, nm, nv = _adam_math(w_ref[...], g, m_ref[...], v_ref[...])
        g_ref[...] = g
        d_ref[...] = d
        nm_ref[...] = nm
        nv_ref[...] = nv

    tile = pl.BlockSpec((tr, c), lambda hf, i: (hf * nt + i, 0))
    out = jax.ShapeDtypeStruct((2 * r, c), F32)
    return pl.pallas_call(
        body, name=name, grid=(2, nt), out_shape=(out,) * 4,
        in_specs=[pl.BlockSpec((1, N_DEV, tr, c), lambda hf, i: (hf, 0, i, 0)), tile, tile, tile],
        out_specs=(tile,) * 4,
        compiler_params=_cparams(("parallel", "parallel"), 2 * (N_DEV * 2 + 7 * 4) * tr * c),
    )(x, w, m, v)


def _t5_buckets():
    rel = jnp.arange(SPAN)[None, :] - WINDOW - jnp.arange(BLOCK)[:, None]
    half = N_BUCKETS // 2
    max_exact = half // 2
    ret = jnp.where(rel > 0, half, 0)
    n = jnp.abs(rel)
    nf = jnp.maximum(n, 1).astype(F32)
    large = max_exact + (jnp.log(nf / max_exact) / math.log(MAX_DISTANCE / max_exact)
                         * (half - max_exact)).astype(jnp.int32)
    large = jnp.minimum(large, half - 1)
    return (ret + jnp.where(n < max_exact, n, large)).astype(jnp.int32)


def _bias_table(table, buckets):
    def body(t_ref, b_ref, o_ref):
        bk = b_ref[...]
        for h in range(N_Q_HEADS):
            acc = jnp.zeros((BLOCK, SPAN), F32)
            for b in range(N_BUCKETS):
                acc = jnp.where(bk == b, t_ref[b, h], acc)
            o_ref[h] = acc

    return pl.pallas_call(
        body, name="bias_table", out_shape=jax.ShapeDtypeStruct((N_Q_HEADS, BLOCK, SPAN), F32),
        in_specs=[pl.BlockSpec(memory_space=pltpu.SMEM), pl.BlockSpec(memory_space=pltpu.VMEM)],
        out_specs=pl.BlockSpec(memory_space=pltpu.VMEM),
    )(table, buckets)


def _bias_table_grad(dbias, buckets):
    def body(d_ref, b_ref, o_ref):
        bk = b_ref[...]
        row = lax.broadcasted_iota(jnp.int32, (N_BUCKETS, LANE), 0)
        lane = lax.broadcasted_iota(jnp.int32, (N_BUCKETS, LANE), 1)
        acc = jnp.zeros((N_BUCKETS, LANE), F32)
        for h in range(N_Q_HEADS):
            d = d_ref[h]
            for b in range(N_BUCKETS):
                s = jnp.sum(jnp.where(bk == b, d, 0.0))
                acc = jnp.where((row == b) & (lane == h), s, acc)
        o_ref[...] = acc

    return pl.pallas_call(
        body, name="bias_table_grad", out_shape=jax.ShapeDtypeStruct((N_BUCKETS, LANE), F32),
        in_specs=[pl.BlockSpec(memory_space=pltpu.VMEM), pl.BlockSpec(memory_space=pltpu.VMEM)],
        out_specs=pl.BlockSpec(memory_space=pltpu.VMEM),
    )(dbias, buckets)


def _ada_forward(sc_all, w_ada):
    d, n = w_ada.shape
    tn = _pick_tile(n, COL_TILE)

    def body(sc_ref, w_ref, o_ref):
        row = lax.broadcasted_iota(jnp.int32, (N_DEV, d), 0)
        sc = jnp.zeros((N_DEV, d), F32)
        for k in range(N_DEV):
            sc = jnp.where(row == k, sc_ref[k], sc)
        o_ref[...] = jnp.dot(sc, w_ref[...], preferred_element_type=F32, precision=lax.Precision.HIGHEST)

    return pl.pallas_call(
        body, name="ada_forward", grid=(n // tn,), out_shape=jax.ShapeDtypeStruct((N_DEV, n), F32),
        in_specs=[_const_spec((N_DEV, 1, d)), pl.BlockSpec((d, tn), lambda j: (0, j))],
        out_specs=pl.BlockSpec((N_DEV, tn), lambda j: (0, j)),
        compiler_params=_cparams(("parallel",), 2 * d * tn * 4 + N_DEV * N_DEV * d * 8),
    )(sc_all, w_ada)


def _mod_finish(m_all, b_ada):
    _, _, n = m_all.shape

    def body(m_ref, b_ref, o_ref):
        x, y, c = _place()
        me = 4 * x + 2 * y + c
        row = lax.broadcasted_iota(jnp.int32, (N_DEV, n), 0)
        for j in range(N_CHIPS):
            blk = m_ref[2 * j]
            mine = jnp.sum(jnp.where(row == me, blk, 0.0), axis=0, keepdims=True)
            o_ref[:, j * n:(j + 1) * n] = mine + b_ref[:, j * n:(j + 1) * n]

    return pl.pallas_call(
        body, name="mod_finish", out_shape=jax.ShapeDtypeStruct((1, N_CHIPS * n), F32),
        in_specs=[pl.BlockSpec(memory_space=pltpu.VMEM), pl.BlockSpec(memory_space=pltpu.VMEM)],
        out_specs=pl.BlockSpec(memory_space=pltpu.VMEM),
    )(m_all, b_ada)


def _ada_backward(sc_t, dmod_cols, w, m, v):
    d, n = w.shape
    tr, tn = _row_tile(d, 512), _pick_tile(n, COL_TILE)

    def body(s_ref, dm_ref, w_ref, m_ref, v_ref, g_ref, d_ref, nm_ref, nv_ref):
        g = jnp.dot(s_ref[...], dm_ref[...], preferred_element_type=F32, precision=lax.Precision.HIGHEST)
        dl, nm, nv = _adam_math(w_ref[...], g, m_ref[...], v_ref[...])
        g_ref[...] = g
        d_ref[...] = dl
        nm_ref[...] = nm
        nv_ref[...] = nv

    tile = pl.BlockSpec((tr, tn), lambda i, j: (i, j))
    out = jax.ShapeDtypeStruct((d, n), F32)
    return pl.pallas_call(
        body, name="ada_backward", grid=(d // tr, n // tn), out_shape=(out,) * 4,
        in_specs=[pl.BlockSpec((tr, N_DEV), lambda i, j: (i, 0)), pl.BlockSpec((N_DEV, tn), lambda i, j: (0, j)),
                  tile, tile, tile],
        out_specs=(tile,) * 4,
        compiler_params=_cparams(("parallel", "parallel"), 2 * 8 * tr * tn * 4),
    )(sc_t, dmod_cols, w, m, v)


def _prenorm(x, mod, pre_g, comm=None):
    s, d = x.shape
    tm = _row_tile(s, 512)

    def body(x_ref, mod_ref, g_ref, h_ref):
        xv = x_ref[...]
        r = lax.rsqrt(jnp.mean(xv * xv, axis=1, keepdims=True) + EPS)
        xn = xv * r * g_ref[...]
        h_ref[...] = (xn * (1.0 + mod_ref[:, d:2 * d]) + mod_ref[:, 0:d]).astype(BF16)

    (h,), extra = _call("prenorm", body, (s // tm,),
                        [pl.BlockSpec((tm, d), lambda i: (i, 0)), _const_spec((1, 3 * d)), _const_spec((1, d))],
                        (x, mod, pre_g), [jax.ShapeDtypeStruct((s, d), BF16)], [pl.BlockSpec((tm, d), lambda i: (i, 0))],
                        [], ("parallel",), 2 * tm * d * 6 + 4 * tm * d * 4, comm=comm)
    return h, extra


def _matmul(name, a, b, bias=None, comm=None):
    s, k = a.shape
    _, n = b.shape
    tm, tn = _row_tile(s, 1024), _pick_tile(n, 1024)

    def body(*refs):
        if bias is None:
            a_ref, b_ref, o_ref = refs
        else:
            a_ref, b_ref, bias_ref, o_ref = refs
        acc = jnp.dot(a_ref[...], b_ref[...], preferred_element_type=F32)
        if bias is not None:
            acc = _sigmoid(acc + bias_ref[...])
        o_ref[...] = acc.astype(BF16)

    in_specs = [pl.BlockSpec((tm, k), lambda i, j: (i, 0)), pl.BlockSpec((k, tn), lambda i, j: (0, j))]
    args = [a, b]
    if bias is not None:
        in_specs.append(pl.BlockSpec((1, tn), lambda i, j: (0, j)))
        args.append(bias)
    (out,), extra = _call(name, body, (s // tm, n // tn), in_specs, args, [jax.ShapeDtypeStruct((s, n), BF16)],
                          [pl.BlockSpec((tm, tn), lambda i, j: (i, j))], [], ("parallel", "arbitrary"),
                          2 * (tm * k + k * tn + tm * tn) * 2 + 2 * tm * tn * 4, comm=comm)
    return out, extra


def _col_specs(off, width, rows, row_index):
    assert off % COL_TILE == 0 and width % COL_TILE == 0
    return [pl.BlockSpec((rows, COL_TILE), functools.partial(lambda p, *ids: (row_index(*ids), p), off // COL_TILE + p))
            for p in range(width // COL_TILE)]


def _cat(refs):
    vals = [r[...] for r in refs]
    return vals[0] if len(vals) == 1 else jnp.concatenate(vals, axis=1)


def _attn_mask(n, s):
    rows = GQA_GROUP * BLOCK
    qi = lax.broadcasted_iota(jnp.int32, (rows, SPAN), 0) & (BLOCK - 1)
    t = lax.broadcasted_iota(jnp.int32, (rows, SPAN), 1)
    rel = t - WINDOW - qi
    kpos = (n - 1) * BLOCK + t
    return (jnp.abs(rel) <= WINDOW) & (kpos >= 0) & (kpos < s)


def _sink_column(sink_ref, kh):
    rows = GQA_GROUP * BLOCK
    grp = lax.broadcasted_iota(jnp.int32, (rows, 1), 0) // BLOCK
    col = jnp.zeros((rows, 1), F32)
    for g in range(GQA_GROUP):
        col = jnp.where(grp == g, sink_ref[0, kh * GQA_GROUP + g], col)
    return col


def _stack_heads(x, kh):
    base = kh * GQA_GROUP * HEAD_DIM
    return jnp.concatenate([x[:, base + g * HEAD_DIM: base + (g + 1) * HEAD_DIM] for g in range(GQA_GROUP)], axis=0)


def _softmax_parts(qs, k, bias, valid, sink_col):
    sc = lax.dot_general(qs, k, (((1,), (1,)), ((), ())), preferred_element_type=F32)
    sc = sc * (HEAD_DIM ** -0.5) + bias
    sc = jnp.where(valid, sc, NEG_INF)
    mx = jnp.maximum(jnp.max(sc, axis=1, keepdims=True), sink_col)
    e = jnp.exp(sc - mx)
    es = jnp.exp(sink_col - mx)
    inv = 1.0 / (jnp.sum(e, axis=1, keepdims=True) + es)
    return e * inv, es * inv


def _attn_forward(proj, bias2, sink, s):
    nblk = s // BLOCK
    nq = ATTN_WIDTH // COL_TILE
    kv_col = ATTN_WIDTH // COL_TILE
    assert 2 * KV_WIDTH == COL_TILE

    def body(*refs):
        q_refs = refs[:nq]
        kvp, kvc, kvn, bias_ref, sink_ref, o_ref = refs[nq:]
        n = pl.program_id(0)
        q = _cat(q_refs)
        kv = jnp.concatenate([kvp[...], kvc[...], kvn[...]], axis=0)
        valid = _attn_mask(n, s)
        for kh in range(N_KV_HEADS):
            qs = _stack_heads(q, kh)
            k = kv[:, kh * HEAD_DIM:(kh + 1) * HEAD_DIM]
            v = kv[:, KV_WIDTH + kh * HEAD_DIM: KV_WIDTH + (kh + 1) * HEAD_DIM]
            p, _ = _softmax_parts(qs, k, bias_ref[kh], valid, _sink_column(sink_ref, kh))
            o = jnp.dot(p.astype(BF16), v, preferred_element_type=F32)
            for g in range(GQA_GROUP):
                h = kh * GQA_GROUP + g
                o_ref[:, h * HEAD_DIM:(h + 1) * HEAD_DIM] = o[g * BLOCK:(g + 1) * BLOCK].astype(BF16)

    in_specs = _col_specs(0, ATTN_WIDTH, BLOCK, lambda n: n)
    in_specs += [pl.BlockSpec((BLOCK, COL_TILE), lambda n: (jnp.maximum(n - 1, 0), kv_col)),
                 pl.BlockSpec((BLOCK, COL_TILE), lambda n: (n, kv_col)),
                 pl.BlockSpec((BLOCK, COL_TILE), lambda n: (jnp.minimum(n + 1, nblk - 1), kv_col)),
                 _const_spec((N_KV_HEADS, GQA_GROUP * BLOCK, SPAN)),
                 pl.BlockSpec(memory_space=pltpu.SMEM)]
    return pl.pallas_call(
        body, name="attn_forward", grid=(nblk,), out_shape=jax.ShapeDtypeStruct((s, ATTN_WIDTH), BF16),
        in_specs=in_specs, out_specs=pl.BlockSpec((BLOCK, ATTN_WIDTH), lambda n: (n, 0)),
        compiler_params=_cparams(("parallel",), 16 << 20),
    )(*([proj] * (nq + 3)), bias2, sink)


def _pool_positions(i, tm, s, width):
    pos = i * tm - HALO + lax.broadcasted_iota(jnp.int32, (tm + 2 * HALO, width), 0)
    return pos, (pos >= 0) & (pos < s)


def _pool_count(pos, w, s):
    return (jnp.minimum(pos + w // 2, s) - jnp.maximum(pos - w // 2, 0)).astype(F32)


def _halo_specs_cols(off, width, tm, s):
    per = tm // HALO
    last = s // HALO - 1
    prev = _col_specs(off, width, HALO, lambda i: jnp.maximum(i * per - 1, 0))
    nxt = _col_specs(off, width, HALO, lambda i: jnp.minimum((i + 1) * per, last))
    return prev, nxt


def _branches(proj, attn, g, w_bra, w_brp, w_grp, pool_scale, s, d):
    a_w, p_w = ATTN_WIDTH, pool_scale.shape[1]
    cg = p_w // N_POOL_GROUPS
    tm = _row_tile(s, ROW_TILE)
    off_ga = ATTN_WIDTH + 2 * KV_WIDTH
    off_u = off_ga + a_w
    off_gp = off_u + p_w
    n_ga, n_u, n_gp = a_w // COL_TILE, p_w // COL_TILE, p_w // COL_TILE

    def body(*refs):
        it = iter(refs)
        attn_ref = next(it)
        ga_refs = [next(it) for _ in range(n_ga)]
        u_refs = [next(it) for _ in range(n_u)]
        up_refs = [next(it) for _ in range(n_u)]
        un_refs = [next(it) for _ in range(n_u)]
        gp_refs = [next(it) for _ in range(n_gp)]
        g_ref, wa_ref, wp_ref, wg_ref, ps_ref = (next(it) for _ in range(5))
        ya_ref, yp_ref, za_ref, zp_ref, mg_ref, pooled_ref, mixed_ref = (next(it) for _ in range(7))
        i = pl.program_id(0)
        ya = (attn_ref[...].astype(F32) * _silu(_cat(ga_refs).astype(F32))).astype(BF16)
        ya_ref[...] = ya
        za = jnp.dot(ya, wa_ref[...], preferred_element_type=F32)
        za_ref[...] = za.astype(BF16)

        u = _cat(u_refs).astype(F32)
        ext = jnp.concatenate([_cat(up_refs).astype(F32), u, _cat(un_refs).astype(F32)], axis=0)
        pos, ok = _pool_positions(i, tm, s, cg)
        mixed = []
        for gi, w in enumerate(POOL_SIZES):
            e = jnp.where(ok, ext[:, gi * cg:(gi + 1) * cg], 0.0)
            acc = e[HALO - w // 2: HALO - w // 2 + tm]
            for dd in range(-w // 2 + 1, w // 2):
                acc = acc + e[HALO + dd: HALO + dd + tm]
            cnt = _pool_count(pos[HALO:HALO + tm], w, s)
            pooled = (acc / cnt - u[:, gi * cg:(gi + 1) * cg]).astype(BF16)
            pooled_ref[:, gi * cg:(gi + 1) * cg] = pooled
            mixed.append(jnp.dot(pooled, wg_ref[gi], preferred_element_type=F32))
        mixed = jnp.concatenate(mixed, axis=1)
        mixed_ref[...] = mixed.astype(BF16)
        yp = (mixed * ps_ref[...] * _silu(_cat(gp_refs).astype(F32))).astype(BF16)
        yp_ref[...] = yp
        zp = jnp.dot(yp, wp_ref[...], preferred_element_type=F32)
        zp_ref[...] = zp.astype(BF16)
        gate = g_ref[...].astype(F32)
        mg_ref[...] = (gate[:, :d] * za + gate[:, d:] * zp).astype(BF16)

    row = lambda i: i
    u_prev, u_next = _halo_specs_cols(off_u, p_w, tm, s)
    in_specs = [pl.BlockSpec((tm, a_w), lambda i: (i, 0))]
    in_specs += _col_specs(off_ga, a_w, tm, row) + _col_specs(off_u, p_w, tm, row) + u_prev + u_next
    in_specs += _col_specs(off_gp, p_w, tm, row)
    in_specs += [pl.BlockSpec((tm, 2 * d), lambda i: (i, 0)), _resident((a_w, d)), _resident((p_w, d)),
                 _resident((N_POOL_GROUPS, cg, cg)), _const_spec((1, p_w))]
    n_proj = n_ga + 3 * n_u + n_gp
    tile = lambda w: pl.BlockSpec((tm, w), lambda i: (i, 0))
    out_widths = (a_w, p_w, d, d, d, p_w, p_w)
    est = 2 * tm * (a_w + a_w + 2 * p_w + 2 * d + sum(out_widths)) * 2 + (a_w + p_w) * d * 2 + 6 * tm * d * 4
    return pl.pallas_call(
        body, name="branches", grid=(s // tm,),
        out_shape=tuple(jax.ShapeDtypeStruct((s, w), BF16) for w in out_widths),
        in_specs=in_specs, out_specs=tuple(tile(w) for w in out_widths),
        compiler_params=_cparams(("parallel",), est),
    )(attn, *([proj] * n_proj), g, w_bra, w_brp, w_grp, pool_scale)


def _sub_rows(tm, sub=128):
    sub = min(sub, tm)
    return [pl.ds(r * sub, sub) for r in range(tm // sub)]


def _out_loss(merged, x, target, w_out, post_g, mod, s, d):
    tm = _row_tile(s, ROW_TILE)
    nsteps = s // tm

    def body(mg_ref, x_ref, t_ref, w_ref, pg_ref, mod_ref, dout_ref, do_ref, loss_ref, dgate_ref, dpg_ref, lacc):
        i = pl.program_id(0)

        @pl.when(i == 0)
        def _():
            lacc[...] = jnp.zeros_like(lacc)
            dgate_ref[...] = jnp.zeros_like(dgate_ref)
            dpg_ref[...] = jnp.zeros_like(dpg_ref)

        pg = pg_ref[...]
        gate = mod_ref[:, 2 * d:3 * d]
        for rows in _sub_rows(tm):
            o = jnp.dot(mg_ref[rows, :], w_ref[...], preferred_element_type=F32)
            r = lax.rsqrt(jnp.mean(o * o, axis=1, keepdims=True) + EPS)
            ohat = o * r
            y = ohat * pg
            e = x_ref[rows, :] + gate * y - t_ref[rows, :]
            lacc[...] += jnp.sum(e * e, axis=0, keepdims=True)
            dout = e * (1.0 / d)
            dout_ref[rows, :] = dout
            dgate_ref[...] += jnp.sum(dout * y, axis=0, keepdims=True)
            dy = dout * gate
            dpg_ref[...] += jnp.sum(dy * ohat, axis=0, keepdims=True)
            dohat = dy * pg
            do = r * (dohat - ohat * jnp.mean(dohat * ohat, axis=1, keepdims=True))
            do_ref[rows, :] = do.astype(BF16)

        @pl.when(i == nsteps - 1)
        def _():
            loss_ref[...] = (0.5 / d) * jnp.sum(lacc[...], axis=1, keepdims=True)

    tile = pl.BlockSpec((tm, d), lambda i: (i, 0))
    vec = _const_spec((1, d))
    return pl.pallas_call(
        body, name="out_loss", grid=(nsteps,),
        out_shape=(jax.ShapeDtypeStruct((s, d), F32), jax.ShapeDtypeStruct((s, d), BF16),
                   jax.ShapeDtypeStruct((1, 1), F32), jax.ShapeDtypeStruct((1, d), F32),
                   jax.ShapeDtypeStruct((1, d), F32)),
        in_specs=[tile, tile, tile, _resident((d, d)), vec, _const_spec((1, 3 * d))],
        out_specs=(tile, tile, _const_spec((1, 1)), vec, vec),
        scratch_shapes=[pltpu.VMEM((1, d), F32)],
        compiler_params=_cparams(("arbitrary",), 2 * tm * d * (2 + 4 + 4 + 4 + 2) + d * d * 2 + 8 * tm * d * 4),
    )(merged, x, target, w_out, post_g, mod)


def _bwd_out(d_o, g, za, zp, w_out, s, d, comm=None):
    tm = _row_tile(s, ROW_TILE)

    def body(do_ref, g_ref, za_ref, zp_ref, w_ref, dza_ref, dzp_ref, dgl_ref, dbm_ref):
        i = pl.program_id(0)

        @pl.when(i == 0)
        def _():
            dbm_ref[...] = jnp.zeros_like(dbm_ref)

        dm = lax.dot_general(do_ref[...], w_ref[...], (((1,), (1,)), ((), ())), preferred_element_type=F32)
        gate = g_ref[...].astype(F32)
        ga, gp = gate[:, :d], gate[:, d:]
        dza_ref[...] = (dm * ga).astype(BF16)
        dzp_ref[...] = (dm * gp).astype(BF16)
        dla = dm * za_ref[...].astype(F32) * ga * (1.0 - ga)
        dlp = dm * zp_ref[...].astype(F32) * gp * (1.0 - gp)
        dgl_ref[:, :d] = dla.astype(BF16)
        dgl_ref[:, d:] = dlp.astype(BF16)
        dbm_ref[:, :d] += jnp.sum(dla, axis=0, keepdims=True)
        dbm_ref[:, d:] += jnp.sum(dlp, axis=0, keepdims=True)

    tile = pl.BlockSpec((tm, d), lambda i: (i, 0))
    wide = pl.BlockSpec((tm, 2 * d), lambda i: (i, 0))
    return _call("bwd_out", body, (s // tm,), [tile, wide, tile, tile, _resident((d, d))], (d_o, g, za, zp, w_out),
                 (jax.ShapeDtypeStruct((s, d), BF16), jax.ShapeDtypeStruct((s, d), BF16),
                  jax.ShapeDtypeStruct((s, 2 * d), BF16), jax.ShapeDtypeStruct((1, 2 * d), F32)),
                 (tile, tile, wide, _const_spec((1, 2 * d))), [], ("arbitrary",),
                 2 * tm * d * 2 * 9 + d * d * 2 + 8 * tm * d * 4, comm=comm)


def _bwd_branches(dza, dzp, attn, proj, mixed, w_bra, w_brp, w_grp, pool_scale, s, d, comm=None):
    a_w, p_w = ATTN_WIDTH, pool_scale.shape[1]
    cg = p_w // N_POOL_GROUPS
    tm = _row_tile(s, ROW_TILE)
    off_ga = ATTN_WIDTH + 2 * KV_WIDTH
    off_gp = off_ga + a_w + p_w
    n_ga, n_gp = a_w // COL_TILE, p_w // COL_TILE

    def body(*refs):
        it = iter(refs)
        dza_ref, dzp_ref, attn_ref = next(it), next(it), next(it)
        ga_refs = [next(it) for _ in range(n_ga)]
        gp_refs = [next(it) for _ in range(n_gp)]
        mixed_ref, wa_ref, wp_ref, wg_ref, ps_ref = (next(it) for _ in range(5))
        dattn_ref, dga_ref, dgp_ref, dmix_ref, dpool_ref, dps_ref = (next(it) for _ in range(6))
        i = pl.program_id(0)

        @pl.when(i == 0)
        def _():
            dps_ref[...] = jnp.zeros_like(dps_ref)

        dya = lax.dot_general(dza_ref[...], wa_ref[...], (((1,), (1,)), ((), ())), preferred_element_type=F32)
        ga = _cat(ga_refs).astype(F32)
        dattn_ref[...] = (dya * _silu(ga)).astype(BF16)
        dga_ref[...] = (dya * attn_ref[...].astype(F32) * _dsilu(ga)).astype(BF16)

        dyp = lax.dot_general(dzp_ref[...], wp_ref[...], (((1,), (1,)), ((), ())), preferred_element_type=F32)
        gp = _cat(gp_refs).astype(F32)
        mixed = mixed_ref[...].astype(F32)
        ps = ps_ref[...]
        sg = _silu(gp)
        dgp_ref[...] = (dyp * mixed * ps * _dsilu(gp)).astype(BF16)
        dps_ref[...] += jnp.sum(dyp * sg * mixed, axis=0, keepdims=True)
        dmix = (dyp * sg * ps).astype(BF16)
        dmix_ref[...] = dmix
        for gi in range(N_POOL_GROUPS):
            dp = lax.dot_general(dmix[:, gi * cg:(gi + 1) * cg], wg_ref[gi], (((1,), (1,)), ((), ())),
                                 preferred_element_type=F32)
            dpool_ref[:, gi * cg:(gi + 1) * cg] = dp.astype(BF16)

    row = lambda i: i
    tile = lambda w: pl.BlockSpec((tm, w), lambda i: (i, 0))
    in_specs = [tile(d), tile(d), tile(a_w)] + _col_specs(off_ga, a_w, tm, row) + _col_specs(off_gp, p_w, tm, row)
    in_specs += [tile(p_w), _resident((a_w, d)), _resident((p_w, d)), _resident((N_POOL_GROUPS, cg, cg)),
                 _const_spec((1, p_w))]
    out_widths = (a_w, a_w, p_w, p_w, p_w)
    est = 2 * tm * (2 * d + 2 * a_w + 2 * p_w + sum(out_widths)) * 2 + (a_w + p_w) * d * 2 + 8 * tm * a_w * 4
    return _call("bwd_branches", body, (s // tm,), in_specs,
                 (dza, dzp, attn, *([proj] * (n_ga + n_gp)), mixed, w_bra, w_brp, w_grp, pool_scale),
                 tuple(jax.ShapeDtypeStruct((s, w), BF16) for w in out_widths) + (jax.ShapeDtypeStruct((1, p_w), F32),),
                 tuple(tile(w) for w in out_widths) + (_const_spec((1, p_w)),), [], ("arbitrary",), est, comm=comm)


def _pool_backward(dpooled, s):
    _, p_w = dpooled.shape
    cg = p_w // N_POOL_GROUPS
    tm = _row_tile(s, ROW_TILE)
    per, last = tm // HALO, s // HALO - 1

    def body(dp_ref, prev_ref, next_ref, du_ref):
        i = pl.program_id(0)
        dp = dp_ref[...].astype(F32)
        ext = jnp.concatenate([prev_ref[...].astype(F32), dp, next_ref[...].astype(F32)], axis=0)
        pos, ok = _pool_positions(i, tm, s, cg)
        for gi, w in enumerate(POOL_SIZES):
            t = jnp.where(ok, ext[:, gi * cg:(gi + 1) * cg], 0.0) / _pool_count(pos, w, s)
            acc = t[HALO - w // 2 + 1: HALO - w // 2 + 1 + tm]
            for dd in range(-w // 2 + 2, w // 2 + 1):
                acc = acc + t[HALO + dd: HALO + dd + tm]
            du_ref[:, gi * cg:(gi + 1) * cg] = (acc - dp[:, gi * cg:(gi + 1) * cg]).astype(BF16)

    return pl.pallas_call(
        body, name="pool_backward", grid=(s // tm,), out_shape=jax.ShapeDtypeStruct((s, p_w), BF16),
        in_specs=[pl.BlockSpec((tm, p_w), lambda i: (i, 0)),
                  pl.BlockSpec((HALO, p_w), lambda i: (jnp.maximum(i * per - 1, 0), 0)),
                  pl.BlockSpec((HALO, p_w), lambda i: (jnp.minimum((i + 1) * per, last), 0))],
        out_specs=pl.BlockSpec((tm, p_w), lambda i: (i, 0)),
        compiler_params=_cparams(("parallel",), 4 * tm * p_w * 2 + 8 * tm * p_w * 4),
    )(dpooled, dpooled, dpooled)


def _attn_backward(proj, dattn, bias2, sink, s):
    nblk = s // BLOCK
    nq = ATTN_WIDTH // COL_TILE
    kv_col = ATTN_WIDTH // COL_TILE
    rows = GQA_GROUP * BLOCK
    scale = HEAD_DIM ** -0.5

    def body(*refs):
        q_refs = refs[:nq]
        kvp, kvc, kvn, do_ref, bias_ref, sink_ref, dq_ref, dkv_ref, dbias_ref, dsink_ref, acc, sacc = refs[nq:]
        n = pl.program_id(0)

        @pl.when(n == 0)
        def _():
            acc[...] = jnp.zeros_like(acc)
            sacc[...] = jnp.zeros_like(sacc)
            dbias_ref[...] = jnp.zeros_like(dbias_ref)
            dsink_ref[...] = jnp.zeros_like(dsink_ref)

        @pl.when(jnp.logical_and(n >= 1, n < nblk))
        def _():
            acc[(n + 1) % 3] = jnp.zeros((BLOCK, 2 * KV_WIDTH), F32)

        @pl.when(n < nblk)
        def _():
            q = _cat(q_refs)
            do = do_ref[...]
            kv = jnp.concatenate([kvp[...], kvc[...], kvn[...]], axis=0)
            valid = _attn_mask(n, s)
            for kh in range(N_KV_HEADS):
                qs = _stack_heads(q, kh)
                dos = _stack_heads(do, kh)
                k = kv[:, kh * HEAD_DIM:(kh + 1) * HEAD_DIM]
                v = kv[:, KV_WIDTH + kh * HEAD_DIM: KV_WIDTH + (kh + 1) * HEAD_DIM]
                p, ps = _softmax_parts(qs, k, bias_ref[kh], valid, _sink_column(sink_ref, kh))
                dp = lax.dot_general(dos, v, (((1,), (1,)), ((), ())), preferred_element_type=F32)
                delta = jnp.sum(p * dp, axis=1, keepdims=True)
                ds = p * (dp - delta)
                dbias_ref[kh] += ds
                sacc[kh] += -ps * delta
                dsb = ds.astype(BF16)
                dq = jnp.dot(dsb, k, preferred_element_type=F32) * scale
                for g in range(GQA_GROUP):
                    h = kh * GQA_GROUP + g
                    dq_ref[:, h * HEAD_DIM:(h + 1) * HEAD_DIM] = dq[g * BLOCK:(g + 1) * BLOCK].astype(BF16)
                dk = lax.dot_general(dsb, qs, (((0,), (0,)), ((), ())), preferred_element_type=F32) * scale
                dv = lax.dot_general(p.astype(BF16), dos, (((0,), (0,)), ((), ())), preferred_element_type=F32)
                for j in range(3):
                    slot = (n + 2 + j) % 3
                    acc[slot, :, kh * HEAD_DIM:(kh + 1) * HEAD_DIM] += dk[j * BLOCK:(j + 1) * BLOCK]
                    acc[slot, :, KV_WIDTH + kh * HEAD_DIM: KV_WIDTH + (kh + 1) * HEAD_DIM] += dv[j * BLOCK:(j + 1) * BLOCK]

        dkv_ref[...] = acc[(n + 2) % 3].astype(BF16)

        @pl.when(n == nblk)
        def _():
            lane = lax.broadcasted_iota(jnp.int32, (1, LANE), 1)
            out = jnp.zeros((1, LANE), F32)
            for kh in range(N_KV_HEADS):
                col = sacc[kh]
                for g in range(GQA_GROUP):
                    out = jnp.where(lane == kh * GQA_GROUP + g, jnp.sum(col[g * BLOCK:(g + 1) * BLOCK]), out)
            dsink_ref[...] = out

    qi = lambda n: jnp.minimum(n, nblk - 1)
    in_specs = _col_specs(0, ATTN_WIDTH, BLOCK, qi)
    in_specs += [pl.BlockSpec((BLOCK, COL_TILE), lambda n: (jnp.maximum(qi(n) - 1, 0), kv_col)),
                 pl.BlockSpec((BLOCK, COL_TILE), lambda n: (qi(n), kv_col)),
                 pl.BlockSpec((BLOCK, COL_TILE), lambda n: (jnp.minimum(qi(n) + 1, nblk - 1), kv_col)),
                 pl.BlockSpec((BLOCK, ATTN_WIDTH), lambda n: (qi(n), 0)),
                 _const_spec((N_KV_HEADS, rows, SPAN)),
                 pl.BlockSpec(memory_space=pltpu.SMEM)]
    return pl.pallas_call(
        body, name="attn_backward", grid=(nblk + 1,),
        out_shape=(jax.ShapeDtypeStruct((s, ATTN_WIDTH), BF16), jax.ShapeDtypeStruct((s, 2 * KV_WIDTH), BF16),
                   jax.ShapeDtypeStruct((N_KV_HEADS, rows, SPAN), F32), jax.ShapeDtypeStruct((1, LANE), F32)),
        in_specs=in_specs,
        out_specs=(pl.BlockSpec((BLOCK, ATTN_WIDTH), lambda n: (qi(n), 0)),
                   pl.BlockSpec((BLOCK, 2 * KV_WIDTH), lambda n: (jnp.clip(n - 1, 0, nblk - 1), 0)),
                   _const_spec((N_KV_HEADS, rows, SPAN)), _const_spec((1, LANE))),
        scratch_shapes=[pltpu.VMEM((3, BLOCK, 2 * KV_WIDTH), F32), pltpu.VMEM((N_KV_HEADS, rows, 1), F32)],
        compiler_params=_cparams(("arbitrary",), 24 << 20),
    )(*([proj] * (nq + 3)), dattn, bias2, sink)


def _pick_tile(n, cap):
    t = cap - cap % LANE
    while n % t:
        t -= LANE
    return t


def _matmul_tn(name, a, b, comm=None):
    s, m = a.shape
    _, n = b.shape
    tk = _row_tile(s, 1024)
    tm, tn = _pick_tile(m, 1024), _pick_tile(n, 1152)
    nk = s // tk

    def body(a_ref, b_ref, o_ref, acc):
        k = pl.program_id(2)

        @pl.when(k == 0)
        def _():
            acc[...] = jnp.zeros_like(acc)

        acc[...] += lax.dot_general(a_ref[...], b_ref[...], (((0,), (0,)), ((), ())), preferred_element_type=F32)

        @pl.when(k == nk - 1)
        def _():
            o_ref[...] = acc[...].astype(BF16)

    (out,), extra = _call(
        name, body, (m // tm, n // tn, nk),
        [pl.BlockSpec((tk, tm), lambda i, j, k: (k, i)), pl.BlockSpec((tk, tn), lambda i, j, k: (k, j))], (a, b),
        [jax.ShapeDtypeStruct((m, n), BF16)], [pl.BlockSpec((tm, tn), lambda i, j, k: (i, j))],
        [pltpu.VMEM((tm, tn), F32)], ("parallel", "parallel", "arbitrary"),
        2 * tk * (tm + tn) * 2 + tm * tn * (4 + 4 + 4), comm=comm)
    return out, extra


def _pool_weight_grad(pooled, dmix, s):
    _, p_w = pooled.shape
    cg = p_w // N_POOL_GROUPS
    tk = _row_tile(s, 512)
    nk = s // tk

    def body(a_ref, b_ref, o_ref, acc):
        k = pl.program_id(1)

        @pl.when(k == 0)
        def _():
            acc[...] = jnp.zeros_like(acc)

        acc[...] += lax.dot_general(a_ref[...], b_ref[...], (((0,), (0,)), ((), ())), preferred_element_type=F32)

        @pl.when(k == nk - 1)
        def _():
            o_ref[0] = acc[...].astype(BF16)

    return pl.pallas_call(
        body, name="pool_weight_grad", grid=(N_POOL_GROUPS, nk),
        out_shape=jax.ShapeDtypeStruct((N_POOL_GROUPS, cg, cg), BF16),
        in_specs=[pl.BlockSpec((tk, cg), lambda g, k: (k, g)), pl.BlockSpec((tk, cg), lambda g, k: (k, g))],
        out_specs=pl.BlockSpec((1, cg, cg), lambda g, k: (g, 0, 0)),
        scratch_shapes=[pltpu.VMEM((cg, cg), F32)],
        compiler_params=_cparams(("parallel", "arbitrary"), 16 << 20),
    )(pooled, dmix)


def _bwd_hidden(dproj, dgl, w_in, w_merge, s, d, comm=None):
    tm = _row_tile(s, 1024)
    n_in = dproj.shape[1] // COL_TILE
    n_mg = dgl.shape[1] // COL_TILE
    nk = n_in + n_mg

    def body(dp_ref, dg_ref, wi_ref, wm_ref, dh_ref):
        k = pl.program_id(1)

        @pl.when(k == 0)
        def _():
            dh_ref[...] = lax.dot_general(dp_ref[...], wi_ref[...], (((1,), (1,)), ((), ())),
                                          preferred_element_type=F32)

        @pl.when(jnp.logical_and(k > 0, k < n_in))
        def _():
            dh_ref[...] += lax.dot_general(dp_ref[...], wi_ref[...], (((1,), (1,)), ((), ())),
                                           preferred_element_type=F32)

        @pl.when(k >= n_in)
        def _():
            dh_ref[...] += lax.dot_general(dg_ref[...], wm_ref[...], (((1,), (1,)), ((), ())),
                                           preferred_element_type=F32)

    in_specs = [pl.BlockSpec((tm, COL_TILE), lambda i, k: (i, jnp.minimum(k, n_in - 1))),
                pl.BlockSpec((tm, COL_TILE), lambda i, k: (i, jnp.maximum(k - n_in, 0))),
                pl.BlockSpec((d, COL_TILE), lambda i, k: (0, jnp.minimum(k, n_in - 1))),
                pl.BlockSpec((d, COL_TILE), lambda i, k: (0, jnp.maximum(k - n_in, 0)))]
    est = 2 * 2 * (tm + d) * COL_TILE * 2 + 3 * tm * d * 4
    (dh,), extra = _call("bwd_hidden", body, (s // tm, nk), in_specs, (dproj, dgl, w_in, w_merge),
                         [jax.ShapeDtypeStruct((s, d), F32)], [pl.BlockSpec((tm, d), lambda i, k: (i, 0))], [],
                         ("parallel", "arbitrary"), est, comm=comm)
    return dh, extra


def _bwd_prenorm(dh, x, dout, mod, pre_g, s, d):
    tm = _row_tile(s, ROW_TILE)

    def body(dh_ref, x_ref, dout_ref, mod_ref, pg_ref, gx_ref, dsh_ref, dsc_ref, dpg_ref):
        i = pl.program_id(0)

        @pl.when(i == 0)
        def _():
            dsh_ref[...] = jnp.zeros_like(dsh_ref)
            dsc_ref[...] = jnp.zeros_like(dsc_ref)
            dpg_ref[...] = jnp.zeros_like(dpg_ref)

        dh = dh_ref[...]
        xv = x_ref[...]
        r = lax.rsqrt(jnp.mean(xv * xv, axis=1, keepdims=True) + EPS)
        xhat = xv * r
        pg = pg_ref[...]
        one_scale = 1.0 + mod_ref[:, d:2 * d]
        dsh_ref[...] += jnp.sum(dh, axis=0, keepdims=True)
        dsc_ref[...] += jnp.sum(dh * xhat, axis=0, keepdims=True) * pg
        dpg_ref[...] += jnp.sum(dh * xhat, axis=0, keepdims=True) * one_scale
        dxh = dh * (one_scale * pg)
        dx = r * (dxh - xhat * jnp.mean(dxh * xhat, axis=1, keepdims=True))
        gx_ref[...] = dout_ref[...] + dx

    tile = pl.BlockSpec((tm, d), lambda i: (i, 0))
    vec = _const_spec((1, d))
    return pl.pallas_call(
        body, name="bwd_prenorm", grid=(s // tm,),
        out_shape=(jax.ShapeDtypeStruct((s, d), F32),) + (jax.ShapeDtypeStruct((1, d), F32),) * 3,
        in_specs=[tile, tile, tile, _const_spec((1, 3 * d)), vec], out_specs=(tile, vec, vec, vec),
        compiler_params=_cparams(("arbitrary",), 2 * 4 * tm * d * 4 + 6 * tm * d * 4),
    )(dh, x, dout, mod, pre_g)


def _pad_lanes(v, width):
    return jnp.pad(v, ((0, 0), (0, width - v.shape[1])))


def kernel(x, c, rel_bias_table, w_ada, b_ada, pre_norm_g, post_norm_g, w_in, attn_sink, w_pool_group, pool_scale, w_branch_attn, w_branch_pool, w_merge, b_merge, w_out, loss_target, m_rel_bias_table, m_w_ada, m_b_ada, m_pre_norm_g, m_post_norm_g, m_w_in, m_attn_sink, m_w_pool_group, m_pool_scale, m_w_branch_attn, m_w_branch_pool, m_w_merge, m_b_merge, m_w_out, v_rel_bias_table, v_w_ada, v_b_ada, v_pre_norm_g, v_post_norm_g, v_w_in, v_attn_sink, v_w_pool_group, v_pool_scale, v_w_branch_attn, v_w_branch_pool, v_w_merge, v_b_merge, v_w_out):
    _, s, d = x.shape
    p_w = pool_scale.shape[-1]
    cg = p_w // N_POOL_GROUPS
    in_w = 2 * ATTN_WIDTH + 2 * KV_WIDTH + 2 * p_w
    x2, t2 = x[0], loss_target[0]
    chip = 2 * lax.axis_index("x") + lax.axis_index("y")

    specs = [_Sharded("col", (d, in_w)), _Sharded("col", (d, 2 * d)), _Sharded("col", (ATTN_WIDTH, d)),
             _Sharded("col", (p_w, d)), _Sharded("row", (d, d)), _Sharded("grp", (N_POOL_GROUPS, cg, cg))]
    shards32 = [w_in[0], w_merge[0], w_branch_attn[0], w_branch_pool[0], w_out[0],
                w_pool_group[0].reshape(N_POOL_GROUPS * cg // N_CHIPS, cg)]
    names = ["w_in", "w_merge", "w_branch_attn", "w_branch_pool", "w_out", "w_pool_group"]
    shards16 = [_cast_bf16("cast_" + nm, w) for nm, w in zip(names, shards32)]
    shards16[5] = shards16[5].reshape(N_POOL_GROUPS, cg // N_CHIPS, cg)

    sc_all = _all_gather8("gather_cond", c, 1, pre=_silu)
    m_all = _all_gather8("gather_mod", _ada_forward(sc_all, w_ada[0]), N_DEV)
    mod = _mod_finish(m_all, b_ada)

    h, (wf_in,) = _prenorm(x2, mod, pre_norm_g, comm=_GatherWeights(specs[:1], shards16[:1]))
    proj, (wf_merge,) = _matmul("proj", h, wf_in, comm=_GatherWeights(specs[1:2], shards16[1:2]))
    gates, (wf_bra, wf_brp, wf_out, wf_grp) = _matmul("merge_gates", h, wf_merge, bias=b_merge,
                                                      comm=_GatherWeights(specs[2:], shards16[2:]))
    buckets = _t5_buckets()
    bias2 = _bias_table(rel_bias_table, buckets).reshape(N_KV_HEADS, GQA_GROUP * BLOCK, SPAN)
    attn = _attn_forward(proj, bias2, attn_sink, s)
    ya, yp, za, zp, merged, pooled, mixed = _branches(proj, attn, gates, wf_bra, wf_brp, wf_grp, pool_scale, s, d)
    dout, d_o, loss_part, dgate, dpostg = _out_loss(merged, x2, t2, wf_out, post_norm_g, mod, s, d)

    pw_out, _ = _matmul_tn("grad_w_out", merged, d_o)
    (dza, dzp, dgl, dbm), (pc_out,) = _bwd_out(d_o, gates, za, zp, wf_out, s, d,
                                                comm=_ScatterGrads(specs[4:5], [pw_out]))
    pw_bra, _ = _matmul_tn("grad_w_branch_attn", ya, dza)
    pw_brp, _ = _matmul_tn("grad_w_branch_pool", yp, dzp)
    pw_merge, _ = _matmul_tn("grad_w_merge", h, dgl)
    (dattn, dga, dgp, dmix, dpooled, dps), (pc_bra, pc_brp) = _bwd_branches(
        dza, dzp, attn, proj, mixed, wf_bra, wf_brp, wf_grp, pool_scale, s, d,
        comm=_ScatterGrads(specs[2:4], [pw_bra, pw_brp]))
    pw_grp = _pool_weight_grad(pooled, dmix, s)
    du = _pool_backward(dpooled, s)
    dq, dkv, dbias, dsink = _attn_backward(proj, dattn, bias2, attn_sink, s)
    dproj = jnp.concatenate([dq, dkv, dga, du, dgp], axis=1)
    pw_in, (pc_merge,) = _matmul_tn("grad_w_in", h, dproj, comm=_ScatterGrads(specs[1:2], [pw_merge]))
    dh, (pc_in, pc_grp) = _bwd_hidden(dproj, dgl, wf_in, wf_merge, s, d,
                                      comm=_ScatterGrads([specs[0], specs[5]], [pw_in, pw_grp]))
    gx, dshift, dscale, dpreg = _bwd_prenorm(dh, x2, dout, mod, pre_norm_g, s, d)

    pieces = [pc_in, pc_merge, pc_bra, pc_brp, pc_out]
    weights = [w_in, w_merge, w_branch_attn, w_branch_pool, w_out]
    moms = [m_w_in, m_w_merge, m_w_branch_attn, m_w_branch_pool, m_w_out]
    vars_ = [v_w_in, v_w_merge, v_w_branch_attn, v_w_branch_pool, v_w_out]
    big = {}
    for nm, pc, w, m, v in zip(names, pieces, weights, moms, vars_):
        shape2 = (-1, w.shape[-1])
        res4 = _reduce_adamw("update_" + nm, pc, w.reshape(shape2), m.reshape(shape2), v.reshape(shape2))
        big[nm] = tuple(a.reshape(w.shape) for a in res4)
    hq = cg // N_CHIPS // 2
    g_grp = _reduce16("reduce_w_pool_group", pc_grp.reshape(2, N_DEV, N_POOL_GROUPS * hq, cg))
    g_grp = g_grp.reshape(2, N_POOL_GROUPS, hq, cg).transpose(1, 0, 2, 3).reshape(N_POOL_GROUPS * 2 * hq, cg)
    res3 = _adamw("adamw_w_pool_group", w_pool_group.reshape(-1, cg), g_grp, m_w_pool_group.reshape(-1, cg),
                  v_w_pool_group.reshape(-1, cg))
    big["w_pool_group"] = tuple(a.reshape(w_pool_group.shape) for a in (g_grp,) + tuple(res3))

    dtable = _bias_table_grad(dbias.reshape(N_Q_HEADS, BLOCK, SPAN), buckets)[:, :N_Q_HEADS]
    segs = [("b_ada", jnp.concatenate([dshift, dscale, dgate], axis=1), 3 * d),
            ("pre_norm_g", dpreg, d), ("post_norm_g", dpostg, d), ("attn_sink", dsink, LANE),
            ("pool_scale", dps, p_w), ("b_merge", dbm, 2 * d), ("rel_bias_table", dtable.reshape(1, -1), 2 * LANE)]
    packed = jnp.concatenate([_pad_lanes(v, w) for _, v, w in segs], axis=1)
    rows = _all_gather8("gather_small", packed, 1)[:, 0, :]

    def pack(vals):
        return jnp.concatenate([_pad_lanes(v.reshape(1, -1), w) for v, (_, _, w) in zip(vals, segs)], axis=1)

    small_w = [b_ada, pre_norm_g, post_norm_g, attn_sink, pool_scale, b_merge, rel_bias_table]
    small_m = [m_b_ada, m_pre_norm_g, m_post_norm_g, m_attn_sink, m_pool_scale, m_b_merge, m_rel_bias_table]
    small_v = [v_b_ada, v_pre_norm_g, v_post_norm_g, v_attn_sink, v_pool_scale, v_b_merge, v_rel_bias_table]
    g_small, d_small, nm_small, nv_small = _small_update(rows, pack(small_w), pack(small_m), pack(small_v))
    small = {}
    off = 0
    for (nm, _, w), ref in zip(segs, small_w):
        cut = lambda a: a[:, off:off + ref.size].reshape(ref.shape)
        small[nm] = (cut(g_small), cut(d_small), cut(nm_small), cut(nv_small))
        off += w

    dmod_cols = lax.dynamic_slice_in_dim(rows[:, :3 * d], chip * (3 * d // N_CHIPS), 3 * d // N_CHIPS, axis=1)
    sc_t = sc_all[:, 0, :].T
    g_ada, d_ada, nm_ada, nv_ada = _ada_backward(sc_t, dmod_cols, w_ada[0], m_w_ada[0], v_w_ada[0])
    big["w_ada"] = tuple(a.reshape(w_ada.shape) for a in (g_ada, d_ada, nm_ada, nv_ada))

    loss = lax.psum(loss_part[0, 0], ("x", "y", "c"))
    order = ["rel_bias_table", "w_ada", "b_ada", "pre_norm_g", "post_norm_g", "w_in", "attn_sink", "w_pool_group",
             "pool_scale", "w_branch_attn", "w_branch_pool", "w_merge", "b_merge", "w_out"]
    res = {**big, **small}
    outs = [loss, gx.reshape(x.shape)]
    for part in range(4):
        outs += [res[nm][part] for nm in order]
    return tuple(outs)


def _small_update(rows, w, m, v):
    _, n = rows.shape

    def body(r_ref, w_ref, m_ref, v_ref, g_ref, d_ref, nm_ref, nv_ref):
        g = r_ref[0:1, :]
        for k in range(1, N_DEV):
            g = g + r_ref[k:k + 1, :]
        dl, nm, nv = _adam_math(w_ref[...], g, m_ref[...], v_ref[...])
        g_ref[...] = g
        d_ref[...] = dl
        nm_ref[...] = nm
        nv_ref[...] = nv

    vm = pl.BlockSpec(memory_space=pltpu.VMEM)
    out = jax.ShapeDtypeStruct((1, n), F32)
    return pl.pallas_call(
        body, name="small_update", out_shape=(out,) * 4, in_specs=[vm] * 4, out_specs=(vm,) * 4,
    )(rows, w, m, v)
```

```python
import functools
import math

import numpy as np
import jax
import jax.numpy as jnp
from jax import lax
from jax.experimental import pallas as pl
from jax.experimental.pallas import tpu as pltpu

F32 = jnp.float32
BF16 = jnp.bfloat16
MESH = pl.DeviceIdType.MESH

HEAD_DIM = 128
N_Q_HEADS = 8
N_KV_HEADS = 2
GQA_GROUP = N_Q_HEADS // N_KV_HEADS
ATTN_WIDTH = N_Q_HEADS * HEAD_DIM
KV_WIDTH = N_KV_HEADS * HEAD_DIM
WINDOW = 128
BLOCK = 128
SPAN = BLOCK + 2 * WINDOW
N_BUCKETS = 32
MAX_DISTANCE = 128
POOL_SIZES = (2, 4, 8, 16)
N_POOL_GROUPS = len(POOL_SIZES)
HALO = 16
EPS = 1e-6
NEG_INF = -1e30
ADAM_LR = 0.001
ADAM_B1 = 0.9
ADAM_B2 = 0.999
ADAM_EPS = 1e-08
ADAM_WD = 0.01
ADAM_STEP = 10

N_DEV = 8
N_CHIPS = 4
LANE = 128
COL_TILE = 512
VMEM_CAP = 60000 * 1024
ROW_TILE = 256


def _cparams(sem, est_bytes):
    limit = int(min(max(est_bytes * 5 // 4 + (4 << 20), 16 << 20), VMEM_CAP))
    return pltpu.CompilerParams(dimension_semantics=sem, vmem_limit_bytes=limit)


def _sigmoid(x):
    return jax.nn.sigmoid(x)


def _silu(x):
    return x * _sigmoid(x)


def _dsilu(x):
    s = _sigmoid(x)
    return s * (1.0 + x * (1.0 - s))


def _place():
    x, y, c = lax.axis_index("x"), lax.axis_index("y"), lax.axis_index("c")
    return x, y, c


def _flip(v, bit):
    return (1 - v) if bit else v


def _xor_peer(k):
    x, y, c = _place()
    return (_flip(x, (k >> 2) & 1), _flip(y, (k >> 1) & 1), _flip(c, k & 1))


def _resident(shape):
    nd = len(shape)
    return pl.BlockSpec(shape, lambda *_: (0,) * nd, pipeline_mode=pl.Buffered(1))


def _const_spec(shape):
    nd = len(shape)
    return pl.BlockSpec(shape, lambda *_: (0,) * nd)


CHUNK_BYTES = 256 << 10
MAX_CHUNKS = 32


def _all_gather8(name, x, nrows, pre=None):
    r, n = x.shape

    def body(x_ref, out_ref, stage, send_sems, recv_sems):
        px, py, pc = _place()
        me = 4 * px + 2 * py + pc
        v = x_ref[...]
        if pre is not None:
            v = pre(v)
        stage[...] = v[0:nrows]
        out_ref[me] = v[0:nrows]
        copies = []
        for k in range(1, N_DEV):
            cp = pltpu.make_async_remote_copy(
                src_ref=stage, dst_ref=out_ref.at[me], send_sem=send_sems.at[k - 1], recv_sem=recv_sems.at[k - 1],
                device_id=_xor_peer(k), device_id_type=MESH)
            cp.start()
            copies.append(cp)
        for cp in copies:
            cp.wait()

    return pl.pallas_call(
        body, name=name,
        out_shape=jax.ShapeDtypeStruct((N_DEV, nrows, n), F32),
        in_specs=[pl.BlockSpec(memory_space=pltpu.VMEM)],
        out_specs=pl.BlockSpec(memory_space=pltpu.VMEM),
        scratch_shapes=[pltpu.VMEM((nrows, n), F32), pltpu.SemaphoreType.DMA((N_DEV - 1,)),
                        pltpu.SemaphoreType.DMA((N_DEV - 1,))],
    )(x)


class _Sharded:
    def __init__(self, kind, full_shape):
        self.kind = kind
        self.full_shape = tuple(full_shape)
        if kind == "col":
            r, c = full_shape
            self.shard_shape = (r, c // N_CHIPS)
        elif kind == "row":
            r, c = full_shape
            self.shard_shape = (r // N_CHIPS, c)
        else:
            g, r, c = full_shape
            self.shard_shape = (g, r // N_CHIPS, c)
        self.axis = 1 if kind == "grp" else 0
        s = list(self.shard_shape)
        s[self.axis] //= 2
        self.piece_shape = tuple(s)

    def _rows(self, ref, start, size):
        idx = (slice(None),) * self.axis + (pl.ds(pl.multiple_of(start, 16), size),)
        return ref.at[idx]

    def shard_half(self, ref, hc):
        h = self.piece_shape[self.axis]
        return self._rows(ref, hc * h, h)

    def window(self, ref, chip, hc=None):
        s = self.shard_shape
        if self.kind == "col":
            cols = pl.ds(pl.multiple_of(chip * s[1], LANE), s[1])
            if hc is None:
                return ref.at[:, cols]
            h = s[0] // 2
            return ref.at[pl.ds(pl.multiple_of(hc * h, 16), h), cols]
        n = s[self.axis]
        if hc is None:
            return self._rows(ref, chip * n, n)
        return self._rows(ref, chip * n + hc * (n // 2), n // 2)

    def chunks(self, view, shape, itemsize):
        rows = shape[self.axis]
        nbytes = math.prod(shape) * itemsize
        n = 1
        while 2 * n <= MAX_CHUNKS and nbytes // (2 * n) >= CHUNK_BYTES and rows % (2 * n * 16) == 0:
            n *= 2
        h = rows // n
        return [view.at[(slice(None),) * self.axis + (pl.ds(j * h, h),)] for j in range(n)]


def _remote(src, dst, send_sem, recv_sem, to):
    return pltpu.make_async_remote_copy(src_ref=src, dst_ref=dst, send_sem=send_sem, recv_sem=recv_sem,
                                        device_id=to, device_id_type=MESH)


def _start_remote(sp, src, dst, shape, itemsize, send_sem, recv_sem, to):
    for s_part, d_part in zip(sp.chunks(src, shape, itemsize), sp.chunks(dst, shape, itemsize)):
        _remote(s_part, d_part, send_sem, recv_sem, to).start()


def _start_local(sp, src, dst, shape, itemsize, sem):
    for s_part, d_part in zip(sp.chunks(src, shape, itemsize), sp.chunks(dst, shape, itemsize)):
        pltpu.make_async_copy(s_part, d_part, sem).start()


class _GatherWeights:
    def __init__(self, specs, shards):
        self.specs = specs
        self.inputs = list(shards)
        self.out_shapes = [jax.ShapeDtypeStruct(sp.full_shape, BF16) for sp in specs]
        nw = len(specs)
        self.scratch = [pltpu.SemaphoreType.DMA((6 * nw,)), pltpu.SemaphoreType.DMA((6 * nw,)),
                        pltpu.SemaphoreType.DMA((nw,))]

    def phases(self, nsteps):
        return [(0, self.start), (max(1, (7 * nsteps) // 10) if nsteps > 1 else 0, self.middle),
                (nsteps - 1, self.end)]

    def _ctx(self):
        x, y, c = _place()
        return x, y, c, 2 * x + y, (x, y, 1 - c), [(1 - x, y), (x, 1 - y), (1 - x, 1 - y)]

    def start(self, shard_refs, full_refs, sems):
        send_sems, recv_sems, local_sems = sems
        x, y, c, my_chip, sibling, chips = self._ctx()
        for w, sp in enumerate(self.specs):
            _start_local(sp, shard_refs[w], sp.window(full_refs[w], my_chip), sp.shard_shape, 2, local_sems.at[w])
            for t, (cx, cy) in enumerate(chips):
                _start_remote(sp, sp.shard_half(shard_refs[w], c), sp.window(full_refs[w], my_chip, c),
                              sp.piece_shape, 2, send_sems.at[6 * w + t], recv_sems.at[6 * w + t], (cx, cy, c))

    def middle(self, shard_refs, full_refs, sems):
        send_sems, recv_sems, local_sems = sems
        x, y, c, my_chip, sibling, chips = self._ctx()
        for w, sp in enumerate(self.specs):
            for t, (cx, cy) in enumerate(chips):
                landed = sp.window(full_refs[w], 2 * cx + cy, c)
                _remote(landed, landed, send_sems.at[6 * w + t], recv_sems.at[6 * w + t], (cx, cy, c)).wait_recv()
                _start_remote(sp, landed, landed, sp.piece_shape, 2, send_sems.at[6 * w + 3 + t],
                              recv_sems.at[6 * w + 3 + t], sibling)

    def end(self, shard_refs, full_refs, sems):
        send_sems, recv_sems, local_sems = sems
        x, y, c, my_chip, sibling, chips = self._ctx()
        for w, sp in enumerate(self.specs):
            for t, (cx, cy) in enumerate(chips):
                other = sp.window(full_refs[w], 2 * cx + cy, 1 - c)
                _remote(other, other, send_sems.at[6 * w + 3 + t], recv_sems.at[6 * w + 3 + t], sibling).wait_recv()
        for w, sp in enumerate(self.specs):
            for t, (cx, cy) in enumerate(chips):
                mine = sp.shard_half(shard_refs[w], c)
                _remote(mine, mine, send_sems.at[6 * w + t], recv_sems.at[6 * w + t], (cx, cy, c)).wait_send()
                landed = sp.window(full_refs[w], 2 * cx + cy, c)
                _remote(landed, landed, send_sems.at[6 * w + 3 + t], recv_sems.at[6 * w + 3 + t], sibling).wait_send()
            pltpu.make_async_copy(shard_refs[w], sp.window(full_refs[w], my_chip), local_sems.at[w]).wait()


class _ScatterGrads:
    def __init__(self, specs, partials):
        self.specs = specs
        self.inputs = list(partials)
        self.out_shapes = [jax.ShapeDtypeStruct((2, N_DEV) + sp.piece_shape, BF16) for sp in specs]
        nw = len(specs)
        self.scratch = [pltpu.SemaphoreType.DMA((15 * nw,)), pltpu.SemaphoreType.DMA((15 * nw,)),
                        pltpu.SemaphoreType.DMA((nw,))]

    def phases(self, nsteps):
        return [(0, self.start), (max(1, (7 * nsteps) // 10), self.middle), (nsteps - 1, self.end)]

    def _own(self, sp, part_ref, recv_ref, x, y, c, sem):
        return pltpu.make_async_copy(sp.window(part_ref, 2 * x + y, c), recv_ref.at[c, 0], sem)

    def start(self, part_refs, recv_refs, sems):
        send_sems, recv_sems, local_sems = sems
        x, y, c = _place()
        for w, sp in enumerate(self.specs):
            _start_local(sp, sp.window(part_refs[w], 2 * x + y, c), recv_refs[w].at[c, 0], sp.piece_shape, 2,
                         local_sems.at[w])
            for k in range(1, N_DEV):
                px, py, pc = _xor_peer(k)
                _start_remote(sp, sp.window(part_refs[w], 2 * px + py, pc), recv_refs[w].at[pc, k], sp.piece_shape,
                              2, send_sems.at[15 * w + k - 1], recv_sems.at[15 * w + k - 1], (px, py, pc))

    def middle(self, part_refs, recv_refs, sems):
        send_sems, recv_sems, local_sems = sems
        x, y, c = _place()
        sibling = (x, y, 1 - c)
        for w, sp in enumerate(self.specs):
            self._own(sp, part_refs[w], recv_refs[w], x, y, c, local_sems.at[w]).wait()
            for k in range(N_DEV):
                landed = recv_refs[w].at[c, k]
                if k:
                    _remote(landed, landed, send_sems.at[15 * w + k - 1], recv_sems.at[15 * w + k - 1],
                            sibling).wait_recv()
                _start_remote(sp, landed, landed, sp.piece_shape, 2, send_sems.at[15 * w + 7 + k],
                              recv_sems.at[15 * w + 7 + k], sibling)

    def end(self, part_refs, recv_refs, sems):
        send_sems, recv_sems, local_sems = sems
        x, y, c = _place()
        sibling = (x, y, 1 - c)
        for w, sp in enumerate(self.specs):
            for k in range(N_DEV):
                other = recv_refs[w].at[1 - c, k]
                _remote(other, other, send_sems.at[15 * w + 7 + k], recv_sems.at[15 * w + 7 + k], sibling).wait_recv()
            for k in range(N_DEV):
                landed = recv_refs[w].at[c, k]
                _remote(landed, landed, send_sems.at[15 * w + 7 + k], recv_sems.at[15 * w + 7 + k],
                        sibling).wait_send()
                if k:
                    px, py, pc = _xor_peer(k)
                    sent = sp.window(part_refs[w], 2 * px + py, pc)
                    _remote(sent, sent, send_sems.at[15 * w + k - 1], recv_sems.at[15 * w + k - 1],
                            (px, py, pc)).wait_send()


def _call(name, body, grid, in_specs, args, out_shape, out_specs, scratch, semantics, est_bytes, comm=None,
          aliases=None):
    out_shape, out_specs = tuple(out_shape), tuple(out_specs)
    if comm is None:
        res = pl.pallas_call(body, name=name, grid=grid, out_shape=out_shape, in_specs=list(in_specs),
                             out_specs=out_specs, scratch_shapes=list(scratch), input_output_aliases=aliases or {},
                             compiler_params=_cparams(semantics, est_bytes))(*args)
        return tuple(res), ()
    n_in, n_out, n_sc = len(in_specs), len(out_shape), len(scratch)
    c_in, c_out = len(comm.inputs), len(comm.out_shapes)
    nsteps = math.prod(grid)
    phases = comm.phases(nsteps)

    def hosted(*refs):
        pos = [0]

        def take(n):
            part = refs[pos[0]:pos[0] + n]
            pos[0] += n
            return part

        ins, cins, outs, couts, scr, sems = take(n_in), take(c_in), take(n_out), take(c_out), take(n_sc), take(3)
        step = 0
        for ax, extent in enumerate(grid):
            step = step * extent + pl.program_id(ax)
        for at, fn in phases:
            if at == 0:
                pl.when(step == 0)(functools.partial(fn, cins, couts, sems))
        body(*ins, *outs, *scr)
        for at, fn in phases:
            if at > 0:
                pl.when(step == at)(functools.partial(fn, cins, couts, sems))

    any_spec = pl.BlockSpec(memory_space=pl.ANY)
    res = pl.pallas_call(
        hosted, name=name, grid=grid, out_shape=out_shape + tuple(comm.out_shapes),
        in_specs=list(in_specs) + [any_spec] * c_in, out_specs=out_specs + (any_spec,) * c_out,
        scratch_shapes=list(scratch) + list(comm.scratch), input_output_aliases=aliases or {},
        compiler_params=_cparams(("arbitrary",) * len(grid), est_bytes))(*args, *comm.inputs)
    return tuple(res[:n_out]), tuple(res[n_out:])


def _row_tile(rows, cap):
    for t in range(min(rows, cap), 0, -1):
        if rows % t == 0 and (t % 16 == 0 or t == rows):
            return t
    return rows


def _cast_bf16(name, x):
    r, c = x.shape
    tr = _row_tile(r, 512)

    def body(x_ref, o_ref):
        o_ref[...] = x_ref[...].astype(BF16)

    return pl.pallas_call(
        body, name=name, grid=(r // tr,), out_shape=jax.ShapeDtypeStruct((r, c), BF16),
        in_specs=[pl.BlockSpec((tr, c), lambda i: (i, 0))], out_specs=pl.BlockSpec((tr, c), lambda i: (i, 0)),
        compiler_params=_cparams(("parallel",), 2 * tr * c * 6),
    )(x)


def _adam_math(w, g, m, v):
    m = ADAM_B1 * m + (1.0 - ADAM_B1) * g
    v = ADAM_B2 * v + (1.0 - ADAM_B2) * (g * g)
    m_hat = m / (1.0 - ADAM_B1 ** ADAM_STEP)
    v_hat = v / (1.0 - ADAM_B2 ** ADAM_STEP)
    delta = -ADAM_LR * (m_hat / (jnp.sqrt(v_hat) + ADAM_EPS) + ADAM_WD * w)
    return delta, m, v


def _adamw(name, w, g, m, v):
    r, c = w.shape
    tr = _row_tile(r, max(8, (1 << 18) // c))

    def body(w_ref, g_ref, m_ref, v_ref, d_ref, nm_ref, nv_ref):
        d, nm, nv = _adam_math(w_ref[...], g_ref[...], m_ref[...], v_ref[...])
        d_ref[...] = d
        nm_ref[...] = nm
        nv_ref[...] = nv

    spec = pl.BlockSpec((tr, c), lambda i: (i, 0))
    out = jax.ShapeDtypeStruct((r, c), F32)
    return pl.pallas_call(
        body, name=name, grid=(r // tr,), out_shape=(out, out, out), in_specs=[spec] * 4, out_specs=(spec,) * 3,
        compiler_params=_cparams(("parallel",), 2 * 7 * tr * c * 4),
    )(w, g, m, v)


def _sum_pieces(x_ref):
    acc = x_ref[0, 0].astype(F32)
    for k in range(1, N_DEV):
        acc = acc + x_ref[0, k].astype(F32)
    return acc


def _reduce16(name, x):
    _, _, r, c = x.shape
    tr = _row_tile(r, max(16, (1 << 17) // c))
    nt = r // tr

    def body(x_ref, o_ref):
        o_ref[...] = _sum_pieces(x_ref)

    return pl.pallas_call(
        body, name=name, grid=(2, nt), out_shape=jax.ShapeDtypeStruct((2 * r, c), F32),
        in_specs=[pl.BlockSpec((1, N_DEV, tr, c), lambda hf, i: (hf, 0, i, 0))],
        out_specs=pl.BlockSpec((tr, c), lambda hf, i: (hf * nt + i, 0)),
        compiler_params=_cparams(("parallel", "parallel"), 2 * (N_DEV * 2 + 4) * tr * c),
    )(x)


def _reduce_adamw(name, x, w, m, v):
    _, _, r, c = x.shape
    tr = _row_tile(r, max(16, (1 << 17) // c))
    nt = r // tr

    def body(x_ref, w_ref, m_ref, v_ref, g_ref, d_ref, nm_ref, nv_ref):
        g = _sum_pieces(x_ref)
        d, nm, nv = _adam_math(w_ref[...], g, m_ref[...], v_ref[...])
        g_ref[...] = g
        d_ref[...] = d
        nm_ref[...] = nm
        nv_ref[...] = nv

    tile = pl.BlockSpec((tr, c), lambda hf, i: (hf * nt + i, 0))
    out = jax.ShapeDtypeStruct((2 * r, c), F32)
    return pl.pallas_call(
        body, name=name, grid=(2, nt), out_shape=(out,) * 4,
        in_specs=[pl.BlockSpec((1, N_DEV, tr, c), lambda hf, i: (hf, 0, i, 0)), tile, tile, tile],
        out_specs=(tile,) * 4,
        compiler_params=_cparams(("parallel", "parallel"), 2 * (N_DEV * 2 + 7 * 4) * tr * c),
    )(x, w, m, v)


def _t5_buckets():
    rel = jnp.arange(SPAN)[None, :] - WINDOW - jnp.arange(BLOCK)[:, None]
    half = N_BUCKETS // 2
    max_exact = half // 2
    ret = jnp.where(rel > 0, half, 0)
    n = jnp.abs(rel)
    nf = jnp.maximum(n, 1).astype(F32)
    large = max_exact + (jnp.log(nf / max_exact) / math.log(MAX_DISTANCE / max_exact)
                         * (half - max_exact)).astype(jnp.int32)
    large = jnp.minimum(large, half - 1)
    return (ret + jnp.where(n < max_exact, n, large)).astype(jnp.int32)


def _bias_table(table, buckets):
    def body(t_ref, b_ref, o_ref):
        bk = b_ref[...]
        rel = (lax.broadcasted_iota(jnp.int32, (BLOCK, SPAN), 1) - WINDOW
               - lax.broadcasted_iota(jnp.int32, (BLOCK, SPAN), 0))
        band = jnp.abs(rel) <= WINDOW
        for h in range(N_Q_HEADS):
            acc = jnp.zeros((BLOCK, SPAN), F32)
            for b in range(N_BUCKETS):
                acc = jnp.where(bk == b, t_ref[b, h], acc)
            o_ref[h] = jnp.where(band, acc, NEG_INF)

    return pl.pallas_call(
        body, name="bias_table", out_shape=jax.ShapeDtypeStruct((N_Q_HEADS, BLOCK, SPAN), F32),
        in_specs=[pl.BlockSpec(memory_space=pltpu.SMEM), pl.BlockSpec(memory_space=pltpu.VMEM)],
        out_specs=pl.BlockSpec(memory_space=pltpu.VMEM),
    )(table, buckets)


def _bias_table_grad(dbias, buckets):
    def body(d_ref, b_ref, o_ref):
        bk = b_ref[...]
        row = lax.broadcasted_iota(jnp.int32, (N_BUCKETS, LANE), 0)
        lane = lax.broadcasted_iota(jnp.int32, (N_BUCKETS, LANE), 1)
        acc = jnp.zeros((N_BUCKETS, LANE), F32)
        for h in range(N_Q_HEADS):
            d = d_ref[h]
            for b in range(N_BUCKETS):
                s = jnp.sum(jnp.where(bk == b, d, 0.0))
                acc = jnp.where((row == b) & (lane == h), s, acc)
        o_ref[...] = acc

    return pl.pallas_call(
        body, name="bias_table_grad", out_shape=jax.ShapeDtypeStruct((N_BUCKETS, LANE), F32),
        in_specs=[pl.BlockSpec(memory_space=pltpu.VMEM), pl.BlockSpec(memory_space=pltpu.VMEM)],
        out_specs=pl.BlockSpec(memory_space=pltpu.VMEM),
    )(dbias, buckets)


def _ada_forward(sc_all, w_ada):
    d, n = w_ada.shape
    tn = _pick_tile(n, COL_TILE)

    def body(sc_ref, w_ref, o_ref):
        row = lax.broadcasted_iota(jnp.int32, (N_DEV, d), 0)
        sc = jnp.zeros((N_DEV, d), F32)
        for k in range(N_DEV):
            sc = jnp.where(row == k, sc_ref[k], sc)
        o_ref[...] = jnp.dot(sc, w_ref[...], preferred_element_type=F32, precision=lax.Precision.HIGHEST)

    return pl.pallas_call(
        body, name="ada_forward", grid=(n // tn,), out_shape=jax.ShapeDtypeStruct((N_DEV, n), F32),
        in_specs=[_const_spec((N_DEV, 1, d)), pl.BlockSpec((d, tn), lambda j: (0, j))],
        out_specs=pl.BlockSpec((N_DEV, tn), lambda j: (0, j)),
        compiler_params=_cparams(("parallel",), 2 * d * tn * 4 + N_DEV * N_DEV * d * 8),
    )(sc_all, w_ada)


def _mod_finish(m_all, b_ada):
    _, _, n = m_all.shape

    def body(m_ref, b_ref, o_ref):
        x, y, c = _place()
        me = 4 * x + 2 * y + c
        row = lax.broadcasted_iota(jnp.int32, (N_DEV, n), 0)
        for j in range(N_CHIPS):
            blk = m_ref[2 * j]
            mine = jnp.sum(jnp.where(row == me, blk, 0.0), axis=0, keepdims=True)
            o_ref[:, j * n:(j + 1) * n] = mine + b_ref[:, j * n:(j + 1) * n]

    return pl.pallas_call(
        body, name="mod_finish", out_shape=jax.ShapeDtypeStruct((1, N_CHIPS * n), F32),
        in_specs=[pl.BlockSpec(memory_space=pltpu.VMEM), pl.BlockSpec(memory_space=pltpu.VMEM)],
        out_specs=pl.BlockSpec(memory_space=pltpu.VMEM),
    )(m_all, b_ada)


def _ada_backward(sc_t, dmod_cols, w, m, v):
    d, n = w.shape
    tr, tn = _row_tile(d, 512), _pick_tile(n, COL_TILE)

    def body(s_ref, dm_ref, w_ref, m_ref, v_ref, g_ref, d_ref, nm_ref, nv_ref):
        g = jnp.dot(s_ref[...], dm_ref[...], preferred_element_type=F32, precision=lax.Precision.HIGHEST)
        dl, nm, nv = _adam_math(w_ref[...], g, m_ref[...], v_ref[...])
        g_ref[...] = g
        d_ref[...] = dl
        nm_ref[...] = nm
        nv_ref[...] = nv

    tile = pl.BlockSpec((tr, tn), lambda i, j: (i, j))
    out = jax.ShapeDtypeStruct((d, n), F32)
    return pl.pallas_call(
        body, name="ada_backward", grid=(d // tr, n // tn), out_shape=(out,) * 4,
        in_specs=[pl.BlockSpec((tr, N_DEV), lambda i, j: (i, 0)), pl.BlockSpec((N_DEV, tn), lambda i, j: (0, j)),
                  tile, tile, tile],
        out_specs=(tile,) * 4,
        compiler_params=_cparams(("parallel", "parallel"), 2 * 8 * tr * tn * 4),
    )(sc_t, dmod_cols, w, m, v)


def _prenorm(x, mod, pre_g):
    s, d = x.shape
    tm = _row_tile(s, 512)

    def body(x_ref, mod_ref, g_ref, h_ref):
        xv = x_ref[...]
        r = lax.rsqrt(jnp.mean(xv * xv, axis=1, keepdims=True) + EPS)
        xn = xv * r * g_ref[...]
        h_ref[...] = (xn * (1.0 + mod_ref[:, d:2 * d]) + mod_ref[:, 0:d]).astype(BF16)

    return pl.pallas_call(
        body, name="prenorm", grid=(s // tm,), out_shape=jax.ShapeDtypeStruct((s, d), BF16),
        in_specs=[pl.BlockSpec((tm, d), lambda i: (i, 0)), _const_spec((1, 3 * d)), _const_spec((1, d))],
        out_specs=pl.BlockSpec((tm, d), lambda i: (i, 0)),
        compiler_params=_cparams(("parallel",), 2 * tm * d * 6 + 4 * tm * d * 4),
    )(x, mod, pre_g)


def _chip():
    return 2 * lax.axis_index("x") + lax.axis_index("y")


def _local_columns(h, w_in_shard, w_merge_shard, b_merge, comm):
    s, k = h.shape
    n1, n2 = w_in_shard.shape[1], w_merge_shard.shape[1]
    tm = _row_tile(s, 1024)

    def body(h_ref, wi_ref, wm_ref, b_ref, p_ref, g_ref):
        hv = h_ref[...]
        p_ref[...] = jnp.dot(hv, wi_ref[...], preferred_element_type=F32).astype(BF16)
        g_ref[...] = _sigmoid(jnp.dot(hv, wm_ref[...], preferred_element_type=F32) + b_ref[...]).astype(BF16)

    in_specs = [pl.BlockSpec((tm, k), lambda i: (i, 0)), _resident((k, n1)), _resident((k, n2)),
                pl.BlockSpec((1, n2), lambda i: (0, _chip()))]
    out_specs = [pl.BlockSpec((tm, n1), lambda i: (i, _chip())), pl.BlockSpec((tm, n2), lambda i: (i, _chip()))]
    est = 2 * tm * (k + n1 + n2) * 2 + k * (n1 + n2) * 2 + 2 * tm * (n1 + n2) * 4
    return _call("local_columns", body, (s // tm,), in_specs, (h, w_in_shard, w_merge_shard, b_merge),
                 [jax.ShapeDtypeStruct((s, N_CHIPS * n1), BF16), jax.ShapeDtypeStruct((s, N_CHIPS * n2), BF16)],
                 out_specs, [], ("arbitrary",), est, comm=comm)


def _other_columns(name, a, b, partial, bias=None, comm=None):
    s, k = a.shape
    _, n = b.shape
    tm, tn = _row_tile(s, 1024), n // N_CHIPS
    col = lambda i, j: (_chip() + 1 + j) % N_CHIPS

    def body(*refs):
        if bias is None:
            a_ref, b_ref, _, o_ref = refs
        else:
            a_ref, b_ref, bias_ref, _, o_ref = refs
        acc = jnp.dot(a_ref[...], b_ref[...], preferred_element_type=F32)
        if bias is not None:
            acc = _sigmoid(acc + bias_ref[...])
        o_ref[...] = acc.astype(BF16)

    in_specs = [pl.BlockSpec((tm, k), lambda i, j: (i, 0)), pl.BlockSpec((k, tn), lambda i, j: (0, col(i, j)))]
    args = [a, b]
    if bias is not None:
        in_specs.append(pl.BlockSpec((1, tn), lambda i, j: (0, col(i, j))))
        args.append(bias)
    in_specs.append(pl.BlockSpec(memory_space=pl.ANY))
    args.append(partial)
    (out,), extra = _call(name, body, (s // tm, N_CHIPS - 1), in_specs, args, [jax.ShapeDtypeStruct((s, n), BF16)],
                          [pl.BlockSpec((tm, tn), lambda i, j: (i, col(i, j)))], [], ("parallel", "arbitrary"),
                          2 * (tm * k + k * tn + tm * tn) * 2 + 2 * tm * tn * 4, comm=comm,
                          aliases={len(args) - 1: 0})
    return out, extra


def _col_specs(off, width, rows, row_index):
    assert off % COL_TILE == 0 and width % COL_TILE == 0
    return [pl.BlockSpec((rows, COL_TILE), functools.partial(lambda p, *ids: (row_index(*ids), p), off // COL_TILE + p))
            for p in range(width // COL_TILE)]


def _cat(refs):
    vals = [r[...] for r in refs]
    return vals[0] if len(vals) == 1 else jnp.concatenate(vals, axis=1)


def _attn_mask(n, s):
    kpos = (n - 1) * BLOCK + lax.broadcasted_iota(jnp.int32, (1, SPAN), 1)
    return (kpos >= 0) & (kpos < s)


def _sink_column(sink_ref, kh):
    rows = GQA_GROUP * BLOCK
    grp = lax.broadcasted_iota(jnp.int32, (rows, 1), 0) // BLOCK
    col = jnp.zeros((rows, 1), F32)
    for g in range(GQA_GROUP):
        col = jnp.where(grp == g, sink_ref[0, kh * GQA_GROUP + g], col)
    return col


def _stack_heads(x, kh):
    base = kh * GQA_GROUP * HEAD_DIM
    return jnp.concatenate([x[:, base + g * HEAD_DIM: base + (g + 1) * HEAD_DIM] for g in range(GQA_GROUP)], axis=0)


def _softmax_parts(qs, k, bias, valid, sink_col):
    sc = lax.dot_general(qs, k, (((1,), (1,)), ((), ())), preferred_element_type=F32)
    sc = sc * (HEAD_DIM ** -0.5) + bias
    sc = jnp.where(valid, sc, NEG_INF)
    mx = jnp.maximum(jnp.max(sc, axis=1, keepdims=True), sink_col)
    e = jnp.exp(sc - mx)
    es = jnp.exp(sink_col - mx)
    inv = 1.0 / (jnp.sum(e, axis=1, keepdims=True) + es)
    return e * inv, es * inv


def _attn_forward(proj, bias2, sink, s):
    nblk = s // BLOCK
    nq = ATTN_WIDTH // COL_TILE
    kv_col = ATTN_WIDTH // COL_TILE
    assert 2 * KV_WIDTH == COL_TILE

    def body(*refs):
        q_refs = refs[:nq]
        kvp, kvc, kvn, bias_ref, sink_ref, o_ref = refs[nq:]
        n = pl.program_id(0)
        q = _cat(q_refs)
        kv = jnp.concatenate([kvp[...], kvc[...], kvn[...]], axis=0)
        valid = _attn_mask(n, s)
        for kh in range(N_KV_HEADS):
            qs = _stack_heads(q, kh)
            k = kv[:, kh * HEAD_DIM:(kh + 1) * HEAD_DIM]
            v = kv[:, KV_WIDTH + kh * HEAD_DIM: KV_WIDTH + (kh + 1) * HEAD_DIM]
            p, _ = _softmax_parts(qs, k, bias_ref[kh], valid, _sink_column(sink_ref, kh))
            o = jnp.dot(p.astype(BF16), v, preferred_element_type=F32)
            for g in range(GQA_GROUP):
                h = kh * GQA_GROUP + g
                o_ref[:, h * HEAD_DIM:(h + 1) * HEAD_DIM] = o[g * BLOCK:(g + 1) * BLOCK].astype(BF16)

    in_specs = _col_specs(0, ATTN_WIDTH, BLOCK, lambda n: n)
    in_specs += [pl.BlockSpec((BLOCK, COL_TILE), lambda n: (jnp.maximum(n - 1, 0), kv_col)),
                 pl.BlockSpec((BLOCK, COL_TILE), lambda n: (n, kv_col)),
                 pl.BlockSpec((BLOCK, COL_TILE), lambda n: (jnp.minimum(n + 1, nblk - 1), kv_col)),
                 _const_spec((N_KV_HEADS, GQA_GROUP * BLOCK, SPAN)),
                 pl.BlockSpec(memory_space=pltpu.SMEM)]
    return pl.pallas_call(
        body, name="attn_forward", grid=(nblk,), out_shape=jax.ShapeDtypeStruct((s, ATTN_WIDTH), BF16),
        in_specs=in_specs, out_specs=pl.BlockSpec((BLOCK, ATTN_WIDTH), lambda n: (n, 0)),
        compiler_params=_cparams(("parallel",), 16 << 20),
    )(*([proj] * (nq + 3)), bias2, sink)


def _pool_positions(i, tm, s, width):
    pos = i * tm - HALO + lax.broadcasted_iota(jnp.int32, (tm + 2 * HALO, width), 0)
    return pos, (pos >= 0) & (pos < s)


def _pool_count(pos, w, s):
    return (jnp.minimum(pos + w // 2, s) - jnp.maximum(pos - w // 2, 0)).astype(F32)


def _halo_specs_cols(off, width, tm, s):
    per = tm // HALO
    last = s // HALO - 1
    prev = _col_specs(off, width, HALO, lambda i: jnp.maximum(i * per - 1, 0))
    nxt = _col_specs(off, width, HALO, lambda i: jnp.minimum((i + 1) * per, last))
    return prev, nxt


def _branches(proj, attn, g, w_bra, w_brp, w_grp, pool_scale, s, d):
    a_w, p_w = ATTN_WIDTH, pool_scale.shape[1]
    cg = p_w // N_POOL_GROUPS
    tm = _row_tile(s, ROW_TILE)
    off_ga = ATTN_WIDTH + 2 * KV_WIDTH
    off_u = off_ga + a_w
    off_gp = off_u + p_w
    n_ga, n_u, n_gp = a_w // COL_TILE, p_w // COL_TILE, p_w // COL_TILE

    def body(*refs):
        it = iter(refs)
        attn_ref = next(it)
        ga_refs = [next(it) for _ in range(n_ga)]
        u_refs = [next(it) for _ in range(n_u)]
        up_refs = [next(it) for _ in range(n_u)]
        un_refs = [next(it) for _ in range(n_u)]
        gp_refs = [next(it) for _ in range(n_gp)]
        g_ref, wa_ref, wp_ref, wg_ref, ps_ref = (next(it) for _ in range(5))
        ya_ref, yp_ref, za_ref, zp_ref, mg_ref, pooled_ref, mixed_ref = (next(it) for _ in range(7))
        i = pl.program_id(0)
        ya = (attn_ref[...].astype(F32) * _silu(_cat(ga_refs).astype(F32))).astype(BF16)
        ya_ref[...] = ya
        za = jnp.dot(ya, wa_ref[...], preferred_element_type=F32)
        za_ref[...] = za.astype(BF16)

        u = _cat(u_refs).astype(F32)
        ext = jnp.concatenate([_cat(up_refs).astype(F32), u, _cat(un_refs).astype(F32)], axis=0)
        pos, ok = _pool_positions(i, tm, s, cg)
        mixed = []
        for gi, w in enumerate(POOL_SIZES):
            e = jnp.where(ok, ext[:, gi * cg:(gi + 1) * cg], 0.0)
            acc = e[HALO - w // 2: HALO - w // 2 + tm]
            for dd in range(-w // 2 + 1, w // 2):
                acc = acc + e[HALO + dd: HALO + dd + tm]
            cnt = _pool_count(pos[HALO:HALO + tm], w, s)
            pooled = (acc / cnt - u[:, gi * cg:(gi + 1) * cg]).astype(BF16)
            pooled_ref[:, gi * cg:(gi + 1) * cg] = pooled
            mixed.append(jnp.dot(pooled, wg_ref[gi], preferred_element_type=F32))
        mixed = jnp.concatenate(mixed, axis=1)
        mixed_ref[...] = mixed.astype(BF16)
        yp = (mixed * ps_ref[...] * _silu(_cat(gp_refs).astype(F32))).astype(BF16)
        yp_ref[...] = yp
        zp = jnp.dot(yp, wp_ref[...], preferred_element_type=F32)
        zp_ref[...] = zp.astype(BF16)
        gate = g_ref[...].astype(F32)
        mg_ref[...] = (gate[:, :d] * za + gate[:, d:] * zp).astype(BF16)

    row = lambda i: i
    u_prev, u_next = _halo_specs_cols(off_u, p_w, tm, s)
    in_specs = [pl.BlockSpec((tm, a_w), lambda i: (i, 0))]
    in_specs += _col_specs(off_ga, a_w, tm, row) + _col_specs(off_u, p_w, tm, row) + u_prev + u_next
    in_specs += _col_specs(off_gp, p_w, tm, row)
    in_specs += [pl.BlockSpec((tm, 2 * d), lambda i: (i, 0)), _resident((a_w, d)), _resident((p_w, d)),
                 _resident((N_POOL_GROUPS, cg, cg)), _const_spec((1, p_w))]
    n_proj = n_ga + 3 * n_u + n_gp
    tile = lambda w: pl.BlockSpec((tm, w), lambda i: (i, 0))
    out_widths = (a_w, p_w, d, d, d, p_w, p_w)
    est = 2 * tm * (a_w + a_w + 2 * p_w + 2 * d + sum(out_widths)) * 2 + (a_w + p_w) * d * 2 + 6 * tm * d * 4
    return pl.pallas_call(
        body, name="branches", grid=(s // tm,),
        out_shape=tuple(jax.ShapeDtypeStruct((s, w), BF16) for w in out_widths),
        in_specs=in_specs, out_specs=tuple(tile(w) for w in out_widths),
        compiler_params=_cparams(("parallel",), est),
    )(attn, *([proj] * n_proj), g, w_bra, w_brp, w_grp, pool_scale)


def _sub_rows(tm, sub=128):
    sub = min(sub, tm)
    return [pl.ds(r * sub, sub) for r in range(tm // sub)]


def _out_loss(merged, x, target, w_out, post_g, mod, s, d):
    tm = _row_tile(s, ROW_TILE)
    nsteps = s // tm

    def body(mg_ref, x_ref, t_ref, w_ref, pg_ref, mod_ref, dout_ref, do_ref, loss_ref, dgate_ref, dpg_ref, lacc):
        i = pl.program_id(0)

        @pl.when(i == 0)
        def _():
            lacc[...] = jnp.zeros_like(lacc)
            dgate_ref[...] = jnp.zeros_like(dgate_ref)
            dpg_ref[...] = jnp.zeros_like(dpg_ref)

        pg = pg_ref[...]
        gate = mod_ref[:, 2 * d:3 * d]
        for rows in _sub_rows(tm):
            o = jnp.dot(mg_ref[rows, :], w_ref[...], preferred_element_type=F32)
            r = lax.rsqrt(jnp.mean(o * o, axis=1, keepdims=True) + EPS)
            ohat = o * r
            y = ohat * pg
            e = x_ref[rows, :] + gate * y - t_ref[rows, :]
            lacc[...] += jnp.sum(e * e, axis=0, keepdims=True)
            dout = e * (1.0 / d)
            dout_ref[rows, :] = dout
            dgate_ref[...] += jnp.sum(dout * y, axis=0, keepdims=True)
            dy = dout * gate
            dpg_ref[...] += jnp.sum(dy * ohat, axis=0, keepdims=True)
            dohat = dy * pg
            do = r * (dohat - ohat * jnp.mean(dohat * ohat, axis=1, keepdims=True))
            do_ref[rows, :] = do.astype(BF16)

        @pl.when(i == nsteps - 1)
        def _():
            loss_ref[...] = (0.5 / d) * jnp.sum(lacc[...], axis=1, keepdims=True)

    tile = pl.BlockSpec((tm, d), lambda i: (i, 0))
    vec = _const_spec((1, d))
    return pl.pallas_call(
        body, name="out_loss", grid=(nsteps,),
        out_shape=(jax.ShapeDtypeStruct((s, d), F32), jax.ShapeDtypeStruct((s, d), BF16),
                   jax.ShapeDtypeStruct((1, 1), F32), jax.ShapeDtypeStruct((1, d), F32),
                   jax.ShapeDtypeStruct((1, d), F32)),
        in_specs=[tile, tile, tile, _resident((d, d)), vec, _const_spec((1, 3 * d))],
        out_specs=(tile, tile, _const_spec((1, 1)), vec, vec),
        scratch_shapes=[pltpu.VMEM((1, d), F32)],
        compiler_params=_cparams(("arbitrary",), 2 * tm * d * (2 + 4 + 4 + 4 + 2) + d * d * 2 + 8 * tm * d * 4),
    )(merged, x, target, w_out, post_g, mod)


def _bwd_out(d_o, g, za, zp, w_out, s, d, comm=None):
    tm = _row_tile(s, ROW_TILE)

    def body(do_ref, g_ref, za_ref, zp_ref, w_ref, dza_ref, dzp_ref, dgl_ref, dbm_ref):
        i = pl.program_id(0)

        @pl.when(i == 0)
        def _():
            dbm_ref[...] = jnp.zeros_like(dbm_ref)

        dm = lax.dot_general(do_ref[...], w_ref[...], (((1,), (1,)), ((), ())), preferred_element_type=F32)
        gate = g_ref[...].astype(F32)
        ga, gp = gate[:, :d], gate[:, d:]
        dza_ref[...] = (dm * ga).astype(BF16)
        dzp_ref[...] = (dm * gp).astype(BF16)
        dla = dm * za_ref[...].astype(F32) * ga * (1.0 - ga)
        dlp = dm * zp_ref[...].astype(F32) * gp * (1.0 - gp)
        dgl_ref[:, :d] = dla.astype(BF16)
        dgl_ref[:, d:] = dlp.astype(BF16)
        dbm_ref[:, :d] += jnp.sum(dla, axis=0, keepdims=True)
        dbm_ref[:, d:] += jnp.sum(dlp, axis=0, keepdims=True)

    tile = pl.BlockSpec((tm, d), lambda i: (i, 0))
    wide = pl.BlockSpec((tm, 2 * d), lambda i: (i, 0))
    return _call("bwd_out", body, (s // tm,), [tile, wide, tile, tile, _resident((d, d))], (d_o, g, za, zp, w_out),
                 (jax.ShapeDtypeStruct((s, d), BF16), jax.ShapeDtypeStruct((s, d), BF16),
                  jax.ShapeDtypeStruct((s, 2 * d), BF16), jax.ShapeDtypeStruct((1, 2 * d), F32)),
                 (tile, tile, wide, _const_spec((1, 2 * d))), [], ("arbitrary",),
                 2 * tm * d * 2 * 9 + d * d * 2 + 8 * tm * d * 4, comm=comm)


def _bwd_branches(dza, dzp, attn, proj, mixed, w_bra, w_brp, w_grp, pool_scale, s, d, comm=None):
    a_w, p_w = ATTN_WIDTH, pool_scale.shape[1]
    cg = p_w // N_POOL_GROUPS
    tm = _row_tile(s, ROW_TILE)
    off_ga = ATTN_WIDTH + 2 * KV_WIDTH
    off_gp = off_ga + a_w + p_w
    n_ga, n_gp = a_w // COL_TILE, p_w // COL_TILE

    def body(*refs):
        it = iter(refs)
        dza_ref, dzp_ref, attn_ref = next(it), next(it), next(it)
        ga_refs = [next(it) for _ in range(n_ga)]
        gp_refs = [next(it) for _ in range(n_gp)]
        mixed_ref, wa_ref, wp_ref, wg_ref, ps_ref = (next(it) for _ in range(5))
        dattn_ref, dga_ref, dgp_ref, dmix_ref, dpool_ref, dps_ref = (next(it) for _ in range(6))
        i = pl.program_id(0)

        @pl.when(i == 0)
        def _():
            dps_ref[...] = jnp.zeros_like(dps_ref)

        dya = lax.dot_general(dza_ref[...], wa_ref[...], (((1,), (1,)), ((), ())), preferred_element_type=F32)
        ga = _cat(ga_refs).astype(F32)
        dattn_ref[...] = (dya * _silu(ga)).astype(BF16)
        dga_ref[...] = (dya * attn_ref[...].astype(F32) * _dsilu(ga)).astype(BF16)

        dyp = lax.dot_general(dzp_ref[...], wp_ref[...], (((1,), (1,)), ((), ())), preferred_element_type=F32)
        gp = _cat(gp_refs).astype(F32)
        mixed = mixed_ref[...].astype(F32)
        ps = ps_ref[...]
        sg = _silu(gp)
        dgp_ref[...] = (dyp * mixed * ps * _dsilu(gp)).astype(BF16)
        dps_ref[...] += jnp.sum(dyp * sg * mixed, axis=0, keepdims=True)
        dmix = (dyp * sg * ps).astype(BF16)
        dmix_ref[...] = dmix
        for gi in range(N_POOL_GROUPS):
            dp = lax.dot_general(dmix[:, gi * cg:(gi + 1) * cg], wg_ref[gi], (((1,), (1,)), ((), ())),
                                 preferred_element_type=F32)
            dpool_ref[:, gi * cg:(gi + 1) * cg] = dp.astype(BF16)

    row = lambda i: i
    tile = lambda w: pl.BlockSpec((tm, w), lambda i: (i, 0))
    in_specs = [tile(d), tile(d), tile(a_w)] + _col_specs(off_ga, a_w, tm, row) + _col_specs(off_gp, p_w, tm, row)
    in_specs += [tile(p_w), _resident((a_w, d)), _resident((p_w, d)), _resident((N_POOL_GROUPS, cg, cg)),
                 _const_spec((1, p_w))]
    out_widths = (a_w, a_w, p_w, p_w, p_w)
    est = 2 * tm * (2 * d + 2 * a_w + 2 * p_w + sum(out_widths)) * 2 + (a_w + p_w) * d * 2 + 8 * tm * a_w * 4
    return _call("bwd_branches", body, (s // tm,), in_specs,
                 (dza, dzp, attn, *([proj] * (n_ga + n_gp)), mixed, w_bra, w_brp, w_grp, pool_scale),
                 tuple(jax.ShapeDtypeStruct((s, w), BF16) for w in out_widths) + (jax.ShapeDtypeStruct((1, p_w), F32),),
                 tuple(tile(w) for w in out_widths) + (_const_spec((1, p_w)),), [], ("arbitrary",), est, comm=comm)


def _pool_backward(dpooled, s):
    _, p_w = dpooled.shape
    cg = p_w // N_POOL_GROUPS
    tm = _row_tile(s, ROW_TILE)
    per, last = tm // HALO, s // HALO - 1

    def body(dp_ref, prev_ref, next_ref, du_ref):
        i = pl.program_id(0)
        dp = dp_ref[...].astype(F32)
        ext = jnp.concatenate([prev_ref[...].astype(F32), dp, next_ref[...].astype(F32)], axis=0)
        pos, ok = _pool_positions(i, tm, s, cg)
        for gi, w in enumerate(POOL_SIZES):
            t = jnp.where(ok, ext[:, gi * cg:(gi + 1) * cg], 0.0) / _pool_count(pos, w, s)
            acc = t[HALO - w // 2 + 1: HALO - w // 2 + 1 + tm]
            for dd in range(-w // 2 + 2, w // 2 + 1):
                acc = acc + t[HALO + dd: HALO + dd + tm]
            du_ref[:, gi * cg:(gi + 1) * cg] = (acc - dp[:, gi * cg:(gi + 1) * cg]).astype(BF16)

    return pl.pallas_call(
        body, name="pool_backward", grid=(s // tm,), out_shape=jax.ShapeDtypeStruct((s, p_w), BF16),
        in_specs=[pl.BlockSpec((tm, p_w), lambda i: (i, 0)),
                  pl.BlockSpec((HALO, p_w), lambda i: (jnp.maximum(i * per - 1, 0), 0)),
                  pl.BlockSpec((HALO, p_w), lambda i: (jnp.minimum((i + 1) * per, last), 0))],
        out_specs=pl.BlockSpec((tm, p_w), lambda i: (i, 0)),
        compiler_params=_cparams(("parallel",), 4 * tm * p_w * 2 + 8 * tm * p_w * 4),
    )(dpooled, dpooled, dpooled)


def _attn_backward(proj, dattn, bias2, sink, s):
    nblk = s // BLOCK
    nq = ATTN_WIDTH // COL_TILE
    kv_col = ATTN_WIDTH // COL_TILE
    rows = GQA_GROUP * BLOCK
    scale = HEAD_DIM ** -0.5

    def body(*refs):
        q_refs = refs[:nq]
        kvp, kvc, kvn, do_ref, bias_ref, sink_ref, dq_ref, dkv_ref, dbias_ref, dsink_ref, acc, sacc = refs[nq:]
        n = pl.program_id(0)

        @pl.when(n == 0)
        def _():
            acc[...] = jnp.zeros_like(acc)
            sacc[...] = jnp.zeros_like(sacc)
            dbias_ref[...] = jnp.zeros_like(dbias_ref)
            dsink_ref[...] = jnp.zeros_like(dsink_ref)

        @pl.when(jnp.logical_and(n >= 1, n < nblk))
        def _():
            acc[(n + 1) % 3] = jnp.zeros((BLOCK, 2 * KV_WIDTH), F32)

        @pl.when(n < nblk)
        def _():
            q = _cat(q_refs)
            do = do_ref[...]
            kv = jnp.concatenate([kvp[...], kvc[...], kvn[...]], axis=0)
            valid = _attn_mask(n, s)
            for kh in range(N_KV_HEADS):
                qs = _stack_heads(q, kh)
                dos = _stack_heads(do, kh)
                k = kv[:, kh * HEAD_DIM:(kh + 1) * HEAD_DIM]
                v = kv[:, KV_WIDTH + kh * HEAD_DIM: KV_WIDTH + (kh + 1) * HEAD_DIM]
                p, ps = _softmax_parts(qs, k, bias_ref[kh], valid, _sink_column(sink_ref, kh))
                dp = lax.dot_general(dos, v, (((1,), (1,)), ((), ())), preferred_element_type=F32)
                delta = jnp.sum(p * dp, axis=1, keepdims=True)
                ds = p * (dp - delta)
                dbias_ref[kh] += ds
                sacc[kh] += -ps * delta
                dsb = ds.astype(BF16)
                dq = jnp.dot(dsb, k, preferred_element_type=F32) * scale
                for g in range(GQA_GROUP):
                    h = kh * GQA_GROUP + g
                    dq_ref[:, h * HEAD_DIM:(h + 1) * HEAD_DIM] = dq[g * BLOCK:(g + 1) * BLOCK].astype(BF16)
                dk = lax.dot_general(dsb, qs, (((0,), (0,)), ((), ())), preferred_element_type=F32) * scale
                dv = lax.dot_general(p.astype(BF16), dos, (((0,), (0,)), ((), ())), preferred_element_type=F32)
                for j in range(3):
                    slot = (n + 2 + j) % 3
                    acc[slot, :, kh * HEAD_DIM:(kh + 1) * HEAD_DIM] += dk[j * BLOCK:(j + 1) * BLOCK]
                    acc[slot, :, KV_WIDTH + kh * HEAD_DIM: KV_WIDTH + (kh + 1) * HEAD_DIM] += dv[j * BLOCK:(j + 1) * BLOCK]

        dkv_ref[...] = acc[(n + 2) % 3].astype(BF16)

        @pl.when(n == nblk)
        def _():
            lane = lax.broadcasted_iota(jnp.int32, (1, LANE), 1)
            out = jnp.zeros((1, LANE), F32)
            for kh in range(N_KV_HEADS):
                col = sacc[kh]
                for g in range(GQA_GROUP):
                    out = jnp.where(lane == kh * GQA_GROUP + g, jnp.sum(col[g * BLOCK:(g + 1) * BLOCK]), out)
            dsink_ref[...] = out

    qi = lambda n: jnp.minimum(n, nblk - 1)
    in_specs = _col_specs(0, ATTN_WIDTH, BLOCK, qi)
    in_specs += [pl.BlockSpec((BLOCK, COL_TILE), lambda n: (jnp.maximum(qi(n) - 1, 0), kv_col)),
                 pl.BlockSpec((BLOCK, COL_TILE), lambda n: (qi(n), kv_col)),
                 pl.BlockSpec((BLOCK, COL_TILE), lambda n: (jnp.minimum(qi(n) + 1, nblk - 1), kv_col)),
                 pl.BlockSpec((BLOCK, ATTN_WIDTH), lambda n: (qi(n), 0)),
                 _const_spec((N_KV_HEADS, rows, SPAN)),
                 pl.BlockSpec(memory_space=pltpu.SMEM)]
    return pl.pallas_call(
        body, name="attn_backward", grid=(nblk + 1,),
        out_shape=(jax.ShapeDtypeStruct((s, ATTN_WIDTH), BF16), jax.ShapeDtypeStruct((s, 2 * KV_WIDTH), BF16),
                   jax.ShapeDtypeStruct((N_KV_HEADS, rows, SPAN), F32), jax.ShapeDtypeStruct((1, LANE), F32)),
        in_specs=in_specs,
        out_specs=(pl.BlockSpec((BLOCK, ATTN_WIDTH), lambda n: (qi(n), 0)),
                   pl.BlockSpec((BLOCK, 2 * KV_WIDTH), lambda n: (jnp.clip(n - 1, 0, nblk - 1), 0)),
                   _const_spec((N_KV_HEADS, rows, SPAN)), _const_spec((1, LANE))),
        scratch_shapes=[pltpu.VMEM((3, BLOCK, 2 * KV_WIDTH), F32), pltpu.VMEM((N_KV_HEADS, rows, 1), F32)],
        compiler_params=_cparams(("arbitrary",), 24 << 20),
    )(*([proj] * (nq + 3)), dattn, bias2, sink)


def _pick_tile(n, cap):
    t = cap - cap % LANE
    while n % t:
        t -= LANE
    return t


def _matmul_tn(name, a, b, comm=None):
    s, m = a.shape
    _, n = b.shape
    tk = _row_tile(s, 1024)
    tm, tn = _pick_tile(m, 1024), _pick_tile(n, 1152)
    nk = s // tk

    def body(a_ref, b_ref, o_ref, acc):
        k = pl.program_id(2)

        @pl.when(k == 0)
        def _():
            acc[...] = jnp.zeros_like(acc)

        acc[...] += lax.dot_general(a_ref[...], b_ref[...], (((0,), (0,)), ((), ())), preferred_element_type=F32)

        @pl.when(k == nk - 1)
        def _():
            o_ref[...] = acc[...].astype(BF16)

    (out,), extra = _call(
        name, body, (m // tm, n // tn, nk),
        [pl.BlockSpec((tk, tm), lambda i, j, k: (k, i)), pl.BlockSpec((tk, tn), lambda i, j, k: (k, j))], (a, b),
        [jax.ShapeDtypeStruct((m, n), BF16)], [pl.BlockSpec((tm, tn), lambda i, j, k: (i, j))],
        [pltpu.VMEM((tm, tn), F32)], ("parallel", "parallel", "arbitrary"),
        2 * tk * (tm + tn) * 2 + tm * tn * (4 + 4 + 4), comm=comm)
    return out, extra


def _pool_weight_grad(pooled, dmix, s):
    _, p_w = pooled.shape
    cg = p_w // N_POOL_GROUPS
    tk = _row_tile(s, 512)
    nk = s // tk

    def body(a_ref, b_ref, o_ref, acc):
        k = pl.program_id(1)

        @pl.when(k == 0)
        def _():
            acc[...] = jnp.zeros_like(acc)

        acc[...] += lax.dot_general(a_ref[...], b_ref[...], (((0,), (0,)), ((), ())), preferred_element_type=F32)

        @pl.when(k == nk - 1)
        def _():
            o_ref[0] = acc[...].astype(BF16)

    return pl.pallas_call(
        body, name="pool_weight_grad", grid=(N_POOL_GROUPS, nk),
        out_shape=jax.ShapeDtypeStruct((N_POOL_GROUPS, cg, cg), BF16),
        in_specs=[pl.BlockSpec((tk, cg), lambda g, k: (k, g)), pl.BlockSpec((tk, cg), lambda g, k: (k, g))],
        out_specs=pl.BlockSpec((1, cg, cg), lambda g, k: (g, 0, 0)),
        scratch_shapes=[pltpu.VMEM((cg, cg), F32)],
        compiler_params=_cparams(("parallel", "arbitrary"), 16 << 20),
    )(pooled, dmix)


def _bwd_hidden(dproj, dgl, w_in, w_merge, s, d, comm=None):
    tm = _row_tile(s, 1024)
    n_in = dproj.shape[1] // COL_TILE
    n_mg = dgl.shape[1] // COL_TILE
    nk = n_in + n_mg

    def body(dp_ref, dg_ref, wi_ref, wm_ref, dh_ref):
        k = pl.program_id(1)

        @pl.when(k == 0)
        def _():
            dh_ref[...] = lax.dot_general(dp_ref[...], wi_ref[...], (((1,), (1,)), ((), ())),
                                          preferred_element_type=F32)

        @pl.when(jnp.logical_and(k > 0, k < n_in))
        def _():
            dh_ref[...] += lax.dot_general(dp_ref[...], wi_ref[...], (((1,), (1,)), ((), ())),
                                           preferred_element_type=F32)

        @pl.when(k >= n_in)
        def _():
            dh_ref[...] += lax.dot_general(dg_ref[...], wm_ref[...], (((1,), (1,)), ((), ())),
                                           preferred_element_type=F32)

    in_specs = [pl.BlockSpec((tm, COL_TILE), lambda i, k: (i, jnp.minimum(k, n_in - 1))),
                pl.BlockSpec((tm, COL_TILE), lambda i, k: (i, jnp.maximum(k - n_in, 0))),
                pl.BlockSpec((d, COL_TILE), lambda i, k: (0, jnp.minimum(k, n_in - 1))),
                pl.BlockSpec((d, COL_TILE), lambda i, k: (0, jnp.maximum(k - n_in, 0)))]
    est = 2 * 2 * (tm + d) * COL_TILE * 2 + 3 * tm * d * 4
    (dh,), extra = _call("bwd_hidden", body, (s // tm, nk), in_specs, (dproj, dgl, w_in, w_merge),
                         [jax.ShapeDtypeStruct((s, d), F32)], [pl.BlockSpec((tm, d), lambda i, k: (i, 0))], [],
                         ("parallel", "arbitrary"), est, comm=comm)
    return dh, extra


def _bwd_prenorm(dh, x, dout, mod, pre_g, s, d):
    tm = _row_tile(s, ROW_TILE)

    def body(dh_ref, x_ref, dout_ref, mod_ref, pg_ref, gx_ref, dsh_ref, dsc_ref, dpg_ref):
        i = pl.program_id(0)

        @pl.when(i == 0)
        def _():
            dsh_ref[...] = jnp.zeros_like(dsh_ref)
            dsc_ref[...] = jnp.zeros_like(dsc_ref)
            dpg_ref[...] = jnp.zeros_like(dpg_ref)

        dh = dh_ref[...]
        xv = x_ref[...]
        r = lax.rsqrt(jnp.mean(xv * xv, axis=1, keepdims=True) + EPS)
        xhat = xv * r
        pg = pg_ref[...]
        one_scale = 1.0 + mod_ref[:, d:2 * d]
        dsh_ref[...] += jnp.sum(dh, axis=0, keepdims=True)
        dsc_ref[...] += jnp.sum(dh * xhat, axis=0, keepdims=True) * pg
        dpg_ref[...] += jnp.sum(dh * xhat, axis=0, keepdims=True) * one_scale
        dxh = dh * (one_scale * pg)
        dx = r * (dxh - xhat * jnp.mean(dxh * xhat, axis=1, keepdims=True))
        gx_ref[...] = dout_ref[...] + dx

    tile = pl.BlockSpec((tm, d), lambda i: (i, 0))
    vec = _const_spec((1, d))
    return pl.pallas_call(
        body, name="bwd_prenorm", grid=(s // tm,),
        out_shape=(jax.ShapeDtypeStruct((s, d), F32),) + (jax.ShapeDtypeStruct((1, d), F32),) * 3,
        in_specs=[tile, tile, tile, _const_spec((1, 3 * d)), vec], out_specs=(tile, vec, vec, vec),
        compiler_params=_cparams(("arbitrary",), 2 * 4 * tm * d * 4 + 6 * tm * d * 4),
    )(dh, x, dout, mod, pre_g)


def _pad_lanes(v, width):
    return jnp.pad(v, ((0, 0), (0, width - v.shape[1])))


def kernel(x, c, rel_bias_table, w_ada, b_ada, pre_norm_g, post_norm_g, w_in, attn_sink, w_pool_group, pool_scale, w_branch_attn, w_branch_pool, w_merge, b_merge, w_out, loss_target, m_rel_bias_table, m_w_ada, m_b_ada, m_pre_norm_g, m_post_norm_g, m_w_in, m_attn_sink, m_w_pool_group, m_pool_scale, m_w_branch_attn, m_w_branch_pool, m_w_merge, m_b_merge, m_w_out, v_rel_bias_table, v_w_ada, v_b_ada, v_pre_norm_g, v_post_norm_g, v_w_in, v_attn_sink, v_w_pool_group, v_pool_scale, v_w_branch_attn, v_w_branch_pool, v_w_merge, v_b_merge, v_w_out):
    _, s, d = x.shape
    p_w = pool_scale.shape[-1]
    cg = p_w // N_POOL_GROUPS
    in_w = 2 * ATTN_WIDTH + 2 * KV_WIDTH + 2 * p_w
    x2, t2 = x[0], loss_target[0]
    chip = 2 * lax.axis_index("x") + lax.axis_index("y")

    specs = [_Sharded("col", (d, in_w)), _Sharded("col", (d, 2 * d)), _Sharded("col", (ATTN_WIDTH, d)),
             _Sharded("col", (p_w, d)), _Sharded("row", (d, d)), _Sharded("grp", (N_POOL_GROUPS, cg, cg))]
    shards32 = [w_in[0], w_merge[0], w_branch_attn[0], w_branch_pool[0], w_out[0],
                w_pool_group[0].reshape(N_POOL_GROUPS * cg // N_CHIPS, cg)]
    names = ["w_in", "w_merge", "w_branch_attn", "w_branch_pool", "w_out", "w_pool_group"]
    shards16 = [_cast_bf16("cast_" + nm, w) for nm, w in zip(names, shards32)]
    shards16[5] = shards16[5].reshape(N_POOL_GROUPS, cg // N_CHIPS, cg)

    sc_all = _all_gather8("gather_cond", c, 1, pre=_silu)
    m_all = _all_gather8("gather_mod", _ada_forward(sc_all, w_ada[0]), N_DEV)
    mod = _mod_finish(m_all, b_ada)

    h = _prenorm(x2, mod, pre_norm_g)
    (proj, gates), (wf_in,) = _local_columns(h, shards16[0], shards16[1], b_merge,
                                             comm=_GatherWeights(specs[:1], shards16[:1]))
    proj, (wf_merge,) = _other_columns("proj", h, wf_in, proj, comm=_GatherWeights(specs[1:2], shards16[1:2]))
    gates, (wf_bra, wf_brp, wf_out, wf_grp) = _other_columns("merge_gates", h, wf_merge, gates, bias=b_merge,
                                                             comm=_GatherWeights(specs[2:], shards16[2:]))
    buckets = _t5_buckets()
    bias2 = _bias_table(rel_bias_table, buckets).reshape(N_KV_HEADS, GQA_GROUP * BLOCK, SPAN)
    attn = _attn_forward(proj, bias2, attn_sink, s)
    ya, yp, za, zp, merged, pooled, mixed = _branches(proj, attn, gates, wf_bra, wf_brp, wf_grp, pool_scale, s, d)
    dout, d_o, loss_part, dgate, dpostg = _out_loss(merged, x2, t2, wf_out, post_norm_g, mod, s, d)

    pw_out, _ = _matmul_tn("grad_w_out", merged, d_o)
    (dza, dzp, dgl, dbm), (pc_out,) = _bwd_out(d_o, gates, za, zp, wf_out, s, d,
                                                comm=_ScatterGrads(specs[4:5], [pw_out]))
    pw_bra, _ = _matmul_tn("grad_w_branch_attn", ya, dza)
    pw_brp, _ = _matmul_tn("grad_w_branch_pool", yp, dzp)
    pw_merge, _ = _matmul_tn("grad_w_merge", h, dgl)
    (dattn, dga, dgp, dmix, dpooled, dps), (pc_bra, pc_brp) = _bwd_branches(
        dza, dzp, attn, proj, mixed, wf_bra, wf_brp, wf_grp, pool_scale, s, d,
        comm=_ScatterGrads(specs[2:4], [pw_bra, pw_brp]))
    pw_grp = _pool_weight_grad(pooled, dmix, s)
    du = _pool_backward(dpooled, s)
    dq, dkv, dbias, dsink = _attn_backward(proj, dattn, bias2, attn_sink, s)
    dproj = jnp.concatenate([dq, dkv, dga, du, dgp], axis=1)
    pw_in, (pc_merge,) = _matmul_tn("grad_w_in", h, dproj, comm=_ScatterGrads(specs[1:2], [pw_merge]))
    dh, (pc_in, pc_grp) = _bwd_hidden(dproj, dgl, wf_in, wf_merge, s, d,
                                      comm=_ScatterGrads([specs[0], specs[5]], [pw_in, pw_grp]))
    gx, dshift, dscale, dpreg = _bwd_prenorm(dh, x2, dout, mod, pre_norm_g, s, d)

    pieces = [pc_in, pc_merge, pc_bra, pc_brp, pc_out]
    weights = [w_in, w_merge, w_branch_attn, w_branch_pool, w_out]
    moms = [m_w_in, m_w_merge, m_w_branch_attn, m_w_branch_pool, m_w_out]
    vars_ = [v_w_in, v_w_merge, v_w_branch_attn, v_w_branch_pool, v_w_out]
    big = {}
    for nm, pc, w, m, v in zip(names, pieces, weights, moms, vars_):
        shape2 = (-1, w.shape[-1])
        res4 = _reduce_adamw("update_" + nm, pc, w.reshape(shape2), m.reshape(shape2), v.reshape(shape2))
        big[nm] = tuple(a.reshape(w.shape) for a in res4)
    hq = cg // N_CHIPS // 2
    g_grp = _reduce16("reduce_w_pool_group", pc_grp.reshape(2, N_DEV, N_POOL_GROUPS * hq, cg))
    g_grp = g_grp.reshape(2, N_POOL_GROUPS, hq, cg).transpose(1, 0, 2, 3).reshape(N_POOL_GROUPS * 2 * hq, cg)
    res3 = _adamw("adamw_w_pool_group", w_pool_group.reshape(-1, cg), g_grp, m_w_pool_group.reshape(-1, cg),
                  v_w_pool_group.reshape(-1, cg))
    big["w_pool_group"] = tuple(a.reshape(w_pool_group.shape) for a in (g_grp,) + tuple(res3))

    dtable = _bias_table_grad(dbias.reshape(N_Q_HEADS, BLOCK, SPAN), buckets)[:, :N_Q_HEADS]
    segs = [("b_ada", jnp.concatenate([dshift, dscale, dgate], axis=1), 3 * d),
            ("pre_norm_g", dpreg, d), ("post_norm_g", dpostg, d), ("attn_sink", dsink, LANE),
            ("pool_scale", dps, p_w), ("b_merge", dbm, 2 * d), ("rel_bias_table", dtable.reshape(1, -1), 2 * LANE)]
    packed = jnp.concatenate([_pad_lanes(v, w) for _, v, w in segs], axis=1)
    rows = _all_gather8("gather_small", packed, 1)[:, 0, :]

    def pack(vals):
        return jnp.concatenate([_pad_lanes(v.reshape(1, -1), w) for v, (_, _, w) in zip(vals, segs)], axis=1)

    small_w = [b_ada, pre_norm_g, post_norm_g, attn_sink, pool_scale, b_merge, rel_bias_table]
    small_m = [m_b_ada, m_pre_norm_g, m_post_norm_g, m_attn_sink, m_pool_scale, m_b_merge, m_rel_bias_table]
    small_v = [v_b_ada, v_pre_norm_g, v_post_norm_g, v_attn_sink, v_pool_scale, v_b_merge, v_rel_bias_table]
    g_small, d_small, nm_small, nv_small = _small_update(rows, pack(small_w), pack(small_m), pack(small_v))
    small = {}
    off = 0
    for (nm, _, w), ref in zip(segs, small_w):
        cut = lambda a: a[:, off:off + ref.size].reshape(ref.shape)
        small[nm] = (cut(g_small), cut(d_small), cut(nm_small), cut(nv_small))
        off += w

    dmod_cols = lax.dynamic_slice_in_dim(rows[:, :3 * d], chip * (3 * d // N_CHIPS), 3 * d // N_CHIPS, axis=1)
    sc_t = sc_all[:, 0, :].T
    g_ada, d_ada, nm_ada, nv_ada = _ada_backward(sc_t, dmod_cols, w_ada[0], m_w_ada[0], v_w_ada[0])
    big["w_ada"] = tuple(a.reshape(w_ada.shape) for a in (g_ada, d_ada, nm_ada, nv_ada))

    loss = lax.psum(loss_part[0, 0], ("x", "y", "c"))
    order = ["rel_bias_table", "w_ada", "b_ada", "pre_norm_g", "post_norm_g", "w_in", "attn_sink", "w_pool_group",
             "pool_scale", "w_branch_attn", "w_branch_pool", "w_merge", "b_merge", "w_out"]
    res = {**big, **small}
    outs = [loss, gx.reshape(x.shape)]
    for part in range(4):
        outs += [res[nm][part] for nm in order]
    return tuple(outs)


def _small_update(rows, w, m, v):
    _, n = rows.shape

    def body(r_ref, w_ref, m_ref, v_ref, g_ref, d_ref, nm_ref, nv_ref):
        g = r_ref[0:1, :]
        for k in range(1, N_DEV):
            g = g + r_ref[k:k + 1, :]
        dl, nm, nv = _adam_math(w_ref[...], g, m_ref[...], v_ref[...])
        g_ref[...] = g
        d_ref[...] = dl
        nm_ref[...] = nm
        nv_ref[...] = nv

    vm = pl.BlockSpec(memory_space=pltpu.VMEM)
    out = jax.ShapeDtypeStruct((1, n), F32)
    return pl.pallas_call(
        body, name="small_update", out_shape=(out,) * 4, in_specs=[vm] * 4, out_specs=(vm,) * 4,
    )(rows, w, m, v)
```

```python
import functools
import math

import numpy as np
import jax
import jax.numpy as jnp
from jax import lax
from jax.experimental import pallas as pl
from jax.experimental.pallas import tpu as pltpu

F32 = jnp.float32
BF16 = jnp.bfloat16
MESH = pl.DeviceIdType.MESH

HEAD_DIM = 128
N_Q_HEADS = 8
N_KV_HEADS = 2
GQA_GROUP = N_Q_HEADS // N_KV_HEADS
ATTN_WIDTH = N_Q_HEADS * HEAD_DIM
KV_WIDTH = N_KV_HEADS * HEAD_DIM
WINDOW = 128
BLOCK = 128
SPAN = BLOCK + 2 * WINDOW
N_BUCKETS = 32
MAX_DISTANCE = 128
POOL_SIZES = (2, 4, 8, 16)
N_POOL_GROUPS = len(POOL_SIZES)
HALO = 16
EPS = 1e-6
NEG_INF = -1e30
ADAM_LR = 0.001
ADAM_B1 = 0.9
ADAM_B2 = 0.999
ADAM_EPS = 1e-08
ADAM_WD = 0.01
ADAM_STEP = 10

N_DEV = 8
N_CHIPS = 4
LANE = 128
COL_TILE = 512
VMEM_CAP = 60000 * 1024
ROW_TILE = 256


def _cparams(sem, est_bytes):
    limit = int(min(max(est_bytes * 5 // 4 + (4 << 20), 16 << 20), VMEM_CAP))
    return pltpu.CompilerParams(dimension_semantics=sem, vmem_limit_bytes=limit)


def _sigmoid(x):
    return jax.nn.sigmoid(x)


def _silu(x):
    return x * _sigmoid(x)


def _dsilu(x):
    s = _sigmoid(x)
    return s * (1.0 + x * (1.0 - s))


def _place():
    x, y, c = lax.axis_index("x"), lax.axis_index("y"), lax.axis_index("c")
    return x, y, c


def _flip(v, bit):
    return (1 - v) if bit else v


def _xor_peer(k):
    x, y, c = _place()
    return (_flip(x, (k >> 2) & 1), _flip(y, (k >> 1) & 1), _flip(c, k & 1))


def _resident(shape):
    nd = len(shape)
    return pl.BlockSpec(shape, lambda *_: (0,) * nd, pipeline_mode=pl.Buffered(1))


def _const_spec(shape):
    nd = len(shape)
    return pl.BlockSpec(shape, lambda *_: (0,) * nd)


CHUNK_BYTES = 256 << 10
MAX_CHUNKS = 32


def _all_gather8(name, x, nrows, pre=None):
    r, n = x.shape

    def body(x_ref, out_ref, stage, send_sems, recv_sems):
        px, py, pc = _place()
        me = 4 * px + 2 * py + pc
        v = x_ref[...]
        if pre is not None:
            v = pre(v)
        stage[...] = v[0:nrows]
        out_ref[me] = v[0:nrows]
        copies = []
        for k in range(1, N_DEV):
            cp = pltpu.make_async_remote_copy(
                src_ref=stage, dst_ref=out_ref.at[me], send_sem=send_sems.at[k - 1], recv_sem=recv_sems.at[k - 1],
                device_id=_xor_peer(k), device_id_type=MESH)
            cp.start()
            copies.append(cp)
        for cp in copies:
            cp.wait()

    return pl.pallas_call(
        body, name=name,
        out_shape=jax.ShapeDtypeStruct((N_DEV, nrows, n), F32),
        in_specs=[pl.BlockSpec(memory_space=pltpu.VMEM)],
        out_specs=pl.BlockSpec(memory_space=pltpu.VMEM),
        scratch_shapes=[pltpu.VMEM((nrows, n), F32), pltpu.SemaphoreType.DMA((N_DEV - 1,)),
                        pltpu.SemaphoreType.DMA((N_DEV - 1,))],
    )(x)


class _Sharded:
    def __init__(self, kind, full_shape):
        self.kind = kind
        self.full_shape = tuple(full_shape)
        if kind == "col":
            r, c = full_shape
            self.shard_shape = (r, c // N_CHIPS)
        elif kind == "row":
            r, c = full_shape
            self.shard_shape = (r // N_CHIPS, c)
        else:
            g, r, c = full_shape
            self.shard_shape = (g, r // N_CHIPS, c)
        self.axis = 1 if kind == "grp" else 0
        s = list(self.shard_shape)
        s[self.axis] //= 2
        self.piece_shape = tuple(s)

    def _rows(self, ref, start, size):
        idx = (slice(None),) * self.axis + (pl.ds(pl.multiple_of(start, 16), size),)
        return ref.at[idx]

    def shard_half(self, ref, hc):
        h = self.piece_shape[self.axis]
        return self._rows(ref, hc * h, h)

    def window(self, ref, chip, hc=None):
        s = self.shard_shape
        if self.kind == "col":
            cols = pl.ds(pl.multiple_of(chip * s[1], LANE), s[1])
            if hc is None:
                return ref.at[:, cols]
            h = s[0] // 2
            return ref.at[pl.ds(pl.multiple_of(hc * h, 16), h), cols]
        n = s[self.axis]
        if hc is None:
            return self._rows(ref, chip * n, n)
        return self._rows(ref, chip * n + hc * (n // 2), n // 2)

    def chunks(self, view, shape, itemsize):
        rows = shape[self.axis]
        nbytes = math.prod(shape) * itemsize
        n = 1
        while 2 * n <= MAX_CHUNKS and nbytes // (2 * n) >= CHUNK_BYTES and rows % (2 * n * 16) == 0:
            n *= 2
        h = rows // n
        return [view.at[(slice(None),) * self.axis + (pl.ds(j * h, h),)] for j in range(n)]


def _remote(src, dst, send_sem, recv_sem, to):
    return pltpu.make_async_remote_copy(src_ref=src, dst_ref=dst, send_sem=send_sem, recv_sem=recv_sem,
                                        device_id=to, device_id_type=MESH)


def _start_remote(sp, src, dst, shape, itemsize, send_sem, recv_sem, to):
    for s_part, d_part in zip(sp.chunks(src, shape, itemsize), sp.chunks(dst, shape, itemsize)):
        _remote(s_part, d_part, send_sem, recv_sem, to).start()


def _start_local(sp, src, dst, shape, itemsize, sem):
    for s_part, d_part in zip(sp.chunks(src, shape, itemsize), sp.chunks(dst, shape, itemsize)):
        pltpu.make_async_copy(s_part, d_part, sem).start()


class _GatherWeights:
    def __init__(self, specs, shards, middle=0.7):
        self.specs = specs
        self.middle_at = middle
        self.inputs = list(shards)
        self.out_shapes = [jax.ShapeDtypeStruct(sp.full_shape, BF16) for sp in specs]
        nw = len(specs)
        self.scratch = [pltpu.SemaphoreType.DMA((6 * nw,)), pltpu.SemaphoreType.DMA((6 * nw,)),
                        pltpu.SemaphoreType.DMA((nw,))]

    aliases = staticmethod(lambda n_in, n_out: {})

    def phases(self, nsteps):
        return [(0, self.start), (min(nsteps - 1, max(1, int(self.middle_at * nsteps))), self.middle),
                (nsteps - 1, self.end)]

    def _ctx(self):
        x, y, c = _place()
        return x, y, c, 2 * x + y, (x, y, 1 - c), [(1 - x, y), (x, 1 - y), (1 - x, 1 - y)]

    def start(self, shard_refs, full_refs, sems):
        send_sems, recv_sems, local_sems = sems
        x, y, c, my_chip, sibling, chips = self._ctx()
        for w, sp in enumerate(self.specs):
            _start_local(sp, shard_refs[w], sp.window(full_refs[w], my_chip), sp.shard_shape, 2, local_sems.at[w])
            for t, (cx, cy) in enumerate(chips):
                _start_remote(sp, sp.shard_half(shard_refs[w], c), sp.window(full_refs[w], my_chip, c),
                              sp.piece_shape, 2, send_sems.at[6 * w + t], recv_sems.at[6 * w + t], (cx, cy, c))

    def middle(self, shard_refs, full_refs, sems):
        send_sems, recv_sems, local_sems = sems
        x, y, c, my_chip, sibling, chips = self._ctx()
        for w, sp in enumerate(self.specs):
            for t, (cx, cy) in enumerate(chips):
                landed = sp.window(full_refs[w], 2 * cx + cy, c)
                _remote(landed, landed, send_sems.at[6 * w + t], recv_sems.at[6 * w + t], (cx, cy, c)).wait_recv()
                _start_remote(sp, landed, landed, sp.piece_shape, 2, send_sems.at[6 * w + 3 + t],
                              recv_sems.at[6 * w + 3 + t], sibling)

    def end(self, shard_refs, full_refs, sems):
        send_sems, recv_sems, local_sems = sems
        x, y, c, my_chip, sibling, chips = self._ctx()
        for w, sp in enumerate(self.specs):
            for t, (cx, cy) in enumerate(chips):
                other = sp.window(full_refs[w], 2 * cx + cy, 1 - c)
                _remote(other, other, send_sems.at[6 * w + 3 + t], recv_sems.at[6 * w + 3 + t], sibling).wait_recv()
        for w, sp in enumerate(self.specs):
            for t, (cx, cy) in enumerate(chips):
                mine = sp.shard_half(shard_refs[w], c)
                _remote(mine, mine, send_sems.at[6 * w + t], recv_sems.at[6 * w + t], (cx, cy, c)).wait_send()
                landed = sp.window(full_refs[w], 2 * cx + cy, c)
                _remote(landed, landed, send_sems.at[6 * w + 3 + t], recv_sems.at[6 * w + 3 + t], sibling).wait_send()
            pltpu.make_async_copy(shard_refs[w], sp.window(full_refs[w], my_chip), local_sems.at[w]).wait()


class _ScatterGrads:
    def __init__(self, specs, partials, peers=tuple(range(N_DEV)), into=None, middle=0.7):
        self.specs = specs
        self.peers = tuple(peers)
        self.middle_at = middle
        self.n_part = len(partials)
        self.into = into is not None
        self.inputs = list(partials) + (list(into) if self.into else [])
        self.out_shapes = [jax.ShapeDtypeStruct((2, N_DEV) + sp.piece_shape, BF16) for sp in specs]
        nw = len(specs)
        self.scratch = [pltpu.SemaphoreType.DMA((15 * nw,)), pltpu.SemaphoreType.DMA((15 * nw,)),
                        pltpu.SemaphoreType.DMA((nw,))]

    def aliases(self, n_in, n_out):
        return {n_in + self.n_part + w: n_out + w for w in range(len(self.specs))} if self.into else {}

    def phases(self, nsteps):
        return [(0, self.start), (min(nsteps - 1, max(1, int(self.middle_at * nsteps))), self.middle),
                (nsteps - 1, self.end)]

    def _own(self, sp, part_ref, recv_ref, x, y, c, sem):
        return pltpu.make_async_copy(sp.window(part_ref, 2 * x + y, c), recv_ref.at[c, 0], sem)

    def start(self, part_refs, recv_refs, sems):
        send_sems, recv_sems, local_sems = sems
        x, y, c = _place()
        for w, sp in enumerate(self.specs):
            for k in self.peers:
                if k == 0:
                    _start_local(sp, sp.window(part_refs[w], 2 * x + y, c), recv_refs[w].at[c, 0], sp.piece_shape,
                                 2, local_sems.at[w])
                    continue
                px, py, pc = _xor_peer(k)
                _start_remote(sp, sp.window(part_refs[w], 2 * px + py, pc), recv_refs[w].at[pc, k], sp.piece_shape,
                              2, send_sems.at[15 * w + k - 1], recv_sems.at[15 * w + k - 1], (px, py, pc))

    def middle(self, part_refs, recv_refs, sems):
        send_sems, recv_sems, local_sems = sems
        x, y, c = _place()
        sibling = (x, y, 1 - c)
        for w, sp in enumerate(self.specs):
            for k in self.peers:
                landed = recv_refs[w].at[c, k]
                if k:
                    _remote(landed, landed, send_sems.at[15 * w + k - 1], recv_sems.at[15 * w + k - 1],
                            sibling).wait_recv()
                else:
                    self._own(sp, part_refs[w], recv_refs[w], x, y, c, local_sems.at[w]).wait()
                _start_remote(sp, landed, landed, sp.piece_shape, 2, send_sems.at[15 * w + 7 + k],
                              recv_sems.at[15 * w + 7 + k], sibling)

    def end(self, part_refs, recv_refs, sems):
        send_sems, recv_sems, local_sems = sems
        x, y, c = _place()
        sibling = (x, y, 1 - c)
        for w, sp in enumerate(self.specs):
            for k in self.peers:
                other = recv_refs[w].at[1 - c, k]
                _remote(other, other, send_sems.at[15 * w + 7 + k], recv_sems.at[15 * w + 7 + k], sibling).wait_recv()
            for k in self.peers:
                landed = recv_refs[w].at[c, k]
                _remote(landed, landed, send_sems.at[15 * w + 7 + k], recv_sems.at[15 * w + 7 + k],
                        sibling).wait_send()
                if k:
                    px, py, pc = _xor_peer(k)
                    sent = sp.window(part_refs[w], 2 * px + py, pc)
                    _remote(sent, sent, send_sems.at[15 * w + k - 1], recv_sems.at[15 * w + k - 1],
                            (px, py, pc)).wait_send()


def _call(name, body, grid, in_specs, args, out_shape, out_specs, scratch, semantics, est_bytes, comm=None,
          aliases=None):
    out_shape, out_specs = tuple(out_shape), tuple(out_specs)
    if comm is None:
        res = pl.pallas_call(body, name=name, grid=grid, out_shape=out_shape, in_specs=list(in_specs),
                             out_specs=out_specs, scratch_shapes=list(scratch), input_output_aliases=aliases or {},
                             compiler_params=_cparams(semantics, est_bytes))(*args)
        return tuple(res), ()
    n_in, n_out, n_sc = len(in_specs), len(out_shape), len(scratch)
    c_in, c_out = len(comm.inputs), len(comm.out_shapes)
    nsteps = math.prod(grid)
    phases = comm.phases(nsteps)

    def hosted(*refs):
        pos = [0]

        def take(n):
            part = refs[pos[0]:pos[0] + n]
            pos[0] += n
            return part

        ins, cins, outs, couts, scr, sems = take(n_in), take(c_in), take(n_out), take(c_out), take(n_sc), take(3)
        step = 0
        for ax, extent in enumerate(grid):
            step = step * extent + pl.program_id(ax)
        for at, fn in phases:
            if at == 0:
                pl.when(step == 0)(functools.partial(fn, cins, couts, sems))
        body(*ins, *outs, *scr)
        for at, fn in phases:
            if at > 0:
                pl.when(step == at)(functools.partial(fn, cins, couts, sems))

    any_spec = pl.BlockSpec(memory_space=pl.ANY)
    res = pl.pallas_call(
        hosted, name=name, grid=grid, out_shape=out_shape + tuple(comm.out_shapes),
        in_specs=list(in_specs) + [any_spec] * c_in, out_specs=out_specs + (any_spec,) * c_out,
        scratch_shapes=list(scratch) + list(comm.scratch),
        input_output_aliases={**(aliases or {}), **comm.aliases(n_in, n_out)},
        compiler_params=_cparams(("arbitrary",) * len(grid), est_bytes))(*args, *comm.inputs)
    return tuple(res[:n_out]), tuple(res[n_out:])


def _row_tile(rows, cap):
    for t in range(min(rows, cap), 0, -1):
        if rows % t == 0 and (t % 16 == 0 or t == rows):
            return t
    return rows


def _cast_bf16(name, x):
    r, c = x.shape
    tr = _row_tile(r, 512)

    def body(x_ref, o_ref):
        o_ref[...] = x_ref[...].astype(BF16)

    return pl.pallas_call(
        body, name=name, grid=(r // tr,), out_shape=jax.ShapeDtypeStruct((r, c), BF16),
        in_specs=[pl.BlockSpec((tr, c), lambda i: (i, 0))], out_specs=pl.BlockSpec((tr, c), lambda i: (i, 0)),
        compiler_params=_cparams(("parallel",), 2 * tr * c * 6),
    )(x)


def _adam_math(w, g, m, v):
    m = ADAM_B1 * m + (1.0 - ADAM_B1) * g
    v = ADAM_B2 * v + (1.0 - ADAM_B2) * (g * g)
    m_hat = m / (1.0 - ADAM_B1 ** ADAM_STEP)
    v_hat = v / (1.0 - ADAM_B2 ** ADAM_STEP)
    delta = -ADAM_LR * (m_hat / (jnp.sqrt(v_hat) + ADAM_EPS) + ADAM_WD * w)
    return delta, m, v


def _adamw(name, w, g, m, v):
    r, c = w.shape
    tr = _row_tile(r, max(8, (1 << 18) // c))

    def body(w_ref, g_ref, m_ref, v_ref, d_ref, nm_ref, nv_ref):
        d, nm, nv = _adam_math(w_ref[...], g_ref[...], m_ref[...], v_ref[...])
        d_ref[...] = d
        nm_ref[...] = nm
        nv_ref[...] = nv

    spec = pl.BlockSpec((tr, c), lambda i: (i, 0))
    out = jax.ShapeDtypeStruct((r, c), F32)
    return pl.pallas_call(
        body, name=name, grid=(r // tr,), out_shape=(out, out, out), in_specs=[spec] * 4, out_specs=(spec,) * 3,
        compiler_params=_cparams(("parallel",), 2 * 7 * tr * c * 4),
    )(w, g, m, v)


def _sum_pieces(x_ref):
    acc = x_ref[0, 0].astype(F32)
    for k in range(1, N_DEV):
        acc = acc + x_ref[0, k].astype(F32)
    return acc


def _reduce16(name, x):
    _, _, r, c = x.shape
    tr = _row_tile(r, max(16, (1 << 17) // c))
    nt = r // tr

    def body(x_ref, o_ref):
        o_ref[...] = _sum_pieces(x_ref)

    return pl.pallas_call(
        body, name=name, grid=(2, nt), out_shape=jax.ShapeDtypeStruct((2 * r, c), F32),
        in_specs=[pl.BlockSpec((1, N_DEV, tr, c), lambda hf, i: (hf, 0, i, 0))],
        out_specs=pl.BlockSpec((tr, c), lambda hf, i: (hf * nt + i, 0)),
        compiler_params=_cparams(("parallel", "parallel"), 2 * (N_DEV * 2 + 4) * tr * c),
    )(x)


def _reduce_adamw(name, x, w, m, v):
    _, _, r, c = x.shape
    tr = _row_tile(r, max(16, (1 << 17) // c))
    nt = r // tr

    def body(x_ref, w_ref, m_ref, v_ref, g_ref, d_ref, nm_ref, nv_ref):
        g = _sum_pieces(x_ref)
        d, nm, nv = _adam_math(w_ref[...], g, m_ref[...], v_ref[...])
        g_ref[...] = g
        d_ref[...] = d
        nm_ref[...] = nm
        nv_ref[...] = nv

    tile = pl.BlockSpec((tr, c), lambda hf, i: (hf * nt + i, 0))
    out = jax.ShapeDtypeStruct((2 * r, c), F32)
    return pl.pallas_call(
        body, name=name, grid=(2, nt), out_shape=(out,) * 4,
        in_specs=[pl.BlockSpec((1, N_DEV, tr, c), lambda hf, i: (hf, 0, i, 0)), tile, tile, tile],
        out_specs=(tile,) * 4,
        compiler_params=_cparams(("parallel", "parallel"), 2 * (N_DEV * 2 + 7 * 4) * tr * c),
    )(x, w, m, v)


def _t5_buckets():
    rel = jnp.arange(SPAN)[None, :] - WINDOW - jnp.arange(BLOCK)[:, None]
    half = N_BUCKETS // 2
    max_exact = half // 2
    ret = jnp.where(rel > 0, half, 0)
    n = jnp.abs(rel)
    nf = jnp.maximum(n, 1).astype(F32)
    large = max_exact + (jnp.log(nf / max_exact) / math.log(MAX_DISTANCE / max_exact)
                         * (half - max_exact)).astype(jnp.int32)
    large = jnp.minimum(large, half - 1)
    return (ret + jnp.where(n < max_exact, n, large)).astype(jnp.int32)


def _bias_table(table, buckets):
    def body(t_ref, b_ref, o_ref):
        bk = b_ref[...]
        rel = (lax.broadcasted_iota(jnp.int32, (BLOCK, SPAN), 1) - WINDOW
               - lax.broadcasted_iota(jnp.int32, (BLOCK, SPAN), 0))
        band = jnp.abs(rel) <= WINDOW
        for h in range(N_Q_HEADS):
            acc = jnp.zeros((BLOCK, SPAN), F32)
            for b in range(N_BUCKETS):
                acc = jnp.where(bk == b, t_ref[b, h], acc)
            o_ref[h] = jnp.where(band, acc, NEG_INF)

    return pl.pallas_call(
        body, name="bias_table", out_shape=jax.ShapeDtypeStruct((N_Q_HEADS, BLOCK, SPAN), F32),
        in_specs=[pl.BlockSpec(memory_space=pltpu.SMEM), pl.BlockSpec(memory_space=pltpu.VMEM)],
        out_specs=pl.BlockSpec(memory_space=pltpu.VMEM),
    )(table, buckets)


def _bias_table_grad(dbias, buckets):
    def body(d_ref, b_ref, o_ref):
        bk = b_ref[...]
        row = lax.broadcasted_iota(jnp.int32, (N_BUCKETS, LANE), 0)
        lane = lax.broadcasted_iota(jnp.int32, (N_BUCKETS, LANE), 1)
        acc = jnp.zeros((N_BUCKETS, LANE), F32)
        for h in range(N_Q_HEADS):
            d = d_ref[h]
            for b in range(N_BUCKETS):
                s = jnp.sum(jnp.where(bk == b, d, 0.0))
                acc = jnp.where((row == b) & (lane == h), s, acc)
        o_ref[...] = acc

    return pl.pallas_call(
        body, name="bias_table_grad", out_shape=jax.ShapeDtypeStruct((N_BUCKETS, LANE), F32),
        in_specs=[pl.BlockSpec(memory_space=pltpu.VMEM), pl.BlockSpec(memory_space=pltpu.VMEM)],
        out_specs=pl.BlockSpec(memory_space=pltpu.VMEM),
    )(dbias, buckets)


def _ada_forward(sc_all, w_ada):
    d, n = w_ada.shape
    tn = _pick_tile(n, COL_TILE)

    def body(sc_ref, w_ref, o_ref):
        row = lax.broadcasted_iota(jnp.int32, (N_DEV, d), 0)
        sc = jnp.zeros((N_DEV, d), F32)
        for k in range(N_DEV):
            sc = jnp.where(row == k, sc_ref[k], sc)
        o_ref[...] = jnp.dot(sc, w_ref[...], preferred_element_type=F32, precision=lax.Precision.HIGHEST)

    return pl.pallas_call(
        body, name="ada_forward", grid=(n // tn,), out_shape=jax.ShapeDtypeStruct((N_DEV, n), F32),
        in_specs=[_const_spec((N_DEV, 1, d)), pl.BlockSpec((d, tn), lambda j: (0, j))],
        out_specs=pl.BlockSpec((N_DEV, tn), lambda j: (0, j)),
        compiler_params=_cparams(("parallel",), 2 * d * tn * 4 + N_DEV * N_DEV * d * 8),
    )(sc_all, w_ada)


def _mod_finish(m_all, b_ada):
    _, _, n = m_all.shape

    def body(m_ref, b_ref, o_ref):
        x, y, c = _place()
        me = 4 * x + 2 * y + c
        row = lax.broadcasted_iota(jnp.int32, (N_DEV, n), 0)
        for j in range(N_CHIPS):
            blk = m_ref[2 * j]
            mine = jnp.sum(jnp.where(row == me, blk, 0.0), axis=0, keepdims=True)
            o_ref[:, j * n:(j + 1) * n] = mine + b_ref[:, j * n:(j + 1) * n]

    return pl.pallas_call(
        body, name="mod_finish", out_shape=jax.ShapeDtypeStruct((1, N_CHIPS * n), F32),
        in_specs=[pl.BlockSpec(memory_space=pltpu.VMEM), pl.BlockSpec(memory_space=pltpu.VMEM)],
        out_specs=pl.BlockSpec(memory_space=pltpu.VMEM),
    )(m_all, b_ada)


def _ada_backward(sc_t, dmod_cols, w, m, v):
    d, n = w.shape
    tr, tn = _row_tile(d, 512), _pick_tile(n, COL_TILE)

    def body(s_ref, dm_ref, w_ref, m_ref, v_ref, g_ref, d_ref, nm_ref, nv_ref):
        g = jnp.dot(s_ref[...], dm_ref[...], preferred_element_type=F32, precision=lax.Precision.HIGHEST)
        dl, nm, nv = _adam_math(w_ref[...], g, m_ref[...], v_ref[...])
        g_ref[...] = g
        d_ref[...] = dl
        nm_ref[...] = nm
        nv_ref[...] = nv

    tile = pl.BlockSpec((tr, tn), lambda i, j: (i, j))
    out = jax.ShapeDtypeStruct((d, n), F32)
    return pl.pallas_call(
        body, name="ada_backward", grid=(d // tr, n // tn), out_shape=(out,) * 4,
        in_specs=[pl.BlockSpec((tr, N_DEV), lambda i, j: (i, 0)), pl.BlockSpec((N_DEV, tn), lambda i, j: (0, j)),
                  tile, tile, tile],
        out_specs=(tile,) * 4,
        compiler_params=_cparams(("parallel", "parallel"), 2 * 8 * tr * tn * 4),
    )(sc_t, dmod_cols, w, m, v)


def _prenorm(x, mod, pre_g):
    s, d = x.shape
    tm = _row_tile(s, 512)

    def body(x_ref, mod_ref, g_ref, h_ref):
        xv = x_ref[...]
        r = lax.rsqrt(jnp.mean(xv * xv, axis=1, keepdims=True) + EPS)
        xn = xv * r * g_ref[...]
        h_ref[...] = (xn * (1.0 + mod_ref[:, d:2 * d]) + mod_ref[:, 0:d]).astype(BF16)

    return pl.pallas_call(
        body, name="prenorm", grid=(s // tm,), out_shape=jax.ShapeDtypeStruct((s, d), BF16),
        in_specs=[pl.BlockSpec((tm, d), lambda i: (i, 0)), _const_spec((1, 3 * d)), _const_spec((1, d))],
        out_specs=pl.BlockSpec((tm, d), lambda i: (i, 0)),
        compiler_params=_cparams(("parallel",), 2 * tm * d * 6 + 4 * tm * d * 4),
    )(x, mod, pre_g)


def _chip():
    return 2 * lax.axis_index("x") + lax.axis_index("y")


def _local_columns(h, w_in_shard, w_merge_shard, b_merge, comm):
    s, k = h.shape
    n1, n2 = w_in_shard.shape[1], w_merge_shard.shape[1]
    tm = _row_tile(s, 1024)

    def body(h_ref, wi_ref, wm_ref, b_ref, p_ref, g_ref):
        hv = h_ref[...]
        p_ref[...] = jnp.dot(hv, wi_ref[...], preferred_element_type=F32).astype(BF16)
        g_ref[...] = _sigmoid(jnp.dot(hv, wm_ref[...], preferred_element_type=F32) + b_ref[...]).astype(BF16)

    in_specs = [pl.BlockSpec((tm, k), lambda i: (i, 0)), _resident((k, n1)), _resident((k, n2)),
                pl.BlockSpec((1, n2), lambda i: (0, _chip()))]
    out_specs = [pl.BlockSpec((tm, n1), lambda i: (i, _chip())), pl.BlockSpec((tm, n2), lambda i: (i, _chip()))]
    est = 2 * tm * (k + n1 + n2) * 2 + k * (n1 + n2) * 2 + 2 * tm * (n1 + n2) * 4
    return _call("local_columns", body, (s // tm,), in_specs, (h, w_in_shard, w_merge_shard, b_merge),
                 [jax.ShapeDtypeStruct((s, N_CHIPS * n1), BF16), jax.ShapeDtypeStruct((s, N_CHIPS * n2), BF16)],
                 out_specs, [], ("arbitrary",), est, comm=comm)


def _other_columns(name, a, b, partial, bias=None, comm=None):
    s, k = a.shape
    _, n = b.shape
    tm, tn = _row_tile(s, 1024), n // N_CHIPS
    col = lambda i, j: (_chip() + 1 + j) % N_CHIPS

    def body(*refs):
        if bias is None:
            a_ref, b_ref, _, o_ref = refs
        else:
            a_ref, b_ref, bias_ref, _, o_ref = refs
        acc = jnp.dot(a_ref[...], b_ref[...], preferred_element_type=F32)
        if bias is not None:
            acc = _sigmoid(acc + bias_ref[...])
        o_ref[...] = acc.astype(BF16)

    in_specs = [pl.BlockSpec((tm, k), lambda i, j: (i, 0)), pl.BlockSpec((k, tn), lambda i, j: (0, col(i, j)))]
    args = [a, b]
    if bias is not None:
        in_specs.append(pl.BlockSpec((1, tn), lambda i, j: (0, col(i, j))))
        args.append(bias)
    in_specs.append(pl.BlockSpec(memory_space=pl.ANY))
    args.append(partial)
    (out,), extra = _call(name, body, (s // tm, N_CHIPS - 1), in_specs, args, [jax.ShapeDtypeStruct((s, n), BF16)],
                          [pl.BlockSpec((tm, tn), lambda i, j: (i, col(i, j)))], [], ("parallel", "arbitrary"),
                          2 * (tm * k + k * tn + tm * tn) * 2 + 2 * tm * tn * 4, comm=comm,
                          aliases={len(args) - 1: 0})
    return out, extra


def _col_specs(off, width, rows, row_index):
    assert off % COL_TILE == 0 and width % COL_TILE == 0
    return [pl.BlockSpec((rows, COL_TILE), functools.partial(lambda p, *ids: (row_index(*ids), p), off // COL_TILE + p))
            for p in range(width // COL_TILE)]


def _cat(refs):
    vals = [r[...] for r in refs]
    return vals[0] if len(vals) == 1 else jnp.concatenate(vals, axis=1)


def _attn_mask(n, s):
    kpos = (n - 1) * BLOCK + lax.broadcasted_iota(jnp.int32, (1, SPAN), 1)
    return (kpos >= 0) & (kpos < s)


def _sink_column(sink_ref, kh):
    rows = GQA_GROUP * BLOCK
    grp = lax.broadcasted_iota(jnp.int32, (rows, 1), 0) // BLOCK
    col = jnp.zeros((rows, 1), F32)
    for g in range(GQA_GROUP):
        col = jnp.where(grp == g, sink_ref[0, kh * GQA_GROUP + g], col)
    return col


def _stack_heads(x, kh):
    base = kh * GQA_GROUP * HEAD_DIM
    return jnp.concatenate([x[:, base + g * HEAD_DIM: base + (g + 1) * HEAD_DIM] for g in range(GQA_GROUP)], axis=0)


def _softmax_parts(qs, k, bias, valid, sink_col):
    sc = lax.dot_general(qs, k, (((1,), (1,)), ((), ())), preferred_element_type=F32)
    sc = sc * (HEAD_DIM ** -0.5) + bias
    sc = jnp.where(valid, sc, NEG_INF)
    mx = jnp.maximum(jnp.max(sc, axis=1, keepdims=True), sink_col)
    e = jnp.exp(sc - mx)
    es = jnp.exp(sink_col - mx)
    inv = 1.0 / (jnp.sum(e, axis=1, keepdims=True) + es)
    return e * inv, es * inv


def _attn_forward(proj, bias2, sink, s):
    nblk = s // BLOCK
    nq = ATTN_WIDTH // COL_TILE
    kv_col = ATTN_WIDTH // COL_TILE
    assert 2 * KV_WIDTH == COL_TILE

    def body(*refs):
        q_refs = refs[:nq]
        kvp, kvc, kvn, bias_ref, sink_ref, o_ref = refs[nq:]
        n = pl.program_id(0)
        q = _cat(q_refs)
        kv = jnp.concatenate([kvp[...], kvc[...], kvn[...]], axis=0)
        valid = _attn_mask(n, s)
        for kh in range(N_KV_HEADS):
            qs = _stack_heads(q, kh)
            k = kv[:, kh * HEAD_DIM:(kh + 1) * HEAD_DIM]
            v = kv[:, KV_WIDTH + kh * HEAD_DIM: KV_WIDTH + (kh + 1) * HEAD_DIM]
            p, _ = _softmax_parts(qs, k, bias_ref[kh], valid, _sink_column(sink_ref, kh))
            o = jnp.dot(p.astype(BF16), v, preferred_element_type=F32)
            for g in range(GQA_GROUP):
                h = kh * GQA_GROUP + g
                o_ref[:, h * HEAD_DIM:(h + 1) * HEAD_DIM] = o[g * BLOCK:(g + 1) * BLOCK].astype(BF16)

    in_specs = _col_specs(0, ATTN_WIDTH, BLOCK, lambda n: n)
    in_specs += [pl.BlockSpec((BLOCK, COL_TILE), lambda n: (jnp.maximum(n - 1, 0), kv_col)),
                 pl.BlockSpec((BLOCK, COL_TILE), lambda n: (n, kv_col)),
                 pl.BlockSpec((BLOCK, COL_TILE), lambda n: (jnp.minimum(n + 1, nblk - 1), kv_col)),
                 _const_spec((N_KV_HEADS, GQA_GROUP * BLOCK, SPAN)),
                 pl.BlockSpec(memory_space=pltpu.SMEM)]
    return pl.pallas_call(
        body, name="attn_forward", grid=(nblk,), out_shape=jax.ShapeDtypeStruct((s, ATTN_WIDTH), BF16),
        in_specs=in_specs, out_specs=pl.BlockSpec((BLOCK, ATTN_WIDTH), lambda n: (n, 0)),
        compiler_params=_cparams(("parallel",), 16 << 20),
    )(*([proj] * (nq + 3)), bias2, sink)


def _pool_positions(i, tm, s, width):
    pos = i * tm - HALO + lax.broadcasted_iota(jnp.int32, (tm + 2 * HALO, width), 0)
    return pos, (pos >= 0) & (pos < s)


def _pool_count(pos, w, s):
    return (jnp.minimum(pos + w // 2, s) - jnp.maximum(pos - w // 2, 0)).astype(F32)


def _halo_specs_cols(off, width, tm, s):
    per = tm // HALO
    last = s // HALO - 1
    prev = _col_specs(off, width, HALO, lambda i: jnp.maximum(i * per - 1, 0))
    nxt = _col_specs(off, width, HALO, lambda i: jnp.minimum((i + 1) * per, last))
    return prev, nxt


def _branches(proj, attn, g, w_bra, w_brp, w_grp, pool_scale, s, d):
    a_w, p_w = ATTN_WIDTH, pool_scale.shape[1]
    cg = p_w // N_POOL_GROUPS
    tm = _row_tile(s, ROW_TILE)
    off_ga = ATTN_WIDTH + 2 * KV_WIDTH
    off_u = off_ga + a_w
    off_gp = off_u + p_w
    n_ga, n_u, n_gp = a_w // COL_TILE, p_w // COL_TILE, p_w // COL_TILE

    def body(*refs):
        it = iter(refs)
        attn_ref = next(it)
        ga_refs = [next(it) for _ in range(n_ga)]
        u_refs = [next(it) for _ in range(n_u)]
        up_refs = [next(it) for _ in range(n_u)]
        un_refs = [next(it) for _ in range(n_u)]
        gp_refs = [next(it) for _ in range(n_gp)]
        g_ref, wa_ref, wp_ref, wg_ref, ps_ref = (next(it) for _ in range(5))
        ya_ref, yp_ref, za_ref, zp_ref, mg_ref, pooled_ref, mixed_ref = (next(it) for _ in range(7))
        i = pl.program_id(0)
        ya = (attn_ref[...].astype(F32) * _silu(_cat(ga_refs).astype(F32))).astype(BF16)
        ya_ref[...] = ya
        za = jnp.dot(ya, wa_ref[...], preferred_element_type=F32)
        za_ref[...] = za.astype(BF16)

        u = _cat(u_refs).astype(F32)
        ext = jnp.concatenate([_cat(up_refs).astype(F32), u, _cat(un_refs).astype(F32)], axis=0)
        pos, ok = _pool_positions(i, tm, s, cg)
        mixed = []
        for gi, w in enumerate(POOL_SIZES):
            e = jnp.where(ok, ext[:, gi * cg:(gi + 1) * cg], 0.0)
            acc = e[HALO - w // 2: HALO - w // 2 + tm]
            for dd in range(-w // 2 + 1, w // 2):
                acc = acc + e[HALO + dd: HALO + dd + tm]
            cnt = _pool_count(pos[HALO:HALO + tm], w, s)
            pooled = (acc / cnt - u[:, gi * cg:(gi + 1) * cg]).astype(BF16)
            pooled_ref[:, gi * cg:(gi + 1) * cg] = pooled
            mixed.append(jnp.dot(pooled, wg_ref[gi], preferred_element_type=F32))
        mixed = jnp.concatenate(mixed, axis=1)
        mixed_ref[...] = mixed.astype(BF16)
        yp = (mixed * ps_ref[...] * _silu(_cat(gp_refs).astype(F32))).astype(BF16)
        yp_ref[...] = yp
        zp = jnp.dot(yp, wp_ref[...], preferred_element_type=F32)
        zp_ref[...] = zp.astype(BF16)
        gate = g_ref[...].astype(F32)
        mg_ref[...] = (gate[:, :d] * za + gate[:, d:] * zp).astype(BF16)

    row = lambda i: i
    u_prev, u_next = _halo_specs_cols(off_u, p_w, tm, s)
    in_specs = [pl.BlockSpec((tm, a_w), lambda i: (i, 0))]
    in_specs += _col_specs(off_ga, a_w, tm, row) + _col_specs(off_u, p_w, tm, row) + u_prev + u_next
    in_specs += _col_specs(off_gp, p_w, tm, row)
    in_specs += [pl.BlockSpec((tm, 2 * d), lambda i: (i, 0)), _resident((a_w, d)), _resident((p_w, d)),
                 _resident((N_POOL_GROUPS, cg, cg)), _const_spec((1, p_w))]
    n_proj = n_ga + 3 * n_u + n_gp
    tile = lambda w: pl.BlockSpec((tm, w), lambda i: (i, 0))
    out_widths = (a_w, p_w, d, d, d, p_w, p_w)
    est = 2 * tm * (a_w + a_w + 2 * p_w + 2 * d + sum(out_widths)) * 2 + (a_w + p_w) * d * 2 + 6 * tm * d * 4
    return pl.pallas_call(
        body, name="branches", grid=(s // tm,),
        out_shape=tuple(jax.ShapeDtypeStruct((s, w), BF16) for w in out_widths),
        in_specs=in_specs, out_specs=tuple(tile(w) for w in out_widths),
        compiler_params=_cparams(("parallel",), est),
    )(attn, *([proj] * n_proj), g, w_bra, w_brp, w_grp, pool_scale)


def _sub_rows(tm, sub=128):
    sub = min(sub, tm)
    return [pl.ds(r * sub, sub) for r in range(tm // sub)]


def _out_loss(merged, x, target, w_out, post_g, mod, s, d):
    tm = _row_tile(s, ROW_TILE)
    nsteps = s // tm

    def body(mg_ref, x_ref, t_ref, w_ref, pg_ref, mod_ref, dout_ref, do_ref, loss_ref, dgate_ref, dpg_ref, lacc):
        i = pl.program_id(0)

        @pl.when(i == 0)
        def _():
            lacc[...] = jnp.zeros_like(lacc)
            dgate_ref[...] = jnp.zeros_like(dgate_ref)
            dpg_ref[...] = jnp.zeros_like(dpg_ref)

        pg = pg_ref[...]
        gate = mod_ref[:, 2 * d:3 * d]
        for rows in _sub_rows(tm):
            o = jnp.dot(mg_ref[rows, :], w_ref[...], preferred_element_type=F32)
            r = lax.rsqrt(jnp.mean(o * o, axis=1, keepdims=True) + EPS)
            ohat = o * r
            y = ohat * pg
            e = x_ref[rows, :] + gate * y - t_ref[rows, :]
            lacc[...] += jnp.sum(e * e, axis=0, keepdims=True)
            dout = e * (1.0 / d)
            dout_ref[rows, :] = dout
            dgate_ref[...] += jnp.sum(dout * y, axis=0, keepdims=True)
            dy = dout * gate
            dpg_ref[...] += jnp.sum(dy * ohat, axis=0, keepdims=True)
            dohat = dy * pg
            do = r * (dohat - ohat * jnp.mean(dohat * ohat, axis=1, keepdims=True))
            do_ref[rows, :] = do.astype(BF16)

        @pl.when(i == nsteps - 1)
        def _():
            loss_ref[...] = (0.5 / d) * jnp.sum(lacc[...], axis=1, keepdims=True)

    tile = pl.BlockSpec((tm, d), lambda i: (i, 0))
    vec = _const_spec((1, d))
    return pl.pallas_call(
        body, name="out_loss", grid=(nsteps,),
        out_shape=(jax.ShapeDtypeStruct((s, d), F32), jax.ShapeDtypeStruct((s, d), BF16),
                   jax.ShapeDtypeStruct((1, 1), F32), jax.ShapeDtypeStruct((1, d), F32),
                   jax.ShapeDtypeStruct((1, d), F32)),
        in_specs=[tile, tile, tile, _resident((d, d)), vec, _const_spec((1, 3 * d))],
        out_specs=(tile, tile, _const_spec((1, 1)), vec, vec),
        scratch_shapes=[pltpu.VMEM((1, d), F32)],
        compiler_params=_cparams(("arbitrary",), 2 * tm * d * (2 + 4 + 4 + 4 + 2) + d * d * 2 + 8 * tm * d * 4),
    )(merged, x, target, w_out, post_g, mod)


def _bwd_out(d_o, g, za, zp, w_out, s, d, comm=None):
    tm = _row_tile(s, ROW_TILE)

    def body(do_ref, g_ref, za_ref, zp_ref, w_ref, dza_ref, dzp_ref, dgl_ref, dbm_ref):
        i = pl.program_id(0)

        @pl.when(i == 0)
        def _():
            dbm_ref[...] = jnp.zeros_like(dbm_ref)

        do = do_ref[...]
        for j in range(d // COL_TILE):
            ca = pl.ds(j * COL_TILE, COL_TILE)
            cp = pl.ds(d + j * COL_TILE, COL_TILE)
            dm = lax.dot_general(do, w_ref[ca, :], (((1,), (1,)), ((), ())), preferred_element_type=F32)
            ga, gp = g_ref[:, ca].astype(F32), g_ref[:, cp].astype(F32)
            dza_ref[:, ca] = (dm * ga).astype(BF16)
            dzp_ref[:, ca] = (dm * gp).astype(BF16)
            dla = dm * za_ref[:, ca].astype(F32) * ga * (1.0 - ga)
            dlp = dm * zp_ref[:, ca].astype(F32) * gp * (1.0 - gp)
            dgl_ref[:, ca] = dla.astype(BF16)
            dgl_ref[:, cp] = dlp.astype(BF16)
            dbm_ref[:, ca] += jnp.sum(dla, axis=0, keepdims=True)
            dbm_ref[:, cp] += jnp.sum(dlp, axis=0, keepdims=True)

    tile = pl.BlockSpec((tm, d), lambda i: (i, 0))
    wide = pl.BlockSpec((tm, 2 * d), lambda i: (i, 0))
    return _call("bwd_out", body, (s // tm,), [tile, wide, tile, tile, _resident((d, d))], (d_o, g, za, zp, w_out),
                 (jax.ShapeDtypeStruct((s, d), BF16), jax.ShapeDtypeStruct((s, d), BF16),
                  jax.ShapeDtypeStruct((s, 2 * d), BF16), jax.ShapeDtypeStruct((1, 2 * d), F32)),
                 (tile, tile, wide, _const_spec((1, 2 * d))), [], ("arbitrary",),
                 2 * tm * d * 2 * 9 + d * d * 2 + 8 * tm * d * 4, comm=comm)


def _bwd_branches(dza, dzp, attn, proj, mixed, w_bra, w_brp, w_grp, pool_scale, s, d, comm=None):
    a_w, p_w = ATTN_WIDTH, pool_scale.shape[1]
    cg = p_w // N_POOL_GROUPS
    tm = _row_tile(s, ROW_TILE)
    off_ga = ATTN_WIDTH + 2 * KV_WIDTH
    off_gp = off_ga + a_w + p_w
    n_ga, n_gp = a_w // COL_TILE, p_w // COL_TILE

    def body(*refs):
        it = iter(refs)
        dza_ref, dzp_ref, attn_ref = next(it), next(it), next(it)
        ga_refs = [next(it) for _ in range(n_ga)]
        gp_refs = [next(it) for _ in range(n_gp)]
        mixed_ref, wa_ref, wp_ref, wg_ref, ps_ref = (next(it) for _ in range(5))
        dattn_ref, dga_ref, dgp_ref, dmix_ref, dpool_ref, dps_ref = (next(it) for _ in range(6))
        i = pl.program_id(0)

        @pl.when(i == 0)
        def _():
            dps_ref[...] = jnp.zeros_like(dps_ref)

        dya = lax.dot_general(dza_ref[...], wa_ref[...], (((1,), (1,)), ((), ())), preferred_element_type=F32)
        ga = _cat(ga_refs).astype(F32)
        dattn_ref[...] = (dya * _silu(ga)).astype(BF16)
        dga_ref[...] = (dya * attn_ref[...].astype(F32) * _dsilu(ga)).astype(BF16)

        dyp = lax.dot_general(dzp_ref[...], wp_ref[...], (((1,), (1,)), ((), ())), preferred_element_type=F32)
        gp = _cat(gp_refs).astype(F32)
        mixed = mixed_ref[...].astype(F32)
        ps = ps_ref[...]
        sg = _silu(gp)
        dgp_ref[...] = (dyp * mixed * ps * _dsilu(gp)).astype(BF16)
        dps_ref[...] += jnp.sum(dyp * sg * mixed, axis=0, keepdims=True)
        dmix = (dyp * sg * ps).astype(BF16)
        dmix_ref[...] = dmix
        for gi in range(N_POOL_GROUPS):
            dp = lax.dot_general(dmix[:, gi * cg:(gi + 1) * cg], wg_ref[gi], (((1,), (1,)), ((), ())),
                                 preferred_element_type=F32)
            dpool_ref[:, gi * cg:(gi + 1) * cg] = dp.astype(BF16)

    row = lambda i: i
    tile = lambda w: pl.BlockSpec((tm, w), lambda i: (i, 0))
    in_specs = [tile(d), tile(d), tile(a_w)] + _col_specs(off_ga, a_w, tm, row) + _col_specs(off_gp, p_w, tm, row)
    in_specs += [tile(p_w), _resident((a_w, d)), _resident((p_w, d)), _resident((N_POOL_GROUPS, cg, cg)),
                 _const_spec((1, p_w))]
    out_widths = (a_w, a_w, p_w, p_w, p_w)
    est = 2 * tm * (2 * d + 2 * a_w + 2 * p_w + sum(out_widths)) * 2 + (a_w + p_w) * d * 2 + 8 * tm * a_w * 4
    return _call("bwd_branches", body, (s // tm,), in_specs,
                 (dza, dzp, attn, *([proj] * (n_ga + n_gp)), mixed, w_bra, w_brp, w_grp, pool_scale),
                 tuple(jax.ShapeDtypeStruct((s, w), BF16) for w in out_widths) + (jax.ShapeDtypeStruct((1, p_w), F32),),
                 tuple(tile(w) for w in out_widths) + (_const_spec((1, p_w)),), [], ("arbitrary",), est, comm=comm)


def _pool_backward(dpooled, s):
    _, p_w = dpooled.shape
    cg = p_w // N_POOL_GROUPS
    tm = _row_tile(s, ROW_TILE)
    per, last = tm // HALO, s // HALO - 1

    def body(dp_ref, prev_ref, next_ref, du_ref):
        i = pl.program_id(0)
        dp = dp_ref[...].astype(F32)
        ext = jnp.concatenate([prev_ref[...].astype(F32), dp, next_ref[...].astype(F32)], axis=0)
        pos, ok = _pool_positions(i, tm, s, cg)
        for gi, w in enumerate(POOL_SIZES):
            t = jnp.where(ok, ext[:, gi * cg:(gi + 1) * cg], 0.0) / _pool_count(pos, w, s)
            acc = t[HALO - w // 2 + 1: HALO - w // 2 + 1 + tm]
            for dd in range(-w // 2 + 2, w // 2 + 1):
                acc = acc + t[HALO + dd: HALO + dd + tm]
            du_ref[:, gi * cg:(gi + 1) * cg] = (acc - dp[:, gi * cg:(gi + 1) * cg]).astype(BF16)

    return pl.pallas_call(
        body, name="pool_backward", grid=(s // tm,), out_shape=jax.ShapeDtypeStruct((s, p_w), BF16),
        in_specs=[pl.BlockSpec((tm, p_w), lambda i: (i, 0)),
                  pl.BlockSpec((HALO, p_w), lambda i: (jnp.maximum(i * per - 1, 0), 0)),
                  pl.BlockSpec((HALO, p_w), lambda i: (jnp.minimum((i + 1) * per, last), 0))],
        out_specs=pl.BlockSpec((tm, p_w), lambda i: (i, 0)),
        compiler_params=_cparams(("parallel",), 4 * tm * p_w * 2 + 8 * tm * p_w * 4),
    )(dpooled, dpooled, dpooled)


def _attn_backward(proj, dattn, bias2, sink, s, comm=None):
    nblk = s // BLOCK
    nq = ATTN_WIDTH // COL_TILE
    kv_col = ATTN_WIDTH // COL_TILE
    rows = GQA_GROUP * BLOCK
    scale = HEAD_DIM ** -0.5

    def body(*refs):
        q_refs = refs[:nq]
        kvp, kvc, kvn, do_ref, bias_ref, sink_ref, dq_ref, dkv_ref, dbias_ref, dsink_ref, acc, sacc = refs[nq:]
        n = pl.program_id(0)

        @pl.when(n == 0)
        def _():
            acc[...] = jnp.zeros_like(acc)
            sacc[...] = jnp.zeros_like(sacc)
            dbias_ref[...] = jnp.zeros_like(dbias_ref)
            dsink_ref[...] = jnp.zeros_like(dsink_ref)

        @pl.when(jnp.logical_and(n >= 1, n < nblk))
        def _():
            acc[(n + 1) % 3] = jnp.zeros((BLOCK, 2 * KV_WIDTH), F32)

        @pl.when(n < nblk)
        def _():
            q = _cat(q_refs)
            do = do_ref[...]
            kv = jnp.concatenate([kvp[...], kvc[...], kvn[...]], axis=0)
            valid = _attn_mask(n, s)
            for kh in range(N_KV_HEADS):
                qs = _stack_heads(q, kh)
                dos = _stack_heads(do, kh)
                k = kv[:, kh * HEAD_DIM:(kh + 1) * HEAD_DIM]
                v = kv[:, KV_WIDTH + kh * HEAD_DIM: KV_WIDTH + (kh + 1) * HEAD_DIM]
                p, ps = _softmax_parts(qs, k, bias_ref[kh], valid, _sink_column(sink_ref, kh))
                dp = lax.dot_general(dos, v, (((1,), (1,)), ((), ())), preferred_element_type=F32)
                delta = jnp.sum(p * dp, axis=1, keepdims=True)
                ds = p * (dp - delta)
                dbias_ref[kh] += ds
                sacc[kh] += -ps * delta
                dsb = ds.astype(BF16)
                dq = jnp.dot(dsb, k, preferred_element_type=F32) * scale
                for g in range(GQA_GROUP):
                    h = kh * GQA_GROUP + g
                    dq_ref[:, h * HEAD_DIM:(h + 1) * HEAD_DIM] = dq[g * BLOCK:(g + 1) * BLOCK].astype(BF16)
                dk = lax.dot_general(dsb, qs, (((0,), (0,)), ((), ())), preferred_element_type=F32) * scale
                dv = lax.dot_general(p.astype(BF16), dos, (((0,), (0,)), ((), ())), preferred_element_type=F32)
                for j in range(3):
                    slot = (n + 2 + j) % 3
                    acc[slot, :, kh * HEAD_DIM:(kh + 1) * HEAD_DIM] += dk[j * BLOCK:(j + 1) * BLOCK]
                    acc[slot, :, KV_WIDTH + kh * HEAD_DIM: KV_WIDTH + (kh + 1) * HEAD_DIM] += dv[j * BLOCK:(j + 1) * BLOCK]

        dkv_ref[...] = acc[(n + 2) % 3].astype(BF16)

        @pl.when(n == nblk)
        def _():
            lane = lax.broadcasted_iota(jnp.int32, (1, LANE), 1)
            out = jnp.zeros((1, LANE), F32)
            for kh in range(N_KV_HEADS):
                col = sacc[kh]
                for g in range(GQA_GROUP):
                    out = jnp.where(lane == kh * GQA_GROUP + g, jnp.sum(col[g * BLOCK:(g + 1) * BLOCK]), out)
            dsink_ref[...] = out

    qi = lambda n: jnp.minimum(n, nblk - 1)
    in_specs = _col_specs(0, ATTN_WIDTH, BLOCK, qi)
    in_specs += [pl.BlockSpec((BLOCK, COL_TILE), lambda n: (jnp.maximum(qi(n) - 1, 0), kv_col)),
                 pl.BlockSpec((BLOCK, COL_TILE), lambda n: (qi(n), kv_col)),
                 pl.BlockSpec((BLOCK, COL_TILE), lambda n: (jnp.minimum(qi(n) + 1, nblk - 1), kv_col)),
                 pl.BlockSpec((BLOCK, ATTN_WIDTH), lambda n: (qi(n), 0)),
                 _const_spec((N_KV_HEADS, rows, SPAN)),
                 pl.BlockSpec(memory_space=pltpu.SMEM)]
    return _call("attn_backward", body, (nblk + 1,), in_specs, (*([proj] * (nq + 3)), dattn, bias2, sink),
                 (jax.ShapeDtypeStruct((s, ATTN_WIDTH), BF16), jax.ShapeDtypeStruct((s, 2 * KV_WIDTH), BF16),
                  jax.ShapeDtypeStruct((N_KV_HEADS, rows, SPAN), F32), jax.ShapeDtypeStruct((1, LANE), F32)),
                 (pl.BlockSpec((BLOCK, ATTN_WIDTH), lambda n: (qi(n), 0)),
                  pl.BlockSpec((BLOCK, 2 * KV_WIDTH), lambda n: (jnp.clip(n - 1, 0, nblk - 1), 0)),
                  _const_spec((N_KV_HEADS, rows, SPAN)), _const_spec((1, LANE))),
                 [pltpu.VMEM((3, BLOCK, 2 * KV_WIDTH), F32), pltpu.VMEM((N_KV_HEADS, rows, 1), F32)],
                 ("arbitrary",), 24 << 20, comm=comm)


def _pick_tile(n, cap):
    t = cap - cap % LANE
    while n % t:
        t -= LANE
    return t


def _matmul_tn(name, a, b, comm=None):
    s, m = a.shape
    _, n = b.shape
    tk = _row_tile(s, 1024)
    tm, tn = _pick_tile(m, 1024), _pick_tile(n, 1152)
    nk = s // tk

    def body(a_ref, b_ref, o_ref, acc):
        k = pl.program_id(2)

        @pl.when(k == 0)
        def _():
            acc[...] = jnp.zeros_like(acc)

        acc[...] += lax.dot_general(a_ref[...], b_ref[...], (((0,), (0,)), ((), ())), preferred_element_type=F32)

        @pl.when(k == nk - 1)
        def _():
            o_ref[...] = acc[...].astype(BF16)

    (out,), extra = _call(
        name, body, (m // tm, n // tn, nk),
        [pl.BlockSpec((tk, tm), lambda i, j, k: (k, i)), pl.BlockSpec((tk, tn), lambda i, j, k: (k, j))], (a, b),
        [jax.ShapeDtypeStruct((m, n), BF16)], [pl.BlockSpec((tm, tn), lambda i, j, k: (i, j))],
        [pltpu.VMEM((tm, tn), F32)], ("parallel", "parallel", "arbitrary"),
        2 * tk * (tm + tn) * 2 + tm * tn * (4 + 4 + 4), comm=comm)
    return out, extra


def _pool_weight_grad(pooled, dmix, s):
    _, p_w = pooled.shape
    cg = p_w // N_POOL_GROUPS
    tk = _row_tile(s, 512)
    nk = s // tk

    def body(a_ref, b_ref, o_ref, acc):
        k = pl.program_id(1)

        @pl.when(k == 0)
        def _():
            acc[...] = jnp.zeros_like(acc)

        acc[...] += lax.dot_general(a_ref[...], b_ref[...], (((0,), (0,)), ((), ())), preferred_element_type=F32)

        @pl.when(k == nk - 1)
        def _():
            o_ref[0] = acc[...].astype(BF16)

    return pl.pallas_call(
        body, name="pool_weight_grad", grid=(N_POOL_GROUPS, nk),
        out_shape=jax.ShapeDtypeStruct((N_POOL_GROUPS, cg, cg), BF16),
        in_specs=[pl.BlockSpec((tk, cg), lambda g, k: (k, g)), pl.BlockSpec((tk, cg), lambda g, k: (k, g))],
        out_specs=pl.BlockSpec((1, cg, cg), lambda g, k: (g, 0, 0)),
        scratch_shapes=[pltpu.VMEM((cg, cg), F32)],
        compiler_params=_cparams(("parallel", "arbitrary"), 16 << 20),
    )(pooled, dmix)


def _bwd_hidden(dproj, dgl, w_in, w_merge, s, d, comm=None):
    tm = _row_tile(s, 1024)
    n_in = dproj.shape[1] // COL_TILE
    n_mg = dgl.shape[1] // COL_TILE
    nk = n_in + n_mg

    def body(dp_ref, dg_ref, wi_ref, wm_ref, dh_ref):
        k = pl.program_id(1)

        @pl.when(k == 0)
        def _():
            dh_ref[...] = lax.dot_general(dp_ref[...], wi_ref[...], (((1,), (1,)), ((), ())),
                                          preferred_element_type=F32)

        @pl.when(jnp.logical_and(k > 0, k < n_in))
        def _():
            dh_ref[...] += lax.dot_general(dp_ref[...], wi_ref[...], (((1,), (1,)), ((), ())),
                                           preferred_element_type=F32)

        @pl.when(k >= n_in)
        def _():
            dh_ref[...] += lax.dot_general(dg_ref[...], wm_ref[...], (((1,), (1,)), ((), ())),
                                           preferred_element_type=F32)

    in_specs = [pl.BlockSpec((tm, COL_TILE), lambda i, k: (i, jnp.minimum(k, n_in - 1))),
                pl.BlockSpec((tm, COL_TILE), lambda i, k: (i, jnp.maximum(k - n_in, 0))),
                pl.BlockSpec((d, COL_TILE), lambda i, k: (0, jnp.minimum(k, n_in - 1))),
                pl.BlockSpec((d, COL_TILE), lambda i, k: (0, jnp.maximum(k - n_in, 0)))]
    est = 2 * 2 * (tm + d) * COL_TILE * 2 + 3 * tm * d * 4
    (dh,), extra = _call("bwd_hidden", body, (s // tm, nk), in_specs, (dproj, dgl, w_in, w_merge),
                         [jax.ShapeDtypeStruct((s, d), F32)], [pl.BlockSpec((tm, d), lambda i, k: (i, 0))], [],
                         ("parallel", "arbitrary"), est, comm=comm)
    return dh, extra


def _bwd_prenorm(dh, x, dout, mod, pre_g, s, d):
    tm = _row_tile(s, ROW_TILE)

    def body(dh_ref, x_ref, dout_ref, mod_ref, pg_ref, gx_ref, dsh_ref, dsc_ref, dpg_ref):
        i = pl.program_id(0)

        @pl.when(i == 0)
        def _():
            dsh_ref[...] = jnp.zeros_like(dsh_ref)
            dsc_ref[...] = jnp.zeros_like(dsc_ref)
            dpg_ref[...] = jnp.zeros_like(dpg_ref)

        dh = dh_ref[...]
        xv = x_ref[...]
        r = lax.rsqrt(jnp.mean(xv * xv, axis=1, keepdims=True) + EPS)
        xhat = xv * r
        pg = pg_ref[...]
        one_scale = 1.0 + mod_ref[:, d:2 * d]
        dsh_ref[...] += jnp.sum(dh, axis=0, keepdims=True)
        dsc_ref[...] += jnp.sum(dh * xhat, axis=0, keepdims=True) * pg
        dpg_ref[...] += jnp.sum(dh * xhat, axis=0, keepdims=True) * one_scale
        dxh = dh * (one_scale * pg)
        dx = r * (dxh - xhat * jnp.mean(dxh * xhat, axis=1, keepdims=True))
        gx_ref[...] = dout_ref[...] + dx

    tile = pl.BlockSpec((tm, d), lambda i: (i, 0))
    vec = _const_spec((1, d))
    return pl.pallas_call(
        body, name="bwd_prenorm", grid=(s // tm,),
        out_shape=(jax.ShapeDtypeStruct((s, d), F32),) + (jax.ShapeDtypeStruct((1, d), F32),) * 3,
        in_specs=[tile, tile, tile, _const_spec((1, 3 * d)), vec], out_specs=(tile, vec, vec, vec),
        compiler_params=_cparams(("arbitrary",), 2 * 4 * tm * d * 4 + 6 * tm * d * 4),
    )(dh, x, dout, mod, pre_g)


def _pad_lanes(v, width):
    return jnp.pad(v, ((0, 0), (0, width - v.shape[1])))


def kernel(x, c, rel_bias_table, w_ada, b_ada, pre_norm_g, post_norm_g, w_in, attn_sink, w_pool_group, pool_scale, w_branch_attn, w_branch_pool, w_merge, b_merge, w_out, loss_target, m_rel_bias_table, m_w_ada, m_b_ada, m_pre_norm_g, m_post_norm_g, m_w_in, m_attn_sink, m_w_pool_group, m_pool_scale, m_w_branch_attn, m_w_branch_pool, m_w_merge, m_b_merge, m_w_out, v_rel_bias_table, v_w_ada, v_b_ada, v_pre_norm_g, v_post_norm_g, v_w_in, v_attn_sink, v_w_pool_group, v_pool_scale, v_w_branch_attn, v_w_branch_pool, v_w_merge, v_b_merge, v_w_out):
    _, s, d = x.shape
    p_w = pool_scale.shape[-1]
    cg = p_w // N_POOL_GROUPS
    in_w = 2 * ATTN_WIDTH + 2 * KV_WIDTH + 2 * p_w
    x2, t2 = x[0], loss_target[0]
    chip = 2 * lax.axis_index("x") + lax.axis_index("y")

    specs = [_Sharded("col", (d, in_w)), _Sharded("col", (d, 2 * d)), _Sharded("col", (ATTN_WIDTH, d)),
             _Sharded("col", (p_w, d)), _Sharded("row", (d, d)), _Sharded("grp", (N_POOL_GROUPS, cg, cg))]
    shards32 = [w_in[0], w_merge[0], w_branch_attn[0], w_branch_pool[0], w_out[0],
                w_pool_group[0].reshape(N_POOL_GROUPS * cg // N_CHIPS, cg)]
    names = ["w_in", "w_merge", "w_branch_attn", "w_branch_pool", "w_out", "w_pool_group"]
    shards16 = [_cast_bf16("cast_" + nm, w) for nm, w in zip(names, shards32)]
    shards16[5] = shards16[5].reshape(N_POOL_GROUPS, cg // N_CHIPS, cg)

    sc_all = _all_gather8("gather_cond", c, 1, pre=_silu)
    m_all = _all_gather8("gather_mod", _ada_forward(sc_all, w_ada[0]), N_DEV)
    mod = _mod_finish(m_all, b_ada)

    h = _prenorm(x2, mod, pre_norm_g)
    (proj, gates), (wf_in,) = _local_columns(h, shards16[0], shards16[1], b_merge,
                                             comm=_GatherWeights(specs[:1], shards16[:1], middle=0.9))
    proj, (wf_merge,) = _other_columns("proj", h, wf_in, proj, comm=_GatherWeights(specs[1:2], shards16[1:2]))
    gates, (wf_bra, wf_brp, wf_out, wf_grp) = _other_columns("merge_gates", h, wf_merge, gates, bias=b_merge,
                                                             comm=_GatherWeights(specs[2:], shards16[2:]))
    buckets = _t5_buckets()
    bias2 = _bias_table(rel_bias_table, buckets).reshape(N_KV_HEADS, GQA_GROUP * BLOCK, SPAN)
    attn = _attn_forward(proj, bias2, attn_sink, s)
    ya, yp, za, zp, merged, pooled, mixed = _branches(proj, attn, gates, wf_bra, wf_brp, wf_grp, pool_scale, s, d)
    dout, d_o, loss_part, dgate, dpostg = _out_loss(merged, x2, t2, wf_out, post_norm_g, mod, s, d)

    pw_out, _ = _matmul_tn("grad_w_out", merged, d_o)
    (dza, dzp, dgl, dbm), (pc_out,) = _bwd_out(d_o, gates, za, zp, wf_out, s, d,
                                                comm=_ScatterGrads(specs[4:5], [pw_out]))
    pw_bra, _ = _matmul_tn("grad_w_branch_attn", ya, dza)
    pw_brp, _ = _matmul_tn("grad_w_branch_pool", yp, dzp)
    pw_merge, _ = _matmul_tn("grad_w_merge", h, dgl)
    (dattn, dga, dgp, dmix, dpooled, dps), (pc_bra, pc_brp) = _bwd_branches(
        dza, dzp, attn, proj, mixed, wf_bra, wf_brp, wf_grp, pool_scale, s, d,
        comm=_ScatterGrads(specs[2:4], [pw_bra, pw_brp]))
    pw_grp = _pool_weight_grad(pooled, dmix, s)
    du = _pool_backward(dpooled, s)
    near, far = (0, 1, 2, 3, 6), (4, 5, 7)
    (dq, dkv, dbias, dsink), (pc_merge,) = _attn_backward(
        proj, dattn, bias2, attn_sink, s, comm=_ScatterGrads(specs[1:2], [pw_merge], peers=near))
    dproj = jnp.concatenate([dq, dkv, dga, du, dgp], axis=1)
    pw_in, (pc_merge,) = _matmul_tn("grad_w_in", h, dproj,
                                    comm=_ScatterGrads(specs[1:2], [pw_merge], peers=far, into=[pc_merge]))
    dh, (pc_in, pc_grp) = _bwd_hidden(dproj, dgl, wf_in, wf_merge, s, d,
                                      comm=_ScatterGrads([specs[0], specs[5]], [pw_in, pw_grp]))
    gx, dshift, dscale, dpreg = _bwd_prenorm(dh, x2, dout, mod, pre_norm_g, s, d)

    pieces = [pc_in, pc_merge, pc_bra, pc_brp, pc_out]
    weights = [w_in, w_merge, w_branch_attn, w_branch_pool, w_out]
    moms = [m_w_in, m_w_merge, m_w_branch_attn, m_w_branch_pool, m_w_out]
    vars_ = [v_w_in, v_w_merge, v_w_branch_attn, v_w_branch_pool, v_w_out]
    big = {}
    for nm, pc, w, m, v in zip(names, pieces, weights, moms, vars_):
        shape2 = (-1, w.shape[-1])
        res4 = _reduce_adamw("update_" + nm, pc, w.reshape(shape2), m.reshape(shape2), v.reshape(shape2))
        big[nm] = tuple(a.reshape(w.shape) for a in res4)
    hq = cg // N_CHIPS // 2
    g_grp = _reduce16("reduce_w_pool_group", pc_grp.reshape(2, N_DEV, N_POOL_GROUPS * hq, cg))
    g_grp = g_grp.reshape(2, N_POOL_GROUPS, hq, cg).transpose(1, 0, 2, 3).reshape(N_POOL_GROUPS * 2 * hq, cg)
    res3 = _adamw("adamw_w_pool_group", w_pool_group.reshape(-1, cg), g_grp, m_w_pool_group.reshape(-1, cg),
                  v_w_pool_group.reshape(-1, cg))
    big["w_pool_group"] = tuple(a.reshape(w_pool_group.shape) for a in (g_grp,) + tuple(res3))

    dtable = _bias_table_grad(dbias.reshape(N_Q_HEADS, BLOCK, SPAN), buckets)[:, :N_Q_HEADS]
    segs = [("b_ada", jnp.concatenate([dshift, dscale, dgate], axis=1), 3 * d),
            ("pre_norm_g", dpreg, d), ("post_norm_g", dpostg, d), ("attn_sink", dsink, LANE),
            ("pool_scale", dps, p_w), ("b_merge", dbm, 2 * d), ("rel_bias_table", dtable.reshape(1, -1), 2 * LANE)]
    packed = jnp.concatenate([_pad_lanes(v, w) for _, v, w in segs], axis=1)
    rows = _all_gather8("gather_small", packed, 1)[:, 0, :]

    def pack(vals):
        return jnp.concatenate([_pad_lanes(v.reshape(1, -1), w) for v, (_, _, w) in zip(vals, segs)], axis=1)

    small_w = [b_ada, pre_norm_g, post_norm_g, attn_sink, pool_scale, b_merge, rel_bias_table]
    small_m = [m_b_ada, m_pre_norm_g, m_post_norm_g, m_attn_sink, m_pool_scale, m_b_merge, m_rel_bias_table]
    small_v = [v_b_ada, v_pre_norm_g, v_post_norm_g, v_attn_sink, v_pool_scale, v_b_merge, v_rel_bias_table]
    g_small, d_small, nm_small, nv_small = _small_update(rows, pack(small_w), pack(small_m), pack(small_v))
    small = {}
    off = 0
    for (nm, _, w), ref in zip(segs, small_w):
        cut = lambda a: a[:, off:off + ref.size].reshape(ref.shape)
        small[nm] = (cut(g_small), cut(d_small), cut(nm_small), cut(nv_small))
        off += w

    dmod_cols = lax.dynamic_slice_in_dim(rows[:, :3 * d], chip * (3 * d // N_CHIPS), 3 * d // N_CHIPS, axis=1)
    sc_t = sc_all[:, 0, :].T
    g_ada, d_ada, nm_ada, nv_ada = _ada_backward(sc_t, dmod_cols, w_ada[0], m_w_ada[0], v_w_ada[0])
    big["w_ada"] = tuple(a.reshape(w_ada.shape) for a in (g_ada, d_ada, nm_ada, nv_ada))

    loss = lax.psum(loss_part[0, 0], ("x", "y", "c"))
    order = ["rel_bias_table", "w_ada", "b_ada", "pre_norm_g", "post_norm_g", "w_in", "attn_sink", "w_pool_group",
             "pool_scale", "w_branch_attn", "w_branch_pool", "w_merge", "b_merge", "w_out"]
    res = {**big, **small}
    outs = [loss, gx.reshape(x.shape)]
    for part in range(4):
        outs += [res[nm][part] for nm in order]
    return tuple(outs)


def _small_update(rows, w, m, v):
    _, n = rows.shape

    def body(r_ref, w_ref, m_ref, v_ref, g_ref, d_ref, nm_ref, nv_ref):
        g = r_ref[0:1, :]
        for k in range(1, N_DEV):
            g = g + r_ref[k:k + 1, :]
        dl, nm, nv = _adam_math(w_ref[...], g, m_ref[...], v_ref[...])
        g_ref[...] = g
        d_ref[...] = dl
        nm_ref[...] = nm
        nv_ref[...] = nv

    vm = pl.BlockSpec(memory_space=pltpu.VMEM)
    out = jax.ShapeDtypeStruct((1, n), F32)
    return pl.pallas_call(
        body, name="small_update", out_shape=(out,) * 4, in_specs=[vm] * 4, out_specs=(vm,) * 4,
    )(rows, w, m, v)
```

```python
import functools
import math

import numpy as np
import jax
import jax.numpy as jnp
from jax import lax
from jax.experimental import pallas as pl
from jax.experimental.pallas import tpu as pltpu

F32 = jnp.float32
BF16 = jnp.bfloat16
MESH = pl.DeviceIdType.MESH

HEAD_DIM = 128
N_Q_HEADS = 8
N_KV_HEADS = 2
GQA_GROUP = N_Q_HEADS // N_KV_HEADS
ATTN_WIDTH = N_Q_HEADS * HEAD_DIM
KV_WIDTH = N_KV_HEADS * HEAD_DIM
WINDOW = 128
BLOCK = 128
SPAN = BLOCK + 2 * WINDOW
N_BUCKETS = 32
MAX_DISTANCE = 128
POOL_SIZES = (2, 4, 8, 16)
N_POOL_GROUPS = len(POOL_SIZES)
HALO = 16
EPS = 1e-6
NEG_INF = -1e30
ADAM_LR = 0.001
ADAM_B1 = 0.9
ADAM_B2 = 0.999
ADAM_EPS = 1e-08
ADAM_WD = 0.01
ADAM_STEP = 10

N_DEV = 8
N_CHIPS = 4
LANE = 128
COL_TILE = 512
VMEM_CAP = 60000 * 1024
ROW_TILE = 256


def _cparams(sem, est_bytes):
    limit = int(min(max(est_bytes * 5 // 4 + (4 << 20), 16 << 20), VMEM_CAP))
    return pltpu.CompilerParams(dimension_semantics=sem, vmem_limit_bytes=limit)


def _sigmoid(x):
    return jax.nn.sigmoid(x)


def _silu(x):
    return x * _sigmoid(x)


def _dsilu(x):
    s = _sigmoid(x)
    return s * (1.0 + x * (1.0 - s))


def _place():
    x, y, c = lax.axis_index("x"), lax.axis_index("y"), lax.axis_index("c")
    return x, y, c


def _flip(v, bit):
    return (1 - v) if bit else v


def _xor_peer(k):
    x, y, c = _place()
    return (_flip(x, (k >> 2) & 1), _flip(y, (k >> 1) & 1), _flip(c, k & 1))


def _resident(shape):
    nd = len(shape)
    return pl.BlockSpec(shape, lambda *_: (0,) * nd, pipeline_mode=pl.Buffered(1))


def _const_spec(shape):
    nd = len(shape)
    return pl.BlockSpec(shape, lambda *_: (0,) * nd)


CHUNK_BYTES = 256 << 10
MAX_CHUNKS = 32


def _all_gather8(name, x, nrows, pre=None):
    r, n = x.shape

    def body(x_ref, out_ref, stage, send_sems, recv_sems):
        px, py, pc = _place()
        me = 4 * px + 2 * py + pc
        v = x_ref[...]
        if pre is not None:
            v = pre(v)
        stage[...] = v[0:nrows]
        out_ref[me] = v[0:nrows]
        copies = []
        for k in range(1, N_DEV):
            cp = pltpu.make_async_remote_copy(
                src_ref=stage, dst_ref=out_ref.at[me], send_sem=send_sems.at[k - 1], recv_sem=recv_sems.at[k - 1],
                device_id=_xor_peer(k), device_id_type=MESH)
            cp.start()
            copies.append(cp)
        for cp in copies:
            cp.wait()

    return pl.pallas_call(
        body, name=name,
        out_shape=jax.ShapeDtypeStruct((N_DEV, nrows, n), F32),
        in_specs=[pl.BlockSpec(memory_space=pltpu.VMEM)],
        out_specs=pl.BlockSpec(memory_space=pltpu.VMEM),
        scratch_shapes=[pltpu.VMEM((nrows, n), F32), pltpu.SemaphoreType.DMA((N_DEV - 1,)),
                        pltpu.SemaphoreType.DMA((N_DEV - 1,))],
    )(x)


class _Sharded:
    def __init__(self, kind, full_shape):
        self.kind = kind
        self.full_shape = tuple(full_shape)
        if kind == "col":
            r, c = full_shape
            self.shard_shape = (r, c // N_CHIPS)
        elif kind == "row":
            r, c = full_shape
            self.shard_shape = (r // N_CHIPS, c)
        else:
            g, r, c = full_shape
            self.shard_shape = (g, r // N_CHIPS, c)
        self.axis = 1 if kind == "grp" else 0
        s = list(self.shard_shape)
        s[self.axis] //= 2
        self.piece_shape = tuple(s)

    def _rows(self, ref, start, size):
        idx = (slice(None),) * self.axis + (pl.ds(pl.multiple_of(start, 16), size),)
        return ref.at[idx]

    def shard_half(self, ref, hc):
        h = self.piece_shape[self.axis]
        return self._rows(ref, hc * h, h)

    def window(self, ref, chip, hc=None):
        s = self.shard_shape
        if self.kind == "col":
            cols = pl.ds(pl.multiple_of(chip * s[1], LANE), s[1])
            if hc is None:
                return ref.at[:, cols]
            h = s[0] // 2
            return ref.at[pl.ds(pl.multiple_of(hc * h, 16), h), cols]
        n = s[self.axis]
        if hc is None:
            return self._rows(ref, chip * n, n)
        return self._rows(ref, chip * n + hc * (n // 2), n // 2)

    def chunks(self, view, shape, itemsize):
        rows = shape[self.axis]
        nbytes = math.prod(shape) * itemsize
        n = 1
        while 2 * n <= MAX_CHUNKS and nbytes // (2 * n) >= CHUNK_BYTES and rows % (2 * n * 16) == 0:
            n *= 2
        h = rows // n
        return [view.at[(slice(None),) * self.axis + (pl.ds(j * h, h),)] for j in range(n)]


def _remote(src, dst, send_sem, recv_sem, to):
    return pltpu.make_async_remote_copy(src_ref=src, dst_ref=dst, send_sem=send_sem, recv_sem=recv_sem,
                                        device_id=to, device_id_type=MESH)


def _start_remote(sp, src, dst, shape, itemsize, send_sem, recv_sem, to):
    for s_part, d_part in zip(sp.chunks(src, shape, itemsize), sp.chunks(dst, shape, itemsize)):
        _remote(s_part, d_part, send_sem, recv_sem, to).start()


def _start_local(sp, src, dst, shape, itemsize, sem):
    for s_part, d_part in zip(sp.chunks(src, shape, itemsize), sp.chunks(dst, shape, itemsize)):
        pltpu.make_async_copy(s_part, d_part, sem).start()


class _GatherWeights:
    def __init__(self, specs, shards, middle=0.7):
        self.specs = specs
        self.middle_at = middle
        self.inputs = list(shards)
        self.out_shapes = [jax.ShapeDtypeStruct(sp.full_shape, BF16) for sp in specs]
        nw = len(specs)
        self.scratch = [pltpu.SemaphoreType.DMA((6 * nw,)), pltpu.SemaphoreType.DMA((6 * nw,)),
                        pltpu.SemaphoreType.DMA((nw,))]

    aliases = staticmethod(lambda n_in, n_out: {})

    def phases(self, nsteps):
        return [(0, self.start), (min(nsteps - 1, max(1, int(self.middle_at * nsteps))), self.middle),
                (nsteps - 1, self.end)]

    def _ctx(self):
        x, y, c = _place()
        return x, y, c, 2 * x + y, (x, y, 1 - c), [(1 - x, y), (x, 1 - y), (1 - x, 1 - y)]

    def start(self, shard_refs, full_refs, sems):
        send_sems, recv_sems, local_sems = sems
        x, y, c, my_chip, sibling, chips = self._ctx()
        for w, sp in enumerate(self.specs):
            _start_local(sp, shard_refs[w], sp.window(full_refs[w], my_chip), sp.shard_shape, 2, local_sems.at[w])
            for t, (cx, cy) in enumerate(chips):
                _start_remote(sp, sp.shard_half(shard_refs[w], c), sp.window(full_refs[w], my_chip, c),
                              sp.piece_shape, 2, send_sems.at[6 * w + t], recv_sems.at[6 * w + t], (cx, cy, c))

    def middle(self, shard_refs, full_refs, sems):
        send_sems, recv_sems, local_sems = sems
        x, y, c, my_chip, sibling, chips = self._ctx()
        for w, sp in enumerate(self.specs):
            for t, (cx, cy) in enumerate(chips):
                landed = sp.window(full_refs[w], 2 * cx + cy, c)
                _remote(landed, landed, send_sems.at[6 * w + t], recv_sems.at[6 * w + t], (cx, cy, c)).wait_recv()
                _start_remote(sp, landed, landed, sp.piece_shape, 2, send_sems.at[6 * w + 3 + t],
                              recv_sems.at[6 * w + 3 + t], sibling)

    def end(self, shard_refs, full_refs, sems):
        send_sems, recv_sems, local_sems = sems
        x, y, c, my_chip, sibling, chips = self._ctx()
        for w, sp in enumerate(self.specs):
            for t, (cx, cy) in enumerate(chips):
                other = sp.window(full_refs[w], 2 * cx + cy, 1 - c)
                _remote(other, other, send_sems.at[6 * w + 3 + t], recv_sems.at[6 * w + 3 + t], sibling).wait_recv()
        for w, sp in enumerate(self.specs):
            for t, (cx, cy) in enumerate(chips):
                mine = sp.shard_half(shard_refs[w], c)
                _remote(mine, mine, send_sems.at[6 * w + t], recv_sems.at[6 * w + t], (cx, cy, c)).wait_send()
                landed = sp.window(full_refs[w], 2 * cx + cy, c)
                _remote(landed, landed, send_sems.at[6 * w + 3 + t], recv_sems.at[6 * w + 3 + t], sibling).wait_send()
            pltpu.make_async_copy(shard_refs[w], sp.window(full_refs[w], my_chip), local_sems.at[w]).wait()


class _ScatterGrads:
    def __init__(self, specs, partials, peers=tuple(range(N_DEV)), into=None, middle=0.7):
        self.specs = specs
        self.peers = tuple(peers)
        self.middle_at = middle
        self.n_part = len(partials)
        self.into = into is not None
        self.inputs = list(partials) + (list(into) if self.into else [])
        self.out_shapes = [jax.ShapeDtypeStruct((2, N_DEV) + sp.piece_shape, BF16) for sp in specs]
        nw = len(specs)
        self.scratch = [pltpu.SemaphoreType.DMA((15 * nw,)), pltpu.SemaphoreType.DMA((15 * nw,)),
                        pltpu.SemaphoreType.DMA((nw,))]

    def aliases(self, n_in, n_out):
        return {n_in + self.n_part + w: n_out + w for w in range(len(self.specs))} if self.into else {}

    def phases(self, nsteps):
        return [(0, self.start), (min(nsteps - 1, max(1, int(self.middle_at * nsteps))), self.middle),
                (nsteps - 1, self.end)]

    def _own(self, sp, part_ref, recv_ref, x, y, c, sem):
        return pltpu.make_async_copy(sp.window(part_ref, 2 * x + y, c), recv_ref.at[c, 0], sem)

    def start(self, part_refs, recv_refs, sems):
        send_sems, recv_sems, local_sems = sems
        x, y, c = _place()
        for w, sp in enumerate(self.specs):
            for k in self.peers:
                if k == 0:
                    _start_local(sp, sp.window(part_refs[w], 2 * x + y, c), recv_refs[w].at[c, 0], sp.piece_shape,
                                 2, local_sems.at[w])
                    continue
                px, py, pc = _xor_peer(k)
                _start_remote(sp, sp.window(part_refs[w], 2 * px + py, pc), recv_refs[w].at[pc, k], sp.piece_shape,
                              2, send_sems.at[15 * w + k - 1], recv_sems.at[15 * w + k - 1], (px, py, pc))

    def middle(self, part_refs, recv_refs, sems):
        send_sems, recv_sems, local_sems = sems
        x, y, c = _place()
        sibling = (x, y, 1 - c)
        for w, sp in enumerate(self.specs):
            for k in self.peers:
                landed = recv_refs[w].at[c, k]
                if k:
                    _remote(landed, landed, send_sems.at[15 * w + k - 1], recv_sems.at[15 * w + k - 1],
                            sibling).wait_recv()
                else:
                    self._own(sp, part_refs[w], recv_refs[w], x, y, c, local_sems.at[w]).wait()
                _start_remote(sp, landed, landed, sp.piece_shape, 2, send_sems.at[15 * w + 7 + k],
                              recv_sems.at[15 * w + 7 + k], sibling)

    def end(self, part_refs, recv_refs, sems):
        send_sems, recv_sems, local_sems = sems
        x, y, c = _place()
        sibling = (x, y, 1 - c)
        for w, sp in enumerate(self.specs):
            for k in self.peers:
                other = recv_refs[w].at[1 - c, k]
                _remote(other, other, send_sems.at[15 * w + 7 + k], recv_sems.at[15 * w + 7 + k], sibling).wait_recv()
            for k in self.peers:
                landed = recv_refs[w].at[c, k]
                _remote(landed, landed, send_sems.at[15 * w + 7 + k], recv_sems.at[15 * w + 7 + k],
                        sibling).wait_send()
                if k:
                    px, py, pc = _xor_peer(k)
                    sent = sp.window(part_refs[w], 2 * px + py, pc)
                    _remote(sent, sent, send_sems.at[15 * w + k - 1], recv_sems.at[15 * w + k - 1],
                            (px, py, pc)).wait_send()


def _call(name, body, grid, in_specs, args, out_shape, out_specs, scratch, semantics, est_bytes, comm=None,
          aliases=None):
    out_shape, out_specs = tuple(out_shape), tuple(out_specs)
    if comm is None:
        res = pl.pallas_call(body, name=name, grid=grid, out_shape=out_shape, in_specs=list(in_specs),
                             out_specs=out_specs, scratch_shapes=list(scratch), input_output_aliases=aliases or {},
                             compiler_params=_cparams(semantics, est_bytes))(*args)
        return tuple(res), ()
    n_in, n_out, n_sc = len(in_specs), len(out_shape), len(scratch)
    c_in, c_out = len(comm.inputs), len(comm.out_shapes)
    nsteps = math.prod(grid)
    phases = comm.phases(nsteps)

    def hosted(*refs):
        pos = [0]

        def take(n):
            part = refs[pos[0]:pos[0] + n]
            pos[0] += n
            return part

        ins, cins, outs, couts, scr, sems = take(n_in), take(c_in), take(n_out), take(c_out), take(n_sc), take(3)
        step = 0
        for ax, extent in enumerate(grid):
            step = step * extent + pl.program_id(ax)
        for at, fn in phases:
            if at == 0:
                pl.when(step == 0)(functools.partial(fn, cins, couts, sems))
        body(*ins, *outs, *scr)
        for at, fn in phases:
            if at > 0:
                pl.when(step == at)(functools.partial(fn, cins, couts, sems))

    any_spec = pl.BlockSpec(memory_space=pl.ANY)
    res = pl.pallas_call(
        hosted, name=name, grid=grid, out_shape=out_shape + tuple(comm.out_shapes),
        in_specs=list(in_specs) + [any_spec] * c_in, out_specs=out_specs + (any_spec,) * c_out,
        scratch_shapes=list(scratch) + list(comm.scratch),
        input_output_aliases={**(aliases or {}), **comm.aliases(n_in, n_out)},
        compiler_params=_cparams(("arbitrary",) * len(grid), est_bytes))(*args, *comm.inputs)
    return tuple(res[:n_out]), tuple(res[n_out:])


def _row_tile(rows, cap):
    for t in range(min(rows, cap), 0, -1):
        if rows % t == 0 and (t % 16 == 0 or t == rows):
            return t
    return rows


def _cast_bf16(name, x):
    r, c = x.shape
    tr = _row_tile(r, 512)

    def body(x_ref, o_ref):
        o_ref[...] = x_ref[...].astype(BF16)

    return pl.pallas_call(
        body, name=name, grid=(r // tr,), out_shape=jax.ShapeDtypeStruct((r, c), BF16),
        in_specs=[pl.BlockSpec((tr, c), lambda i: (i, 0))], out_specs=pl.BlockSpec((tr, c), lambda i: (i, 0)),
        compiler_params=_cparams(("parallel",), 2 * tr * c * 6),
    )(x)


def _adam_math(w, g, m, v):
    m = ADAM_B1 * m + (1.0 - ADAM_B1) * g
    v = ADAM_B2 * v + (1.0 - ADAM_B2) * (g * g)
    m_hat = m / (1.0 - ADAM_B1 ** ADAM_STEP)
    v_hat = v / (1.0 - ADAM_B2 ** ADAM_STEP)
    delta = -ADAM_LR * (m_hat / (jnp.sqrt(v_hat) + ADAM_EPS) + ADAM_WD * w)
    return delta, m, v


def _adamw(name, w, g, m, v):
    r, c = w.shape
    tr = _row_tile(r, max(8, (1 << 18) // c))

    def body(w_ref, g_ref, m_ref, v_ref, d_ref, nm_ref, nv_ref):
        d, nm, nv = _adam_math(w_ref[...], g_ref[...], m_ref[...], v_ref[...])
        d_ref[...] = d
        nm_ref[...] = nm
        nv_ref[...] = nv

    spec = pl.BlockSpec((tr, c), lambda i: (i, 0))
    out = jax.ShapeDtypeStruct((r, c), F32)
    return pl.pallas_call(
        body, name=name, grid=(r // tr,), out_shape=(out, out, out), in_specs=[spec] * 4, out_specs=(spec,) * 3,
        compiler_params=_cparams(("parallel",), 2 * 7 * tr * c * 4),
    )(w, g, m, v)


def _sum_pieces(x_ref):
    acc = x_ref[0, 0].astype(F32)
    for k in range(1, N_DEV):
        acc = acc + x_ref[0, k].astype(F32)
    return acc


def _reduce16(name, x):
    _, _, r, c = x.shape
    tr = _row_tile(r, max(16, (1 << 18) // c))
    nt = r // tr

    def body(x_ref, o_ref):
        o_ref[...] = _sum_pieces(x_ref)

    return pl.pallas_call(
        body, name=name, grid=(2, nt), out_shape=jax.ShapeDtypeStruct((2 * r, c), F32),
        in_specs=[pl.BlockSpec((1, N_DEV, tr, c), lambda hf, i: (hf, 0, i, 0))],
        out_specs=pl.BlockSpec((tr, c), lambda hf, i: (hf * nt + i, 0)),
        compiler_params=_cparams(("parallel", "parallel"), 2 * (N_DEV * 2 + 4) * tr * c),
    )(x)


def _reduce_adamw(name, x, w, m, v):
    _, _, r, c = x.shape
    tr = _row_tile(r, max(16, (1 << 18) // c))
    nt = r // tr

    def body(x_ref, w_ref, m_ref, v_ref, g_ref, d_ref, nm_ref, nv_ref):
        g = _sum_pieces(x_ref)
        d, nm, nv = _adam_math(w_ref[...], g, m_ref[...], v_ref[...])
        g_ref[...] = g
        d_ref[...] = d
        nm_ref[...] = nm
        nv_ref[...] = nv

    tile = pl.BlockSpec((tr, c), lambda hf, i: (hf * nt + i, 0))
    out = jax.ShapeDtypeStruct((2 * r, c), F32)
    return pl.pallas_call(
        body, name=name, grid=(2, nt), out_shape=(out,) * 4,
        in_specs=[pl.BlockSpec((1, N_DEV, tr, c), lambda hf, i: (hf, 0, i, 0)), tile, tile, tile],
        out_specs=(tile,) * 4,
        compiler_params=_cparams(("parallel", "parallel"), 2 * (N_DEV * 2 + 7 * 4) * tr * c),
    )(x, w, m, v)


def _t5_buckets():
    rel = jnp.arange(SPAN)[None, :] - WINDOW - jnp.arange(BLOCK)[:, None]
    half = N_BUCKETS // 2
    max_exact = half // 2
    ret = jnp.where(rel > 0, half, 0)
    n = jnp.abs(rel)
    nf = jnp.maximum(n, 1).astype(F32)
    large = max_exact + (jnp.log(nf / max_exact) / math.log(MAX_DISTANCE / max_exact)
                         * (half - max_exact)).astype(jnp.int32)
    large = jnp.minimum(large, half - 1)
    return (ret + jnp.where(n < max_exact, n, large)).astype(jnp.int32)


def _bias_table(table, buckets):
    def body(t_ref, b_ref, o_ref):
        bk = b_ref[...]
        rel = (lax.broadcasted_iota(jnp.int32, (BLOCK, SPAN), 1) - WINDOW
               - lax.broadcasted_iota(jnp.int32, (BLOCK, SPAN), 0))
        band = jnp.abs(rel) <= WINDOW
        for h in range(N_Q_HEADS):
            acc = jnp.zeros((BLOCK, SPAN), F32)
            for b in range(N_BUCKETS):
                acc = jnp.where(bk == b, t_ref[b, h], acc)
            o_ref[h] = jnp.where(band, acc, NEG_INF)

    return pl.pallas_call(
        body, name="bias_table", out_shape=jax.ShapeDtypeStruct((N_Q_HEADS, BLOCK, SPAN), F32),
        in_specs=[pl.BlockSpec(memory_space=pltpu.SMEM), pl.BlockSpec(memory_space=pltpu.VMEM)],
        out_specs=pl.BlockSpec(memory_space=pltpu.VMEM),
    )(table, buckets)


def _bias_table_grad(dbias, buckets):
    def body(d_ref, b_ref, o_ref):
        bk = b_ref[...]
        row = lax.broadcasted_iota(jnp.int32, (N_BUCKETS, LANE), 0)
        lane = lax.broadcasted_iota(jnp.int32, (N_BUCKETS, LANE), 1)
        acc = jnp.zeros((N_BUCKETS, LANE), F32)
        for h in range(N_Q_HEADS):
            d = d_ref[h]
            for b in range(N_BUCKETS):
                s = jnp.sum(jnp.where(bk == b, d, 0.0))
                acc = jnp.where((row == b) & (lane == h), s, acc)
        o_ref[...] = acc

    return pl.pallas_call(
        body, name="bias_table_grad", out_shape=jax.ShapeDtypeStruct((N_BUCKETS, LANE), F32),
        in_specs=[pl.BlockSpec(memory_space=pltpu.VMEM), pl.BlockSpec(memory_space=pltpu.VMEM)],
        out_specs=pl.BlockSpec(memory_space=pltpu.VMEM),
    )(dbias, buckets)


def _ada_forward(sc_all, w_ada):
    d, n = w_ada.shape
    tn = _pick_tile(n, COL_TILE)

    def body(sc_ref, w_ref, o_ref):
        row = lax.broadcasted_iota(jnp.int32, (N_DEV, d), 0)
        sc = jnp.zeros((N_DEV, d), F32)
        for k in range(N_DEV):
            sc = jnp.where(row == k, sc_ref[k], sc)
        o_ref[...] = jnp.dot(sc, w_ref[...], preferred_element_type=F32, precision=lax.Precision.HIGHEST)

    return pl.pallas_call(
        body, name="ada_forward", grid=(n // tn,), out_shape=jax.ShapeDtypeStruct((N_DEV, n), F32),
        in_specs=[_const_spec((N_DEV, 1, d)), pl.BlockSpec((d, tn), lambda j: (0, j))],
        out_specs=pl.BlockSpec((N_DEV, tn), lambda j: (0, j)),
        compiler_params=_cparams(("parallel",), 2 * d * tn * 4 + N_DEV * N_DEV * d * 8),
    )(sc_all, w_ada)


def _mod_finish(m_all, b_ada):
    _, _, n = m_all.shape

    def body(m_ref, b_ref, o_ref):
        x, y, c = _place()
        me = 4 * x + 2 * y + c
        row = lax.broadcasted_iota(jnp.int32, (N_DEV, n), 0)
        for j in range(N_CHIPS):
            blk = m_ref[2 * j]
            mine = jnp.sum(jnp.where(row == me, blk, 0.0), axis=0, keepdims=True)
            o_ref[:, j * n:(j + 1) * n] = mine + b_ref[:, j * n:(j + 1) * n]

    return pl.pallas_call(
        body, name="mod_finish", out_shape=jax.ShapeDtypeStruct((1, N_CHIPS * n), F32),
        in_specs=[pl.BlockSpec(memory_space=pltpu.VMEM), pl.BlockSpec(memory_space=pltpu.VMEM)],
        out_specs=pl.BlockSpec(memory_space=pltpu.VMEM),
    )(m_all, b_ada)


def _ada_backward(sc_t, dmod_cols, w, m, v):
    d, n = w.shape
    tr, tn = _row_tile(d, 512), _pick_tile(n, COL_TILE)

    def body(s_ref, dm_ref, w_ref, m_ref, v_ref, g_ref, d_ref, nm_ref, nv_ref):
        g = jnp.dot(s_ref[...], dm_ref[...], preferred_element_type=F32, precision=lax.Precision.HIGHEST)
        dl, nm, nv = _adam_math(w_ref[...], g, m_ref[...], v_ref[...])
        g_ref[...] = g
        d_ref[...] = dl
        nm_ref[...] = nm
        nv_ref[...] = nv

    tile = pl.BlockSpec((tr, tn), lambda i, j: (i, j))
    out = jax.ShapeDtypeStruct((d, n), F32)
    return pl.pallas_call(
        body, name="ada_backward", grid=(d // tr, n // tn), out_shape=(out,) * 4,
        in_specs=[pl.BlockSpec((tr, N_DEV), lambda i, j: (i, 0)), pl.BlockSpec((N_DEV, tn), lambda i, j: (0, j)),
                  tile, tile, tile],
        out_specs=(tile,) * 4,
        compiler_params=_cparams(("parallel", "parallel"), 2 * 8 * tr * tn * 4),
    )(sc_t, dmod_cols, w, m, v)


def _chip():
    return 2 * lax.axis_index("x") + lax.axis_index("y")


def _local_columns(x, mod, pre_g, w_in_shard, w_merge_shard, b_merge, comm):
    s, d = x.shape
    n1, n2 = w_in_shard.shape[1], w_merge_shard.shape[1]
    tm = _row_tile(s, 512)

    def body(x_ref, mod_ref, pg_ref, wi_ref, wm_ref, b_ref, h_ref, p_ref, g_ref):
        xv = x_ref[...]
        r = lax.rsqrt(jnp.mean(xv * xv, axis=1, keepdims=True) + EPS)
        xn = xv * r * pg_ref[...]
        hv = (xn * (1.0 + mod_ref[:, d:2 * d]) + mod_ref[:, 0:d]).astype(BF16)
        h_ref[...] = hv
        p_ref[...] = jnp.dot(hv, wi_ref[...], preferred_element_type=F32).astype(BF16)
        g_ref[...] = _sigmoid(jnp.dot(hv, wm_ref[...], preferred_element_type=F32) + b_ref[...]).astype(BF16)

    in_specs = [pl.BlockSpec((tm, d), lambda i: (i, 0)), _const_spec((1, 3 * d)), _const_spec((1, d)),
                _resident((d, n1)), _resident((d, n2)), pl.BlockSpec((1, n2), lambda i: (0, _chip()))]
    out_specs = [pl.BlockSpec((tm, d), lambda i: (i, 0)), pl.BlockSpec((tm, n1), lambda i: (i, _chip())),
                 pl.BlockSpec((tm, n2), lambda i: (i, _chip()))]
    est = 2 * tm * (4 * d + 2 * d + 2 * n1 + 2 * n2) + d * (n1 + n2) * 2 + 4 * tm * d * 4 + 2 * tm * (n1 + n2) * 4
    return _call("local_columns", body, (s // tm,), in_specs, (x, mod, pre_g, w_in_shard, w_merge_shard, b_merge),
                 [jax.ShapeDtypeStruct((s, d), BF16), jax.ShapeDtypeStruct((s, N_CHIPS * n1), BF16),
                  jax.ShapeDtypeStruct((s, N_CHIPS * n2), BF16)],
                 out_specs, [], ("arbitrary",), est, comm=comm)


def _other_columns(name, a, b, partial, bias=None, comm=None):
    s, k = a.shape
    _, n = b.shape
    tm, tn = _row_tile(s, 2048), n // N_CHIPS
    col = lambda i, j: (_chip() + 1 + j) % N_CHIPS

    def body(*refs):
        if bias is None:
            a_ref, b_ref, _, o_ref = refs
        else:
            a_ref, b_ref, bias_ref, _, o_ref = refs
        acc = jnp.dot(a_ref[...], b_ref[...], preferred_element_type=F32)
        if bias is not None:
            acc = _sigmoid(acc + bias_ref[...])
        o_ref[...] = acc.astype(BF16)

    in_specs = [pl.BlockSpec((tm, k), lambda i, j: (i, 0)), pl.BlockSpec((k, tn), lambda i, j: (0, col(i, j)))]
    args = [a, b]
    if bias is not None:
        in_specs.append(pl.BlockSpec((1, tn), lambda i, j: (0, col(i, j))))
        args.append(bias)
    in_specs.append(pl.BlockSpec(memory_space=pl.ANY))
    args.append(partial)
    (out,), extra = _call(name, body, (s // tm, N_CHIPS - 1), in_specs, args, [jax.ShapeDtypeStruct((s, n), BF16)],
                          [pl.BlockSpec((tm, tn), lambda i, j: (i, col(i, j)))], [], ("parallel", "arbitrary"),
                          2 * (tm * k + k * tn + tm * tn) * 2 + 2 * tm * tn * 4, comm=comm,
                          aliases={len(args) - 1: 0})
    return out, extra


def _col_specs(off, width, rows, row_index):
    assert off % COL_TILE == 0 and width % COL_TILE == 0
    return [pl.BlockSpec((rows, COL_TILE), functools.partial(lambda p, *ids: (row_index(*ids), p), off // COL_TILE + p))
            for p in range(width // COL_TILE)]


def _cat(refs):
    vals = [r[...] for r in refs]
    return vals[0] if len(vals) == 1 else jnp.concatenate(vals, axis=1)


def _attn_mask(n, s):
    kpos = (n - 1) * BLOCK + lax.broadcasted_iota(jnp.int32, (1, SPAN), 1)
    return (kpos >= 0) & (kpos < s)


def _sink_column(sink_ref, kh):
    rows = GQA_GROUP * BLOCK
    grp = lax.broadcasted_iota(jnp.int32, (rows, 1), 0) // BLOCK
    col = jnp.zeros((rows, 1), F32)
    for g in range(GQA_GROUP):
        col = jnp.where(grp == g, sink_ref[0, kh * GQA_GROUP + g], col)
    return col


def _stack_heads(x, kh):
    base = kh * GQA_GROUP * HEAD_DIM
    return jnp.concatenate([x[:, base + g * HEAD_DIM: base + (g + 1) * HEAD_DIM] for g in range(GQA_GROUP)], axis=0)


def _softmax_parts(qs, k, bias, valid, sink_col):
    sc = lax.dot_general(qs, k, (((1,), (1,)), ((), ())), preferred_element_type=F32)
    sc = sc * (HEAD_DIM ** -0.5) + bias
    sc = jnp.where(valid, sc, NEG_INF)
    mx = jnp.maximum(jnp.max(sc, axis=1, keepdims=True), sink_col)
    e = jnp.exp(sc - mx)
    es = jnp.exp(sink_col - mx)
    inv = 1.0 / (jnp.sum(e, axis=1, keepdims=True) + es)
    return e * inv, es * inv


def _attn_forward(proj, bias2, sink, s):
    nblk = s // BLOCK
    nq = ATTN_WIDTH // COL_TILE
    kv_col = ATTN_WIDTH // COL_TILE
    assert 2 * KV_WIDTH == COL_TILE

    def body(*refs):
        q_refs = refs[:nq]
        kvp, kvc, kvn, bias_ref, sink_ref, o_ref = refs[nq:]
        n = pl.program_id(0)
        q = _cat(q_refs)
        kv = jnp.concatenate([kvp[...], kvc[...], kvn[...]], axis=0)
        valid = _attn_mask(n, s)
        for kh in range(N_KV_HEADS):
            qs = _stack_heads(q, kh)
            k = kv[:, kh * HEAD_DIM:(kh + 1) * HEAD_DIM]
            v = kv[:, KV_WIDTH + kh * HEAD_DIM: KV_WIDTH + (kh + 1) * HEAD_DIM]
            p, _ = _softmax_parts(qs, k, bias_ref[kh], valid, _sink_column(sink_ref, kh))
            o = jnp.dot(p.astype(BF16), v, preferred_element_type=F32)
            for g in range(GQA_GROUP):
                h = kh * GQA_GROUP + g
                o_ref[:, h * HEAD_DIM:(h + 1) * HEAD_DIM] = o[g * BLOCK:(g + 1) * BLOCK].astype(BF16)

    in_specs = _col_specs(0, ATTN_WIDTH, BLOCK, lambda n: n)
    in_specs += [pl.BlockSpec((BLOCK, COL_TILE), lambda n: (jnp.maximum(n - 1, 0), kv_col)),
                 pl.BlockSpec((BLOCK, COL_TILE), lambda n: (n, kv_col)),
                 pl.BlockSpec((BLOCK, COL_TILE), lambda n: (jnp.minimum(n + 1, nblk - 1), kv_col)),
                 _const_spec((N_KV_HEADS, GQA_GROUP * BLOCK, SPAN)),
                 pl.BlockSpec(memory_space=pltpu.SMEM)]
    return pl.pallas_call(
        body, name="attn_forward", grid=(nblk,), out_shape=jax.ShapeDtypeStruct((s, ATTN_WIDTH), BF16),
        in_specs=in_specs, out_specs=pl.BlockSpec((BLOCK, ATTN_WIDTH), lambda n: (n, 0)),
        compiler_params=_cparams(("parallel",), 16 << 20),
    )(*([proj] * (nq + 3)), bias2, sink)


def _pool_positions(i, tm, s, width):
    pos = i * tm - HALO + lax.broadcasted_iota(jnp.int32, (tm + 2 * HALO, width), 0)
    return pos, (pos >= 0) & (pos < s)


def _pool_count(pos, w, s):
    return (jnp.minimum(pos + w // 2, s) - jnp.maximum(pos - w // 2, 0)).astype(F32)


def _halo_specs_cols(off, width, tm, s):
    per = tm // HALO
    last = s // HALO - 1
    prev = _col_specs(off, width, HALO, lambda i: jnp.maximum(i * per - 1, 0))
    nxt = _col_specs(off, width, HALO, lambda i: jnp.minimum((i + 1) * per, last))
    return prev, nxt


def _branches(proj, attn, g, w_bra, w_brp, w_grp, pool_scale, s, d):
    a_w, p_w = ATTN_WIDTH, pool_scale.shape[1]
    cg = p_w // N_POOL_GROUPS
    tm = _row_tile(s, ROW_TILE)
    off_ga = ATTN_WIDTH + 2 * KV_WIDTH
    off_u = off_ga + a_w
    off_gp = off_u + p_w
    n_ga, n_u, n_gp = a_w // COL_TILE, p_w // COL_TILE, p_w // COL_TILE

    def body(*refs):
        it = iter(refs)
        attn_ref = next(it)
        ga_refs = [next(it) for _ in range(n_ga)]
        u_refs = [next(it) for _ in range(n_u)]
        up_refs = [next(it) for _ in range(n_u)]
        un_refs = [next(it) for _ in range(n_u)]
        gp_refs = [next(it) for _ in range(n_gp)]
        g_ref, wa_ref, wp_ref, wg_ref, ps_ref = (next(it) for _ in range(5))
        ya_ref, yp_ref, za_ref, zp_ref, mg_ref, pooled_ref, mixed_ref = (next(it) for _ in range(7))
        i = pl.program_id(0)
        ya = (attn_ref[...].astype(F32) * _silu(_cat(ga_refs).astype(F32))).astype(BF16)
        ya_ref[...] = ya
        za = jnp.dot(ya, wa_ref[...], preferred_element_type=F32)
        za_ref[...] = za.astype(BF16)

        u = _cat(u_refs).astype(F32)
        ext = jnp.concatenate([_cat(up_refs).astype(F32), u, _cat(un_refs).astype(F32)], axis=0)
        pos, ok = _pool_positions(i, tm, s, cg)
        mixed = []
        for gi, w in enumerate(POOL_SIZES):
            e = jnp.where(ok, ext[:, gi * cg:(gi + 1) * cg], 0.0)
            acc = e[HALO - w // 2: HALO - w // 2 + tm]
            for dd in range(-w // 2 + 1, w // 2):
                acc = acc + e[HALO + dd: HALO + dd + tm]
            cnt = _pool_count(pos[HALO:HALO + tm], w, s)
            pooled = (acc / cnt - u[:, gi * cg:(gi + 1) * cg]).astype(BF16)
            pooled_ref[:, gi * cg:(gi + 1) * cg] = pooled
            mixed.append(jnp.dot(pooled, wg_ref[gi], preferred_element_type=F32))
        mixed = jnp.concatenate(mixed, axis=1)
        mixed_ref[...] = mixed.astype(BF16)
        yp = (mixed * ps_ref[...] * _silu(_cat(gp_refs).astype(F32))).astype(BF16)
        yp_ref[...] = yp
        zp = jnp.dot(yp, wp_ref[...], preferred_element_type=F32)
        zp_ref[...] = zp.astype(BF16)
        gate = g_ref[...].astype(F32)
        mg_ref[...] = (gate[:, :d] * za + gate[:, d:] * zp).astype(BF16)

    row = lambda i: i
    u_prev, u_next = _halo_specs_cols(off_u, p_w, tm, s)
    in_specs = [pl.BlockSpec((tm, a_w), lambda i: (i, 0))]
    in_specs += _col_specs(off_ga, a_w, tm, row) + _col_specs(off_u, p_w, tm, row) + u_prev + u_next
    in_specs += _col_specs(off_gp, p_w, tm, row)
    in_specs += [pl.BlockSpec((tm, 2 * d), lambda i: (i, 0)), _resident((a_w, d)), _resident((p_w, d)),
                 _resident((N_POOL_GROUPS, cg, cg)), _const_spec((1, p_w))]
    n_proj = n_ga + 3 * n_u + n_gp
    tile = lambda w: pl.BlockSpec((tm, w), lambda i: (i, 0))
    out_widths = (a_w, p_w, d, d, d, p_w, p_w)
    est = 2 * tm * (a_w + a_w + 2 * p_w + 2 * d + sum(out_widths)) * 2 + (a_w + p_w) * d * 2 + 6 * tm * d * 4
    return pl.pallas_call(
        body, name="branches", grid=(s // tm,),
        out_shape=tuple(jax.ShapeDtypeStruct((s, w), BF16) for w in out_widths),
        in_specs=in_specs, out_specs=tuple(tile(w) for w in out_widths),
        compiler_params=_cparams(("parallel",), est),
    )(attn, *([proj] * n_proj), g, w_bra, w_brp, w_grp, pool_scale)


def _sub_rows(tm, sub=128):
    sub = min(sub, tm)
    return [pl.ds(r * sub, sub) for r in range(tm // sub)]


def _out_loss(merged, x, target, w_out, post_g, mod, s, d):
    tm = _row_tile(s, ROW_TILE)
    nsteps = s // tm

    def body(mg_ref, x_ref, t_ref, w_ref, pg_ref, mod_ref, dout_ref, do_ref, loss_ref, dgate_ref, dpg_ref, lacc):
        i = pl.program_id(0)

        @pl.when(i == 0)
        def _():
            lacc[...] = jnp.zeros_like(lacc)
            dgate_ref[...] = jnp.zeros_like(dgate_ref)
            dpg_ref[...] = jnp.zeros_like(dpg_ref)

        pg = pg_ref[...]
        gate = mod_ref[:, 2 * d:3 * d]
        for rows in _sub_rows(tm):
            o = jnp.dot(mg_ref[rows, :], w_ref[...], preferred_element_type=F32)
            r = lax.rsqrt(jnp.mean(o * o, axis=1, keepdims=True) + EPS)
            ohat = o * r
            y = ohat * pg
            e = x_ref[rows, :] + gate * y - t_ref[rows, :]
            lacc[...] += jnp.sum(e * e, axis=0, keepdims=True)
            dout = e * (1.0 / d)
            dout_ref[rows, :] = dout
            dgate_ref[...] += jnp.sum(dout * y, axis=0, keepdims=True)
            dy = dout * gate
            dpg_ref[...] += jnp.sum(dy * ohat, axis=0, keepdims=True)
            dohat = dy * pg
            do = r * (dohat - ohat * jnp.mean(dohat * ohat, axis=1, keepdims=True))
            do_ref[rows, :] = do.astype(BF16)

        @pl.when(i == nsteps - 1)
        def _():
            loss_ref[...] = (0.5 / d) * jnp.sum(lacc[...], axis=1, keepdims=True)

    tile = pl.BlockSpec((tm, d), lambda i: (i, 0))
    vec = _const_spec((1, d))
    return pl.pallas_call(
        body, name="out_loss", grid=(nsteps,),
        out_shape=(jax.ShapeDtypeStruct((s, d), F32), jax.ShapeDtypeStruct((s, d), BF16),
                   jax.ShapeDtypeStruct((1, 1), F32), jax.ShapeDtypeStruct((1, d), F32),
                   jax.ShapeDtypeStruct((1, d), F32)),
        in_specs=[tile, tile, tile, _resident((d, d)), vec, _const_spec((1, 3 * d))],
        out_specs=(tile, tile, _const_spec((1, 1)), vec, vec),
        scratch_shapes=[pltpu.VMEM((1, d), F32)],
        compiler_params=_cparams(("arbitrary",), 2 * tm * d * (2 + 4 + 4 + 4 + 2) + d * d * 2 + 8 * tm * d * 4),
    )(merged, x, target, w_out, post_g, mod)


def _bwd_out(d_o, g, za, zp, w_out, s, d, comm=None):
    tm = _row_tile(s, ROW_TILE)

    def body(do_ref, g_ref, za_ref, zp_ref, w_ref, dza_ref, dzp_ref, dgl_ref, dbm_ref):
        i = pl.program_id(0)

        @pl.when(i == 0)
        def _():
            dbm_ref[...] = jnp.zeros_like(dbm_ref)

        do = do_ref[...]
        for j in range(d // COL_TILE):
            ca = pl.ds(j * COL_TILE, COL_TILE)
            cp = pl.ds(d + j * COL_TILE, COL_TILE)
            dm = lax.dot_general(do, w_ref[ca, :], (((1,), (1,)), ((), ())), preferred_element_type=F32)
            ga, gp = g_ref[:, ca].astype(F32), g_ref[:, cp].astype(F32)
            dza_ref[:, ca] = (dm * ga).astype(BF16)
            dzp_ref[:, ca] = (dm * gp).astype(BF16)
            dla = dm * za_ref[:, ca].astype(F32) * ga * (1.0 - ga)
            dlp = dm * zp_ref[:, ca].astype(F32) * gp * (1.0 - gp)
            dgl_ref[:, ca] = dla.astype(BF16)
            dgl_ref[:, cp] = dlp.astype(BF16)
            dbm_ref[:, ca] += jnp.sum(dla, axis=0, keepdims=True)
            dbm_ref[:, cp] += jnp.sum(dlp, axis=0, keepdims=True)

    tile = pl.BlockSpec((tm, d), lambda i: (i, 0))
    wide = pl.BlockSpec((tm, 2 * d), lambda i: (i, 0))
    return _call("bwd_out", body, (s // tm,), [tile, wide, tile, tile, _resident((d, d))], (d_o, g, za, zp, w_out),
                 (jax.ShapeDtypeStruct((s, d), BF16), jax.ShapeDtypeStruct((s, d), BF16),
                  jax.ShapeDtypeStruct((s, 2 * d), BF16), jax.ShapeDtypeStruct((1, 2 * d), F32)),
                 (tile, tile, wide, _const_spec((1, 2 * d))), [], ("arbitrary",),
                 2 * tm * d * 2 * 9 + d * d * 2 + 8 * tm * d * 4, comm=comm)


def _bwd_branches(dza, dzp, attn, proj, mixed, w_bra, w_brp, w_grp, pool_scale, s, d, comm=None):
    a_w, p_w = ATTN_WIDTH, pool_scale.shape[1]
    cg = p_w // N_POOL_GROUPS
    tm = _row_tile(s, ROW_TILE)
    off_ga = ATTN_WIDTH + 2 * KV_WIDTH
    off_gp = off_ga + a_w + p_w
    n_ga, n_gp = a_w // COL_TILE, p_w // COL_TILE

    def body(*refs):
        it = iter(refs)
        dza_ref, dzp_ref, attn_ref = next(it), next(it), next(it)
        ga_refs = [next(it) for _ in range(n_ga)]
        gp_refs = [next(it) for _ in range(n_gp)]
        mixed_ref, wa_ref, wp_ref, wg_ref, ps_ref = (next(it) for _ in range(5))
        dattn_ref, dga_ref, dgp_ref, dmix_ref, dpool_ref, dps_ref = (next(it) for _ in range(6))
        i = pl.program_id(0)

        @pl.when(i == 0)
        def _():
            dps_ref[...] = jnp.zeros_like(dps_ref)

        dya = lax.dot_general(dza_ref[...], wa_ref[...], (((1,), (1,)), ((), ())), preferred_element_type=F32)
        ga = _cat(ga_refs).astype(F32)
        dattn_ref[...] = (dya * _silu(ga)).astype(BF16)
        dga_ref[...] = (dya * attn_ref[...].astype(F32) * _dsilu(ga)).astype(BF16)

        dyp = lax.dot_general(dzp_ref[...], wp_ref[...], (((1,), (1,)), ((), ())), preferred_element_type=F32)
        gp = _cat(gp_refs).astype(F32)
        mixed = mixed_ref[...].astype(F32)
        ps = ps_ref[...]
        sg = _silu(gp)
        dgp_ref[...] = (dyp * mixed * ps * _dsilu(gp)).astype(BF16)
        dps_ref[...] += jnp.sum(dyp * sg * mixed, axis=0, keepdims=True)
        dmix = (dyp * sg * ps).astype(BF16)
        dmix_ref[...] = dmix
        for gi in range(N_POOL_GROUPS):
            dp = lax.dot_general(dmix[:, gi * cg:(gi + 1) * cg], wg_ref[gi], (((1,), (1,)), ((), ())),
                                 preferred_element_type=F32)
            dpool_ref[:, gi * cg:(gi + 1) * cg] = dp.astype(BF16)

    row = lambda i: i
    tile = lambda w: pl.BlockSpec((tm, w), lambda i: (i, 0))
    in_specs = [tile(d), tile(d), tile(a_w)] + _col_specs(off_ga, a_w, tm, row) + _col_specs(off_gp, p_w, tm, row)
    in_specs += [tile(p_w), _resident((a_w, d)), _resident((p_w, d)), _resident((N_POOL_GROUPS, cg, cg)),
                 _const_spec((1, p_w))]
    out_widths = (a_w, a_w, p_w, p_w, p_w)
    est = 2 * tm * (2 * d + 2 * a_w + 2 * p_w + sum(out_widths)) * 2 + (a_w + p_w) * d * 2 + 8 * tm * a_w * 4
    return _call("bwd_branches", body, (s // tm,), in_specs,
                 (dza, dzp, attn, *([proj] * (n_ga + n_gp)), mixed, w_bra, w_brp, w_grp, pool_scale),
                 tuple(jax.ShapeDtypeStruct((s, w), BF16) for w in out_widths) + (jax.ShapeDtypeStruct((1, p_w), F32),),
                 tuple(tile(w) for w in out_widths) + (_const_spec((1, p_w)),), [], ("arbitrary",), est, comm=comm)


def _pool_backward(dpooled, s):
    _, p_w = dpooled.shape
    cg = p_w // N_POOL_GROUPS
    tm = _row_tile(s, ROW_TILE)
    per, last = tm // HALO, s // HALO - 1

    def body(dp_ref, prev_ref, next_ref, du_ref):
        i = pl.program_id(0)
        dp = dp_ref[...].astype(F32)
        ext = jnp.concatenate([prev_ref[...].astype(F32), dp, next_ref[...].astype(F32)], axis=0)
        pos, ok = _pool_positions(i, tm, s, cg)
        for gi, w in enumerate(POOL_SIZES):
            t = jnp.where(ok, ext[:, gi * cg:(gi + 1) * cg], 0.0) / _pool_count(pos, w, s)
            acc = t[HALO - w // 2 + 1: HALO - w // 2 + 1 + tm]
            for dd in range(-w // 2 + 2, w // 2 + 1):
                acc = acc + t[HALO + dd: HALO + dd + tm]
            du_ref[:, gi * cg:(gi + 1) * cg] = (acc - dp[:, gi * cg:(gi + 1) * cg]).astype(BF16)

    return pl.pallas_call(
        body, name="pool_backward", grid=(s // tm,), out_shape=jax.ShapeDtypeStruct((s, p_w), BF16),
        in_specs=[pl.BlockSpec((tm, p_w), lambda i: (i, 0)),
                  pl.BlockSpec((HALO, p_w), lambda i: (jnp.maximum(i * per - 1, 0), 0)),
                  pl.BlockSpec((HALO, p_w), lambda i: (jnp.minimum((i + 1) * per, last), 0))],
        out_specs=pl.BlockSpec((tm, p_w), lambda i: (i, 0)),
        compiler_params=_cparams(("parallel",), 4 * tm * p_w * 2 + 8 * tm * p_w * 4),
    )(dpooled, dpooled, dpooled)


def _attn_backward(proj, dattn, bias2, sink, s, comm=None):
    nblk = s // BLOCK
    nq = ATTN_WIDTH // COL_TILE
    kv_col = ATTN_WIDTH // COL_TILE
    rows = GQA_GROUP * BLOCK
    scale = HEAD_DIM ** -0.5

    def body(*refs):
        q_refs = refs[:nq]
        kvp, kvc, kvn, do_ref, bias_ref, sink_ref, dq_ref, dkv_ref, dbias_ref, dsink_ref, acc, sacc = refs[nq:]
        n = pl.program_id(0)

        @pl.when(n == 0)
        def _():
            acc[...] = jnp.zeros_like(acc)
            sacc[...] = jnp.zeros_like(sacc)
            dbias_ref[...] = jnp.zeros_like(dbias_ref)
            dsink_ref[...] = jnp.zeros_like(dsink_ref)

        @pl.when(jnp.logical_and(n >= 1, n < nblk))
        def _():
            acc[(n + 1) % 3] = jnp.zeros((BLOCK, 2 * KV_WIDTH), F32)

        @pl.when(n < nblk)
        def _():
            q = _cat(q_refs)
            do = do_ref[...]
            kv = jnp.concatenate([kvp[...], kvc[...], kvn[...]], axis=0)
            valid = _attn_mask(n, s)
            for kh in range(N_KV_HEADS):
                qs = _stack_heads(q, kh)
                dos = _stack_heads(do, kh)
                k = kv[:, kh * HEAD_DIM:(kh + 1) * HEAD_DIM]
                v = kv[:, KV_WIDTH + kh * HEAD_DIM: KV_WIDTH + (kh + 1) * HEAD_DIM]
                p, ps = _softmax_parts(qs, k, bias_ref[kh], valid, _sink_column(sink_ref, kh))
                dp = lax.dot_general(dos, v, (((1,), (1,)), ((), ())), preferred_element_type=F32)
                delta = jnp.sum(p * dp, axis=1, keepdims=True)
                ds = p * (dp - delta)
                dbias_ref[kh] += ds
                sacc[kh] += -ps * delta
                dsb = ds.astype(BF16)
                dq = jnp.dot(dsb, k, preferred_element_type=F32) * scale
                for g in range(GQA_GROUP):
                    h = kh * GQA_GROUP + g
                    dq_ref[:, h * HEAD_DIM:(h + 1) * HEAD_DIM] = dq[g * BLOCK:(g + 1) * BLOCK].astype(BF16)
                dk = lax.dot_general(dsb, qs, (((0,), (0,)), ((), ())), preferred_element_type=F32) * scale
                dv = lax.dot_general(p.astype(BF16), dos, (((0,), (0,)), ((), ())), preferred_element_type=F32)
                for j in range(3):
                    slot = (n + 2 + j) % 3
                    acc[slot, :, kh * HEAD_DIM:(kh + 1) * HEAD_DIM] += dk[j * BLOCK:(j + 1) * BLOCK]
                    acc[slot, :, KV_WIDTH + kh * HEAD_DIM: KV_WIDTH + (kh + 1) * HEAD_DIM] += dv[j * BLOCK:(j + 1) * BLOCK]

        dkv_ref[...] = acc[(n + 2) % 3].astype(BF16)

        @pl.when(n == nblk)
        def _():
            lane = lax.broadcasted_iota(jnp.int32, (1, LANE), 1)
            out = jnp.zeros((1, LANE), F32)
            for kh in range(N_KV_HEADS):
                col = sacc[kh]
                for g in range(GQA_GROUP):
                    out = jnp.where(lane == kh * GQA_GROUP + g, jnp.sum(col[g * BLOCK:(g + 1) * BLOCK]), out)
            dsink_ref[...] = out

    qi = lambda n: jnp.minimum(n, nblk - 1)
    in_specs = _col_specs(0, ATTN_WIDTH, BLOCK, qi)
    in_specs += [pl.BlockSpec((BLOCK, COL_TILE), lambda n: (jnp.maximum(qi(n) - 1, 0), kv_col)),
                 pl.BlockSpec((BLOCK, COL_TILE), lambda n: (qi(n), kv_col)),
                 pl.BlockSpec((BLOCK, COL_TILE), lambda n: (jnp.minimum(qi(n) + 1, nblk - 1), kv_col)),
                 pl.BlockSpec((BLOCK, ATTN_WIDTH), lambda n: (qi(n), 0)),
                 _const_spec((N_KV_HEADS, rows, SPAN)),
                 pl.BlockSpec(memory_space=pltpu.SMEM)]
    return _call("attn_backward", body, (nblk + 1,), in_specs, (*([proj] * (nq + 3)), dattn, bias2, sink),
                 (jax.ShapeDtypeStruct((s, ATTN_WIDTH), BF16), jax.ShapeDtypeStruct((s, 2 * KV_WIDTH), BF16),
                  jax.ShapeDtypeStruct((N_KV_HEADS, rows, SPAN), F32), jax.ShapeDtypeStruct((1, LANE), F32)),
                 (pl.BlockSpec((BLOCK, ATTN_WIDTH), lambda n: (qi(n), 0)),
                  pl.BlockSpec((BLOCK, 2 * KV_WIDTH), lambda n: (jnp.clip(n - 1, 0, nblk - 1), 0)),
                  _const_spec((N_KV_HEADS, rows, SPAN)), _const_spec((1, LANE))),
                 [pltpu.VMEM((3, BLOCK, 2 * KV_WIDTH), F32), pltpu.VMEM((N_KV_HEADS, rows, 1), F32)],
                 ("arbitrary",), 24 << 20, comm=comm)


def _pick_tile(n, cap):
    t = cap - cap % LANE
    while n % t:
        t -= LANE
    return t


def _matmul_tn(name, a, b, comm=None):
    s, m = a.shape
    _, n = b.shape
    tk = _row_tile(s, 2048)
    tm, tn = _pick_tile(m, 2048), _pick_tile(n, 1152)
    nk = s // tk

    def body(a_ref, b_ref, o_ref, acc):
        k = pl.program_id(2)

        @pl.when(k == 0)
        def _():
            acc[...] = jnp.zeros_like(acc)

        acc[...] += lax.dot_general(a_ref[...], b_ref[...], (((0,), (0,)), ((), ())), preferred_element_type=F32)

        @pl.when(k == nk - 1)
        def _():
            o_ref[...] = acc[...].astype(BF16)

    (out,), extra = _call(
        name, body, (m // tm, n // tn, nk),
        [pl.BlockSpec((tk, tm), lambda i, j, k: (k, i)), pl.BlockSpec((tk, tn), lambda i, j, k: (k, j))], (a, b),
        [jax.ShapeDtypeStruct((m, n), BF16)], [pl.BlockSpec((tm, tn), lambda i, j, k: (i, j))],
        [pltpu.VMEM((tm, tn), F32)], ("parallel", "parallel", "arbitrary"),
        2 * tk * (tm + tn) * 2 + tm * tn * (4 + 4 + 4), comm=comm)
    return out, extra


def _pool_weight_grad(pooled, dmix, s):
    _, p_w = pooled.shape
    cg = p_w // N_POOL_GROUPS
    tk = _row_tile(s, 512)
    nk = s // tk

    def body(a_ref, b_ref, o_ref, acc):
        k = pl.program_id(1)

        @pl.when(k == 0)
        def _():
            acc[...] = jnp.zeros_like(acc)

        acc[...] += lax.dot_general(a_ref[...], b_ref[...], (((0,), (0,)), ((), ())), preferred_element_type=F32)

        @pl.when(k == nk - 1)
        def _():
            o_ref[0] = acc[...].astype(BF16)

    return pl.pallas_call(
        body, name="pool_weight_grad", grid=(N_POOL_GROUPS, nk),
        out_shape=jax.ShapeDtypeStruct((N_POOL_GROUPS, cg, cg), BF16),
        in_specs=[pl.BlockSpec((tk, cg), lambda g, k: (k, g)), pl.BlockSpec((tk, cg), lambda g, k: (k, g))],
        out_specs=pl.BlockSpec((1, cg, cg), lambda g, k: (g, 0, 0)),
        scratch_shapes=[pltpu.VMEM((cg, cg), F32)],
        compiler_params=_cparams(("parallel", "arbitrary"), 16 << 20),
    )(pooled, dmix)


def _bwd_hidden(dproj, dgl, w_in, w_merge, s, d, comm=None):
    tm = _row_tile(s, 1024)
    t_in, t_mg = _pick_tile(dproj.shape[1], 1024), _pick_tile(dgl.shape[1], 1024)
    n_in = dproj.shape[1] // t_in
    n_mg = dgl.shape[1] // t_mg
    nk = n_in + n_mg

    def body(dp_ref, dg_ref, wi_ref, wm_ref, dh_ref):
        k = pl.program_id(1)

        @pl.when(k == 0)
        def _():
            dh_ref[...] = lax.dot_general(dp_ref[...], wi_ref[...], (((1,), (1,)), ((), ())),
                                          preferred_element_type=F32)

        @pl.when(jnp.logical_and(k > 0, k < n_in))
        def _():
            dh_ref[...] += lax.dot_general(dp_ref[...], wi_ref[...], (((1,), (1,)), ((), ())),
                                           preferred_element_type=F32)

        @pl.when(k >= n_in)
        def _():
            dh_ref[...] += lax.dot_general(dg_ref[...], wm_ref[...], (((1,), (1,)), ((), ())),
                                           preferred_element_type=F32)

    in_specs = [pl.BlockSpec((tm, t_in), lambda i, k: (i, jnp.minimum(k, n_in - 1))),
                pl.BlockSpec((tm, t_mg), lambda i, k: (i, jnp.maximum(k - n_in, 0))),
                pl.BlockSpec((d, t_in), lambda i, k: (0, jnp.minimum(k, n_in - 1))),
                pl.BlockSpec((d, t_mg), lambda i, k: (0, jnp.maximum(k - n_in, 0)))]
    est = 2 * (tm + d) * (t_in + t_mg) * 2 + 3 * tm * d * 4
    (dh,), extra = _call("bwd_hidden", body, (s // tm, nk), in_specs, (dproj, dgl, w_in, w_merge),
                         [jax.ShapeDtypeStruct((s, d), F32)], [pl.BlockSpec((tm, d), lambda i, k: (i, 0))], [],
                         ("parallel", "arbitrary"), est, comm=comm)
    return dh, extra


def _bwd_prenorm(dh, x, dout, mod, pre_g, s, d):
    tm = _row_tile(s, ROW_TILE)

    def body(dh_ref, x_ref, dout_ref, mod_ref, pg_ref, gx_ref, dsh_ref, dsc_ref, dpg_ref):
        i = pl.program_id(0)

        @pl.when(i == 0)
        def _():
            dsh_ref[...] = jnp.zeros_like(dsh_ref)
            dsc_ref[...] = jnp.zeros_like(dsc_ref)
            dpg_ref[...] = jnp.zeros_like(dpg_ref)

        dh = dh_ref[...]
        xv = x_ref[...]
        r = lax.rsqrt(jnp.mean(xv * xv, axis=1, keepdims=True) + EPS)
        xhat = xv * r
        pg = pg_ref[...]
        one_scale = 1.0 + mod_ref[:, d:2 * d]
        dsh_ref[...] += jnp.sum(dh, axis=0, keepdims=True)
        dsc_ref[...] += jnp.sum(dh * xhat, axis=0, keepdims=True) * pg
        dpg_ref[...] += jnp.sum(dh * xhat, axis=0, keepdims=True) * one_scale
        dxh = dh * (one_scale * pg)
        dx = r * (dxh - xhat * jnp.mean(dxh * xhat, axis=1, keepdims=True))
        gx_ref[...] = dout_ref[...] + dx

    tile = pl.BlockSpec((tm, d), lambda i: (i, 0))
    vec = _const_spec((1, d))
    return pl.pallas_call(
        body, name="bwd_prenorm", grid=(s // tm,),
        out_shape=(jax.ShapeDtypeStruct((s, d), F32),) + (jax.ShapeDtypeStruct((1, d), F32),) * 3,
        in_specs=[tile, tile, tile, _const_spec((1, 3 * d)), vec], out_specs=(tile, vec, vec, vec),
        compiler_params=_cparams(("arbitrary",), 2 * 4 * tm * d * 4 + 6 * tm * d * 4),
    )(dh, x, dout, mod, pre_g)


def _pad_lanes(v, width):
    return jnp.pad(v, ((0, 0), (0, width - v.shape[1])))


def kernel(x, c, rel_bias_table, w_ada, b_ada, pre_norm_g, post_norm_g, w_in, attn_sink, w_pool_group, pool_scale, w_branch_attn, w_branch_pool, w_merge, b_merge, w_out, loss_target, m_rel_bias_table, m_w_ada, m_b_ada, m_pre_norm_g, m_post_norm_g, m_w_in, m_attn_sink, m_w_pool_group, m_pool_scale, m_w_branch_attn, m_w_branch_pool, m_w_merge, m_b_merge, m_w_out, v_rel_bias_table, v_w_ada, v_b_ada, v_pre_norm_g, v_post_norm_g, v_w_in, v_attn_sink, v_w_pool_group, v_pool_scale, v_w_branch_attn, v_w_branch_pool, v_w_merge, v_b_merge, v_w_out):
    _, s, d = x.shape
    p_w = pool_scale.shape[-1]
    cg = p_w // N_POOL_GROUPS
    in_w = 2 * ATTN_WIDTH + 2 * KV_WIDTH + 2 * p_w
    x2, t2 = x[0], loss_target[0]
    chip = 2 * lax.axis_index("x") + lax.axis_index("y")

    specs = [_Sharded("col", (d, in_w)), _Sharded("col", (d, 2 * d)), _Sharded("col", (ATTN_WIDTH, d)),
             _Sharded("col", (p_w, d)), _Sharded("row", (d, d)), _Sharded("grp", (N_POOL_GROUPS, cg, cg))]
    shards32 = [w_in[0], w_merge[0], w_branch_attn[0], w_branch_pool[0], w_out[0],
                w_pool_group[0].reshape(N_POOL_GROUPS * cg // N_CHIPS, cg)]
    names = ["w_in", "w_merge", "w_branch_attn", "w_branch_pool", "w_out", "w_pool_group"]
    shards16 = [_cast_bf16("cast_" + nm, w) for nm, w in zip(names, shards32)]
    shards16[5] = shards16[5].reshape(N_POOL_GROUPS, cg // N_CHIPS, cg)

    sc_all = _all_gather8("gather_cond", c, 1, pre=_silu)
    m_all = _all_gather8("gather_mod", _ada_forward(sc_all, w_ada[0]), N_DEV)
    mod = _mod_finish(m_all, b_ada)

    (h, proj, gates), (wf_in,) = _local_columns(x2, mod, pre_norm_g, shards16[0], shards16[1], b_merge,
                                                comm=_GatherWeights(specs[:1], shards16[:1], middle=0.9))
    proj, (wf_merge,) = _other_columns("proj", h, wf_in, proj, comm=_GatherWeights(specs[1:2], shards16[1:2]))
    gates, (wf_bra, wf_brp, wf_out, wf_grp) = _other_columns("merge_gates", h, wf_merge, gates, bias=b_merge,
                                                             comm=_GatherWeights(specs[2:], shards16[2:]))
    buckets = _t5_buckets()
    bias2 = _bias_table(rel_bias_table, buckets).reshape(N_KV_HEADS, GQA_GROUP * BLOCK, SPAN)
    attn = _attn_forward(proj, bias2, attn_sink, s)
    ya, yp, za, zp, merged, pooled, mixed = _branches(proj, attn, gates, wf_bra, wf_brp, wf_grp, pool_scale, s, d)
    dout, d_o, loss_part, dgate, dpostg = _out_loss(merged, x2, t2, wf_out, post_norm_g, mod, s, d)

    pw_out, _ = _matmul_tn("grad_w_out", merged, d_o)
    (dza, dzp, dgl, dbm), (pc_out,) = _bwd_out(d_o, gates, za, zp, wf_out, s, d,
                                                comm=_ScatterGrads(specs[4:5], [pw_out]))
    pw_bra, _ = _matmul_tn("grad_w_branch_attn", ya, dza)
    pw_brp, _ = _matmul_tn("grad_w_branch_pool", yp, dzp)
    pw_merge, _ = _matmul_tn("grad_w_merge", h, dgl)
    (dattn, dga, dgp, dmix, dpooled, dps), (pc_bra, pc_brp) = _bwd_branches(
        dza, dzp, attn, proj, mixed, wf_bra, wf_brp, wf_grp, pool_scale, s, d,
        comm=_ScatterGrads(specs[2:4], [pw_bra, pw_brp]))
    pw_grp = _pool_weight_grad(pooled, dmix, s)
    du = _pool_backward(dpooled, s)
    near, far = (0, 1, 2, 3, 6), (4, 5, 7)
    (dq, dkv, dbias, dsink), (pc_merge,) = _attn_backward(
        proj, dattn, bias2, attn_sink, s, comm=_ScatterGrads(specs[1:2], [pw_merge], peers=near))
    dproj = jnp.concatenate([dq, dkv, dga, du, dgp], axis=1)
    pw_in, (pc_merge,) = _matmul_tn("grad_w_in", h, dproj,
                                    comm=_ScatterGrads(specs[1:2], [pw_merge], peers=far, into=[pc_merge]))
    dh, (pc_in, pc_grp) = _bwd_hidden(dproj, dgl, wf_in, wf_merge, s, d,
                                      comm=_ScatterGrads([specs[0], specs[5]], [pw_in, pw_grp]))
    gx, dshift, dscale, dpreg = _bwd_prenorm(dh, x2, dout, mod, pre_norm_g, s, d)

    pieces = [pc_in, pc_merge, pc_bra, pc_brp, pc_out]
    weights = [w_in, w_merge, w_branch_attn, w_branch_pool, w_out]
    moms = [m_w_in, m_w_merge, m_w_branch_attn, m_w_branch_pool, m_w_out]
    vars_ = [v_w_in, v_w_merge, v_w_branch_attn, v_w_branch_pool, v_w_out]
    big = {}
    for nm, pc, w, m, v in zip(names, pieces, weights, moms, vars_):
        shape2 = (-1, w.shape[-1])
        res4 = _reduce_adamw("update_" + nm, pc, w.reshape(shape2), m.reshape(shape2), v.reshape(shape2))
        big[nm] = tuple(a.reshape(w.shape) for a in res4)
    hq = cg // N_CHIPS // 2
    g_grp = _reduce16("reduce_w_pool_group", pc_grp.reshape(2, N_DEV, N_POOL_GROUPS * hq, cg))
    g_grp = g_grp.reshape(2, N_POOL_GROUPS, hq, cg).transpose(1, 0, 2, 3).reshape(N_POOL_GROUPS * 2 * hq, cg)
    res3 = _adamw("adamw_w_pool_group", w_pool_group.reshape(-1, cg), g_grp, m_w_pool_group.reshape(-1, cg),
                  v_w_pool_group.reshape(-1, cg))
    big["w_pool_group"] = tuple(a.reshape(w_pool_group.shape) for a in (g_grp,) + tuple(res3))

    dtable = _bias_table_grad(dbias.reshape(N_Q_HEADS, BLOCK, SPAN), buckets)[:, :N_Q_HEADS]
    segs = [("b_ada", jnp.concatenate([dshift, dscale, dgate], axis=1), 3 * d),
            ("pre_norm_g", dpreg, d), ("post_norm_g", dpostg, d), ("attn_sink", dsink, LANE),
            ("pool_scale", dps, p_w), ("b_merge", dbm, 2 * d), ("rel_bias_table", dtable.reshape(1, -1), 2 * LANE)]
    packed = jnp.concatenate([_pad_lanes(v, w) for _, v, w in segs], axis=1)
    rows = _all_gather8("gather_small", packed, 1)[:, 0, :]

    def pack(vals):
        return jnp.concatenate([_pad_lanes(v.reshape(1, -1), w) for v, (_, _, w) in zip(vals, segs)], axis=1)

    small_w = [b_ada, pre_norm_g, post_norm_g, attn_sink, pool_scale, b_merge, rel_bias_table]
    small_m = [m_b_ada, m_pre_norm_g, m_post_norm_g, m_attn_sink, m_pool_scale, m_b_merge, m_rel_bias_table]
    small_v = [v_b_ada, v_pre_norm_g, v_post_norm_g, v_attn_sink, v_pool_scale, v_b_merge, v_rel_bias_table]
    g_small, d_small, nm_small, nv_small = _small_update(rows, pack(small_w), pack(small_m), pack(small_v))
    small = {}
    off = 0
    for (nm, _, w), ref in zip(segs, small_w):
        cut = lambda a: a[:, off:off + ref.size].reshape(ref.shape)
        small[nm] = (cut(g_small), cut(d_small), cut(nm_small), cut(nv_small))
        off += w

    dmod_cols = lax.dynamic_slice_in_dim(rows[:, :3 * d], chip * (3 * d // N_CHIPS), 3 * d // N_CHIPS, axis=1)
    sc_t = sc_all[:, 0, :].T
    g_ada, d_ada, nm_ada, nv_ada = _ada_backward(sc_t, dmod_cols, w_ada[0], m_w_ada[0], v_w_ada[0])
    big["w_ada"] = tuple(a.reshape(w_ada.shape) for a in (g_ada, d_ada, nm_ada, nv_ada))

    loss = lax.psum(loss_part[0, 0], ("x", "y", "c"))
    order = ["rel_bias_table", "w_ada", "b_ada", "pre_norm_g", "post_norm_g", "w_in", "attn_sink", "w_pool_group",
             "pool_scale", "w_branch_attn", "w_branch_pool", "w_merge", "b_merge", "w_out"]
    res = {**big, **small}
    outs = [loss, gx.reshape(x.shape)]
    for part in range(4):
        outs += [res[nm][part] for nm in order]
    return tuple(outs)


def _small_update(rows, w, m, v):
    _, n = rows.shape

    def body(r_ref, w_ref, m_ref, v_ref, g_ref, d_ref, nm_ref, nv_ref):
        g = r_ref[0:1, :]
        for k in range(1, N_DEV):
            g = g + r_ref[k:k + 1, :]
        dl, nm, nv = _adam_math(w_ref[...], g, m_ref[...], v_ref[...])
        g_ref[...] = g
        d_ref[...] = dl
        nm_ref[...] = nm
        nv_ref[...] = nv

    vm = pl.BlockSpec(memory_space=pltpu.VMEM)
    out = jax.ShapeDtypeStruct((1, n), F32)
    return pl.pallas_call(
        body, name="small_update", out_shape=(out,) * 4, in_specs=[vm] * 4, out_specs=(vm,) * 4,
    )(rows, w, m, v)
```

```python
import functools
import math

import numpy as np
import jax
import jax.numpy as jnp
from jax import lax
from jax.experimental import pallas as pl
from jax.experimental.pallas import tpu as pltpu

F32 = jnp.float32
BF16 = jnp.bfloat16
MESH = pl.DeviceIdType.MESH

HEAD_DIM = 128
N_Q_HEADS = 8
N_KV_HEADS = 2
GQA_GROUP = N_Q_HEADS // N_KV_HEADS
ATTN_WIDTH = N_Q_HEADS * HEAD_DIM
KV_WIDTH = N_KV_HEADS * HEAD_DIM
WINDOW = 128
BLOCK = 128
SPAN = BLOCK + 2 * WINDOW
N_BUCKETS = 32
MAX_DISTANCE = 128
POOL_SIZES = (2, 4, 8, 16)
N_POOL_GROUPS = len(POOL_SIZES)
HALO = 16
EPS = 1e-6
NEG_INF = -1e30
ADAM_LR = 0.001
ADAM_B1 = 0.9
ADAM_B2 = 0.999
ADAM_EPS = 1e-08
ADAM_WD = 0.01
ADAM_STEP = 10

N_DEV = 8
N_CHIPS = 4
LANE = 128
COL_TILE = 512
VMEM_CAP = 60000 * 1024
ROW_TILE = 256


def _cparams(sem, est_bytes):
    limit = int(min(max(est_bytes * 5 // 4 + (4 << 20), 16 << 20), VMEM_CAP))
    return pltpu.CompilerParams(dimension_semantics=sem, vmem_limit_bytes=limit)


def _sigmoid(x):
    return jax.nn.sigmoid(x)


def _silu(x):
    return x * _sigmoid(x)


def _dsilu(x):
    s = _sigmoid(x)
    return s * (1.0 + x * (1.0 - s))


def _place():
    x, y, c = lax.axis_index("x"), lax.axis_index("y"), lax.axis_index("c")
    return x, y, c


def _flip(v, bit):
    return (1 - v) if bit else v


def _xor_peer(k):
    x, y, c = _place()
    return (_flip(x, (k >> 2) & 1), _flip(y, (k >> 1) & 1), _flip(c, k & 1))


def _resident(shape):
    nd = len(shape)
    return pl.BlockSpec(shape, lambda *_: (0,) * nd, pipeline_mode=pl.Buffered(1))


def _const_spec(shape):
    nd = len(shape)
    return pl.BlockSpec(shape, lambda *_: (0,) * nd)


def _all_gather8(name, x, nrows, pre=None):
    r, n = x.shape

    def body(x_ref, out_ref, stage, send_sems, recv_sems):
        px, py, pc = _place()
        me = 4 * px + 2 * py + pc
        v = x_ref[...]
        if pre is not None:
            v = pre(v)
        stage[...] = v[0:nrows]
        out_ref[me] = v[0:nrows]
        copies = []
        for k in range(1, N_DEV):
            cp = pltpu.make_async_remote_copy(
                src_ref=stage, dst_ref=out_ref.at[me], send_sem=send_sems.at[k - 1], recv_sem=recv_sems.at[k - 1],
                device_id=_xor_peer(k), device_id_type=MESH)
            cp.start()
            copies.append(cp)
        for cp in copies:
            cp.wait()

    return pl.pallas_call(
        body, name=name,
        out_shape=jax.ShapeDtypeStruct((N_DEV, nrows, n), F32),
        in_specs=[pl.BlockSpec(memory_space=pltpu.VMEM)],
        out_specs=pl.BlockSpec(memory_space=pltpu.VMEM),
        scratch_shapes=[pltpu.VMEM((nrows, n), F32), pltpu.SemaphoreType.DMA((N_DEV - 1,)),
                        pltpu.SemaphoreType.DMA((N_DEV - 1,))],
    )(x)


class _Sharded:
    def __init__(self, kind, full_shape):
        self.kind = kind
        self.full_shape = tuple(full_shape)
        if kind == "col":
            r, c = full_shape
            self.shard_shape = (r, c // N_CHIPS)
        elif kind == "row":
            r, c = full_shape
            self.shard_shape = (r // N_CHIPS, c)
        else:
            g, r, c = full_shape
            self.shard_shape = (g, r // N_CHIPS, c)
        self.axis = 1 if kind == "grp" else 0
        s = list(self.shard_shape)
        s[self.axis] //= 2
        self.piece_shape = tuple(s)

    def _rows(self, ref, start, size):
        idx = (slice(None),) * self.axis + (pl.ds(pl.multiple_of(start, 16), size),)
        return ref.at[idx]

    def shard_half(self, ref, hc):
        h = self.piece_shape[self.axis]
        return self._rows(ref, hc * h, h)

    def window(self, ref, chip, hc=None):
        s = self.shard_shape
        if self.kind == "col":
            cols = pl.ds(pl.multiple_of(chip * s[1], LANE), s[1])
            if hc is None:
                return ref.at[:, cols]
            h = s[0] // 2
            return ref.at[pl.ds(pl.multiple_of(hc * h, 16), h), cols]
        n = s[self.axis]
        if hc is None:
            return self._rows(ref, chip * n, n)
        return self._rows(ref, chip * n + hc * (n // 2), n // 2)


def _remote(src, dst, send_sem, recv_sem, to):
    return pltpu.make_async_remote_copy(src_ref=src, dst_ref=dst, send_sem=send_sem, recv_sem=recv_sem,
                                        device_id=to, device_id_type=MESH)


class _GatherWeights:
    def __init__(self, specs, shards, middle=0.7):
        self.specs = specs
        self.middle_at = middle
        self.inputs = list(shards)
        self.out_shapes = [jax.ShapeDtypeStruct(sp.full_shape, BF16) for sp in specs]
        nw = len(specs)
        self.scratch = [pltpu.SemaphoreType.DMA((6 * nw,)), pltpu.SemaphoreType.DMA((6 * nw,)),
                        pltpu.SemaphoreType.DMA((nw,))]

    aliases = staticmethod(lambda n_in, n_out: {})

    def phases(self, nsteps):
        return [(0, self.start), (min(nsteps - 1, max(1, int(self.middle_at * nsteps))), self.middle),
                (nsteps - 1, self.end)]

    def _ctx(self):
        x, y, c = _place()
        return x, y, c, 2 * x + y, (x, y, 1 - c), [(1 - x, y), (x, 1 - y), (1 - x, 1 - y)]

    def start(self, shard_refs, full_refs, sems):
        send_sems, recv_sems, local_sems = sems
        x, y, c, my_chip, sibling, chips = self._ctx()
        for w, sp in enumerate(self.specs):
            pltpu.make_async_copy(shard_refs[w], sp.window(full_refs[w], my_chip), local_sems.at[w]).start()
            for t, (cx, cy) in enumerate(chips):
                _remote(sp.shard_half(shard_refs[w], c), sp.window(full_refs[w], my_chip, c),
                        send_sems.at[6 * w + t], recv_sems.at[6 * w + t], (cx, cy, c)).start()

    def middle(self, shard_refs, full_refs, sems):
        send_sems, recv_sems, local_sems = sems
        x, y, c, my_chip, sibling, chips = self._ctx()
        for w, sp in enumerate(self.specs):
            for t, (cx, cy) in enumerate(chips):
                landed = sp.window(full_refs[w], 2 * cx + cy, c)
                _remote(landed, landed, send_sems.at[6 * w + t], recv_sems.at[6 * w + t], (cx, cy, c)).wait_recv()
                _remote(landed, landed, send_sems.at[6 * w + 3 + t], recv_sems.at[6 * w + 3 + t], sibling).start()

    def end(self, shard_refs, full_refs, sems):
        send_sems, recv_sems, local_sems = sems
        x, y, c, my_chip, sibling, chips = self._ctx()
        for w, sp in enumerate(self.specs):
            for t, (cx, cy) in enumerate(chips):
                other = sp.window(full_refs[w], 2 * cx + cy, 1 - c)
                _remote(other, other, send_sems.at[6 * w + 3 + t], recv_sems.at[6 * w + 3 + t], sibling).wait_recv()
        for w, sp in enumerate(self.specs):
            for t, (cx, cy) in enumerate(chips):
                mine = sp.shard_half(shard_refs[w], c)
                _remote(mine, mine, send_sems.at[6 * w + t], recv_sems.at[6 * w + t], (cx, cy, c)).wait_send()
                landed = sp.window(full_refs[w], 2 * cx + cy, c)
                _remote(landed, landed, send_sems.at[6 * w + 3 + t], recv_sems.at[6 * w + 3 + t], sibling).wait_send()
            pltpu.make_async_copy(shard_refs[w], sp.window(full_refs[w], my_chip), local_sems.at[w]).wait()


class _ScatterGrads:
    def __init__(self, specs, partials, peers=tuple(range(N_DEV)), into=None, middle=0.7):
        self.specs = specs
        self.peers = tuple(peers)
        self.middle_at = middle
        self.n_part = len(partials)
        self.into = into is not None
        self.inputs = list(partials) + (list(into) if self.into else [])
        self.out_shapes = [jax.ShapeDtypeStruct((2, N_DEV) + sp.piece_shape, BF16) for sp in specs]
        nw = len(specs)
        self.scratch = [pltpu.SemaphoreType.DMA((15 * nw,)), pltpu.SemaphoreType.DMA((15 * nw,)),
                        pltpu.SemaphoreType.DMA((nw,))]

    def aliases(self, n_in, n_out):
        return {n_in + self.n_part + w: n_out + w for w in range(len(self.specs))} if self.into else {}

    def phases(self, nsteps):
        return [(0, self.start), (min(nsteps - 1, max(1, int(self.middle_at * nsteps))), self.middle),
                (nsteps - 1, self.end)]

    def _own(self, sp, part_ref, recv_ref, x, y, c, sem):
        return pltpu.make_async_copy(sp.window(part_ref, 2 * x + y, c), recv_ref.at[c, 0], sem)

    def start(self, part_refs, recv_refs, sems):
        send_sems, recv_sems, local_sems = sems
        x, y, c = _place()
        for w, sp in enumerate(self.specs):
            for k in self.peers:
                if k == 0:
                    self._own(sp, part_refs[w], recv_refs[w], x, y, c, local_sems.at[w]).start()
                    continue
                px, py, pc = _xor_peer(k)
                _remote(sp.window(part_refs[w], 2 * px + py, pc), recv_refs[w].at[pc, k],
                        send_sems.at[15 * w + k - 1], recv_sems.at[15 * w + k - 1], (px, py, pc)).start()

    def middle(self, part_refs, recv_refs, sems):
        send_sems, recv_sems, local_sems = sems
        x, y, c = _place()
        sibling = (x, y, 1 - c)
        for w, sp in enumerate(self.specs):
            for k in self.peers:
                landed = recv_refs[w].at[c, k]
                if k:
                    _remote(landed, landed, send_sems.at[15 * w + k - 1], recv_sems.at[15 * w + k - 1],
                            sibling).wait_recv()
                else:
                    self._own(sp, part_refs[w], recv_refs[w], x, y, c, local_sems.at[w]).wait()
                _remote(landed, landed, send_sems.at[15 * w + 7 + k], recv_sems.at[15 * w + 7 + k], sibling).start()

    def end(self, part_refs, recv_refs, sems):
        send_sems, recv_sems, local_sems = sems
        x, y, c = _place()
        sibling = (x, y, 1 - c)
        for w, sp in enumerate(self.specs):
            for k in self.peers:
                other = recv_refs[w].at[1 - c, k]
                _remote(other, other, send_sems.at[15 * w + 7 + k], recv_sems.at[15 * w + 7 + k], sibling).wait_recv()
            for k in self.peers:
                landed = recv_refs[w].at[c, k]
                _remote(landed, landed, send_sems.at[15 * w + 7 + k], recv_sems.at[15 * w + 7 + k],
                        sibling).wait_send()
                if k:
                    px, py, pc = _xor_peer(k)
                    sent = sp.window(part_refs[w], 2 * px + py, pc)
                    _remote(sent, sent, send_sems.at[15 * w + k - 1], recv_sems.at[15 * w + k - 1],
                            (px, py, pc)).wait_send()


def _call(name, body, grid, in_specs, args, out_shape, out_specs, scratch, semantics, est_bytes, comm=None,
          aliases=None):
    out_shape, out_specs = tuple(out_shape), tuple(out_specs)
    if comm is None:
        res = pl.pallas_call(body, name=name, grid=grid, out_shape=out_shape, in_specs=list(in_specs),
                             out_specs=out_specs, scratch_shapes=list(scratch), input_output_aliases=aliases or {},
                             compiler_params=_cparams(semantics, est_bytes))(*args)
        return tuple(res), ()
    n_in, n_out, n_sc = len(in_specs), len(out_shape), len(scratch)
    c_in, c_out = len(comm.inputs), len(comm.out_shapes)
    nsteps = math.prod(grid)
    phases = comm.phases(nsteps)

    def hosted(*refs):
        pos = [0]

        def take(n):
            part = refs[pos[0]:pos[0] + n]
            pos[0] += n
            return part

        ins, cins, outs, couts, scr, sems = take(n_in), take(c_in), take(n_out), take(c_out), take(n_sc), take(3)
        step = 0
        for ax, extent in enumerate(grid):
            step = step * extent + pl.program_id(ax)
        for at, fn in phases:
            if at == 0:
                pl.when(step == 0)(functools.partial(fn, cins, couts, sems))
        body(*ins, *outs, *scr)
        for at, fn in phases:
            if at > 0:
                pl.when(step == at)(functools.partial(fn, cins, couts, sems))

    any_spec = pl.BlockSpec(memory_space=pl.ANY)
    res = pl.pallas_call(
        hosted, name=name, grid=grid, out_shape=out_shape + tuple(comm.out_shapes),
        in_specs=list(in_specs) + [any_spec] * c_in, out_specs=out_specs + (any_spec,) * c_out,
        scratch_shapes=list(scratch) + list(comm.scratch),
        input_output_aliases={**(aliases or {}), **comm.aliases(n_in, n_out)},
        compiler_params=_cparams(("arbitrary",) * len(grid), est_bytes))(*args, *comm.inputs)
    return tuple(res[:n_out]), tuple(res[n_out:])


def _row_tile(rows, cap):
    for t in range(min(rows, cap), 0, -1):
        if rows % t == 0 and (t % 16 == 0 or t == rows):
            return t
    return rows


def _cast_bf16(name, x):
    r, c = x.shape
    tr = _row_tile(r, 512)

    def body(x_ref, o_ref):
        o_ref[...] = x_ref[...].astype(BF16)

    return pl.pallas_call(
        body, name=name, grid=(r // tr,), out_shape=jax.ShapeDtypeStruct((r, c), BF16),
        in_specs=[pl.BlockSpec((tr, c), lambda i: (i, 0))], out_specs=pl.BlockSpec((tr, c), lambda i: (i, 0)),
        compiler_params=_cparams(("parallel",), 2 * tr * c * 6),
    )(x)


def _adam_math(w, g, m, v):
    m = ADAM_B1 * m + (1.0 - ADAM_B1) * g
    v = ADAM_B2 * v + (1.0 - ADAM_B2) * (g * g)
    m_hat = m / (1.0 - ADAM_B1 ** ADAM_STEP)
    v_hat = v / (1.0 - ADAM_B2 ** ADAM_STEP)
    delta = -ADAM_LR * (m_hat / (jnp.sqrt(v_hat) + ADAM_EPS) + ADAM_WD * w)
    return delta, m, v


def _adamw(name, w, g, m, v):
    r, c = w.shape
    tr = _row_tile(r, max(8, (1 << 18) // c))

    def body(w_ref, g_ref, m_ref, v_ref, d_ref, nm_ref, nv_ref):
        d, nm, nv = _adam_math(w_ref[...], g_ref[...], m_ref[...], v_ref[...])
        d_ref[...] = d
        nm_ref[...] = nm
        nv_ref[...] = nv

    spec = pl.BlockSpec((tr, c), lambda i: (i, 0))
    out = jax.ShapeDtypeStruct((r, c), F32)
    return pl.pallas_call(
        body, name=name, grid=(r // tr,), out_shape=(out, out, out), in_specs=[spec] * 4, out_specs=(spec,) * 3,
        compiler_params=_cparams(("parallel",), 2 * 7 * tr * c * 4),
    )(w, g, m, v)


def _sum_pieces(x_ref):
    acc = x_ref[0, 0].astype(F32)
    for k in range(1, N_DEV):
        acc = acc + x_ref[0, k].astype(F32)
    return acc


def _reduce16(name, x):
    _, _, r, c = x.shape
    tr = _row_tile(r, max(16, (1 << 18) // c))
    nt = r // tr

    def body(x_ref, o_ref):
        o_ref[...] = _sum_pieces(x_ref)

    return pl.pallas_call(
        body, name=name, grid=(2, nt), out_shape=jax.ShapeDtypeStruct((2 * r, c), F32),
        in_specs=[pl.BlockSpec((1, N_DEV, tr, c), lambda hf, i: (hf, 0, i, 0))],
        out_specs=pl.BlockSpec((tr, c), lambda hf, i: (hf * nt + i, 0)),
        compiler_params=_cparams(("parallel", "parallel"), 2 * (N_DEV * 2 + 4) * tr * c),
    )(x)


def _reduce_adamw(name, x, w, m, v):
    _, _, r, c = x.shape
    tr = _row_tile(r, max(16, (1 << 18) // c))
    nt = r // tr

    def body(x_ref, w_ref, m_ref, v_ref, g_ref, d_ref, nm_ref, nv_ref):
        g = _sum_pieces(x_ref)
        d, nm, nv = _adam_math(w_ref[...], g, m_ref[...], v_ref[...])
        g_ref[...] = g
        d_ref[...] = d
        nm_ref[...] = nm
        nv_ref[...] = nv

    tile = pl.BlockSpec((tr, c), lambda hf, i: (hf * nt + i, 0))
    out = jax.ShapeDtypeStruct((2 * r, c), F32)
    return pl.pallas_call(
        body, name=name, grid=(2, nt), out_shape=(out,) * 4,
        in_specs=[pl.BlockSpec((1, N_DEV, tr, c), lambda hf, i: (hf, 0, i, 0)), tile, tile, tile],
        out_specs=(tile,) * 4,
        compiler_params=_cparams(("parallel", "parallel"), 2 * (N_DEV * 2 + 7 * 4) * tr * c),
    )(x, w, m, v)


def _t5_buckets():
    rel = jnp.arange(SPAN)[None, :] - WINDOW - jnp.arange(BLOCK)[:, None]
    half = N_BUCKETS // 2
    max_exact = half // 2
    ret = jnp.where(rel > 0, half, 0)
    n = jnp.abs(rel)
    nf = jnp.maximum(n, 1).astype(F32)
    large = max_exact + (jnp.log(nf / max_exact) / math.log(MAX_DISTANCE / max_exact)
                         * (half - max_exact)).astype(jnp.int32)
    large = jnp.minimum(large, half - 1)
    return (ret + jnp.where(n < max_exact, n, large)).astype(jnp.int32)


def _bias_table(table, buckets):
    def body(t_ref, b_ref, o_ref):
        bk = b_ref[...]
        rel = (lax.broadcasted_iota(jnp.int32, (BLOCK, SPAN), 1) - WINDOW
               - lax.broadcasted_iota(jnp.int32, (BLOCK, SPAN), 0))
        band = jnp.abs(rel) <= WINDOW
        for h in range(N_Q_HEADS):
            acc = jnp.zeros((BLOCK, SPAN), F32)
            for b in range(N_BUCKETS):
                acc = jnp.where(bk == b, t_ref[b, h], acc)
            o_ref[h] = jnp.where(band, acc, NEG_INF)

    return pl.pallas_call(
        body, name="bias_table", out_shape=jax.ShapeDtypeStruct((N_Q_HEADS, BLOCK, SPAN), F32),
        in_specs=[pl.BlockSpec(memory_space=pltpu.SMEM), pl.BlockSpec(memory_space=pltpu.VMEM)],
        out_specs=pl.BlockSpec(memory_space=pltpu.VMEM),
    )(table, buckets)


def _bias_table_grad(dbias, buckets):
    def body(d_ref, b_ref, o_ref):
        bk = b_ref[...]
        row = lax.broadcasted_iota(jnp.int32, (N_BUCKETS, LANE), 0)
        lane = lax.broadcasted_iota(jnp.int32, (N_BUCKETS, LANE), 1)
        acc = jnp.zeros((N_BUCKETS, LANE), F32)
        for h in range(N_Q_HEADS):
            d = d_ref[h]
            for b in range(N_BUCKETS):
                s = jnp.sum(jnp.where(bk == b, d, 0.0))
                acc = jnp.where((row == b) & (lane == h), s, acc)
        o_ref[...] = acc

    return pl.pallas_call(
        body, name="bias_table_grad", out_shape=jax.ShapeDtypeStruct((N_BUCKETS, LANE), F32),
        in_specs=[pl.BlockSpec(memory_space=pltpu.VMEM), pl.BlockSpec(memory_space=pltpu.VMEM)],
        out_specs=pl.BlockSpec(memory_space=pltpu.VMEM),
    )(dbias, buckets)


def _ada_forward(sc_all, w_ada):
    d, n = w_ada.shape
    tn = _pick_tile(n, COL_TILE)

    def body(sc_ref, w_ref, o_ref):
        row = lax.broadcasted_iota(jnp.int32, (N_DEV, d), 0)
        sc = jnp.zeros((N_DEV, d), F32)
        for k in range(N_DEV):
            sc = jnp.where(row == k, sc_ref[k], sc)
        o_ref[...] = jnp.dot(sc, w_ref[...], preferred_element_type=F32, precision=lax.Precision.HIGHEST)

    return pl.pallas_call(
        body, name="ada_forward", grid=(n // tn,), out_shape=jax.ShapeDtypeStruct((N_DEV, n), F32),
        in_specs=[_const_spec((N_DEV, 1, d)), pl.BlockSpec((d, tn), lambda j: (0, j))],
        out_specs=pl.BlockSpec((N_DEV, tn), lambda j: (0, j)),
        compiler_params=_cparams(("parallel",), 2 * d * tn * 4 + N_DEV * N_DEV * d * 8),
    )(sc_all, w_ada)


def _mod_finish(m_all, b_ada):
    _, _, n = m_all.shape

    def body(m_ref, b_ref, o_ref):
        x, y, c = _place()
        me = 4 * x + 2 * y + c
        row = lax.broadcasted_iota(jnp.int32, (N_DEV, n), 0)
        for j in range(N_CHIPS):
            blk = m_ref[2 * j]
            mine = jnp.sum(jnp.where(row == me, blk, 0.0), axis=0, keepdims=True)
            o_ref[:, j * n:(j + 1) * n] = mine + b_ref[:, j * n:(j + 1) * n]

    return pl.pallas_call(
        body, name="mod_finish", out_shape=jax.ShapeDtypeStruct((1, N_CHIPS * n), F32),
        in_specs=[pl.BlockSpec(memory_space=pltpu.VMEM), pl.BlockSpec(memory_space=pltpu.VMEM)],
        out_specs=pl.BlockSpec(memory_space=pltpu.VMEM),
    )(m_all, b_ada)


def _ada_backward(sc_t, dmod_cols, w, m, v):
    d, n = w.shape
    tr, tn = _row_tile(d, 512), _pick_tile(n, COL_TILE)

    def body(s_ref, dm_ref, w_ref, m_ref, v_ref, g_ref, d_ref, nm_ref, nv_ref):
        g = jnp.dot(s_ref[...], dm_ref[...], preferred_element_type=F32, precision=lax.Precision.HIGHEST)
        dl, nm, nv = _adam_math(w_ref[...], g, m_ref[...], v_ref[...])
        g_ref[...] = g
        d_ref[...] = dl
        nm_ref[...] = nm
        nv_ref[...] = nv

    tile = pl.BlockSpec((tr, tn), lambda i, j: (i, j))
    out = jax.ShapeDtypeStruct((d, n), F32)
    return pl.pallas_call(
        body, name="ada_backward", grid=(d // tr, n // tn), out_shape=(out,) * 4,
        in_specs=[pl.BlockSpec((tr, N_DEV), lambda i, j: (i, 0)), pl.BlockSpec((N_DEV, tn), lambda i, j: (0, j)),
                  tile, tile, tile],
        out_specs=(tile,) * 4,
        compiler_params=_cparams(("parallel", "parallel"), 2 * 8 * tr * tn * 4),
    )(sc_t, dmod_cols, w, m, v)


def _chip():
    return 2 * lax.axis_index("x") + lax.axis_index("y")


def _local_columns(x, mod, pre_g, w_in_shard, w_merge_shard, b_merge, comm):
    s, d = x.shape
    n1, n2 = w_in_shard.shape[1], w_merge_shard.shape[1]
    tm = _row_tile(s, 512)

    def body(x_ref, mod_ref, pg_ref, wi_ref, wm_ref, b_ref, h_ref, p_ref, g_ref):
        xv = x_ref[...]
        r = lax.rsqrt(jnp.mean(xv * xv, axis=1, keepdims=True) + EPS)
        xn = xv * r * pg_ref[...]
        hv = (xn * (1.0 + mod_ref[:, d:2 * d]) + mod_ref[:, 0:d]).astype(BF16)
        h_ref[...] = hv
        p_ref[...] = jnp.dot(hv, wi_ref[...], preferred_element_type=F32).astype(BF16)
        g_ref[...] = _sigmoid(jnp.dot(hv, wm_ref[...], preferred_element_type=F32) + b_ref[...]).astype(BF16)

    in_specs = [pl.BlockSpec((tm, d), lambda i: (i, 0)), _const_spec((1, 3 * d)), _const_spec((1, d)),
                _resident((d, n1)), _resident((d, n2)), pl.BlockSpec((1, n2), lambda i: (0, _chip()))]
    out_specs = [pl.BlockSpec((tm, d), lambda i: (i, 0)), pl.BlockSpec((tm, n1), lambda i: (i, _chip())),
                 pl.BlockSpec((tm, n2), lambda i: (i, _chip()))]
    est = 2 * tm * (4 * d + 2 * d + 2 * n1 + 2 * n2) + d * (n1 + n2) * 2 + 4 * tm * d * 4 + 2 * tm * (n1 + n2) * 4
    return _call("local_columns", body, (s // tm,), in_specs, (x, mod, pre_g, w_in_shard, w_merge_shard, b_merge),
                 [jax.ShapeDtypeStruct((s, d), BF16), jax.ShapeDtypeStruct((s, N_CHIPS * n1), BF16),
                  jax.ShapeDtypeStruct((s, N_CHIPS * n2), BF16)],
                 out_specs, [], ("arbitrary",), est, comm=comm)


def _other_columns(name, a, b, partial, bias=None, comm=None):
    s, k = a.shape
    _, n = b.shape
    tm, tn = _row_tile(s, 2048), n // N_CHIPS
    col = lambda i, j: (_chip() + 1 + j) % N_CHIPS

    def body(*refs):
        if bias is None:
            a_ref, b_ref, _, o_ref = refs
        else:
            a_ref, b_ref, bias_ref, _, o_ref = refs
        acc = jnp.dot(a_ref[...], b_ref[...], preferred_element_type=F32)
        if bias is not None:
            acc = _sigmoid(acc + bias_ref[...])
        o_ref[...] = acc.astype(BF16)

    in_specs = [pl.BlockSpec((tm, k), lambda i, j: (i, 0)), pl.BlockSpec((k, tn), lambda i, j: (0, col(i, j)))]
    args = [a, b]
    if bias is not None:
        in_specs.append(pl.BlockSpec((1, tn), lambda i, j: (0, col(i, j))))
        args.append(bias)
    in_specs.append(pl.BlockSpec(memory_space=pl.ANY))
    args.append(partial)
    (out,), extra = _call(name, body, (s // tm, N_CHIPS - 1), in_specs, args, [jax.ShapeDtypeStruct((s, n), BF16)],
                          [pl.BlockSpec((tm, tn), lambda i, j: (i, col(i, j)))], [], ("parallel", "arbitrary"),
                          2 * (tm * k + k * tn + tm * tn) * 2 + 2 * tm * tn * 4, comm=comm,
                          aliases={len(args) - 1: 0})
    return out, extra


def _col_specs(off, width, rows, row_index):
    assert off % COL_TILE == 0 and width % COL_TILE == 0
    return [pl.BlockSpec((rows, COL_TILE), functools.partial(lambda p, *ids: (row_index(*ids), p), off // COL_TILE + p))
            for p in range(width // COL_TILE)]


def _cat(refs):
    vals = [r[...] for r in refs]
    return vals[0] if len(vals) == 1 else jnp.concatenate(vals, axis=1)


def _attn_mask(n, s):
    kpos = (n - 1) * BLOCK + lax.broadcasted_iota(jnp.int32, (1, SPAN), 1)
    return (kpos >= 0) & (kpos < s)


def _sink_column(sink_ref, kh):
    rows = GQA_GROUP * BLOCK
    grp = lax.broadcasted_iota(jnp.int32, (rows, 1), 0) // BLOCK
    col = jnp.zeros((rows, 1), F32)
    for g in range(GQA_GROUP):
        col = jnp.where(grp == g, sink_ref[0, kh * GQA_GROUP + g], col)
    return col


def _stack_heads(x, kh):
    base = kh * GQA_GROUP * HEAD_DIM
    return jnp.concatenate([x[:, base + g * HEAD_DIM: base + (g + 1) * HEAD_DIM] for g in range(GQA_GROUP)], axis=0)


def _softmax_parts(qs, k, bias, valid, sink_col):
    sc = lax.dot_general(qs, k, (((1,), (1,)), ((), ())), preferred_element_type=F32)
    sc = sc * (HEAD_DIM ** -0.5) + bias
    sc = jnp.where(valid, sc, NEG_INF)
    mx = jnp.maximum(jnp.max(sc, axis=1, keepdims=True), sink_col)
    e = jnp.exp(sc - mx)
    es = jnp.exp(sink_col - mx)
    inv = 1.0 / (jnp.sum(e, axis=1, keepdims=True) + es)
    return e * inv, es * inv


def _attn_forward(proj, bias2, sink, s):
    nblk = s // BLOCK
    nq = ATTN_WIDTH // COL_TILE
    kv_col = ATTN_WIDTH // COL_TILE
    assert 2 * KV_WIDTH == COL_TILE

    def body(*refs):
        q_refs = refs[:nq]
        kvp, kvc, kvn, bias_ref, sink_ref, o_ref = refs[nq:]
        n = pl.program_id(0)
        q = _cat(q_refs)
        kv = jnp.concatenate([kvp[...], kvc[...], kvn[...]], axis=0)
        valid = _attn_mask(n, s)
        for kh in range(N_KV_HEADS):
            qs = _stack_heads(q, kh)
            k = kv[:, kh * HEAD_DIM:(kh + 1) * HEAD_DIM]
            v = kv[:, KV_WIDTH + kh * HEAD_DIM: KV_WIDTH + (kh + 1) * HEAD_DIM]
            p, _ = _softmax_parts(qs, k, bias_ref[kh], valid, _sink_column(sink_ref, kh))
            o = jnp.dot(p.astype(BF16), v, preferred_element_type=F32)
            for g in range(GQA_GROUP):
                h = kh * GQA_GROUP + g
                o_ref[:, h * HEAD_DIM:(h + 1) * HEAD_DIM] = o[g * BLOCK:(g + 1) * BLOCK].astype(BF16)

    in_specs = _col_specs(0, ATTN_WIDTH, BLOCK, lambda n: n)
    in_specs += [pl.BlockSpec((BLOCK, COL_TILE), lambda n: (jnp.maximum(n - 1, 0), kv_col)),
                 pl.BlockSpec((BLOCK, COL_TILE), lambda n: (n, kv_col)),
                 pl.BlockSpec((BLOCK, COL_TILE), lambda n: (jnp.minimum(n + 1, nblk - 1), kv_col)),
                 _const_spec((N_KV_HEADS, GQA_GROUP * BLOCK, SPAN)),
                 pl.BlockSpec(memory_space=pltpu.SMEM)]
    return pl.pallas_call(
        body, name="attn_forward", grid=(nblk,), out_shape=jax.ShapeDtypeStruct((s, ATTN_WIDTH), BF16),
        in_specs=in_specs, out_specs=pl.BlockSpec((BLOCK, ATTN_WIDTH), lambda n: (n, 0)),
        compiler_params=_cparams(("parallel",), 16 << 20),
    )(*([proj] * (nq + 3)), bias2, sink)


def _pool_positions(i, tm, s, width):
    pos = i * tm - HALO + lax.broadcasted_iota(jnp.int32, (tm + 2 * HALO, width), 0)
    return pos, (pos >= 0) & (pos < s)


def _pool_count(pos, w, s):
    return (jnp.minimum(pos + w // 2, s) - jnp.maximum(pos - w // 2, 0)).astype(F32)


def _halo_specs_cols(off, width, tm, s):
    per = tm // HALO
    last = s // HALO - 1
    prev = _col_specs(off, width, HALO, lambda i: jnp.maximum(i * per - 1, 0))
    nxt = _col_specs(off, width, HALO, lambda i: jnp.minimum((i + 1) * per, last))
    return prev, nxt


def _branches(proj, attn, g, w_bra, w_brp, w_grp, pool_scale, s, d):
    a_w, p_w = ATTN_WIDTH, pool_scale.shape[1]
    cg = p_w // N_POOL_GROUPS
    tm = _row_tile(s, 2 * ROW_TILE)
    off_ga = ATTN_WIDTH + 2 * KV_WIDTH
    off_u = off_ga + a_w
    off_gp = off_u + p_w
    n_ga, n_u, n_gp = a_w // COL_TILE, p_w // COL_TILE, p_w // COL_TILE

    def body(*refs):
        it = iter(refs)
        attn_ref = next(it)
        ga_refs = [next(it) for _ in range(n_ga)]
        u_refs = [next(it) for _ in range(n_u)]
        up_refs = [next(it) for _ in range(n_u)]
        un_refs = [next(it) for _ in range(n_u)]
        gp_refs = [next(it) for _ in range(n_gp)]
        g_ref, wa_ref, wp_ref, wg_ref, ps_ref = (next(it) for _ in range(5))
        ya_ref, yp_ref, za_ref, zp_ref, mg_ref, pooled_ref, mixed_ref = (next(it) for _ in range(7))
        i = pl.program_id(0)
        ya = (attn_ref[...].astype(F32) * _silu(_cat(ga_refs).astype(F32))).astype(BF16)
        ya_ref[...] = ya
        za = jnp.dot(ya, wa_ref[...], preferred_element_type=F32)
        za_ref[...] = za.astype(BF16)

        u = _cat(u_refs).astype(F32)
        ext = jnp.concatenate([_cat(up_refs).astype(F32), u, _cat(un_refs).astype(F32)], axis=0)
        pos, ok = _pool_positions(i, tm, s, cg)
        mixed = []
        for gi, w in enumerate(POOL_SIZES):
            e = jnp.where(ok, ext[:, gi * cg:(gi + 1) * cg], 0.0)
            acc = e[HALO - w // 2: HALO - w // 2 + tm]
            for dd in range(-w // 2 + 1, w // 2):
                acc = acc + e[HALO + dd: HALO + dd + tm]
            cnt = _pool_count(pos[HALO:HALO + tm], w, s)
            pooled = (acc / cnt - u[:, gi * cg:(gi + 1) * cg]).astype(BF16)
            pooled_ref[:, gi * cg:(gi + 1) * cg] = pooled
            mixed.append(jnp.dot(pooled, wg_ref[gi], preferred_element_type=F32))
        mixed = jnp.concatenate(mixed, axis=1)
        mixed_ref[...] = mixed.astype(BF16)
        yp = (mixed * ps_ref[...] * _silu(_cat(gp_refs).astype(F32))).astype(BF16)
        yp_ref[...] = yp
        zp = jnp.dot(yp, wp_ref[...], preferred_element_type=F32)
        zp_ref[...] = zp.astype(BF16)
        gate = g_ref[...].astype(F32)
        mg_ref[...] = (gate[:, :d] * za + gate[:, d:] * zp).astype(BF16)

    row = lambda i: i
    u_prev, u_next = _halo_specs_cols(off_u, p_w, tm, s)
    in_specs = [pl.BlockSpec((tm, a_w), lambda i: (i, 0))]
    in_specs += _col_specs(off_ga, a_w, tm, row) + _col_specs(off_u, p_w, tm, row) + u_prev + u_next
    in_specs += _col_specs(off_gp, p_w, tm, row)
    in_specs += [pl.BlockSpec((tm, 2 * d), lambda i: (i, 0)), _resident((a_w, d)), _resident((p_w, d)),
                 _resident((N_POOL_GROUPS, cg, cg)), _const_spec((1, p_w))]
    n_proj = n_ga + 3 * n_u + n_gp
    tile = lambda w: pl.BlockSpec((tm, w), lambda i: (i, 0))
    out_widths = (a_w, p_w, d, d, d, p_w, p_w)
    est = 2 * tm * (a_w + a_w + 2 * p_w + 2 * d + sum(out_widths)) * 2 + (a_w + p_w) * d * 2 + 6 * tm * d * 4
    return pl.pallas_call(
        body, name="branches", grid=(s // tm,),
        out_shape=tuple(jax.ShapeDtypeStruct((s, w), BF16) for w in out_widths),
        in_specs=in_specs, out_specs=tuple(tile(w) for w in out_widths),
        compiler_params=_cparams(("parallel",), est),
    )(attn, *([proj] * n_proj), g, w_bra, w_brp, w_grp, pool_scale)


def _sub_rows(tm, sub=128):
    sub = min(sub, tm)
    return [pl.ds(r * sub, sub) for r in range(tm // sub)]


def _out_loss(merged, x, target, w_out, post_g, mod, s, d):
    tm = _row_tile(s, ROW_TILE)
    nsteps = s // tm

    def body(mg_ref, x_ref, t_ref, w_ref, pg_ref, mod_ref, dout_ref, do_ref, loss_ref, dgate_ref, dpg_ref, lacc):
        i = pl.program_id(0)

        @pl.when(i == 0)
        def _():
            lacc[...] = jnp.zeros_like(lacc)
            dgate_ref[...] = jnp.zeros_like(dgate_ref)
            dpg_ref[...] = jnp.zeros_like(dpg_ref)

        pg = pg_ref[...]
        gate = mod_ref[:, 2 * d:3 * d]
        for rows in _sub_rows(tm):
            o = jnp.dot(mg_ref[rows, :], w_ref[...], preferred_element_type=F32)
            r = lax.rsqrt(jnp.mean(o * o, axis=1, keepdims=True) + EPS)
            ohat = o * r
            y = ohat * pg
            e = x_ref[rows, :] + gate * y - t_ref[rows, :]
            lacc[...] += jnp.sum(e * e, axis=0, keepdims=True)
            dout = e * (1.0 / d)
            dout_ref[rows, :] = dout
            dgate_ref[...] += jnp.sum(dout * y, axis=0, keepdims=True)
            dy = dout * gate
            dpg_ref[...] += jnp.sum(dy * ohat, axis=0, keepdims=True)
            dohat = dy * pg
            do = r * (dohat - ohat * jnp.mean(dohat * ohat, axis=1, keepdims=True))
            do_ref[rows, :] = do.astype(BF16)

        @pl.when(i == nsteps - 1)
        def _():
            loss_ref[...] = (0.5 / d) * jnp.sum(lacc[...], axis=1, keepdims=True)

    tile = pl.BlockSpec((tm, d), lambda i: (i, 0))
    vec = _const_spec((1, d))
    return pl.pallas_call(
        body, name="out_loss", grid=(nsteps,),
        out_shape=(jax.ShapeDtypeStruct((s, d), F32), jax.ShapeDtypeStruct((s, d), BF16),
                   jax.ShapeDtypeStruct((1, 1), F32), jax.ShapeDtypeStruct((1, d), F32),
                   jax.ShapeDtypeStruct((1, d), F32)),
        in_specs=[tile, tile, tile, _resident((d, d)), vec, _const_spec((1, 3 * d))],
        out_specs=(tile, tile, _const_spec((1, 1)), vec, vec),
        scratch_shapes=[pltpu.VMEM((1, d), F32)],
        compiler_params=_cparams(("arbitrary",), 2 * tm * d * (2 + 4 + 4 + 4 + 2) + d * d * 2 + 8 * tm * d * 4),
    )(merged, x, target, w_out, post_g, mod)


def _bwd_out(d_o, g, za, zp, w_out, s, d, comm=None):
    tm = _row_tile(s, ROW_TILE)

    def body(do_ref, g_ref, za_ref, zp_ref, w_ref, dza_ref, dzp_ref, dgl_ref, dbm_ref):
        i = pl.program_id(0)

        @pl.when(i == 0)
        def _():
            dbm_ref[...] = jnp.zeros_like(dbm_ref)

        dm = lax.dot_general(do_ref[...], w_ref[...], (((1,), (1,)), ((), ())), preferred_element_type=F32)
        gate = g_ref[...].astype(F32)
        ga, gp = gate[:, :d], gate[:, d:]
        dza_ref[...] = (dm * ga).astype(BF16)
        dzp_ref[...] = (dm * gp).astype(BF16)
        dla = dm * za_ref[...].astype(F32) * ga * (1.0 - ga)
        dlp = dm * zp_ref[...].astype(F32) * gp * (1.0 - gp)
        dgl_ref[:, :d] = dla.astype(BF16)
        dgl_ref[:, d:] = dlp.astype(BF16)
        dbm_ref[:, :d] += jnp.sum(dla, axis=0, keepdims=True)
        dbm_ref[:, d:] += jnp.sum(dlp, axis=0, keepdims=True)

    tile = pl.BlockSpec((tm, d), lambda i: (i, 0))
    wide = pl.BlockSpec((tm, 2 * d), lambda i: (i, 0))
    return _call("bwd_out", body, (s // tm,), [tile, wide, tile, tile, _resident((d, d))], (d_o, g, za, zp, w_out),
                 (jax.ShapeDtypeStruct((s, d), BF16), jax.ShapeDtypeStruct((s, d), BF16),
                  jax.ShapeDtypeStruct((s, 2 * d), BF16), jax.ShapeDtypeStruct((1, 2 * d), F32)),
                 (tile, tile, wide, _const_spec((1, 2 * d))), [], ("arbitrary",),
                 2 * tm * d * 2 * 9 + d * d * 2 + 8 * tm * d * 4, comm=comm)


def _bwd_branches(dza, dzp, attn, proj, mixed, w_bra, w_brp, w_grp, pool_scale, s, d, comm=None):
    a_w, p_w = ATTN_WIDTH, pool_scale.shape[1]
    cg = p_w // N_POOL_GROUPS
    tm = _row_tile(s, 2 * ROW_TILE)
    off_ga = ATTN_WIDTH + 2 * KV_WIDTH
    off_gp = off_ga + a_w + p_w
    n_ga, n_gp = a_w // COL_TILE, p_w // COL_TILE

    def body(*refs):
        it = iter(refs)
        dza_ref, dzp_ref, attn_ref = next(it), next(it), next(it)
        ga_refs = [next(it) for _ in range(n_ga)]
        gp_refs = [next(it) for _ in range(n_gp)]
        mixed_ref, wa_ref, wp_ref, wg_ref, ps_ref = (next(it) for _ in range(5))
        dattn_ref, dga_ref, dgp_ref, dmix_ref, dpool_ref, dps_ref = (next(it) for _ in range(6))
        i = pl.program_id(0)

        @pl.when(i == 0)
        def _():
            dps_ref[...] = jnp.zeros_like(dps_ref)

        dya = lax.dot_general(dza_ref[...], wa_ref[...], (((1,), (1,)), ((), ())), preferred_element_type=F32)
        ga = _cat(ga_refs).astype(F32)
        dattn_ref[...] = (dya * _silu(ga)).astype(BF16)
        dga_ref[...] = (dya * attn_ref[...].astype(F32) * _dsilu(ga)).astype(BF16)

        dyp = lax.dot_general(dzp_ref[...], wp_ref[...], (((1,), (1,)), ((), ())), preferred_element_type=F32)
        gp = _cat(gp_refs).astype(F32)
        mixed = mixed_ref[...].astype(F32)
        ps = ps_ref[...]
        sg = _silu(gp)
        dgp_ref[...] = (dyp * mixed * ps * _dsilu(gp)).astype(BF16)
        dps_ref[...] += jnp.sum(dyp * sg * mixed, axis=0, keepdims=True)
        dmix = (dyp * sg * ps).astype(BF16)
        dmix_ref[...] = dmix
        for gi in range(N_POOL_GROUPS):
            dp = lax.dot_general(dmix[:, gi * cg:(gi + 1) * cg], wg_ref[gi], (((1,), (1,)), ((), ())),
                                 preferred_element_type=F32)
            dpool_ref[:, gi * cg:(gi + 1) * cg] = dp.astype(BF16)

    row = lambda i: i
    tile = lambda w: pl.BlockSpec((tm, w), lambda i: (i, 0))
    in_specs = [tile(d), tile(d), tile(a_w)] + _col_specs(off_ga, a_w, tm, row) + _col_specs(off_gp, p_w, tm, row)
    in_specs += [tile(p_w), _resident((a_w, d)), _resident((p_w, d)), _resident((N_POOL_GROUPS, cg, cg)),
                 _const_spec((1, p_w))]
    out_widths = (a_w, a_w, p_w, p_w, p_w)
    est = 2 * tm * (2 * d + 2 * a_w + 2 * p_w + sum(out_widths)) * 2 + (a_w + p_w) * d * 2 + 8 * tm * a_w * 4
    return _call("bwd_branches", body, (s // tm,), in_specs,
                 (dza, dzp, attn, *([proj] * (n_ga + n_gp)), mixed, w_bra, w_brp, w_grp, pool_scale),
                 tuple(jax.ShapeDtypeStruct((s, w), BF16) for w in out_widths) + (jax.ShapeDtypeStruct((1, p_w), F32),),
                 tuple(tile(w) for w in out_widths) + (_const_spec((1, p_w)),), [], ("arbitrary",), est, comm=comm)


def _pool_backward(dpooled, s):
    _, p_w = dpooled.shape
    cg = p_w // N_POOL_GROUPS
    tm = _row_tile(s, ROW_TILE)
    per, last = tm // HALO, s // HALO - 1

    def body(dp_ref, prev_ref, next_ref, du_ref):
        i = pl.program_id(0)
        dp = dp_ref[...].astype(F32)
        ext = jnp.concatenate([prev_ref[...].astype(F32), dp, next_ref[...].astype(F32)], axis=0)
        pos, ok = _pool_positions(i, tm, s, cg)
        for gi, w in enumerate(POOL_SIZES):
            t = jnp.where(ok, ext[:, gi * cg:(gi + 1) * cg], 0.0) / _pool_count(pos, w, s)
            acc = t[HALO - w // 2 + 1: HALO - w // 2 + 1 + tm]
            for dd in range(-w // 2 + 2, w // 2 + 1):
                acc = acc + t[HALO + dd: HALO + dd + tm]
            du_ref[:, gi * cg:(gi + 1) * cg] = (acc - dp[:, gi * cg:(gi + 1) * cg]).astype(BF16)

    return pl.pallas_call(
        body, name="pool_backward", grid=(s // tm,), out_shape=jax.ShapeDtypeStruct((s, p_w), BF16),
        in_specs=[pl.BlockSpec((tm, p_w), lambda i: (i, 0)),
                  pl.BlockSpec((HALO, p_w), lambda i: (jnp.maximum(i * per - 1, 0), 0)),
                  pl.BlockSpec((HALO, p_w), lambda i: (jnp.minimum((i + 1) * per, last), 0))],
        out_specs=pl.BlockSpec((tm, p_w), lambda i: (i, 0)),
        compiler_params=_cparams(("parallel",), 4 * tm * p_w * 2 + 8 * tm * p_w * 4),
    )(dpooled, dpooled, dpooled)


def _attn_backward(proj, dattn, bias2, sink, s, comm=None):
    nblk = s // BLOCK
    nq = ATTN_WIDTH // COL_TILE
    kv_col = ATTN_WIDTH // COL_TILE
    rows = GQA_GROUP * BLOCK
    scale = HEAD_DIM ** -0.5

    def body(*refs):
        q_refs = refs[:nq]
        kvp, kvc, kvn, do_ref, bias_ref, sink_ref, dq_ref, dkv_ref, dbias_ref, dsink_ref, acc, sacc = refs[nq:]
        n = pl.program_id(0)

        @pl.when(n == 0)
        def _():
            acc[...] = jnp.zeros_like(acc)
            sacc[...] = jnp.zeros_like(sacc)
            dbias_ref[...] = jnp.zeros_like(dbias_ref)
            dsink_ref[...] = jnp.zeros_like(dsink_ref)

        @pl.when(jnp.logical_and(n >= 1, n < nblk))
        def _():
            acc[(n + 1) % 3] = jnp.zeros((BLOCK, 2 * KV_WIDTH), F32)

        @pl.when(n < nblk)
        def _():
            q = _cat(q_refs)
            do = do_ref[...]
            kv = jnp.concatenate([kvp[...], kvc[...], kvn[...]], axis=0)
            valid = _attn_mask(n, s)
            for kh in range(N_KV_HEADS):
                qs = _stack_heads(q, kh)
                dos = _stack_heads(do, kh)
                k = kv[:, kh * HEAD_DIM:(kh + 1) * HEAD_DIM]
                v = kv[:, KV_WIDTH + kh * HEAD_DIM: KV_WIDTH + (kh + 1) * HEAD_DIM]
                p, ps = _softmax_parts(qs, k, bias_ref[kh], valid, _sink_column(sink_ref, kh))
                dp = lax.dot_general(dos, v, (((1,), (1,)), ((), ())), preferred_element_type=F32)
                delta = jnp.sum(p * dp, axis=1, keepdims=True)
                ds = p * (dp - delta)
                dbias_ref[kh] += ds
                sacc[kh] += -ps * delta
                dsb = ds.astype(BF16)
                dq = jnp.dot(dsb, k, preferred_element_type=F32) * scale
                for g in range(GQA_GROUP):
                    h = kh * GQA_GROUP + g
                    dq_ref[:, h * HEAD_DIM:(h + 1) * HEAD_DIM] = dq[g * BLOCK:(g + 1) * BLOCK].astype(BF16)
                dk = lax.dot_general(dsb, qs, (((0,), (0,)), ((), ())), preferred_element_type=F32) * scale
                dv = lax.dot_general(p.astype(BF16), dos, (((0,), (0,)), ((), ())), preferred_element_type=F32)
                for j in range(3):
                    slot = (n + 2 + j) % 3
                    acc[slot, :, kh * HEAD_DIM:(kh + 1) * HEAD_DIM] += dk[j * BLOCK:(j + 1) * BLOCK]
                    acc[slot, :, KV_WIDTH + kh * HEAD_DIM: KV_WIDTH + (kh + 1) * HEAD_DIM] += dv[j * BLOCK:(j + 1) * BLOCK]

        dkv_ref[...] = acc[(n + 2) % 3].astype(BF16)

        @pl.when(n == nblk)
        def _():
            lane = lax.broadcasted_iota(jnp.int32, (1, LANE), 1)
            out = jnp.zeros((1, LANE), F32)
            for kh in range(N_KV_HEADS):
                col = sacc[kh]
                for g in range(GQA_GROUP):
                    out = jnp.where(lane == kh * GQA_GROUP + g, jnp.sum(col[g * BLOCK:(g + 1) * BLOCK]), out)
            dsink_ref[...] = out

    qi = lambda n: jnp.minimum(n, nblk - 1)
    in_specs = _col_specs(0, ATTN_WIDTH, BLOCK, qi)
    in_specs += [pl.BlockSpec((BLOCK, COL_TILE), lambda n: (jnp.maximum(qi(n) - 1, 0), kv_col)),
                 pl.BlockSpec((BLOCK, COL_TILE), lambda n: (qi(n), kv_col)),
                 pl.BlockSpec((BLOCK, COL_TILE), lambda n: (jnp.minimum(qi(n) + 1, nblk - 1), kv_col)),
                 pl.BlockSpec((BLOCK, ATTN_WIDTH), lambda n: (qi(n), 0)),
                 _const_spec((N_KV_HEADS, rows, SPAN)),
                 pl.BlockSpec(memory_space=pltpu.SMEM)]
    return _call("attn_backward", body, (nblk + 1,), in_specs, (*([proj] * (nq + 3)), dattn, bias2, sink),
                 (jax.ShapeDtypeStruct((s, ATTN_WIDTH), BF16), jax.ShapeDtypeStruct((s, 2 * KV_WIDTH), BF16),
                  jax.ShapeDtypeStruct((N_KV_HEADS, rows, SPAN), F32), jax.ShapeDtypeStruct((1, LANE), F32)),
                 (pl.BlockSpec((BLOCK, ATTN_WIDTH), lambda n: (qi(n), 0)),
                  pl.BlockSpec((BLOCK, 2 * KV_WIDTH), lambda n: (jnp.clip(n - 1, 0, nblk - 1), 0)),
                  _const_spec((N_KV_HEADS, rows, SPAN)), _const_spec((1, LANE))),
                 [pltpu.VMEM((3, BLOCK, 2 * KV_WIDTH), F32), pltpu.VMEM((N_KV_HEADS, rows, 1), F32)],
                 ("arbitrary",), 24 << 20, comm=comm)


def _pick_tile(n, cap):
    t = cap - cap % LANE
    while n % t:
        t -= LANE
    return t


def _matmul_tn(name, a, b, comm=None):
    s, m = a.shape
    _, n = b.shape
    tk = _row_tile(s, 2048)
    tm, tn = _pick_tile(m, 2048), _pick_tile(n, 1152)
    nk = s // tk

    def body(a_ref, b_ref, o_ref, acc):
        k = pl.program_id(2)

        @pl.when(k == 0)
        def _():
            acc[...] = jnp.zeros_like(acc)

        acc[...] += lax.dot_general(a_ref[...], b_ref[...], (((0,), (0,)), ((), ())), preferred_element_type=F32)

        @pl.when(k == nk - 1)
        def _():
            o_ref[...] = acc[...].astype(BF16)

    (out,), extra = _call(
        name, body, (m // tm, n // tn, nk),
        [pl.BlockSpec((tk, tm), lambda i, j, k: (k, i)), pl.BlockSpec((tk, tn), lambda i, j, k: (k, j))], (a, b),
        [jax.ShapeDtypeStruct((m, n), BF16)], [pl.BlockSpec((tm, tn), lambda i, j, k: (i, j))],
        [pltpu.VMEM((tm, tn), F32)], ("parallel", "parallel", "arbitrary"),
        2 * tk * (tm + tn) * 2 + tm * tn * (4 + 4 + 4), comm=comm)
    return out, extra


def _pool_weight_grad(pooled, dmix, s):
    _, p_w = pooled.shape
    cg = p_w // N_POOL_GROUPS
    tk = _row_tile(s, 512)
    nk = s // tk

    def body(a_ref, b_ref, o_ref, acc):
        k = pl.program_id(1)

        @pl.when(k == 0)
        def _():
            acc[...] = jnp.zeros_like(acc)

        acc[...] += lax.dot_general(a_ref[...], b_ref[...], (((0,), (0,)), ((), ())), preferred_element_type=F32)

        @pl.when(k == nk - 1)
        def _():
            o_ref[0] = acc[...].astype(BF16)

    return pl.pallas_call(
        body, name="pool_weight_grad", grid=(N_POOL_GROUPS, nk),
        out_shape=jax.ShapeDtypeStruct((N_POOL_GROUPS, cg, cg), BF16),
        in_specs=[pl.BlockSpec((tk, cg), lambda g, k: (k, g)), pl.BlockSpec((tk, cg), lambda g, k: (k, g))],
        out_specs=pl.BlockSpec((1, cg, cg), lambda g, k: (g, 0, 0)),
        scratch_shapes=[pltpu.VMEM((cg, cg), F32)],
        compiler_params=_cparams(("parallel", "arbitrary"), 16 << 20),
    )(pooled, dmix)


def _bwd_hidden(dproj, dgl, w_in, w_merge, s, d, comm=None):
    tm = _row_tile(s, 1024)
    t_in, t_mg = _pick_tile(dproj.shape[1], 1024), _pick_tile(dgl.shape[1], 1024)
    n_in = dproj.shape[1] // t_in
    n_mg = dgl.shape[1] // t_mg
    nk = n_in + n_mg

    def body(dp_ref, dg_ref, wi_ref, wm_ref, dh_ref, acc):
        k = pl.program_id(1)

        @pl.when(k == 0)
        def _():
            acc[...] = lax.dot_general(dp_ref[...], wi_ref[...], (((1,), (1,)), ((), ())),
                                       preferred_element_type=F32)

        @pl.when(jnp.logical_and(k > 0, k < n_in))
        def _():
            acc[...] += lax.dot_general(dp_ref[...], wi_ref[...], (((1,), (1,)), ((), ())),
                                        preferred_element_type=F32)

        @pl.when(jnp.logical_and(k >= n_in, k < nk - 1))
        def _():
            acc[...] += lax.dot_general(dg_ref[...], wm_ref[...], (((1,), (1,)), ((), ())),
                                        preferred_element_type=F32)

        @pl.when(k == nk - 1)
        def _():
            dh_ref[...] = (acc[...] + lax.dot_general(dg_ref[...], wm_ref[...], (((1,), (1,)), ((), ())),
                                                      preferred_element_type=F32)).astype(BF16)

    in_specs = [pl.BlockSpec((tm, t_in), lambda i, k: (i, jnp.minimum(k, n_in - 1))),
                pl.BlockSpec((tm, t_mg), lambda i, k: (i, jnp.maximum(k - n_in, 0))),
                pl.BlockSpec((d, t_in), lambda i, k: (0, jnp.minimum(k, n_in - 1))),
                pl.BlockSpec((d, t_mg), lambda i, k: (0, jnp.maximum(k - n_in, 0)))]
    est = 2 * (tm + d) * (t_in + t_mg) * 2 + 3 * tm * d * 4
    (dh,), extra = _call("bwd_hidden", body, (s // tm, nk), in_specs, (dproj, dgl, w_in, w_merge),
                         [jax.ShapeDtypeStruct((s, d), BF16)], [pl.BlockSpec((tm, d), lambda i, k: (i, 0))],
                         [pltpu.VMEM((tm, d), F32)], ("parallel", "arbitrary"), est, comm=comm)
    return dh, extra


def _bwd_prenorm(dh, x, dout, mod, pre_g, s, d):
    tm = _row_tile(s, 2 * ROW_TILE)

    def body(dh_ref, x_ref, dout_ref, mod_ref, pg_ref, gx_ref, dsh_ref, dsc_ref, dpg_ref):
        i = pl.program_id(0)

        @pl.when(i == 0)
        def _():
            dsh_ref[...] = jnp.zeros_like(dsh_ref)
            dsc_ref[...] = jnp.zeros_like(dsc_ref)
            dpg_ref[...] = jnp.zeros_like(dpg_ref)

        dh = dh_ref[...].astype(F32)
        xv = x_ref[...]
        r = lax.rsqrt(jnp.mean(xv * xv, axis=1, keepdims=True) + EPS)
        xhat = xv * r
        pg = pg_ref[...]
        one_scale = 1.0 + mod_ref[:, d:2 * d]
        dsh_ref[...] += jnp.sum(dh, axis=0, keepdims=True)
        dsc_ref[...] += jnp.sum(dh * xhat, axis=0, keepdims=True) * pg
        dpg_ref[...] += jnp.sum(dh * xhat, axis=0, keepdims=True) * one_scale
        dxh = dh * (one_scale * pg)
        dx = r * (dxh - xhat * jnp.mean(dxh * xhat, axis=1, keepdims=True))
        gx_ref[...] = dout_ref[...] + dx

    tile = pl.BlockSpec((tm, d), lambda i: (i, 0))
    vec = _const_spec((1, d))
    return pl.pallas_call(
        body, name="bwd_prenorm", grid=(s // tm,),
        out_shape=(jax.ShapeDtypeStruct((s, d), F32),) + (jax.ShapeDtypeStruct((1, d), F32),) * 3,
        in_specs=[tile, tile, tile, _const_spec((1, 3 * d)), vec], out_specs=(tile, vec, vec, vec),
        compiler_params=_cparams(("arbitrary",), 2 * 4 * tm * d * 4 + 6 * tm * d * 4),
    )(dh, x, dout, mod, pre_g)


def _pad_lanes(v, width):
    return jnp.pad(v, ((0, 0), (0, width - v.shape[1])))


def kernel(x, c, rel_bias_table, w_ada, b_ada, pre_norm_g, post_norm_g, w_in, attn_sink, w_pool_group, pool_scale, w_branch_attn, w_branch_pool, w_merge, b_merge, w_out, loss_target, m_rel_bias_table, m_w_ada, m_b_ada, m_pre_norm_g, m_post_norm_g, m_w_in, m_attn_sink, m_w_pool_group, m_pool_scale, m_w_branch_attn, m_w_branch_pool, m_w_merge, m_b_merge, m_w_out, v_rel_bias_table, v_w_ada, v_b_ada, v_pre_norm_g, v_post_norm_g, v_w_in, v_attn_sink, v_w_pool_group, v_pool_scale, v_w_branch_attn, v_w_branch_pool, v_w_merge, v_b_merge, v_w_out):
    _, s, d = x.shape
    p_w = pool_scale.shape[-1]
    cg = p_w // N_POOL_GROUPS
    in_w = 2 * ATTN_WIDTH + 2 * KV_WIDTH + 2 * p_w
    x2, t2 = x[0], loss_target[0]
    chip = 2 * lax.axis_index("x") + lax.axis_index("y")

    specs = [_Sharded("col", (d, in_w)), _Sharded("col", (d, 2 * d)), _Sharded("col", (ATTN_WIDTH, d)),
             _Sharded("col", (p_w, d)), _Sharded("row", (d, d)), _Sharded("grp", (N_POOL_GROUPS, cg, cg))]
    shards32 = [w_in[0], w_merge[0], w_branch_attn[0], w_branch_pool[0], w_out[0],
                w_pool_group[0].reshape(N_POOL_GROUPS * cg // N_CHIPS, cg)]
    names = ["w_in", "w_merge", "w_branch_attn", "w_branch_pool", "w_out", "w_pool_group"]
    shards16 = [_cast_bf16("cast_" + nm, w) for nm, w in zip(names, shards32)]
    shards16[5] = shards16[5].reshape(N_POOL_GROUPS, cg // N_CHIPS, cg)

    sc_all = _all_gather8("gather_cond", c, 1, pre=_silu)
    m_all = _all_gather8("gather_mod", _ada_forward(sc_all, w_ada[0]), N_DEV)
    mod = _mod_finish(m_all, b_ada)

    (h, proj, gates), (wf_in,) = _local_columns(x2, mod, pre_norm_g, shards16[0], shards16[1], b_merge,
                                                comm=_GatherWeights(specs[:1], shards16[:1], middle=0.9))
    proj, (wf_merge,) = _other_columns("proj", h, wf_in, proj, comm=_GatherWeights(specs[1:2], shards16[1:2]))
    gates, (wf_bra, wf_brp, wf_out, wf_grp) = _other_columns("merge_gates", h, wf_merge, gates, bias=b_merge,
                                                             comm=_GatherWeights(specs[2:], shards16[2:]))
    buckets = _t5_buckets()
    bias2 = _bias_table(rel_bias_table, buckets).reshape(N_KV_HEADS, GQA_GROUP * BLOCK, SPAN)
    attn = _attn_forward(proj, bias2, attn_sink, s)
    ya, yp, za, zp, merged, pooled, mixed = _branches(proj, attn, gates, wf_bra, wf_brp, wf_grp, pool_scale, s, d)
    dout, d_o, loss_part, dgate, dpostg = _out_loss(merged, x2, t2, wf_out, post_norm_g, mod, s, d)

    pw_out, _ = _matmul_tn("grad_w_out", merged, d_o)
    (dza, dzp, dgl, dbm), (pc_out,) = _bwd_out(d_o, gates, za, zp, wf_out, s, d,
                                                comm=_ScatterGrads(specs[4:5], [pw_out]))
    pw_bra, _ = _matmul_tn("grad_w_branch_attn", ya, dza)
    pw_brp, _ = _matmul_tn("grad_w_branch_pool", yp, dzp)
    pw_merge, _ = _matmul_tn("grad_w_merge", h, dgl)
    (dattn, dga, dgp, dmix, dpooled, dps), (pc_bra, pc_brp) = _bwd_branches(
        dza, dzp, attn, proj, mixed, wf_bra, wf_brp, wf_grp, pool_scale, s, d,
        comm=_ScatterGrads(specs[2:4], [pw_bra, pw_brp]))
    pw_grp = _pool_weight_grad(pooled, dmix, s)
    du = _pool_backward(dpooled, s)
    near, far = (0, 1, 2, 3, 6), (4, 5, 7)
    (dq, dkv, dbias, dsink), (pc_merge,) = _attn_backward(
        proj, dattn, bias2, attn_sink, s, comm=_ScatterGrads(specs[1:2], [pw_merge], peers=near))
    dproj = jnp.concatenate([dq, dkv, dga, du, dgp], axis=1)
    pw_in, (pc_merge,) = _matmul_tn("grad_w_in", h, dproj,
                                    comm=_ScatterGrads(specs[1:2], [pw_merge], peers=far, into=[pc_merge]))
    dh, (pc_in, pc_grp) = _bwd_hidden(dproj, dgl, wf_in, wf_merge, s, d,
                                      comm=_ScatterGrads([specs[0], specs[5]], [pw_in, pw_grp]))
    gx, dshift, dscale, dpreg = _bwd_prenorm(dh, x2, dout, mod, pre_norm_g, s, d)

    pieces = [pc_in, pc_merge, pc_bra, pc_brp, pc_out]
    weights = [w_in, w_merge, w_branch_attn, w_branch_pool, w_out]
    moms = [m_w_in, m_w_merge, m_w_branch_attn, m_w_branch_pool, m_w_out]
    vars_ = [v_w_in, v_w_merge, v_w_branch_attn, v_w_branch_pool, v_w_out]
    big = {}
    for nm, pc, w, m, v in zip(names, pieces, weights, moms, vars_):
        shape2 = (-1, w.shape[-1])
        res4 = _reduce_adamw("update_" + nm, pc, w.reshape(shape2), m.reshape(shape2), v.reshape(shape2))
        big[nm] = tuple(a.reshape(w.shape) for a in res4)
    hq = cg // N_CHIPS // 2
    g_grp = _reduce16("reduce_w_pool_group", pc_grp.reshape(2, N_DEV, N_POOL_GROUPS * hq, cg))
    g_grp = g_grp.reshape(2, N_POOL_GROUPS, hq, cg).transpose(1, 0, 2, 3).reshape(N_POOL_GROUPS * 2 * hq, cg)
    res3 = _adamw("adamw_w_pool_group", w_pool_group.reshape(-1, cg), g_grp, m_w_pool_group.reshape(-1, cg),
                  v_w_pool_group.reshape(-1, cg))
    big["w_pool_group"] = tuple(a.reshape(w_pool_group.shape) for a in (g_grp,) + tuple(res3))

    dtable = _bias_table_grad(dbias.reshape(N_Q_HEADS, BLOCK, SPAN), buckets)[:, :N_Q_HEADS]
    segs = [("b_ada", jnp.concatenate([dshift, dscale, dgate], axis=1), 3 * d),
            ("pre_norm_g", dpreg, d), ("post_norm_g", dpostg, d), ("attn_sink", dsink, LANE),
            ("pool_scale", dps, p_w), ("b_merge", dbm, 2 * d), ("rel_bias_table", dtable.reshape(1, -1), 2 * LANE)]
    packed = jnp.concatenate([_pad_lanes(v, w) for _, v, w in segs], axis=1)
    rows = _all_gather8("gather_small", packed, 1)[:, 0, :]

    def pack(vals):
        return jnp.concatenate([_pad_lanes(v.reshape(1, -1), w) for v, (_, _, w) in zip(vals, segs)], axis=1)

    small_w = [b_ada, pre_norm_g, post_norm_g, attn_sink, pool_scale, b_merge, rel_bias_table]
    small_m = [m_b_ada, m_pre_norm_g, m_post_norm_g, m_attn_sink, m_pool_scale, m_b_merge, m_rel_bias_table]
    small_v = [v_b_ada, v_pre_norm_g, v_post_norm_g, v_attn_sink, v_pool_scale, v_b_merge, v_rel_bias_table]
    g_small, d_small, nm_small, nv_small = _small_update(rows, pack(small_w), pack(small_m), pack(small_v))
    small = {}
    off = 0
    for (nm, _, w), ref in zip(segs, small_w):
        cut = lambda a: a[:, off:off + ref.size].reshape(ref.shape)
        small[nm] = (cut(g_small), cut(d_small), cut(nm_small), cut(nv_small))
        off += w

    dmod_cols = lax.dynamic_slice_in_dim(rows[:, :3 * d], chip * (3 * d // N_CHIPS), 3 * d // N_CHIPS, axis=1)
    sc_t = sc_all[:, 0, :].T
    g_ada, d_ada, nm_ada, nv_ada = _ada_backward(sc_t, dmod_cols, w_ada[0], m_w_ada[0], v_w_ada[0])
    big["w_ada"] = tuple(a.reshape(w_ada.shape) for a in (g_ada, d_ada, nm_ada, nv_ada))

    loss = lax.psum(loss_part[0, 0], ("x", "y", "c"))
    order = ["rel_bias_table", "w_ada", "b_ada", "pre_norm_g", "post_norm_g", "w_in", "attn_sink", "w_pool_group",
             "pool_scale", "w_branch_attn", "w_branch_pool", "w_merge", "b_merge", "w_out"]
    res = {**big, **small}
    outs = [loss, gx.reshape(x.shape)]
    for part in range(4):
        outs += [res[nm][part] for nm in order]
    return tuple(outs)


def _small_update(rows, w, m, v):
    _, n = rows.shape

    def body(r_ref, w_ref, m_ref, v_ref, g_ref, d_ref, nm_ref, nv_ref):
        g = r_ref[0:1, :]
        for k in range(1, N_DEV):
            g = g + r_ref[k:k + 1, :]
        dl, nm, nv = _adam_math(w_ref[...], g, m_ref[...], v_ref[...])
        g_ref[...] = g
        d_ref[...] = dl
        nm_ref[...] = nm
        nv_ref[...] = nv

    vm = pl.BlockSpec(memory_space=pltpu.VMEM)
    out = jax.ShapeDtypeStruct((1, n), F32)
    return pl.pallas_call(
        body, name="small_update", out_shape=(out,) * 4, in_specs=[vm] * 4, out_specs=(vm,) * 4,
    )(rows, w, m, v)
```

```python
import functools
import math

import jax
import jax.numpy as jnp
from jax import lax
from jax.experimental import pallas as pl
from jax.experimental.pallas import tpu as pltpu

F32 = jnp.float32
BF16 = jnp.bfloat16
MESH = pl.DeviceIdType.MESH

HEAD_DIM = 128
N_Q_HEADS = 8
N_KV_HEADS = 2
GQA_GROUP = N_Q_HEADS // N_KV_HEADS
ATTN_WIDTH = N_Q_HEADS * HEAD_DIM
KV_WIDTH = N_KV_HEADS * HEAD_DIM
WINDOW = 128
BLOCK = 128
SPAN = BLOCK + 2 * WINDOW
N_BUCKETS = 32
MAX_DISTANCE = 128
POOL_SIZES = (2, 4, 8, 16)
N_POOL_GROUPS = len(POOL_SIZES)
HALO = 16
EPS = 1e-6
NEG_INF = -1e30
ADAM_LR = 0.001
ADAM_B1 = 0.9
ADAM_B2 = 0.999
ADAM_EPS = 1e-08
ADAM_WD = 0.01
ADAM_STEP = 10

N_DEV = 8
N_CHIPS = 4
LANE = 128
COL_TILE = 512
VMEM_CAP = 60000 * 1024
ROW_TILE = 256


def _cparams(sem, est_bytes):
    limit = int(min(max(est_bytes * 5 // 4 + (4 << 20), 16 << 20), VMEM_CAP))
    return pltpu.CompilerParams(dimension_semantics=sem, vmem_limit_bytes=limit)


def _sigmoid(x):
    return jax.nn.sigmoid(x)


def _silu(x):
    return x * _sigmoid(x)


def _dsilu(x):
    s = _sigmoid(x)
    return s * (1.0 + x * (1.0 - s))


def _place():
    x, y, c = lax.axis_index("x"), lax.axis_index("y"), lax.axis_index("c")
    return x, y, c


def _flip(v, bit):
    return (1 - v) if bit else v


def _xor_peer(k):
    x, y, c = _place()
    return (_flip(x, (k >> 2) & 1), _flip(y, (k >> 1) & 1), _flip(c, k & 1))


def _resident(shape):
    nd = len(shape)
    return pl.BlockSpec(shape, lambda *_: (0,) * nd, pipeline_mode=pl.Buffered(1))


def _const_spec(shape):
    nd = len(shape)
    return pl.BlockSpec(shape, lambda *_: (0,) * nd)


def _all_gather8(name, x, nrows, pre=None):
    r, n = x.shape

    def body(x_ref, out_ref, stage, send_sems, recv_sems):
        px, py, pc = _place()
        me = 4 * px + 2 * py + pc
        v = x_ref[...]
        if pre is not None:
            v = pre(v)
        stage[...] = v[0:nrows]
        out_ref[me] = v[0:nrows]
        copies = []
        for k in range(1, N_DEV):
            cp = pltpu.make_async_remote_copy(
                src_ref=stage, dst_ref=out_ref.at[me], send_sem=send_sems.at[k - 1], recv_sem=recv_sems.at[k - 1],
                device_id=_xor_peer(k), device_id_type=MESH)
            cp.start()
            copies.append(cp)
        for cp in copies:
            cp.wait()

    return pl.pallas_call(
        body, name=name,
        out_shape=jax.ShapeDtypeStruct((N_DEV, nrows, n), F32),
        in_specs=[pl.BlockSpec(memory_space=pltpu.VMEM)],
        out_specs=pl.BlockSpec(memory_space=pltpu.VMEM),
        scratch_shapes=[pltpu.VMEM((nrows, n), F32), pltpu.SemaphoreType.DMA((N_DEV - 1,)),
                        pltpu.SemaphoreType.DMA((N_DEV - 1,))],
    )(x)


class _Sharded:
    def __init__(self, kind, full_shape):
        self.kind = kind
        self.full_shape = tuple(full_shape)
        if kind == "col":
            r, c = full_shape
            self.shard_shape = (r, c // N_CHIPS)
        elif kind == "row":
            r, c = full_shape
            self.shard_shape = (r // N_CHIPS, c)
        else:
            g, r, c = full_shape
            self.shard_shape = (g, r // N_CHIPS, c)
        self.axis = 1 if kind == "grp" else 0
        s = list(self.shard_shape)
        s[self.axis] //= 2
        self.piece_shape = tuple(s)

    def _rows(self, ref, start, size):
        idx = (slice(None),) * self.axis + (pl.ds(pl.multiple_of(start, 16), size),)
        return ref.at[idx]

    def shard_half(self, ref, hc):
        h = self.piece_shape[self.axis]
        return self._rows(ref, hc * h, h)

    def window(self, ref, chip, hc=None):
        s = self.shard_shape
        if self.kind == "col":
            cols = pl.ds(pl.multiple_of(chip * s[1], LANE), s[1])
            if hc is None:
                return ref.at[:, cols]
            h = s[0] // 2
            return ref.at[pl.ds(pl.multiple_of(hc * h, 16), h), cols]
        n = s[self.axis]
        if hc is None:
            return self._rows(ref, chip * n, n)
        return self._rows(ref, chip * n + hc * (n // 2), n // 2)


def _remote(src, dst, send_sem, recv_sem, to):
    return pltpu.make_async_remote_copy(src_ref=src, dst_ref=dst, send_sem=send_sem, recv_sem=recv_sem,
                                        device_id=to, device_id_type=MESH)


class _GatherWeights:
    def __init__(self, specs, shards, middle=0.7):
        self.specs = specs
        self.middle_at = middle
        self.inputs = list(shards)
        self.out_shapes = [jax.ShapeDtypeStruct(sp.full_shape, BF16) for sp in specs]
        nw = len(specs)
        self.scratch = [pltpu.SemaphoreType.DMA((6 * nw,)), pltpu.SemaphoreType.DMA((6 * nw,)),
                        pltpu.SemaphoreType.DMA((nw,))]

    aliases = staticmethod(lambda n_in, n_out: {})

    def phases(self, nsteps):
        return [(0, self.start), (min(nsteps - 1, max(1, int(self.middle_at * nsteps))), self.middle),
                (nsteps - 1, self.end)]

    def _ctx(self):
        x, y, c = _place()
        return x, y, c, 2 * x + y, (x, y, 1 - c), [(1 - x, y), (x, 1 - y), (1 - x, 1 - y)]

    def start(self, shard_refs, full_refs, sems):
        send_sems, recv_sems, local_sems = sems
        x, y, c, my_chip, sibling, chips = self._ctx()
        for w, sp in enumerate(self.specs):
            pltpu.make_async_copy(shard_refs[w], sp.window(full_refs[w], my_chip), local_sems.at[w]).start()
            for t, (cx, cy) in enumerate(chips):
                _remote(sp.shard_half(shard_refs[w], c), sp.window(full_refs[w], my_chip, c),
                        send_sems.at[6 * w + t], recv_sems.at[6 * w + t], (cx, cy, c)).start()

    def middle(self, shard_refs, full_refs, sems):
        send_sems, recv_sems, local_sems = sems
        x, y, c, my_chip, sibling, chips = self._ctx()
        for w, sp in enumerate(self.specs):
            for t, (cx, cy) in enumerate(chips):
                landed = sp.window(full_refs[w], 2 * cx + cy, c)
                _remote(landed, landed, send_sems.at[6 * w + t], recv_sems.at[6 * w + t], (cx, cy, c)).wait_recv()
                _remote(landed, landed, send_sems.at[6 * w + 3 + t], recv_sems.at[6 * w + 3 + t], sibling).start()

    def end(self, shard_refs, full_refs, sems):
        send_sems, recv_sems, local_sems = sems
        x, y, c, my_chip, sibling, chips = self._ctx()
        for w, sp in enumerate(self.specs):
            for t, (cx, cy) in enumerate(chips):
                other = sp.window(full_refs[w], 2 * cx + cy, 1 - c)
                _remote(other, other, send_sems.at[6 * w + 3 + t], recv_sems.at[6 * w + 3 + t], sibling).wait_recv()
        for w, sp in enumerate(self.specs):
            for t, (cx, cy) in enumerate(chips):
                mine = sp.shard_half(shard_refs[w], c)
                _remote(mine, mine, send_sems.at[6 * w + t], recv_sems.at[6 * w + t], (cx, cy, c)).wait_send()
                landed = sp.window(full_refs[w], 2 * cx + cy, c)
                _remote(landed, landed, send_sems.at[6 * w + 3 + t], recv_sems.at[6 * w + 3 + t], sibling).wait_send()
            pltpu.make_async_copy(shard_refs[w], sp.window(full_refs[w], my_chip), local_sems.at[w]).wait()


class _ScatterGrads:
    def __init__(self, specs, partials, peers=tuple(range(N_DEV)), into=None, middle=0.7):
        self.specs = specs
        self.peers = tuple(peers)
        self.middle_at = middle
        self.n_part = len(partials)
        self.into = into is not None
        self.inputs = list(partials) + (list(into) if self.into else [])
        self.out_shapes = [jax.ShapeDtypeStruct((2, N_DEV) + sp.piece_shape, BF16) for sp in specs]
        nw = len(specs)
        self.scratch = [pltpu.SemaphoreType.DMA((15 * nw,)), pltpu.SemaphoreType.DMA((15 * nw,)),
                        pltpu.SemaphoreType.DMA((nw,))]

    def aliases(self, n_in, n_out):
        return {n_in + self.n_part + w: n_out + w for w in range(len(self.specs))} if self.into else {}

    def phases(self, nsteps):
        return [(0, self.start), (min(nsteps - 1, max(1, int(self.middle_at * nsteps))), self.middle),
                (nsteps - 1, self.end)]

    def _own(self, sp, part_ref, recv_ref, x, y, c, sem):
        return pltpu.make_async_copy(sp.window(part_ref, 2 * x + y, c), recv_ref.at[c, 0], sem)

    def start(self, part_refs, recv_refs, sems):
        send_sems, recv_sems, local_sems = sems
        x, y, c = _place()
        for w, sp in enumerate(self.specs):
            for k in self.peers:
                if k == 0:
                    self._own(sp, part_refs[w], recv_refs[w], x, y, c, local_sems.at[w]).start()
                    continue
                px, py, pc = _xor_peer(k)
                _remote(sp.window(part_refs[w], 2 * px + py, pc), recv_refs[w].at[pc, k],
                        send_sems.at[15 * w + k - 1], recv_sems.at[15 * w + k - 1], (px, py, pc)).start()

    def middle(self, part_refs, recv_refs, sems):
        send_sems, recv_sems, local_sems = sems
        x, y, c = _place()
        sibling = (x, y, 1 - c)
        for w, sp in enumerate(self.specs):
            for k in self.peers:
                landed = recv_refs[w].at[c, k]
                if k:
                    _remote(landed, landed, send_sems.at[15 * w + k - 1], recv_sems.at[15 * w + k - 1],
                            sibling).wait_recv()
                else:
                    self._own(sp, part_refs[w], recv_refs[w], x, y, c, local_sems.at[w]).wait()
                _remote(landed, landed, send_sems.at[15 * w + 7 + k], recv_sems.at[15 * w + 7 + k], sibling).start()

    def end(self, part_refs, recv_refs, sems):
        send_sems, recv_sems, local_sems = sems
        x, y, c = _place()
        sibling = (x, y, 1 - c)
        for w, sp in enumerate(self.specs):
            for k in self.peers:
                other = recv_refs[w].at[1 - c, k]
                _remote(other, other, send_sems.at[15 * w + 7 + k], recv_sems.at[15 * w + 7 + k], sibling).wait_recv()
            for k in self.peers:
                landed = recv_refs[w].at[c, k]
                _remote(landed, landed, send_sems.at[15 * w + 7 + k], recv_sems.at[15 * w + 7 + k],
                        sibling).wait_send()
                if k:
                    px, py, pc = _xor_peer(k)
                    sent = sp.window(part_refs[w], 2 * px + py, pc)
                    _remote(sent, sent, send_sems.at[15 * w + k - 1], recv_sems.at[15 * w + k - 1],
                            (px, py, pc)).wait_send()


def _call(name, body, grid, in_specs, args, out_shape, out_specs, scratch, semantics, est_bytes, comm=None,
          aliases=None):
    out_shape, out_specs = tuple(out_shape), tuple(out_specs)
    if comm is None:
        res = pl.pallas_call(body, name=name, grid=grid, out_shape=out_shape, in_specs=list(in_specs),
                             out_specs=out_specs, scratch_shapes=list(scratch), input_output_aliases=aliases or {},
                             compiler_params=_cparams(semantics, est_bytes))(*args)
        return tuple(res), ()
    n_in, n_out, n_sc = len(in_specs), len(out_shape), len(scratch)
    c_in, c_out = len(comm.inputs), len(comm.out_shapes)
    nsteps = math.prod(grid)
    phases = comm.phases(nsteps)

    def hosted(*refs):
        pos = [0]

        def take(n):
            part = refs[pos[0]:pos[0] + n]
            pos[0] += n
            return part

        ins, cins, outs, couts, scr, sems = take(n_in), take(c_in), take(n_out), take(c_out), take(n_sc), take(3)
        step = 0
        for ax, extent in enumerate(grid):
            step = step * extent + pl.program_id(ax)
        for at, fn in phases:
            if at == 0:
                pl.when(step == 0)(functools.partial(fn, cins, couts, sems))
        body(*ins, *outs, *scr)
        for at, fn in phases:
            if at > 0:
                pl.when(step == at)(functools.partial(fn, cins, couts, sems))

    any_spec = pl.BlockSpec(memory_space=pl.ANY)
    res = pl.pallas_call(
        hosted, name=name, grid=grid, out_shape=out_shape + tuple(comm.out_shapes),
        in_specs=list(in_specs) + [any_spec] * c_in, out_specs=out_specs + (any_spec,) * c_out,
        scratch_shapes=list(scratch) + list(comm.scratch),
        input_output_aliases={**(aliases or {}), **comm.aliases(n_in, n_out)},
        compiler_params=_cparams(("arbitrary",) * len(grid), est_bytes))(*args, *comm.inputs)
    return tuple(res[:n_out]), tuple(res[n_out:])


def _row_tile(rows, cap):
    for t in range(min(rows, cap), 0, -1):
        if rows % t == 0 and (t % 16 == 0 or t == rows):
            return t
    return rows


def _cast_bf16(name, x):
    r, c = x.shape
    tr = _row_tile(r, 512)

    def body(x_ref, o_ref):
        o_ref[...] = x_ref[...].astype(BF16)

    return pl.pallas_call(
        body, name=name, grid=(r // tr,), out_shape=jax.ShapeDtypeStruct((r, c), BF16),
        in_specs=[pl.BlockSpec((tr, c), lambda i: (i, 0))], out_specs=pl.BlockSpec((tr, c), lambda i: (i, 0)),
        compiler_params=_cparams(("parallel",), 2 * tr * c * 6),
    )(x)


def _adam_math(w, g, m, v):
    m = ADAM_B1 * m + (1.0 - ADAM_B1) * g
    v = ADAM_B2 * v + (1.0 - ADAM_B2) * (g * g)
    m_hat = m / (1.0 - ADAM_B1 ** ADAM_STEP)
    v_hat = v / (1.0 - ADAM_B2 ** ADAM_STEP)
    delta = -ADAM_LR * (m_hat / (jnp.sqrt(v_hat) + ADAM_EPS) + ADAM_WD * w)
    return delta, m, v


def _adamw(name, w, g, m, v):
    r, c = w.shape
    tr = _row_tile(r, max(8, (1 << 18) // c))

    def body(w_ref, g_ref, m_ref, v_ref, d_ref, nm_ref, nv_ref):
        d, nm, nv = _adam_math(w_ref[...], g_ref[...], m_ref[...], v_ref[...])
        d_ref[...] = d
        nm_ref[...] = nm
        nv_ref[...] = nv

    spec = pl.BlockSpec((tr, c), lambda i: (i, 0))
    out = jax.ShapeDtypeStruct((r, c), F32)
    return pl.pallas_call(
        body, name=name, grid=(r // tr,), out_shape=(out, out, out), in_specs=[spec] * 4, out_specs=(spec,) * 3,
        compiler_params=_cparams(("parallel",), 2 * 7 * tr * c * 4),
    )(w, g, m, v)


def _sum_pieces(x_ref):
    acc = x_ref[0, 0].astype(F32)
    for k in range(1, N_DEV):
        acc = acc + x_ref[0, k].astype(F32)
    return acc


def _reduce16(name, x):
    _, _, r, c = x.shape
    tr = _row_tile(r, max(16, (1 << 18) // c))
    nt = r // tr

    def body(x_ref, o_ref):
        o_ref[...] = _sum_pieces(x_ref)

    return pl.pallas_call(
        body, name=name, grid=(2, nt), out_shape=jax.ShapeDtypeStruct((2 * r, c), F32),
        in_specs=[pl.BlockSpec((1, N_DEV, tr, c), lambda hf, i: (hf, 0, i, 0))],
        out_specs=pl.BlockSpec((tr, c), lambda hf, i: (hf * nt + i, 0)),
        compiler_params=_cparams(("parallel", "parallel"), 2 * (N_DEV * 2 + 4) * tr * c),
    )(x)


def _reduce_adamw(name, x, w, m, v):
    _, _, r, c = x.shape
    tr = _row_tile(r, max(16, (1 << 18) // c))
    nt = r // tr

    def body(x_ref, w_ref, m_ref, v_ref, g_ref, d_ref, nm_ref, nv_ref):
        g = _sum_pieces(x_ref)
        d, nm, nv = _adam_math(w_ref[...], g, m_ref[...], v_ref[...])
        g_ref[...] = g
        d_ref[...] = d
        nm_ref[...] = nm
        nv_ref[...] = nv

    tile = pl.BlockSpec((tr, c), lambda hf, i: (hf * nt + i, 0))
    out = jax.ShapeDtypeStruct((2 * r, c), F32)
    return pl.pallas_call(
        body, name=name, grid=(2, nt), out_shape=(out,) * 4,
        in_specs=[pl.BlockSpec((1, N_DEV, tr, c), lambda hf, i: (hf, 0, i, 0)), tile, tile, tile],
        out_specs=(tile,) * 4,
        compiler_params=_cparams(("parallel", "parallel"), 2 * (N_DEV * 2 + 7 * 4) * tr * c),
    )(x, w, m, v)


def _t5_buckets():
    rel = jnp.arange(SPAN)[None, :] - WINDOW - jnp.arange(BLOCK)[:, None]
    half = N_BUCKETS // 2
    max_exact = half // 2
    ret = jnp.where(rel > 0, half, 0)
    n = jnp.abs(rel)
    nf = jnp.maximum(n, 1).astype(F32)
    large = max_exact + (jnp.log(nf / max_exact) / math.log(MAX_DISTANCE / max_exact)
                         * (half - max_exact)).astype(jnp.int32)
    large = jnp.minimum(large, half - 1)
    return (ret + jnp.where(n < max_exact, n, large)).astype(jnp.int32)


def _bias_table(table, buckets):
    def body(t_ref, b_ref, o_ref):
        bk = b_ref[...]
        rel = (lax.broadcasted_iota(jnp.int32, (BLOCK, SPAN), 1) - WINDOW
               - lax.broadcasted_iota(jnp.int32, (BLOCK, SPAN), 0))
        band = jnp.abs(rel) <= WINDOW
        for h in range(N_Q_HEADS):
            acc = jnp.zeros((BLOCK, SPAN), F32)
            for b in range(N_BUCKETS):
                acc = jnp.where(bk == b, t_ref[b, h], acc)
            o_ref[h] = jnp.where(band, acc, NEG_INF)

    return pl.pallas_call(
        body, name="bias_table", out_shape=jax.ShapeDtypeStruct((N_Q_HEADS, BLOCK, SPAN), F32),
        in_specs=[pl.BlockSpec(memory_space=pltpu.SMEM), pl.BlockSpec(memory_space=pltpu.VMEM)],
        out_specs=pl.BlockSpec(memory_space=pltpu.VMEM),
    )(table, buckets)


def _bias_table_grad(dbias, buckets):
    def body(d_ref, b_ref, o_ref):
        bk = b_ref[...]
        row = lax.broadcasted_iota(jnp.int32, (N_BUCKETS, LANE), 0)
        lane = lax.broadcasted_iota(jnp.int32, (N_BUCKETS, LANE), 1)
        acc = jnp.zeros((N_BUCKETS, LANE), F32)
        for h in range(N_Q_HEADS):
            d = d_ref[h]
            for b in range(N_BUCKETS):
                s = jnp.sum(jnp.where(bk == b, d, 0.0))
                acc = jnp.where((row == b) & (lane == h), s, acc)
        o_ref[...] = acc

    return pl.pallas_call(
        body, name="bias_table_grad", out_shape=jax.ShapeDtypeStruct((N_BUCKETS, LANE), F32),
        in_specs=[pl.BlockSpec(memory_space=pltpu.VMEM), pl.BlockSpec(memory_space=pltpu.VMEM)],
        out_specs=pl.BlockSpec(memory_space=pltpu.VMEM),
    )(dbias, buckets)


def _ada_forward(sc_all, w_ada):
    d, n = w_ada.shape
    tn = _pick_tile(n, COL_TILE)

    def body(sc_ref, w_ref, o_ref):
        row = lax.broadcasted_iota(jnp.int32, (N_DEV, d), 0)
        sc = jnp.zeros((N_DEV, d), F32)
        for k in range(N_DEV):
            sc = jnp.where(row == k, sc_ref[k], sc)
        o_ref[...] = jnp.dot(sc, w_ref[...], preferred_element_type=F32, precision=lax.Precision.HIGHEST)

    return pl.pallas_call(
        body, name="ada_forward", grid=(n // tn,), out_shape=jax.ShapeDtypeStruct((N_DEV, n), F32),
        in_specs=[_const_spec((N_DEV, 1, d)), pl.BlockSpec((d, tn), lambda j: (0, j))],
        out_specs=pl.BlockSpec((N_DEV, tn), lambda j: (0, j)),
        compiler_params=_cparams(("parallel",), 2 * d * tn * 4 + N_DEV * N_DEV * d * 8),
    )(sc_all, w_ada)


def _mod_finish(m_all, b_ada):
    _, _, n = m_all.shape

    def body(m_ref, b_ref, o_ref):
        x, y, c = _place()
        me = 4 * x + 2 * y + c
        row = lax.broadcasted_iota(jnp.int32, (N_DEV, n), 0)
        for j in range(N_CHIPS):
            blk = m_ref[2 * j]
            mine = jnp.sum(jnp.where(row == me, blk, 0.0), axis=0, keepdims=True)
            o_ref[:, j * n:(j + 1) * n] = mine + b_ref[:, j * n:(j + 1) * n]

    return pl.pallas_call(
        body, name="mod_finish", out_shape=jax.ShapeDtypeStruct((1, N_CHIPS * n), F32),
        in_specs=[pl.BlockSpec(memory_space=pltpu.VMEM), pl.BlockSpec(memory_space=pltpu.VMEM)],
        out_specs=pl.BlockSpec(memory_space=pltpu.VMEM),
    )(m_all, b_ada)


def _ada_backward(sc_t, dmod_cols, w, m, v):
    d, n = w.shape
    tr, tn = _row_tile(d, 512), _pick_tile(n, COL_TILE)

    def body(s_ref, dm_ref, w_ref, m_ref, v_ref, g_ref, d_ref, nm_ref, nv_ref):
        g = jnp.dot(s_ref[...], dm_ref[...], preferred_element_type=F32, precision=lax.Precision.HIGHEST)
        dl, nm, nv = _adam_math(w_ref[...], g, m_ref[...], v_ref[...])
        g_ref[...] = g
        d_ref[...] = dl
        nm_ref[...] = nm
        nv_ref[...] = nv

    tile = pl.BlockSpec((tr, tn), lambda i, j: (i, j))
    out = jax.ShapeDtypeStruct((d, n), F32)
    return pl.pallas_call(
        body, name="ada_backward", grid=(d // tr, n // tn), out_shape=(out,) * 4,
        in_specs=[pl.BlockSpec((tr, N_DEV), lambda i, j: (i, 0)), pl.BlockSpec((N_DEV, tn), lambda i, j: (0, j)),
                  tile, tile, tile],
        out_specs=(tile,) * 4,
        compiler_params=_cparams(("parallel", "parallel"), 2 * 8 * tr * tn * 4),
    )(sc_t, dmod_cols, w, m, v)


def _chip():
    return 2 * lax.axis_index("x") + lax.axis_index("y")


def _local_columns(x, mod, pre_g, w_in_shard, w_merge_shard, b_merge, comm):
    s, d = x.shape
    n1, n2 = w_in_shard.shape[1], w_merge_shard.shape[1]
    tm = _row_tile(s, 512)

    def body(x_ref, mod_ref, pg_ref, wi_ref, wm_ref, b_ref, h_ref, p_ref, g_ref):
        xv = x_ref[...]
        r = lax.rsqrt(jnp.mean(xv * xv, axis=1, keepdims=True) + EPS)
        xn = xv * r * pg_ref[...]
        hv = (xn * (1.0 + mod_ref[:, d:2 * d]) + mod_ref[:, 0:d]).astype(BF16)
        h_ref[...] = hv
        p_ref[...] = jnp.dot(hv, wi_ref[...], preferred_element_type=F32).astype(BF16)
        g_ref[...] = _sigmoid(jnp.dot(hv, wm_ref[...], preferred_element_type=F32) + b_ref[...]).astype(BF16)

    in_specs = [pl.BlockSpec((tm, d), lambda i: (i, 0)), _const_spec((1, 3 * d)), _const_spec((1, d)),
                _resident((d, n1)), _resident((d, n2)), pl.BlockSpec((1, n2), lambda i: (0, _chip()))]
    out_specs = [pl.BlockSpec((tm, d), lambda i: (i, 0)), pl.BlockSpec((tm, n1), lambda i: (i, _chip())),
                 pl.BlockSpec((tm, n2), lambda i: (i, _chip()))]
    est = 2 * tm * (4 * d + 2 * d + 2 * n1 + 2 * n2) + d * (n1 + n2) * 2 + 4 * tm * d * 4 + 2 * tm * (n1 + n2) * 4
    return _call("local_columns", body, (s // tm,), in_specs, (x, mod, pre_g, w_in_shard, w_merge_shard, b_merge),
                 [jax.ShapeDtypeStruct((s, d), BF16), jax.ShapeDtypeStruct((s, N_CHIPS * n1), BF16),
                  jax.ShapeDtypeStruct((s, N_CHIPS * n2), BF16)],
                 out_specs, [], ("arbitrary",), est, comm=comm)


def _other_columns(name, a, b, partial, bias=None, comm=None):
    s, k = a.shape
    _, n = b.shape
    tm, tn = _row_tile(s, 2048), n // N_CHIPS
    col = lambda i, j: (_chip() + 1 + j) % N_CHIPS

    def body(*refs):
        if bias is None:
            a_ref, b_ref, _, o_ref = refs
        else:
            a_ref, b_ref, bias_ref, _, o_ref = refs
        acc = jnp.dot(a_ref[...], b_ref[...], preferred_element_type=F32)
        if bias is not None:
            acc = _sigmoid(acc + bias_ref[...])
        o_ref[...] = acc.astype(BF16)

    in_specs = [pl.BlockSpec((tm, k), lambda i, j: (i, 0)), pl.BlockSpec((k, tn), lambda i, j: (0, col(i, j)))]
    args = [a, b]
    if bias is not None:
        in_specs.append(pl.BlockSpec((1, tn), lambda i, j: (0, col(i, j))))
        args.append(bias)
    in_specs.append(pl.BlockSpec(memory_space=pl.ANY))
    args.append(partial)
    (out,), extra = _call(name, body, (s // tm, N_CHIPS - 1), in_specs, args, [jax.ShapeDtypeStruct((s, n), BF16)],
                          [pl.BlockSpec((tm, tn), lambda i, j: (i, col(i, j)))], [], ("parallel", "arbitrary"),
                          2 * (tm * k + k * tn + tm * tn) * 2 + 2 * tm * tn * 4, comm=comm,
                          aliases={len(args) - 1: 0})
    return out, extra


def _col_specs(off, width, rows, row_index):
    assert off % COL_TILE == 0 and width % COL_TILE == 0
    return [pl.BlockSpec((rows, COL_TILE), functools.partial(lambda p, *ids: (row_index(*ids), p), off // COL_TILE + p))
            for p in range(width // COL_TILE)]


def _cat(refs):
    vals = [r[...] for r in refs]
    return vals[0] if len(vals) == 1 else jnp.concatenate(vals, axis=1)


def _attn_mask(n, s):
    kpos = (n - 1) * BLOCK + lax.broadcasted_iota(jnp.int32, (1, SPAN), 1)
    return (kpos >= 0) & (kpos < s)


def _sink_column(sink_ref, kh):
    rows = GQA_GROUP * BLOCK
    grp = lax.broadcasted_iota(jnp.int32, (rows, 1), 0) // BLOCK
    col = jnp.zeros((rows, 1), F32)
    for g in range(GQA_GROUP):
        col = jnp.where(grp == g, sink_ref[0, kh * GQA_GROUP + g], col)
    return col


def _stack_heads(x, kh):
    base = kh * GQA_GROUP * HEAD_DIM
    return jnp.concatenate([x[:, base + g * HEAD_DIM: base + (g + 1) * HEAD_DIM] for g in range(GQA_GROUP)], axis=0)


def _softmax_parts(qs, k, bias, valid, sink_col):
    sc = lax.dot_general(qs, k, (((1,), (1,)), ((), ())), preferred_element_type=F32)
    sc = sc * (HEAD_DIM ** -0.5) + bias
    sc = jnp.where(valid, sc, NEG_INF)
    mx = jnp.maximum(jnp.max(sc, axis=1, keepdims=True), sink_col)
    e = jnp.exp(sc - mx)
    es = jnp.exp(sink_col - mx)
    inv = 1.0 / (jnp.sum(e, axis=1, keepdims=True) + es)
    return e * inv, es * inv


def _attn_forward(proj, bias2, sink, s):
    nblk = s // BLOCK
    nq = ATTN_WIDTH // COL_TILE
    kv_col = ATTN_WIDTH // COL_TILE
    assert 2 * KV_WIDTH == COL_TILE

    def body(*refs):
        q_refs = refs[:nq]
        kvp, kvc, kvn, bias_ref, sink_ref, o_ref = refs[nq:]
        n = pl.program_id(0)
        q = _cat(q_refs)
        kv = jnp.concatenate([kvp[...], kvc[...], kvn[...]], axis=0)
        valid = _attn_mask(n, s)
        for kh in range(N_KV_HEADS):
            qs = _stack_heads(q, kh)
            k = kv[:, kh * HEAD_DIM:(kh + 1) * HEAD_DIM]
            v = kv[:, KV_WIDTH + kh * HEAD_DIM: KV_WIDTH + (kh + 1) * HEAD_DIM]
            p, _ = _softmax_parts(qs, k, bias_ref[kh], valid, _sink_column(sink_ref, kh))
            o = jnp.dot(p.astype(BF16), v, preferred_element_type=F32)
            for g in range(GQA_GROUP):
                h = kh * GQA_GROUP + g
                o_ref[:, h * HEAD_DIM:(h + 1) * HEAD_DIM] = o[g * BLOCK:(g + 1) * BLOCK].astype(BF16)

    in_specs = _col_specs(0, ATTN_WIDTH, BLOCK, lambda n: n)
    in_specs += [pl.BlockSpec((BLOCK, COL_TILE), lambda n: (jnp.maximum(n - 1, 0), kv_col)),
                 pl.BlockSpec((BLOCK, COL_TILE), lambda n: (n, kv_col)),
                 pl.BlockSpec((BLOCK, COL_TILE), lambda n: (jnp.minimum(n + 1, nblk - 1), kv_col)),
                 _const_spec((N_KV_HEADS, GQA_GROUP * BLOCK, SPAN)),
                 pl.BlockSpec(memory_space=pltpu.SMEM)]
    return pl.pallas_call(
        body, name="attn_forward", grid=(nblk,), out_shape=jax.ShapeDtypeStruct((s, ATTN_WIDTH), BF16),
        in_specs=in_specs, out_specs=pl.BlockSpec((BLOCK, ATTN_WIDTH), lambda n: (n, 0)),
        compiler_params=_cparams(("parallel",), 16 << 20),
    )(*([proj] * (nq + 3)), bias2, sink)


def _pool_positions(i, tm, s, width):
    pos = i * tm - HALO + lax.broadcasted_iota(jnp.int32, (tm + 2 * HALO, width), 0)
    return pos, (pos >= 0) & (pos < s)


def _pool_count(pos, w, s):
    return (jnp.minimum(pos + w // 2, s) - jnp.maximum(pos - w // 2, 0)).astype(F32)


def _halo_specs_cols(off, width, tm, s):
    per = tm // HALO
    last = s // HALO - 1
    prev = _col_specs(off, width, HALO, lambda i: jnp.maximum(i * per - 1, 0))
    nxt = _col_specs(off, width, HALO, lambda i: jnp.minimum((i + 1) * per, last))
    return prev, nxt


def _branches(proj, attn, g, w_bra, w_brp, w_grp, pool_scale, s, d):
    a_w, p_w = ATTN_WIDTH, pool_scale.shape[1]
    cg = p_w // N_POOL_GROUPS
    tm = _row_tile(s, 2 * ROW_TILE)
    off_ga = ATTN_WIDTH + 2 * KV_WIDTH
    off_u = off_ga + a_w
    off_gp = off_u + p_w
    n_ga, n_u, n_gp = a_w // COL_TILE, p_w // COL_TILE, p_w // COL_TILE

    def body(*refs):
        it = iter(refs)
        attn_ref = next(it)
        ga_refs = [next(it) for _ in range(n_ga)]
        u_refs = [next(it) for _ in range(n_u)]
        up_refs = [next(it) for _ in range(n_u)]
        un_refs = [next(it) for _ in range(n_u)]
        gp_refs = [next(it) for _ in range(n_gp)]
        g_ref, wa_ref, wp_ref, wg_ref, ps_ref = (next(it) for _ in range(5))
        ya_ref, yp_ref, za_ref, zp_ref, mg_ref, pooled_ref, mixed_ref = (next(it) for _ in range(7))
        i = pl.program_id(0)
        ya = (attn_ref[...].astype(F32) * _silu(_cat(ga_refs).astype(F32))).astype(BF16)
        ya_ref[...] = ya
        za = jnp.dot(ya, wa_ref[...], preferred_element_type=F32)
        za_ref[...] = za.astype(BF16)

        u = _cat(u_refs).astype(F32)
        ext = jnp.concatenate([_cat(up_refs).astype(F32), u, _cat(un_refs).astype(F32)], axis=0)
        pos, ok = _pool_positions(i, tm, s, cg)
        mixed = []
        for gi, w in enumerate(POOL_SIZES):
            e = jnp.where(ok, ext[:, gi * cg:(gi + 1) * cg], 0.0)
            acc = e[HALO - w // 2: HALO - w // 2 + tm]
            for dd in range(-w // 2 + 1, w // 2):
                acc = acc + e[HALO + dd: HALO + dd + tm]
            cnt = _pool_count(pos[HALO:HALO + tm], w, s)
            pooled = (acc / cnt - u[:, gi * cg:(gi + 1) * cg]).astype(BF16)
            pooled_ref[:, gi * cg:(gi + 1) * cg] = pooled
            mixed.append(jnp.dot(pooled, wg_ref[gi], preferred_element_type=F32))
        mixed = jnp.concatenate(mixed, axis=1)
        mixed_ref[...] = mixed.astype(BF16)
        yp = (mixed * ps_ref[...] * _silu(_cat(gp_refs).astype(F32))).astype(BF16)
        yp_ref[...] = yp
        zp = jnp.dot(yp, wp_ref[...], preferred_element_type=F32)
        zp_ref[...] = zp.astype(BF16)
        gate = g_ref[...].astype(F32)
        mg_ref[...] = (gate[:, :d] * za + gate[:, d:] * zp).astype(BF16)

    row = lambda i: i
    u_prev, u_next = _halo_specs_cols(off_u, p_w, tm, s)
    in_specs = [pl.BlockSpec((tm, a_w), lambda i: (i, 0))]
    in_specs += _col_specs(off_ga, a_w, tm, row) + _col_specs(off_u, p_w, tm, row) + u_prev + u_next
    in_specs += _col_specs(off_gp, p_w, tm, row)
    in_specs += [pl.BlockSpec((tm, 2 * d), lambda i: (i, 0)), _resident((a_w, d)), _resident((p_w, d)),
                 _resident((N_POOL_GROUPS, cg, cg)), _const_spec((1, p_w))]
    n_proj = n_ga + 3 * n_u + n_gp
    tile = lambda w: pl.BlockSpec((tm, w), lambda i: (i, 0))
    out_widths = (a_w, p_w, d, d, d, p_w, p_w)
    est = 2 * tm * (a_w + a_w + 2 * p_w + 2 * d + sum(out_widths)) * 2 + (a_w + p_w) * d * 2 + 6 * tm * d * 4
    return pl.pallas_call(
        body, name="branches", grid=(s // tm,),
        out_shape=tuple(jax.ShapeDtypeStruct((s, w), BF16) for w in out_widths),
        in_specs=in_specs, out_specs=tuple(tile(w) for w in out_widths),
        compiler_params=_cparams(("parallel",), est),
    )(attn, *([proj] * n_proj), g, w_bra, w_brp, w_grp, pool_scale)


def _sub_rows(tm, sub=128):
    sub = min(sub, tm)
    return [pl.ds(r * sub, sub) for r in range(tm // sub)]


def _out_loss(merged, x, target, w_out, post_g, mod, s, d):
    tm = _row_tile(s, ROW_TILE)
    nsteps = s // tm

    def body(mg_ref, x_ref, t_ref, w_ref, pg_ref, mod_ref, dout_ref, do_ref, loss_ref, dgate_ref, dpg_ref, lacc):
        i = pl.program_id(0)

        @pl.when(i == 0)
        def _():
            lacc[...] = jnp.zeros_like(lacc)
            dgate_ref[...] = jnp.zeros_like(dgate_ref)
            dpg_ref[...] = jnp.zeros_like(dpg_ref)

        pg = pg_ref[...]
        gate = mod_ref[:, 2 * d:3 * d]
        for rows in _sub_rows(tm):
            o = jnp.dot(mg_ref[rows, :], w_ref[...], preferred_element_type=F32)
            r = lax.rsqrt(jnp.mean(o * o, axis=1, keepdims=True) + EPS)
            ohat = o * r
            y = ohat * pg
            e = x_ref[rows, :] + gate * y - t_ref[rows, :]
            lacc[...] += jnp.sum(e * e, axis=0, keepdims=True)
            dout = e * (1.0 / d)
            dout_ref[rows, :] = dout
            dgate_ref[...] += jnp.sum(dout * y, axis=0, keepdims=True)
            dy = dout * gate
            dpg_ref[...] += jnp.sum(dy * ohat, axis=0, keepdims=True)
            dohat = dy * pg
            do = r * (dohat - ohat * jnp.mean(dohat * ohat, axis=1, keepdims=True))
            do_ref[rows, :] = do.astype(BF16)

        @pl.when(i == nsteps - 1)
        def _():
            loss_ref[...] = (0.5 / d) * jnp.sum(lacc[...], axis=1, keepdims=True)

    tile = pl.BlockSpec((tm, d), lambda i: (i, 0))
    vec = _const_spec((1, d))
    return pl.pallas_call(
        body, name="out_loss", grid=(nsteps,),
        out_shape=(jax.ShapeDtypeStruct((s, d), F32), jax.ShapeDtypeStruct((s, d), BF16),
                   jax.ShapeDtypeStruct((1, 1), F32), jax.ShapeDtypeStruct((1, d), F32),
                   jax.ShapeDtypeStruct((1, d), F32)),
        in_specs=[tile, tile, tile, _resident((d, d)), vec, _const_spec((1, 3 * d))],
        out_specs=(tile, tile, _const_spec((1, 1)), vec, vec),
        scratch_shapes=[pltpu.VMEM((1, d), F32)],
        compiler_params=_cparams(("arbitrary",), 2 * tm * d * (2 + 4 + 4 + 4 + 2) + d * d * 2 + 8 * tm * d * 4),
    )(merged, x, target, w_out, post_g, mod)


def _bwd_out(d_o, g, za, zp, w_out, s, d, comm=None):
    tm = _row_tile(s, ROW_TILE)

    def body(do_ref, g_ref, za_ref, zp_ref, w_ref, dza_ref, dzp_ref, dgl_ref, dbm_ref):
        i = pl.program_id(0)

        @pl.when(i == 0)
        def _():
            dbm_ref[...] = jnp.zeros_like(dbm_ref)

        dm = lax.dot_general(do_ref[...], w_ref[...], (((1,), (1,)), ((), ())), preferred_element_type=F32)
        gate = g_ref[...].astype(F32)
        ga, gp = gate[:, :d], gate[:, d:]
        dza_ref[...] = (dm * ga).astype(BF16)
        dzp_ref[...] = (dm * gp).astype(BF16)
        dla = dm * za_ref[...].astype(F32) * ga * (1.0 - ga)
        dlp = dm * zp_ref[...].astype(F32) * gp * (1.0 - gp)
        dgl_ref[:, :d] = dla.astype(BF16)
        dgl_ref[:, d:] = dlp.astype(BF16)
        dbm_ref[:, :d] += jnp.sum(dla, axis=0, keepdims=True)
        dbm_ref[:, d:] += jnp.sum(dlp, axis=0, keepdims=True)

    tile = pl.BlockSpec((tm, d), lambda i: (i, 0))
    wide = pl.BlockSpec((tm, 2 * d), lambda i: (i, 0))
    return _call("bwd_out", body, (s // tm,), [tile, wide, tile, tile, _resident((d, d))], (d_o, g, za, zp, w_out),
                 (jax.ShapeDtypeStruct((s, d), BF16), jax.ShapeDtypeStruct((s, d), BF16),
                  jax.ShapeDtypeStruct((s, 2 * d), BF16), jax.ShapeDtypeStruct((1, 2 * d), F32)),
                 (tile, tile, wide, _const_spec((1, 2 * d))), [], ("arbitrary",),
                 2 * tm * d * 2 * 9 + d * d * 2 + 8 * tm * d * 4, comm=comm)


def _bwd_branches(dza, dzp, attn, proj, mixed, w_bra, w_brp, w_grp, pool_scale, s, d, comm=None):
    a_w, p_w = ATTN_WIDTH, pool_scale.shape[1]
    cg = p_w // N_POOL_GROUPS
    tm = _row_tile(s, 2 * ROW_TILE)
    off_ga = ATTN_WIDTH + 2 * KV_WIDTH
    off_gp = off_ga + a_w + p_w
    n_ga, n_gp = a_w // COL_TILE, p_w // COL_TILE

    def body(*refs):
        it = iter(refs)
        dza_ref, dzp_ref, attn_ref = next(it), next(it), next(it)
        ga_refs = [next(it) for _ in range(n_ga)]
        gp_refs = [next(it) for _ in range(n_gp)]
        mixed_ref, wa_ref, wp_ref, wg_ref, ps_ref = (next(it) for _ in range(5))
        dattn_ref, dga_ref, dgp_ref, dmix_ref, dpool_ref, dps_ref = (next(it) for _ in range(6))
        i = pl.program_id(0)

        @pl.when(i == 0)
        def _():
            dps_ref[...] = jnp.zeros_like(dps_ref)

        dya = lax.dot_general(dza_ref[...], wa_ref[...], (((1,), (1,)), ((), ())), preferred_element_type=F32)
        ga = _cat(ga_refs).astype(F32)
        dattn_ref[...] = (dya * _silu(ga)).astype(BF16)
        dga_ref[...] = (dya * attn_ref[...].astype(F32) * _dsilu(ga)).astype(BF16)

        dyp = lax.dot_general(dzp_ref[...], wp_ref[...], (((1,), (1,)), ((), ())), preferred_element_type=F32)
        gp = _cat(gp_refs).astype(F32)
        mixed = mixed_ref[...].astype(F32)
        ps = ps_ref[...]
        sg = _silu(gp)
        dgp_ref[...] = (dyp * mixed * ps * _dsilu(gp)).astype(BF16)
        dps_ref[...] += jnp.sum(dyp * sg * mixed, axis=0, keepdims=True)
        dmix = (dyp * sg * ps).astype(BF16)
        dmix_ref[...] = dmix
        for gi in range(N_POOL_GROUPS):
            dp = lax.dot_general(dmix[:, gi * cg:(gi + 1) * cg], wg_ref[gi], (((1,), (1,)), ((), ())),
                                 preferred_element_type=F32)
            dpool_ref[:, gi * cg:(gi + 1) * cg] = dp.astype(BF16)

    row = lambda i: i
    tile = lambda w: pl.BlockSpec((tm, w), lambda i: (i, 0))
    in_specs = [tile(d), tile(d), tile(a_w)] + _col_specs(off_ga, a_w, tm, row) + _col_specs(off_gp, p_w, tm, row)
    in_specs += [tile(p_w), _resident((a_w, d)), _resident((p_w, d)), _resident((N_POOL_GROUPS, cg, cg)),
                 _const_spec((1, p_w))]
    out_widths = (a_w, a_w, p_w, p_w, p_w)
    est = 2 * tm * (2 * d + 2 * a_w + 2 * p_w + sum(out_widths)) * 2 + (a_w + p_w) * d * 2 + 8 * tm * a_w * 4
    return _call("bwd_branches", body, (s // tm,), in_specs,
                 (dza, dzp, attn, *([proj] * (n_ga + n_gp)), mixed, w_bra, w_brp, w_grp, pool_scale),
                 tuple(jax.ShapeDtypeStruct((s, w), BF16) for w in out_widths) + (jax.ShapeDtypeStruct((1, p_w), F32),),
                 tuple(tile(w) for w in out_widths) + (_const_spec((1, p_w)),), [], ("arbitrary",), est, comm=comm)


def _pool_backward(dpooled, s):
    _, p_w = dpooled.shape
    cg = p_w // N_POOL_GROUPS
    tm = _row_tile(s, ROW_TILE)
    per, last = tm // HALO, s // HALO - 1

    def body(dp_ref, prev_ref, next_ref, du_ref):
        i = pl.program_id(0)
        dp = dp_ref[...].astype(F32)
        ext = jnp.concatenate([prev_ref[...].astype(F32), dp, next_ref[...].astype(F32)], axis=0)
        pos, ok = _pool_positions(i, tm, s, cg)
        for gi, w in enumerate(POOL_SIZES):
            t = jnp.where(ok, ext[:, gi * cg:(gi + 1) * cg], 0.0) / _pool_count(pos, w, s)
            acc = t[HALO - w // 2 + 1: HALO - w // 2 + 1 + tm]
            for dd in range(-w // 2 + 2, w // 2 + 1):
                acc = acc + t[HALO + dd: HALO + dd + tm]
            du_ref[:, gi * cg:(gi + 1) * cg] = (acc - dp[:, gi * cg:(gi + 1) * cg]).astype(BF16)

    return pl.pallas_call(
        body, name="pool_backward", grid=(s // tm,), out_shape=jax.ShapeDtypeStruct((s, p_w), BF16),
        in_specs=[pl.BlockSpec((tm, p_w), lambda i: (i, 0)),
                  pl.BlockSpec((HALO, p_w), lambda i: (jnp.maximum(i * per - 1, 0), 0)),
                  pl.BlockSpec((HALO, p_w), lambda i: (jnp.minimum((i + 1) * per, last), 0))],
        out_specs=pl.BlockSpec((tm, p_w), lambda i: (i, 0)),
        compiler_params=_cparams(("parallel",), 4 * tm * p_w * 2 + 8 * tm * p_w * 4),
    )(dpooled, dpooled, dpooled)


def _attn_backward(proj, dattn, bias2, sink, s, comm=None):
    nblk = s // BLOCK
    nq = ATTN_WIDTH // COL_TILE
    kv_col = ATTN_WIDTH // COL_TILE
    rows = GQA_GROUP * BLOCK
    scale = HEAD_DIM ** -0.5

    def body(*refs):
        q_refs = refs[:nq]
        kvp, kvc, kvn, do_ref, bias_ref, sink_ref, dq_ref, dkv_ref, dbias_ref, dsink_ref, acc, sacc = refs[nq:]
        n = pl.program_id(0)

        @pl.when(n == 0)
        def _():
            acc[...] = jnp.zeros_like(acc)
            sacc[...] = jnp.zeros_like(sacc)
            dbias_ref[...] = jnp.zeros_like(dbias_ref)
            dsink_ref[...] = jnp.zeros_like(dsink_ref)

        @pl.when(jnp.logical_and(n >= 1, n < nblk))
        def _():
            acc[(n + 1) % 3] = jnp.zeros((BLOCK, 2 * KV_WIDTH), F32)

        @pl.when(n < nblk)
        def _():
            q = _cat(q_refs)
            do = do_ref[...]
            kv = jnp.concatenate([kvp[...], kvc[...], kvn[...]], axis=0)
            valid = _attn_mask(n, s)
            for kh in range(N_KV_HEADS):
                qs = _stack_heads(q, kh)
                dos = _stack_heads(do, kh)
                k = kv[:, kh * HEAD_DIM:(kh + 1) * HEAD_DIM]
                v = kv[:, KV_WIDTH + kh * HEAD_DIM: KV_WIDTH + (kh + 1) * HEAD_DIM]
                p, ps = _softmax_parts(qs, k, bias_ref[kh], valid, _sink_column(sink_ref, kh))
                dp = lax.dot_general(dos, v, (((1,), (1,)), ((), ())), preferred_element_type=F32)
                delta = jnp.sum(p * dp, axis=1, keepdims=True)
                ds = p * (dp - delta)
                dbias_ref[kh] += ds
                sacc[kh] += -ps * delta
                dsb = ds.astype(BF16)
                dq = jnp.dot(dsb, k, preferred_element_type=F32) * scale
                for g in range(GQA_GROUP):
                    h = kh * GQA_GROUP + g
                    dq_ref[:, h * HEAD_DIM:(h + 1) * HEAD_DIM] = dq[g * BLOCK:(g + 1) * BLOCK].astype(BF16)
                dk = lax.dot_general(dsb, qs, (((0,), (0,)), ((), ())), preferred_element_type=F32) * scale
                dv = lax.dot_general(p.astype(BF16), dos, (((0,), (0,)), ((), ())), preferred_element_type=F32)
                for j in range(3):
                    slot = (n + 2 + j) % 3
                    acc[slot, :, kh * HEAD_DIM:(kh + 1) * HEAD_DIM] += dk[j * BLOCK:(j + 1) * BLOCK]
                    acc[slot, :, KV_WIDTH + kh * HEAD_DIM: KV_WIDTH + (kh + 1) * HEAD_DIM] += dv[j * BLOCK:(j + 1) * BLOCK]

        dkv_ref[...] = acc[(n + 2) % 3].astype(BF16)

        @pl.when(n == nblk)
        def _():
            lane = lax.broadcasted_iota(jnp.int32, (1, LANE), 1)
            out = jnp.zeros((1, LANE), F32)
            for kh in range(N_KV_HEADS):
                col = sacc[kh]
                for g in range(GQA_GROUP):
                    out = jnp.where(lane == kh * GQA_GROUP + g, jnp.sum(col[g * BLOCK:(g + 1) * BLOCK]), out)
            dsink_ref[...] = out

    qi = lambda n: jnp.minimum(n, nblk - 1)
    in_specs = _col_specs(0, ATTN_WIDTH, BLOCK, qi)
    in_specs += [pl.BlockSpec((BLOCK, COL_TILE), lambda n: (jnp.maximum(qi(n) - 1, 0), kv_col)),
                 pl.BlockSpec((BLOCK, COL_TILE), lambda n: (qi(n), kv_col)),
                 pl.BlockSpec((BLOCK, COL_TILE), lambda n: (jnp.minimum(qi(n) + 1, nblk - 1), kv_col)),
                 pl.BlockSpec((BLOCK, ATTN_WIDTH), lambda n: (qi(n), 0)),
                 _const_spec((N_KV_HEADS, rows, SPAN)),
                 pl.BlockSpec(memory_space=pltpu.SMEM)]
    return _call("attn_backward", body, (nblk + 1,), in_specs, (*([proj] * (nq + 3)), dattn, bias2, sink),
                 (jax.ShapeDtypeStruct((s, ATTN_WIDTH), BF16), jax.ShapeDtypeStruct((s, 2 * KV_WIDTH), BF16),
                  jax.ShapeDtypeStruct((N_KV_HEADS, rows, SPAN), F32), jax.ShapeDtypeStruct((1, LANE), F32)),
                 (pl.BlockSpec((BLOCK, ATTN_WIDTH), lambda n: (qi(n), 0)),
                  pl.BlockSpec((BLOCK, 2 * KV_WIDTH), lambda n: (jnp.clip(n - 1, 0, nblk - 1), 0)),
                  _const_spec((N_KV_HEADS, rows, SPAN)), _const_spec((1, LANE))),
                 [pltpu.VMEM((3, BLOCK, 2 * KV_WIDTH), F32), pltpu.VMEM((N_KV_HEADS, rows, 1), F32)],
                 ("arbitrary",), 24 << 20, comm=comm)


def _pick_tile(n, cap):
    t = cap - cap % LANE
    while n % t:
        t -= LANE
    return t


def _matmul_tn(name, a, b, comm=None):
    s, m = a.shape
    _, n = b.shape
    tk = _row_tile(s, 2048)
    tm, tn = _pick_tile(m, 2048), _pick_tile(n, 1152)
    nk = s // tk

    def body(a_ref, b_ref, o_ref, acc):
        k = pl.program_id(2)

        @pl.when(k == 0)
        def _():
            acc[...] = jnp.zeros_like(acc)

        acc[...] += lax.dot_general(a_ref[...], b_ref[...], (((0,), (0,)), ((), ())), preferred_element_type=F32)

        @pl.when(k == nk - 1)
        def _():
            o_ref[...] = acc[...].astype(BF16)

    (out,), extra = _call(
        name, body, (m // tm, n // tn, nk),
        [pl.BlockSpec((tk, tm), lambda i, j, k: (k, i)), pl.BlockSpec((tk, tn), lambda i, j, k: (k, j))], (a, b),
        [jax.ShapeDtypeStruct((m, n), BF16)], [pl.BlockSpec((tm, tn), lambda i, j, k: (i, j))],
        [pltpu.VMEM((tm, tn), F32)], ("parallel", "parallel", "arbitrary"),
        2 * tk * (tm + tn) * 2 + tm * tn * (4 + 4 + 4), comm=comm)
    return out, extra


def _pool_weight_grad(pooled, dmix, s):
    _, p_w = pooled.shape
    cg = p_w // N_POOL_GROUPS
    tk = _row_tile(s, 2048)
    nk = s // tk

    def body(a_ref, b_ref, o_ref, acc):
        k = pl.program_id(0)

        @pl.when(k == 0)
        def _():
            acc[...] = jnp.zeros_like(acc)

        for gi in range(N_POOL_GROUPS):
            cols = pl.ds(gi * cg, cg)
            acc[gi] += lax.dot_general(a_ref[:, cols], b_ref[:, cols], (((0,), (0,)), ((), ())),
                                       preferred_element_type=F32)

        @pl.when(k == nk - 1)
        def _():
            o_ref[...] = acc[...].astype(BF16)

    return pl.pallas_call(
        body, name="pool_weight_grad", grid=(nk,),
        out_shape=jax.ShapeDtypeStruct((N_POOL_GROUPS, cg, cg), BF16),
        in_specs=[pl.BlockSpec((tk, p_w), lambda k: (k, 0)), pl.BlockSpec((tk, p_w), lambda k: (k, 0))],
        out_specs=_const_spec((N_POOL_GROUPS, cg, cg)),
        scratch_shapes=[pltpu.VMEM((N_POOL_GROUPS, cg, cg), F32)],
        compiler_params=_cparams(("arbitrary",), 4 * tk * p_w * 2 + 3 * N_POOL_GROUPS * cg * cg * 4),
    )(pooled, dmix)


def _bwd_hidden(dproj, dgl, w_in, w_merge, s, d, comm=None):
    tm = _row_tile(s, 1024)
    t_in, t_mg = _pick_tile(dproj.shape[1], 1024), _pick_tile(dgl.shape[1], 1024)
    n_in = dproj.shape[1] // t_in
    n_mg = dgl.shape[1] // t_mg
    nk = n_in + n_mg

    def body(dp_ref, dg_ref, wi_ref, wm_ref, dh_ref, acc):
        k = pl.program_id(1)

        @pl.when(k == 0)
        def _():
            acc[...] = lax.dot_general(dp_ref[...], wi_ref[...], (((1,), (1,)), ((), ())),
                                       preferred_element_type=F32)

        @pl.when(jnp.logical_and(k > 0, k < n_in))
        def _():
            acc[...] += lax.dot_general(dp_ref[...], wi_ref[...], (((1,), (1,)), ((), ())),
                                        preferred_element_type=F32)

        @pl.when(jnp.logical_and(k >= n_in, k < nk - 1))
        def _():
            acc[...] += lax.dot_general(dg_ref[...], wm_ref[...], (((1,), (1,)), ((), ())),
                                        preferred_element_type=F32)

        @pl.when(k == nk - 1)
        def _():
            dh_ref[...] = (acc[...] + lax.dot_general(dg_ref[...], wm_ref[...], (((1,), (1,)), ((), ())),
                                                      preferred_element_type=F32)).astype(BF16)

    in_specs = [pl.BlockSpec((tm, t_in), lambda i, k: (i, jnp.minimum(k, n_in - 1))),
                pl.BlockSpec((tm, t_mg), lambda i, k: (i, jnp.maximum(k - n_in, 0))),
                pl.BlockSpec((d, t_in), lambda i, k: (0, jnp.minimum(k, n_in - 1))),
                pl.BlockSpec((d, t_mg), lambda i, k: (0, jnp.maximum(k - n_in, 0)))]
    est = 2 * (tm + d) * (t_in + t_mg) * 2 + 3 * tm * d * 4
    (dh,), extra = _call("bwd_hidden", body, (s // tm, nk), in_specs, (dproj, dgl, w_in, w_merge),
                         [jax.ShapeDtypeStruct((s, d), BF16)], [pl.BlockSpec((tm, d), lambda i, k: (i, 0))],
                         [pltpu.VMEM((tm, d), F32)], ("parallel", "arbitrary"), est, comm=comm)
    return dh, extra


def _bwd_prenorm(dh, x, dout, mod, pre_g, s, d):
    tm = _row_tile(s, 2 * ROW_TILE)

    def body(dh_ref, x_ref, dout_ref, mod_ref, pg_ref, gx_ref, dsh_ref, dsc_ref, dpg_ref):
        i = pl.program_id(0)

        @pl.when(i == 0)
        def _():
            dsh_ref[...] = jnp.zeros_like(dsh_ref)
            dsc_ref[...] = jnp.zeros_like(dsc_ref)
            dpg_ref[...] = jnp.zeros_like(dpg_ref)

        dh = dh_ref[...].astype(F32)
        xv = x_ref[...]
        r = lax.rsqrt(jnp.mean(xv * xv, axis=1, keepdims=True) + EPS)
        xhat = xv * r
        pg = pg_ref[...]
        one_scale = 1.0 + mod_ref[:, d:2 * d]
        dsh_ref[...] += jnp.sum(dh, axis=0, keepdims=True)
        dsc_ref[...] += jnp.sum(dh * xhat, axis=0, keepdims=True) * pg
        dpg_ref[...] += jnp.sum(dh * xhat, axis=0, keepdims=True) * one_scale
        dxh = dh * (one_scale * pg)
        dx = r * (dxh - xhat * jnp.mean(dxh * xhat, axis=1, keepdims=True))
        gx_ref[...] = dout_ref[...] + dx

    tile = pl.BlockSpec((tm, d), lambda i: (i, 0))
    vec = _const_spec((1, d))
    return pl.pallas_call(
        body, name="bwd_prenorm", grid=(s // tm,),
        out_shape=(jax.ShapeDtypeStruct((s, d), F32),) + (jax.ShapeDtypeStruct((1, d), F32),) * 3,
        in_specs=[tile, tile, tile, _const_spec((1, 3 * d)), vec], out_specs=(tile, vec, vec, vec),
        compiler_params=_cparams(("arbitrary",), 2 * 4 * tm * d * 4 + 6 * tm * d * 4),
    )(dh, x, dout, mod, pre_g)


def _pad_lanes(v, width):
    return jnp.pad(v, ((0, 0), (0, width - v.shape[1])))


def kernel(x, c, rel_bias_table, w_ada, b_ada, pre_norm_g, post_norm_g, w_in, attn_sink, w_pool_group, pool_scale, w_branch_attn, w_branch_pool, w_merge, b_merge, w_out, loss_target, m_rel_bias_table, m_w_ada, m_b_ada, m_pre_norm_g, m_post_norm_g, m_w_in, m_attn_sink, m_w_pool_group, m_pool_scale, m_w_branch_attn, m_w_branch_pool, m_w_merge, m_b_merge, m_w_out, v_rel_bias_table, v_w_ada, v_b_ada, v_pre_norm_g, v_post_norm_g, v_w_in, v_attn_sink, v_w_pool_group, v_pool_scale, v_w_branch_attn, v_w_branch_pool, v_w_merge, v_b_merge, v_w_out):
    _, s, d = x.shape
    p_w = pool_scale.shape[-1]
    cg = p_w // N_POOL_GROUPS
    in_w = 2 * ATTN_WIDTH + 2 * KV_WIDTH + 2 * p_w
    x2, t2 = x[0], loss_target[0]
    chip = 2 * lax.axis_index("x") + lax.axis_index("y")

    specs = [_Sharded("col", (d, in_w)), _Sharded("col", (d, 2 * d)), _Sharded("col", (ATTN_WIDTH, d)),
             _Sharded("col", (p_w, d)), _Sharded("row", (d, d)), _Sharded("grp", (N_POOL_GROUPS, cg, cg))]
    shards32 = [w_in[0], w_merge[0], w_branch_attn[0], w_branch_pool[0], w_out[0],
                w_pool_group[0].reshape(N_POOL_GROUPS * cg // N_CHIPS, cg)]
    names = ["w_in", "w_merge", "w_branch_attn", "w_branch_pool", "w_out", "w_pool_group"]
    shards16 = [_cast_bf16("cast_" + nm, w) for nm, w in zip(names, shards32)]
    shards16[5] = shards16[5].reshape(N_POOL_GROUPS, cg // N_CHIPS, cg)

    sc_all = _all_gather8("gather_cond", c, 1, pre=_silu)
    m_all = _all_gather8("gather_mod", _ada_forward(sc_all, w_ada[0]), N_DEV)
    mod = _mod_finish(m_all, b_ada)

    (h, proj, gates), (wf_in,) = _local_columns(x2, mod, pre_norm_g, shards16[0], shards16[1], b_merge,
                                                comm=_GatherWeights(specs[:1], shards16[:1], middle=0.9))
    proj, (wf_merge,) = _other_columns("proj", h, wf_in, proj, comm=_GatherWeights(specs[1:2], shards16[1:2]))
    gates, (wf_bra, wf_brp, wf_out, wf_grp) = _other_columns("merge_gates", h, wf_merge, gates, bias=b_merge,
                                                             comm=_GatherWeights(specs[2:], shards16[2:]))
    buckets = _t5_buckets()
    bias2 = _bias_table(rel_bias_table, buckets).reshape(N_KV_HEADS, GQA_GROUP * BLOCK, SPAN)
    attn = _attn_forward(proj, bias2, attn_sink, s)
    ya, yp, za, zp, merged, pooled, mixed = _branches(proj, attn, gates, wf_bra, wf_brp, wf_grp, pool_scale, s, d)
    dout, d_o, loss_part, dgate, dpostg = _out_loss(merged, x2, t2, wf_out, post_norm_g, mod, s, d)

    pw_out, _ = _matmul_tn("grad_w_out", merged, d_o)
    (dza, dzp, dgl, dbm), (pc_out,) = _bwd_out(d_o, gates, za, zp, wf_out, s, d,
                                                comm=_ScatterGrads(specs[4:5], [pw_out]))
    pw_bra, _ = _matmul_tn("grad_w_branch_attn", ya, dza)
    pw_brp, _ = _matmul_tn("grad_w_branch_pool", yp, dzp)
    pw_merge, _ = _matmul_tn("grad_w_merge", h, dgl)
    (dattn, dga, dgp, dmix, dpooled, dps), (pc_bra, pc_brp) = _bwd_branches(
        dza, dzp, attn, proj, mixed, wf_bra, wf_brp, wf_grp, pool_scale, s, d,
        comm=_ScatterGrads(specs[2:4], [pw_bra, pw_brp]))
    pw_grp = _pool_weight_grad(pooled, dmix, s)
    du = _pool_backward(dpooled, s)
    near, far = (0, 1, 2, 3, 6), (4, 5, 7)
    (dq, dkv, dbias, dsink), (pc_merge,) = _attn_backward(
        proj, dattn, bias2, attn_sink, s, comm=_ScatterGrads(specs[1:2], [pw_merge], peers=near))
    dproj = jnp.concatenate([dq, dkv, dga, du, dgp], axis=1)
    pw_in, (pc_merge,) = _matmul_tn("grad_w_in", h, dproj,
                                    comm=_ScatterGrads(specs[1:2], [pw_merge], peers=far, into=[pc_merge]))
    dh, (pc_in, pc_grp) = _bwd_hidden(dproj, dgl, wf_in, wf_merge, s, d,
                                      comm=_ScatterGrads([specs[0], specs[5]], [pw_in, pw_grp]))
    gx, dshift, dscale, dpreg = _bwd_prenorm(dh, x2, dout, mod, pre_norm_g, s, d)

    pieces = [pc_in, pc_merge, pc_bra, pc_brp, pc_out]
    weights = [w_in, w_merge, w_branch_attn, w_branch_pool, w_out]
    moms = [m_w_in, m_w_merge, m_w_branch_attn, m_w_branch_pool, m_w_out]
    vars_ = [v_w_in, v_w_merge, v_w_branch_attn, v_w_branch_pool, v_w_out]
    big = {}
    for nm, pc, w, m, v in zip(names, pieces, weights, moms, vars_):
        shape2 = (-1, w.shape[-1])
        res4 = _reduce_adamw("update_" + nm, pc, w.reshape(shape2), m.reshape(shape2), v.reshape(shape2))
        big[nm] = tuple(a.reshape(w.shape) for a in res4)
    hq = cg // N_CHIPS // 2
    g_grp = _reduce16("reduce_w_pool_group", pc_grp.reshape(2, N_DEV, N_POOL_GROUPS * hq, cg))
    g_grp = g_grp.reshape(2, N_POOL_GROUPS, hq, cg).transpose(1, 0, 2, 3).reshape(N_POOL_GROUPS * 2 * hq, cg)
    res3 = _adamw("adamw_w_pool_group", w_pool_group.reshape(-1, cg), g_grp, m_w_pool_group.reshape(-1, cg),
                  v_w_pool_group.reshape(-1, cg))
    big["w_pool_group"] = tuple(a.reshape(w_pool_group.shape) for a in (g_grp,) + tuple(res3))

    dtable = _bias_table_grad(dbias.reshape(N_Q_HEADS, BLOCK, SPAN), buckets)[:, :N_Q_HEADS]
    segs = [("b_ada", jnp.concatenate([dshift, dscale, dgate], axis=1), 3 * d),
            ("pre_norm_g", dpreg, d), ("post_norm_g", dpostg, d), ("attn_sink", dsink, LANE),
            ("pool_scale", dps, p_w), ("b_merge", dbm, 2 * d), ("rel_bias_table", dtable.reshape(1, -1), 2 * LANE)]
    packed = jnp.concatenate([_pad_lanes(v, w) for _, v, w in segs], axis=1)
    rows = _all_gather8("gather_small", packed, 1)[:, 0, :]

    def pack(vals):
        return jnp.concatenate([_pad_lanes(v.reshape(1, -1), w) for v, (_, _, w) in zip(vals, segs)], axis=1)

    small_w = [b_ada, pre_norm_g, post_norm_g, attn_sink, pool_scale, b_merge, rel_bias_table]
    small_m = [m_b_ada, m_pre_norm_g, m_post_norm_g, m_attn_sink, m_pool_scale, m_b_merge, m_rel_bias_table]
    small_v = [v_b_ada, v_pre_norm_g, v_post_norm_g, v_attn_sink, v_pool_scale, v_b_merge, v_rel_bias_table]
    g_small, d_small, nm_small, nv_small = _small_update(rows, pack(small_w), pack(small_m), pack(small_v))
    small = {}
    off = 0
    for (nm, _, w), ref in zip(segs, small_w):
        cut = lambda a: a[:, off:off + ref.size].reshape(ref.shape)
        small[nm] = (cut(g_small), cut(d_small), cut(nm_small), cut(nv_small))
        off += w

    dmod_cols = lax.dynamic_slice_in_dim(rows[:, :3 * d], chip * (3 * d // N_CHIPS), 3 * d // N_CHIPS, axis=1)
    sc_t = sc_all[:, 0, :].T
    g_ada, d_ada, nm_ada, nv_ada = _ada_backward(sc_t, dmod_cols, w_ada[0], m_w_ada[0], v_w_ada[0])
    big["w_ada"] = tuple(a.reshape(w_ada.shape) for a in (g_ada, d_ada, nm_ada, nv_ada))

    loss = lax.psum(loss_part[0, 0], ("x", "y", "c"))
    order = ["rel_bias_table", "w_ada", "b_ada", "pre_norm_g", "post_norm_g", "w_in", "attn_sink", "w_pool_group",
             "pool_scale", "w_branch_attn", "w_branch_pool", "w_merge", "b_merge", "w_out"]
    res = {**big, **small}
    outs = [loss, gx.reshape(x.shape)]
    for part in range(4):
        outs += [res[nm][part] for nm in order]
    return tuple(outs)


def _small_update(rows, w, m, v):
    _, n = rows.shape

    def body(r_ref, w_ref, m_ref, v_ref, g_ref, d_ref, nm_ref, nv_ref):
        g = r_ref[0:1, :]
        for k in range(1, N_DEV):
            g = g + r_ref[k:k + 1, :]
        dl, nm, nv = _adam_math(w_ref[...], g, m_ref[...], v_ref[...])
        g_ref[...] = g
        d_ref[...] = dl
        nm_ref[...] = nm
        nv_ref[...] = nv

    vm = pl.BlockSpec(memory_space=pltpu.VMEM)
    out = jax.ShapeDtypeStruct((1, n), F32)
    return pl.pallas_call(
        body, name="small_update", out_shape=(out,) * 4, in_specs=[vm] * 4, out_specs=(vm,) * 4,
    )(rows, w, m, v)
```

```python
import functools
import math

import jax
import jax.numpy as jnp
from jax import lax
from jax.experimental import pallas as pl
from jax.experimental.pallas import tpu as pltpu

F32 = jnp.float32
BF16 = jnp.bfloat16
MESH = pl.DeviceIdType.MESH

HEAD_DIM = 128
N_Q_HEADS = 8
N_KV_HEADS = 2
GQA_GROUP = N_Q_HEADS // N_KV_HEADS
ATTN_WIDTH = N_Q_HEADS * HEAD_DIM
KV_WIDTH = N_KV_HEADS * HEAD_DIM
WINDOW = 128
BLOCK = 128
SPAN = BLOCK + 2 * WINDOW
N_BUCKETS = 32
MAX_DISTANCE = 128
POOL_SIZES = (2, 4, 8, 16)
N_POOL_GROUPS = len(POOL_SIZES)
HALO = 16
EPS = 1e-6
NEG_INF = -1e30
ADAM_LR = 0.001
ADAM_B1 = 0.9
ADAM_B2 = 0.999
ADAM_EPS = 1e-08
ADAM_WD = 0.01
ADAM_STEP = 10

N_DEV = 8
N_CHIPS = 4
LANE = 128
COL_TILE = 512
VMEM_CAP = 60000 * 1024
ROW_TILE = 256


def _cparams(sem, est_bytes):
    limit = int(min(max(est_bytes * 5 // 4 + (4 << 20), 16 << 20), VMEM_CAP))
    return pltpu.CompilerParams(dimension_semantics=sem, vmem_limit_bytes=limit)


def _sigmoid(x):
    return jax.nn.sigmoid(x)


def _silu(x):
    return x * _sigmoid(x)


def _dsilu(x):
    s = _sigmoid(x)
    return s * (1.0 + x * (1.0 - s))


def _place():
    x, y, c = lax.axis_index("x"), lax.axis_index("y"), lax.axis_index("c")
    return x, y, c


def _flip(v, bit):
    return (1 - v) if bit else v


def _xor_peer(k):
    x, y, c = _place()
    return (_flip(x, (k >> 2) & 1), _flip(y, (k >> 1) & 1), _flip(c, k & 1))


def _resident(shape):
    nd = len(shape)
    return pl.BlockSpec(shape, lambda *_: (0,) * nd, pipeline_mode=pl.Buffered(1))


def _const_spec(shape):
    nd = len(shape)
    return pl.BlockSpec(shape, lambda *_: (0,) * nd)


def _all_gather8(name, x, nrows, pre=None):
    r, n = x.shape

    def body(x_ref, out_ref, stage, send_sems, recv_sems):
        px, py, pc = _place()
        me = 4 * px + 2 * py + pc
        v = x_ref[...]
        if pre is not None:
            v = pre(v)
        stage[...] = v[0:nrows]
        out_ref[me] = v[0:nrows]
        copies = []
        for k in range(1, N_DEV):
            cp = pltpu.make_async_remote_copy(
                src_ref=stage, dst_ref=out_ref.at[me], send_sem=send_sems.at[k - 1], recv_sem=recv_sems.at[k - 1],
                device_id=_xor_peer(k), device_id_type=MESH)
            cp.start()
            copies.append(cp)
        for cp in copies:
            cp.wait()

    return pl.pallas_call(
        body, name=name,
        out_shape=jax.ShapeDtypeStruct((N_DEV, nrows, n), F32),
        in_specs=[pl.BlockSpec(memory_space=pltpu.VMEM)],
        out_specs=pl.BlockSpec(memory_space=pltpu.VMEM),
        scratch_shapes=[pltpu.VMEM((nrows, n), F32), pltpu.SemaphoreType.DMA((N_DEV - 1,)),
                        pltpu.SemaphoreType.DMA((N_DEV - 1,))],
    )(x)


class _Sharded:
    def __init__(self, kind, full_shape):
        self.kind = kind
        self.full_shape = tuple(full_shape)
        if kind == "col":
            r, c = full_shape
            self.shard_shape = (r, c // N_CHIPS)
        elif kind == "row":
            r, c = full_shape
            self.shard_shape = (r // N_CHIPS, c)
        else:
            g, r, c = full_shape
            self.shard_shape = (g, r // N_CHIPS, c)
        self.axis = 1 if kind == "grp" else 0
        s = list(self.shard_shape)
        s[self.axis] //= 2
        self.piece_shape = tuple(s)

    def _rows(self, ref, start, size):
        idx = (slice(None),) * self.axis + (pl.ds(pl.multiple_of(start, 16), size),)
        return ref.at[idx]

    def shard_half(self, ref, hc):
        h = self.piece_shape[self.axis]
        return self._rows(ref, hc * h, h)

    def window(self, ref, chip, hc=None):
        s = self.shard_shape
        if self.kind == "col":
            cols = pl.ds(pl.multiple_of(chip * s[1], LANE), s[1])
            if hc is None:
                return ref.at[:, cols]
            h = s[0] // 2
            return ref.at[pl.ds(pl.multiple_of(hc * h, 16), h), cols]
        n = s[self.axis]
        if hc is None:
            return self._rows(ref, chip * n, n)
        return self._rows(ref, chip * n + hc * (n // 2), n // 2)


def _remote(src, dst, send_sem, recv_sem, to):
    return pltpu.make_async_remote_copy(src_ref=src, dst_ref=dst, send_sem=send_sem, recv_sem=recv_sem,
                                        device_id=to, device_id_type=MESH)


class _GatherWeights:
    def __init__(self, specs, shards, middle=0.7):
        self.specs = specs
        self.middle_at = middle
        self.inputs = list(shards)
        self.out_shapes = [jax.ShapeDtypeStruct(sp.full_shape, BF16) for sp in specs]
        nw = len(specs)
        self.scratch = [pltpu.SemaphoreType.DMA((6 * nw,)), pltpu.SemaphoreType.DMA((6 * nw,)),
                        pltpu.SemaphoreType.DMA((nw,))]

    aliases = staticmethod(lambda n_in, n_out: {})

    def phases(self, nsteps):
        return [(0, self.start), (min(nsteps - 1, max(1, int(self.middle_at * nsteps))), self.middle),
                (nsteps - 1, self.end)]

    def _ctx(self):
        x, y, c = _place()
        return x, y, c, 2 * x + y, (x, y, 1 - c), [(1 - x, y), (x, 1 - y), (1 - x, 1 - y)]

    def start(self, shard_refs, full_refs, sems):
        send_sems, recv_sems, local_sems = sems
        x, y, c, my_chip, sibling, chips = self._ctx()
        for w, sp in enumerate(self.specs):
            pltpu.make_async_copy(shard_refs[w], sp.window(full_refs[w], my_chip), local_sems.at[w]).start()
            for t, (cx, cy) in enumerate(chips):
                _remote(sp.shard_half(shard_refs[w], c), sp.window(full_refs[w], my_chip, c),
                        send_sems.at[6 * w + t], recv_sems.at[6 * w + t], (cx, cy, c)).start()

    def middle(self, shard_refs, full_refs, sems):
        send_sems, recv_sems, local_sems = sems
        x, y, c, my_chip, sibling, chips = self._ctx()
        for w, sp in enumerate(self.specs):
            for t, (cx, cy) in enumerate(chips):
                landed = sp.window(full_refs[w], 2 * cx + cy, c)
                _remote(landed, landed, send_sems.at[6 * w + t], recv_sems.at[6 * w + t], (cx, cy, c)).wait_recv()
                _remote(landed, landed, send_sems.at[6 * w + 3 + t], recv_sems.at[6 * w + 3 + t], sibling).start()

    def end(self, shard_refs, full_refs, sems):
        send_sems, recv_sems, local_sems = sems
        x, y, c, my_chip, sibling, chips = self._ctx()
        for w, sp in enumerate(self.specs):
            for t, (cx, cy) in enumerate(chips):
                other = sp.window(full_refs[w], 2 * cx + cy, 1 - c)
                _remote(other, other, send_sems.at[6 * w + 3 + t], recv_sems.at[6 * w + 3 + t], sibling).wait_recv()
        for w, sp in enumerate(self.specs):
            for t, (cx, cy) in enumerate(chips):
                mine = sp.shard_half(shard_refs[w], c)
                _remote(mine, mine, send_sems.at[6 * w + t], recv_sems.at[6 * w + t], (cx, cy, c)).wait_send()
                landed = sp.window(full_refs[w], 2 * cx + cy, c)
                _remote(landed, landed, send_sems.at[6 * w + 3 + t], recv_sems.at[6 * w + 3 + t], sibling).wait_send()
            pltpu.make_async_copy(shard_refs[w], sp.window(full_refs[w], my_chip), local_sems.at[w]).wait()


class _ScatterGrads:
    def __init__(self, specs, partials, peers=tuple(range(N_DEV)), into=None, middle=0.88):
        self.specs = specs
        self.peers = tuple(peers)
        self.middle_at = middle
        self.n_part = len(partials)
        self.into = into is not None
        self.inputs = list(partials) + (list(into) if self.into else [])
        self.out_shapes = [jax.ShapeDtypeStruct((2, N_DEV) + sp.piece_shape, BF16) for sp in specs]
        nw = len(specs)
        self.scratch = [pltpu.SemaphoreType.DMA((15 * nw,)), pltpu.SemaphoreType.DMA((15 * nw,)),
                        pltpu.SemaphoreType.DMA((nw,))]

    def aliases(self, n_in, n_out):
        return {n_in + self.n_part + w: n_out + w for w in range(len(self.specs))} if self.into else {}

    def phases(self, nsteps):
        return [(0, self.start), (min(nsteps - 1, max(1, int(self.middle_at * nsteps))), self.middle),
                (nsteps - 1, self.end)]

    def _own(self, sp, part_ref, recv_ref, x, y, c, sem):
        return pltpu.make_async_copy(sp.window(part_ref, 2 * x + y, c), recv_ref.at[c, 0], sem)

    def start(self, part_refs, recv_refs, sems):
        send_sems, recv_sems, local_sems = sems
        x, y, c = _place()
        for w, sp in enumerate(self.specs):
            for k in self.peers:
                if k == 0:
                    self._own(sp, part_refs[w], recv_refs[w], x, y, c, local_sems.at[w]).start()
                    continue
                px, py, pc = _xor_peer(k)
                _remote(sp.window(part_refs[w], 2 * px + py, pc), recv_refs[w].at[pc, k],
                        send_sems.at[15 * w + k - 1], recv_sems.at[15 * w + k - 1], (px, py, pc)).start()

    def middle(self, part_refs, recv_refs, sems):
        send_sems, recv_sems, local_sems = sems
        x, y, c = _place()
        sibling = (x, y, 1 - c)
        for w, sp in enumerate(self.specs):
            for k in self.peers:
                landed = recv_refs[w].at[c, k]
                if k:
                    _remote(landed, landed, send_sems.at[15 * w + k - 1], recv_sems.at[15 * w + k - 1],
                            sibling).wait_recv()
                else:
                    self._own(sp, part_refs[w], recv_refs[w], x, y, c, local_sems.at[w]).wait()
                _remote(landed, landed, send_sems.at[15 * w + 7 + k], recv_sems.at[15 * w + 7 + k], sibling).start()

    def end(self, part_refs, recv_refs, sems):
        send_sems, recv_sems, local_sems = sems
        x, y, c = _place()
        sibling = (x, y, 1 - c)
        for w, sp in enumerate(self.specs):
            for k in self.peers:
                other = recv_refs[w].at[1 - c, k]
                _remote(other, other, send_sems.at[15 * w + 7 + k], recv_sems.at[15 * w + 7 + k], sibling).wait_recv()
            for k in self.peers:
                landed = recv_refs[w].at[c, k]
                _remote(landed, landed, send_sems.at[15 * w + 7 + k], recv_sems.at[15 * w + 7 + k],
                        sibling).wait_send()
                if k:
                    px, py, pc = _xor_peer(k)
                    sent = sp.window(part_refs[w], 2 * px + py, pc)
                    _remote(sent, sent, send_sems.at[15 * w + k - 1], recv_sems.at[15 * w + k - 1],
                            (px, py, pc)).wait_send()


def _call(name, body, grid, in_specs, args, out_shape, out_specs, scratch, semantics, est_bytes, comm=None,
          aliases=None):
    out_shape, out_specs = tuple(out_shape), tuple(out_specs)
    if comm is None:
        res = pl.pallas_call(body, name=name, grid=grid, out_shape=out_shape, in_specs=list(in_specs),
                             out_specs=out_specs, scratch_shapes=list(scratch), input_output_aliases=aliases or {},
                             compiler_params=_cparams(semantics, est_bytes))(*args)
        return tuple(res), ()
    n_in, n_out, n_sc = len(in_specs), len(out_shape), len(scratch)
    c_in, c_out = len(comm.inputs), len(comm.out_shapes)
    nsteps = math.prod(grid)
    phases = comm.phases(nsteps)

    def hosted(*refs):
        pos = [0]

        def take(n):
            part = refs[pos[0]:pos[0] + n]
            pos[0] += n
            return part

        ins, cins, outs, couts, scr, sems = take(n_in), take(c_in), take(n_out), take(c_out), take(n_sc), take(3)
        step = 0
        for ax, extent in enumerate(grid):
            step = step * extent + pl.program_id(ax)
        for at, fn in phases:
            if at == 0:
                pl.when(step == 0)(functools.partial(fn, cins, couts, sems))
        body(*ins, *outs, *scr)
        for at, fn in phases:
            if at > 0:
                pl.when(step == at)(functools.partial(fn, cins, couts, sems))

    any_spec = pl.BlockSpec(memory_space=pl.ANY)
    res = pl.pallas_call(
        hosted, name=name, grid=grid, out_shape=out_shape + tuple(comm.out_shapes),
        in_specs=list(in_specs) + [any_spec] * c_in, out_specs=out_specs + (any_spec,) * c_out,
        scratch_shapes=list(scratch) + list(comm.scratch),
        input_output_aliases={**(aliases or {}), **comm.aliases(n_in, n_out)},
        compiler_params=_cparams(("arbitrary",) * len(grid), est_bytes))(*args, *comm.inputs)
    return tuple(res[:n_out]), tuple(res[n_out:])


def _row_tile(rows, cap):
    for t in range(min(rows, cap), 0, -1):
        if rows % t == 0 and (t % 16 == 0 or t == rows):
            return t
    return rows


def _cast_bf16(name, x):
    r, c = x.shape
    tr = _row_tile(r, 512)

    def body(x_ref, o_ref):
        o_ref[...] = x_ref[...].astype(BF16)

    return pl.pallas_call(
        body, name=name, grid=(r // tr,), out_shape=jax.ShapeDtypeStruct((r, c), BF16),
        in_specs=[pl.BlockSpec((tr, c), lambda i: (i, 0))], out_specs=pl.BlockSpec((tr, c), lambda i: (i, 0)),
        compiler_params=_cparams(("parallel",), 2 * tr * c * 6),
    )(x)


def _adam_math(w, g, m, v):
    m = ADAM_B1 * m + (1.0 - ADAM_B1) * g
    v = ADAM_B2 * v + (1.0 - ADAM_B2) * (g * g)
    m_hat = m / (1.0 - ADAM_B1 ** ADAM_STEP)
    v_hat = v / (1.0 - ADAM_B2 ** ADAM_STEP)
    delta = -ADAM_LR * (m_hat / (jnp.sqrt(v_hat) + ADAM_EPS) + ADAM_WD * w)
    return delta, m, v


def _adamw(name, w, g, m, v):
    r, c = w.shape
    tr = _row_tile(r, max(8, (1 << 18) // c))

    def body(w_ref, g_ref, m_ref, v_ref, d_ref, nm_ref, nv_ref):
        d, nm, nv = _adam_math(w_ref[...], g_ref[...], m_ref[...], v_ref[...])
        d_ref[...] = d
        nm_ref[...] = nm
        nv_ref[...] = nv

    spec = pl.BlockSpec((tr, c), lambda i: (i, 0))
    out = jax.ShapeDtypeStruct((r, c), F32)
    return pl.pallas_call(
        body, name=name, grid=(r // tr,), out_shape=(out, out, out), in_specs=[spec] * 4, out_specs=(spec,) * 3,
        compiler_params=_cparams(("parallel",), 2 * 7 * tr * c * 4),
    )(w, g, m, v)


def _sum_pieces(x_ref):
    acc = x_ref[0, 0].astype(F32)
    for k in range(1, N_DEV):
        acc = acc + x_ref[0, k].astype(F32)
    return acc


def _reduce16(name, x):
    _, _, r, c = x.shape
    tr = _row_tile(r, max(16, (1 << 18) // c))
    nt = r // tr

    def body(x_ref, o_ref):
        o_ref[...] = _sum_pieces(x_ref)

    return pl.pallas_call(
        body, name=name, grid=(2, nt), out_shape=jax.ShapeDtypeStruct((2 * r, c), F32),
        in_specs=[pl.BlockSpec((1, N_DEV, tr, c), lambda hf, i: (hf, 0, i, 0))],
        out_specs=pl.BlockSpec((tr, c), lambda hf, i: (hf * nt + i, 0)),
        compiler_params=_cparams(("parallel", "parallel"), 2 * (N_DEV * 2 + 4) * tr * c),
    )(x)


def _reduce_adamw(name, x, w, m, v):
    _, _, r, c = x.shape
    tr = _row_tile(r, max(16, (1 << 18) // c))
    nt = r // tr

    def body(x_ref, w_ref, m_ref, v_ref, g_ref, d_ref, nm_ref, nv_ref):
        g = _sum_pieces(x_ref)
        d, nm, nv = _adam_math(w_ref[...], g, m_ref[...], v_ref[...])
        g_ref[...] = g
        d_ref[...] = d
        nm_ref[...] = nm
        nv_ref[...] = nv

    tile = pl.BlockSpec((tr, c), lambda hf, i: (hf * nt + i, 0))
    out = jax.ShapeDtypeStruct((2 * r, c), F32)
    return pl.pallas_call(
        body, name=name, grid=(2, nt), out_shape=(out,) * 4,
        in_specs=[pl.BlockSpec((1, N_DEV, tr, c), lambda hf, i: (hf, 0, i, 0)), tile, tile, tile],
        out_specs=(tile,) * 4,
        compiler_params=_cparams(("parallel", "parallel"), 2 * (N_DEV * 2 + 7 * 4) * tr * c),
    )(x, w, m, v)


def _t5_buckets():
    rel = jnp.arange(SPAN)[None, :] - WINDOW - jnp.arange(BLOCK)[:, None]
    half = N_BUCKETS // 2
    max_exact = half // 2
    ret = jnp.where(rel > 0, half, 0)
    n = jnp.abs(rel)
    nf = jnp.maximum(n, 1).astype(F32)
    large = max_exact + (jnp.log(nf / max_exact) / math.log(MAX_DISTANCE / max_exact)
                         * (half - max_exact)).astype(jnp.int32)
    large = jnp.minimum(large, half - 1)
    return (ret + jnp.where(n < max_exact, n, large)).astype(jnp.int32)


def _bias_table(table, buckets):
    def body(t_ref, b_ref, o_ref):
        bk = b_ref[...]
        rel = (lax.broadcasted_iota(jnp.int32, (BLOCK, SPAN), 1) - WINDOW
               - lax.broadcasted_iota(jnp.int32, (BLOCK, SPAN), 0))
        band = jnp.abs(rel) <= WINDOW
        for h in range(N_Q_HEADS):
            acc = jnp.zeros((BLOCK, SPAN), F32)
            for b in range(N_BUCKETS):
                acc = jnp.where(bk == b, t_ref[b, h], acc)
            o_ref[h] = jnp.where(band, acc, NEG_INF)

    return pl.pallas_call(
        body, name="bias_table", out_shape=jax.ShapeDtypeStruct((N_Q_HEADS, BLOCK, SPAN), F32),
        in_specs=[pl.BlockSpec(memory_space=pltpu.SMEM), pl.BlockSpec(memory_space=pltpu.VMEM)],
        out_specs=pl.BlockSpec(memory_space=pltpu.VMEM),
    )(table, buckets)


def _bias_table_grad(dbias, buckets):
    def body(d_ref, b_ref, o_ref):
        bk = b_ref[...]
        row = lax.broadcasted_iota(jnp.int32, (N_BUCKETS, LANE), 0)
        lane = lax.broadcasted_iota(jnp.int32, (N_BUCKETS, LANE), 1)
        acc = jnp.zeros((N_BUCKETS, LANE), F32)
        for h in range(N_Q_HEADS):
            d = d_ref[h]
            for b in range(N_BUCKETS):
                s = jnp.sum(jnp.where(bk == b, d, 0.0))
                acc = jnp.where((row == b) & (lane == h), s, acc)
        o_ref[...] = acc

    return pl.pallas_call(
        body, name="bias_table_grad", out_shape=jax.ShapeDtypeStruct((N_BUCKETS, LANE), F32),
        in_specs=[pl.BlockSpec(memory_space=pltpu.VMEM), pl.BlockSpec(memory_space=pltpu.VMEM)],
        out_specs=pl.BlockSpec(memory_space=pltpu.VMEM),
    )(dbias, buckets)


def _ada_forward(sc_all, w_ada):
    d, n = w_ada.shape
    tn = _pick_tile(n, COL_TILE)

    def body(sc_ref, w_ref, o_ref):
        row = lax.broadcasted_iota(jnp.int32, (N_DEV, d), 0)
        sc = jnp.zeros((N_DEV, d), F32)
        for k in range(N_DEV):
            sc = jnp.where(row == k, sc_ref[k], sc)
        o_ref[...] = jnp.dot(sc, w_ref[...], preferred_element_type=F32, precision=lax.Precision.HIGHEST)

    return pl.pallas_call(
        body, name="ada_forward", grid=(n // tn,), out_shape=jax.ShapeDtypeStruct((N_DEV, n), F32),
        in_specs=[_const_spec((N_DEV, 1, d)), pl.BlockSpec((d, tn), lambda j: (0, j))],
        out_specs=pl.BlockSpec((N_DEV, tn), lambda j: (0, j)),
        compiler_params=_cparams(("parallel",), 2 * d * tn * 4 + N_DEV * N_DEV * d * 8),
    )(sc_all, w_ada)


def _mod_finish(m_all, b_ada):
    _, _, n = m_all.shape

    def body(m_ref, b_ref, o_ref):
        x, y, c = _place()
        me = 4 * x + 2 * y + c
        row = lax.broadcasted_iota(jnp.int32, (N_DEV, n), 0)
        for j in range(N_CHIPS):
            blk = m_ref[2 * j]
            mine = jnp.sum(jnp.where(row == me, blk, 0.0), axis=0, keepdims=True)
            o_ref[:, j * n:(j + 1) * n] = mine + b_ref[:, j * n:(j + 1) * n]

    return pl.pallas_call(
        body, name="mod_finish", out_shape=jax.ShapeDtypeStruct((1, N_CHIPS * n), F32),
        in_specs=[pl.BlockSpec(memory_space=pltpu.VMEM), pl.BlockSpec(memory_space=pltpu.VMEM)],
        out_specs=pl.BlockSpec(memory_space=pltpu.VMEM),
    )(m_all, b_ada)


def _ada_backward(sc_t, dmod_cols, w, m, v):
    d, n = w.shape
    tr, tn = _row_tile(d, 512), _pick_tile(n, COL_TILE)

    def body(s_ref, dm_ref, w_ref, m_ref, v_ref, g_ref, d_ref, nm_ref, nv_ref):
        g = jnp.dot(s_ref[...], dm_ref[...], preferred_element_type=F32, precision=lax.Precision.HIGHEST)
        dl, nm, nv = _adam_math(w_ref[...], g, m_ref[...], v_ref[...])
        g_ref[...] = g
        d_ref[...] = dl
        nm_ref[...] = nm
        nv_ref[...] = nv

    tile = pl.BlockSpec((tr, tn), lambda i, j: (i, j))
    out = jax.ShapeDtypeStruct((d, n), F32)
    return pl.pallas_call(
        body, name="ada_backward", grid=(d // tr, n // tn), out_shape=(out,) * 4,
        in_specs=[pl.BlockSpec((tr, N_DEV), lambda i, j: (i, 0)), pl.BlockSpec((N_DEV, tn), lambda i, j: (0, j)),
                  tile, tile, tile],
        out_specs=(tile,) * 4,
        compiler_params=_cparams(("parallel", "parallel"), 2 * 8 * tr * tn * 4),
    )(sc_t, dmod_cols, w, m, v)


def _chip():
    return 2 * lax.axis_index("x") + lax.axis_index("y")


def _local_columns(x, mod, pre_g, w_in_shard, w_merge_shard, b_merge, comm):
    s, d = x.shape
    n1, n2 = w_in_shard.shape[1], w_merge_shard.shape[1]
    tm = _row_tile(s, 512)

    def body(x_ref, mod_ref, pg_ref, wi_ref, wm_ref, b_ref, h_ref, p_ref, g_ref):
        xv = x_ref[...]
        r = lax.rsqrt(jnp.mean(xv * xv, axis=1, keepdims=True) + EPS)
        xn = xv * r * pg_ref[...]
        hv = (xn * (1.0 + mod_ref[:, d:2 * d]) + mod_ref[:, 0:d]).astype(BF16)
        h_ref[...] = hv
        p_ref[...] = jnp.dot(hv, wi_ref[...], preferred_element_type=F32).astype(BF16)
        g_ref[...] = _sigmoid(jnp.dot(hv, wm_ref[...], preferred_element_type=F32) + b_ref[...]).astype(BF16)

    in_specs = [pl.BlockSpec((tm, d), lambda i: (i, 0)), _const_spec((1, 3 * d)), _const_spec((1, d)),
                _resident((d, n1)), _resident((d, n2)), pl.BlockSpec((1, n2), lambda i: (0, _chip()))]
    out_specs = [pl.BlockSpec((tm, d), lambda i: (i, 0)), pl.BlockSpec((tm, n1), lambda i: (i, _chip())),
                 pl.BlockSpec((tm, n2), lambda i: (i, _chip()))]
    est = 2 * tm * (4 * d + 2 * d + 2 * n1 + 2 * n2) + d * (n1 + n2) * 2 + 4 * tm * d * 4 + 2 * tm * (n1 + n2) * 4
    return _call("local_columns", body, (s // tm,), in_specs, (x, mod, pre_g, w_in_shard, w_merge_shard, b_merge),
                 [jax.ShapeDtypeStruct((s, d), BF16), jax.ShapeDtypeStruct((s, N_CHIPS * n1), BF16),
                  jax.ShapeDtypeStruct((s, N_CHIPS * n2), BF16)],
                 out_specs, [], ("arbitrary",), est, comm=comm)


def _other_columns(name, a, b, partial, bias=None, comm=None):
    s, k = a.shape
    _, n = b.shape
    tm, tn = _row_tile(s, 2048), n // N_CHIPS
    col = lambda i, j: (_chip() + 1 + j) % N_CHIPS

    def body(*refs):
        if bias is None:
            a_ref, b_ref, _, o_ref = refs
        else:
            a_ref, b_ref, bias_ref, _, o_ref = refs
        acc = jnp.dot(a_ref[...], b_ref[...], preferred_element_type=F32)
        if bias is not None:
            acc = _sigmoid(acc + bias_ref[...])
        o_ref[...] = acc.astype(BF16)

    in_specs = [pl.BlockSpec((tm, k), lambda i, j: (i, 0)), pl.BlockSpec((k, tn), lambda i, j: (0, col(i, j)))]
    args = [a, b]
    if bias is not None:
        in_specs.append(pl.BlockSpec((1, tn), lambda i, j: (0, col(i, j))))
        args.append(bias)
    in_specs.append(pl.BlockSpec(memory_space=pl.ANY))
    args.append(partial)
    (out,), extra = _call(name, body, (s // tm, N_CHIPS - 1), in_specs, args, [jax.ShapeDtypeStruct((s, n), BF16)],
                          [pl.BlockSpec((tm, tn), lambda i, j: (i, col(i, j)))], [], ("parallel", "arbitrary"),
                          2 * (tm * k + k * tn + tm * tn) * 2 + 2 * tm * tn * 4, comm=comm,
                          aliases={len(args) - 1: 0})
    return out, extra


def _col_specs(off, width, rows, row_index):
    assert off % COL_TILE == 0 and width % COL_TILE == 0
    return [pl.BlockSpec((rows, COL_TILE), functools.partial(lambda p, *ids: (row_index(*ids), p), off // COL_TILE + p))
            for p in range(width // COL_TILE)]


def _cat(refs):
    vals = [r[...] for r in refs]
    return vals[0] if len(vals) == 1 else jnp.concatenate(vals, axis=1)


def _attn_mask(n, s):
    kpos = (n - 1) * BLOCK + lax.broadcasted_iota(jnp.int32, (1, SPAN), 1)
    return (kpos >= 0) & (kpos < s)


def _sink_column(sink_ref, kh):
    rows = GQA_GROUP * BLOCK
    grp = lax.broadcasted_iota(jnp.int32, (rows, 1), 0) // BLOCK
    col = jnp.zeros((rows, 1), F32)
    for g in range(GQA_GROUP):
        col = jnp.where(grp == g, sink_ref[0, kh * GQA_GROUP + g], col)
    return col


def _stack_heads(x, kh):
    base = kh * GQA_GROUP * HEAD_DIM
    return jnp.concatenate([x[:, base + g * HEAD_DIM: base + (g + 1) * HEAD_DIM] for g in range(GQA_GROUP)], axis=0)


def _softmax_parts(qs, k, bias, valid, sink_col):
    sc = lax.dot_general(qs, k, (((1,), (1,)), ((), ())), preferred_element_type=F32)
    sc = sc * (HEAD_DIM ** -0.5) + bias
    sc = jnp.where(valid, sc, NEG_INF)
    mx = jnp.maximum(jnp.max(sc, axis=1, keepdims=True), sink_col)
    e = jnp.exp(sc - mx)
    es = jnp.exp(sink_col - mx)
    inv = 1.0 / (jnp.sum(e, axis=1, keepdims=True) + es)
    return e * inv, es * inv


def _attn_forward(proj, bias2, sink, s):
    nblk = s // BLOCK
    nq = ATTN_WIDTH // COL_TILE
    kv_col = ATTN_WIDTH // COL_TILE
    assert 2 * KV_WIDTH == COL_TILE

    def body(*refs):
        q_refs = refs[:nq]
        kvp, kvc, kvn, bias_ref, sink_ref, o_ref = refs[nq:]
        n = pl.program_id(0)
        q = _cat(q_refs)
        kv = jnp.concatenate([kvp[...], kvc[...], kvn[...]], axis=0)
        valid = _attn_mask(n, s)
        for kh in range(N_KV_HEADS):
            qs = _stack_heads(q, kh)
            k = kv[:, kh * HEAD_DIM:(kh + 1) * HEAD_DIM]
            v = kv[:, KV_WIDTH + kh * HEAD_DIM: KV_WIDTH + (kh + 1) * HEAD_DIM]
            p, _ = _softmax_parts(qs, k, bias_ref[kh], valid, _sink_column(sink_ref, kh))
            o = jnp.dot(p.astype(BF16), v, preferred_element_type=F32)
            for g in range(GQA_GROUP):
                h = kh * GQA_GROUP + g
                o_ref[:, h * HEAD_DIM:(h + 1) * HEAD_DIM] = o[g * BLOCK:(g + 1) * BLOCK].astype(BF16)

    in_specs = _col_specs(0, ATTN_WIDTH, BLOCK, lambda n: n)
    in_specs += [pl.BlockSpec((BLOCK, COL_TILE), lambda n: (jnp.maximum(n - 1, 0), kv_col)),
                 pl.BlockSpec((BLOCK, COL_TILE), lambda n: (n, kv_col)),
                 pl.BlockSpec((BLOCK, COL_TILE), lambda n: (jnp.minimum(n + 1, nblk - 1), kv_col)),
                 _const_spec((N_KV_HEADS, GQA_GROUP * BLOCK, SPAN)),
                 pl.BlockSpec(memory_space=pltpu.SMEM)]
    return pl.pallas_call(
        body, name="attn_forward", grid=(nblk,), out_shape=jax.ShapeDtypeStruct((s, ATTN_WIDTH), BF16),
        in_specs=in_specs, out_specs=pl.BlockSpec((BLOCK, ATTN_WIDTH), lambda n: (n, 0)),
        compiler_params=_cparams(("parallel",), 16 << 20),
    )(*([proj] * (nq + 3)), bias2, sink)


def _pool_positions(i, tm, s, width):
    pos = i * tm - HALO + lax.broadcasted_iota(jnp.int32, (tm + 2 * HALO, width), 0)
    return pos, (pos >= 0) & (pos < s)


def _pool_count(pos, w, s):
    return (jnp.minimum(pos + w // 2, s) - jnp.maximum(pos - w // 2, 0)).astype(F32)


def _halo_specs_cols(off, width, tm, s):
    per = tm // HALO
    last = s // HALO - 1
    prev = _col_specs(off, width, HALO, lambda i: jnp.maximum(i * per - 1, 0))
    nxt = _col_specs(off, width, HALO, lambda i: jnp.minimum((i + 1) * per, last))
    return prev, nxt


def _branches(proj, attn, g, w_bra, w_brp, w_grp, pool_scale, s, d):
    a_w, p_w = ATTN_WIDTH, pool_scale.shape[1]
    cg = p_w // N_POOL_GROUPS
    tm = _row_tile(s, 2 * ROW_TILE)
    off_ga = ATTN_WIDTH + 2 * KV_WIDTH
    off_u = off_ga + a_w
    off_gp = off_u + p_w
    n_ga, n_u, n_gp = a_w // COL_TILE, p_w // COL_TILE, p_w // COL_TILE

    def body(*refs):
        it = iter(refs)
        attn_ref = next(it)
        ga_refs = [next(it) for _ in range(n_ga)]
        u_refs = [next(it) for _ in range(n_u)]
        up_refs = [next(it) for _ in range(n_u)]
        un_refs = [next(it) for _ in range(n_u)]
        gp_refs = [next(it) for _ in range(n_gp)]
        g_ref, wa_ref, wp_ref, wg_ref, ps_ref = (next(it) for _ in range(5))
        ya_ref, yp_ref, za_ref, zp_ref, mg_ref, pooled_ref, mixed_ref = (next(it) for _ in range(7))
        i = pl.program_id(0)
        ya = (attn_ref[...].astype(F32) * _silu(_cat(ga_refs).astype(F32))).astype(BF16)
        ya_ref[...] = ya
        za = jnp.dot(ya, wa_ref[...], preferred_element_type=F32)
        za_ref[...] = za.astype(BF16)

        u = _cat(u_refs).astype(F32)
        ext = jnp.concatenate([_cat(up_refs).astype(F32), u, _cat(un_refs).astype(F32)], axis=0)
        pos, ok = _pool_positions(i, tm, s, cg)
        mixed = []
        for gi, w in enumerate(POOL_SIZES):
            e = jnp.where(ok, ext[:, gi * cg:(gi + 1) * cg], 0.0)
            acc = e[HALO - w // 2: HALO - w // 2 + tm]
            for dd in range(-w // 2 + 1, w // 2):
                acc = acc + e[HALO + dd: HALO + dd + tm]
            cnt = _pool_count(pos[HALO:HALO + tm], w, s)
            pooled = (acc / cnt - u[:, gi * cg:(gi + 1) * cg]).astype(BF16)
            pooled_ref[:, gi * cg:(gi + 1) * cg] = pooled
            mixed.append(jnp.dot(pooled, wg_ref[gi], preferred_element_type=F32))
        mixed = jnp.concatenate(mixed, axis=1)
        mixed_ref[...] = mixed.astype(BF16)
        yp = (mixed * ps_ref[...] * _silu(_cat(gp_refs).astype(F32))).astype(BF16)
        yp_ref[...] = yp
        zp = jnp.dot(yp, wp_ref[...], preferred_element_type=F32)
        zp_ref[...] = zp.astype(BF16)
        gate = g_ref[...].astype(F32)
        mg_ref[...] = (gate[:, :d] * za + gate[:, d:] * zp).astype(BF16)

    row = lambda i: i
    u_prev, u_next = _halo_specs_cols(off_u, p_w, tm, s)
    in_specs = [pl.BlockSpec((tm, a_w), lambda i: (i, 0))]
    in_specs += _col_specs(off_ga, a_w, tm, row) + _col_specs(off_u, p_w, tm, row) + u_prev + u_next
    in_specs += _col_specs(off_gp, p_w, tm, row)
    in_specs += [pl.BlockSpec((tm, 2 * d), lambda i: (i, 0)), _resident((a_w, d)), _resident((p_w, d)),
                 _resident((N_POOL_GROUPS, cg, cg)), _const_spec((1, p_w))]
    n_proj = n_ga + 3 * n_u + n_gp
    tile = lambda w: pl.BlockSpec((tm, w), lambda i: (i, 0))
    out_widths = (a_w, p_w, d, d, d, p_w, p_w)
    est = 2 * tm * (a_w + a_w + 2 * p_w + 2 * d + sum(out_widths)) * 2 + (a_w + p_w) * d * 2 + 6 * tm * d * 4
    return pl.pallas_call(
        body, name="branches", grid=(s // tm,),
        out_shape=tuple(jax.ShapeDtypeStruct((s, w), BF16) for w in out_widths),
        in_specs=in_specs, out_specs=tuple(tile(w) for w in out_widths),
        compiler_params=_cparams(("parallel",), est),
    )(attn, *([proj] * n_proj), g, w_bra, w_brp, w_grp, pool_scale)


def _sub_rows(tm, sub=128):
    sub = min(sub, tm)
    return [pl.ds(r * sub, sub) for r in range(tm // sub)]


def _out_loss(merged, x, target, w_out, post_g, mod, s, d):
    tm = _row_tile(s, ROW_TILE)
    nsteps = s // tm

    def body(mg_ref, x_ref, t_ref, w_ref, pg_ref, mod_ref, dout_ref, do_ref, loss_ref, dgate_ref, dpg_ref, lacc):
        i = pl.program_id(0)

        @pl.when(i == 0)
        def _():
            lacc[...] = jnp.zeros_like(lacc)
            dgate_ref[...] = jnp.zeros_like(dgate_ref)
            dpg_ref[...] = jnp.zeros_like(dpg_ref)

        pg = pg_ref[...]
        gate = mod_ref[:, 2 * d:3 * d]
        for rows in _sub_rows(tm):
            o = jnp.dot(mg_ref[rows, :], w_ref[...], preferred_element_type=F32)
            r = lax.rsqrt(jnp.mean(o * o, axis=1, keepdims=True) + EPS)
            ohat = o * r
            y = ohat * pg
            e = x_ref[rows, :] + gate * y - t_ref[rows, :]
            lacc[...] += jnp.sum(e * e, axis=0, keepdims=True)
            dout = e * (1.0 / d)
            dout_ref[rows, :] = dout
            dgate_ref[...] += jnp.sum(dout * y, axis=0, keepdims=True)
            dy = dout * gate
            dpg_ref[...] += jnp.sum(dy * ohat, axis=0, keepdims=True)
            dohat = dy * pg
            do = r * (dohat - ohat * jnp.mean(dohat * ohat, axis=1, keepdims=True))
            do_ref[rows, :] = do.astype(BF16)

        @pl.when(i == nsteps - 1)
        def _():
            loss_ref[...] = (0.5 / d) * jnp.sum(lacc[...], axis=1, keepdims=True)

    tile = pl.BlockSpec((tm, d), lambda i: (i, 0))
    vec = _const_spec((1, d))
    return pl.pallas_call(
        body, name="out_loss", grid=(nsteps,),
        out_shape=(jax.ShapeDtypeStruct((s, d), F32), jax.ShapeDtypeStruct((s, d), BF16),
                   jax.ShapeDtypeStruct((1, 1), F32), jax.ShapeDtypeStruct((1, d), F32),
                   jax.ShapeDtypeStruct((1, d), F32)),
        in_specs=[tile, tile, tile, _resident((d, d)), vec, _const_spec((1, 3 * d))],
        out_specs=(tile, tile, _const_spec((1, 1)), vec, vec),
        scratch_shapes=[pltpu.VMEM((1, d), F32)],
        compiler_params=_cparams(("arbitrary",), 2 * tm * d * (2 + 4 + 4 + 4 + 2) + d * d * 2 + 8 * tm * d * 4),
    )(merged, x, target, w_out, post_g, mod)


def _bwd_out(d_o, g, za, zp, w_out, s, d, comm=None):
    tm = _row_tile(s, ROW_TILE)

    def body(do_ref, g_ref, za_ref, zp_ref, w_ref, dza_ref, dzp_ref, dgl_ref, dbm_ref):
        i = pl.program_id(0)

        @pl.when(i == 0)
        def _():
            dbm_ref[...] = jnp.zeros_like(dbm_ref)

        dm = lax.dot_general(do_ref[...], w_ref[...], (((1,), (1,)), ((), ())), preferred_element_type=F32)
        gate = g_ref[...].astype(F32)
        ga, gp = gate[:, :d], gate[:, d:]
        dza_ref[...] = (dm * ga).astype(BF16)
        dzp_ref[...] = (dm * gp).astype(BF16)
        dla = dm * za_ref[...].astype(F32) * ga * (1.0 - ga)
        dlp = dm * zp_ref[...].astype(F32) * gp * (1.0 - gp)
        dgl_ref[:, :d] = dla.astype(BF16)
        dgl_ref[:, d:] = dlp.astype(BF16)
        dbm_ref[:, :d] += jnp.sum(dla, axis=0, keepdims=True)
        dbm_ref[:, d:] += jnp.sum(dlp, axis=0, keepdims=True)

    tile = pl.BlockSpec((tm, d), lambda i: (i, 0))
    wide = pl.BlockSpec((tm, 2 * d), lambda i: (i, 0))
    return _call("bwd_out", body, (s // tm,), [tile, wide, tile, tile, _resident((d, d))], (d_o, g, za, zp, w_out),
                 (jax.ShapeDtypeStruct((s, d), BF16), jax.ShapeDtypeStruct((s, d), BF16),
                  jax.ShapeDtypeStruct((s, 2 * d), BF16), jax.ShapeDtypeStruct((1, 2 * d), F32)),
                 (tile, tile, wide, _const_spec((1, 2 * d))), [], ("arbitrary",),
                 2 * tm * d * 2 * 9 + d * d * 2 + 8 * tm * d * 4, comm=comm)


def _bwd_branches(dza, dzp, attn, proj, mixed, w_bra, w_brp, w_grp, pool_scale, s, d, comm=None):
    a_w, p_w = ATTN_WIDTH, pool_scale.shape[1]
    cg = p_w // N_POOL_GROUPS
    tm = _row_tile(s, 2 * ROW_TILE)
    off_ga = ATTN_WIDTH + 2 * KV_WIDTH
    off_gp = off_ga + a_w + p_w
    n_ga, n_gp = a_w // COL_TILE, p_w // COL_TILE

    def body(*refs):
        it = iter(refs)
        dza_ref, dzp_ref, attn_ref = next(it), next(it), next(it)
        ga_refs = [next(it) for _ in range(n_ga)]
        gp_refs = [next(it) for _ in range(n_gp)]
        mixed_ref, wa_ref, wp_ref, wg_ref, ps_ref = (next(it) for _ in range(5))
        dattn_ref, dga_ref, dgp_ref, dmix_ref, dpool_ref, dps_ref = (next(it) for _ in range(6))
        i = pl.program_id(0)

        @pl.when(i == 0)
        def _():
            dps_ref[...] = jnp.zeros_like(dps_ref)

        dya = lax.dot_general(dza_ref[...], wa_ref[...], (((1,), (1,)), ((), ())), preferred_element_type=F32)
        ga = _cat(ga_refs).astype(F32)
        dattn_ref[...] = (dya * _silu(ga)).astype(BF16)
        dga_ref[...] = (dya * attn_ref[...].astype(F32) * _dsilu(ga)).astype(BF16)

        dyp = lax.dot_general(dzp_ref[...], wp_ref[...], (((1,), (1,)), ((), ())), preferred_element_type=F32)
        gp = _cat(gp_refs).astype(F32)
        mixed = mixed_ref[...].astype(F32)
        ps = ps_ref[...]
        sg = _silu(gp)
        dgp_ref[...] = (dyp * mixed * ps * _dsilu(gp)).astype(BF16)
        dps_ref[...] += jnp.sum(dyp * sg * mixed, axis=0, keepdims=True)
        dmix = (dyp * sg * ps).astype(BF16)
        dmix_ref[...] = dmix
        for gi in range(N_POOL_GROUPS):
            dp = lax.dot_general(dmix[:, gi * cg:(gi + 1) * cg], wg_ref[gi], (((1,), (1,)), ((), ())),
                                 preferred_element_type=F32)
            dpool_ref[:, gi * cg:(gi + 1) * cg] = dp.astype(BF16)

    row = lambda i: i
    tile = lambda w: pl.BlockSpec((tm, w), lambda i: (i, 0))
    in_specs = [tile(d), tile(d), tile(a_w)] + _col_specs(off_ga, a_w, tm, row) + _col_specs(off_gp, p_w, tm, row)
    in_specs += [tile(p_w), _resident((a_w, d)), _resident((p_w, d)), _resident((N_POOL_GROUPS, cg, cg)),
                 _const_spec((1, p_w))]
    out_widths = (a_w, a_w, p_w, p_w, p_w)
    est = 2 * tm * (2 * d + 2 * a_w + 2 * p_w + sum(out_widths)) * 2 + (a_w + p_w) * d * 2 + 8 * tm * a_w * 4
    return _call("bwd_branches", body, (s // tm,), in_specs,
                 (dza, dzp, attn, *([proj] * (n_ga + n_gp)), mixed, w_bra, w_brp, w_grp, pool_scale),
                 tuple(jax.ShapeDtypeStruct((s, w), BF16) for w in out_widths) + (jax.ShapeDtypeStruct((1, p_w), F32),),
                 tuple(tile(w) for w in out_widths) + (_const_spec((1, p_w)),), [], ("arbitrary",), est, comm=comm)


def _pool_backward(dpooled, s):
    _, p_w = dpooled.shape
    cg = p_w // N_POOL_GROUPS
    tm = _row_tile(s, ROW_TILE)
    per, last = tm // HALO, s // HALO - 1

    def body(dp_ref, prev_ref, next_ref, du_ref):
        i = pl.program_id(0)
        dp = dp_ref[...].astype(F32)
        ext = jnp.concatenate([prev_ref[...].astype(F32), dp, next_ref[...].astype(F32)], axis=0)
        pos, ok = _pool_positions(i, tm, s, cg)
        for gi, w in enumerate(POOL_SIZES):
            t = jnp.where(ok, ext[:, gi * cg:(gi + 1) * cg], 0.0) / _pool_count(pos, w, s)
            acc = t[HALO - w // 2 + 1: HALO - w // 2 + 1 + tm]
            for dd in range(-w // 2 + 2, w // 2 + 1):
                acc = acc + t[HALO + dd: HALO + dd + tm]
            du_ref[:, gi * cg:(gi + 1) * cg] = (acc - dp[:, gi * cg:(gi + 1) * cg]).astype(BF16)

    return pl.pallas_call(
        body, name="pool_backward", grid=(s // tm,), out_shape=jax.ShapeDtypeStruct((s, p_w), BF16),
        in_specs=[pl.BlockSpec((tm, p_w), lambda i: (i, 0)),
                  pl.BlockSpec((HALO, p_w), lambda i: (jnp.maximum(i * per - 1, 0), 0)),
                  pl.BlockSpec((HALO, p_w), lambda i: (jnp.minimum((i + 1) * per, last), 0))],
        out_specs=pl.BlockSpec((tm, p_w), lambda i: (i, 0)),
        compiler_params=_cparams(("parallel",), 4 * tm * p_w * 2 + 8 * tm * p_w * 4),
    )(dpooled, dpooled, dpooled)


def _attn_backward(proj, dattn, bias2, sink, s, comm=None):
    nblk = s // BLOCK
    nq = ATTN_WIDTH // COL_TILE
    kv_col = ATTN_WIDTH // COL_TILE
    rows = GQA_GROUP * BLOCK
    scale = HEAD_DIM ** -0.5

    def body(*refs):
        q_refs = refs[:nq]
        kvp, kvc, kvn, do_ref, bias_ref, sink_ref, dq_ref, dkv_ref, dbias_ref, dsink_ref, acc, sacc = refs[nq:]
        n = pl.program_id(0)

        @pl.when(n == 0)
        def _():
            acc[...] = jnp.zeros_like(acc)
            sacc[...] = jnp.zeros_like(sacc)
            dbias_ref[...] = jnp.zeros_like(dbias_ref)
            dsink_ref[...] = jnp.zeros_like(dsink_ref)

        @pl.when(jnp.logical_and(n >= 1, n < nblk))
        def _():
            acc[(n + 1) % 3] = jnp.zeros((BLOCK, 2 * KV_WIDTH), F32)

        @pl.when(n < nblk)
        def _():
            q = _cat(q_refs)
            do = do_ref[...]
            kv = jnp.concatenate([kvp[...], kvc[...], kvn[...]], axis=0)
            valid = _attn_mask(n, s)
            for kh in range(N_KV_HEADS):
                qs = _stack_heads(q, kh)
                dos = _stack_heads(do, kh)
                k = kv[:, kh * HEAD_DIM:(kh + 1) * HEAD_DIM]
                v = kv[:, KV_WIDTH + kh * HEAD_DIM: KV_WIDTH + (kh + 1) * HEAD_DIM]
                p, ps = _softmax_parts(qs, k, bias_ref[kh], valid, _sink_column(sink_ref, kh))
                dp = lax.dot_general(dos, v, (((1,), (1,)), ((), ())), preferred_element_type=F32)
                delta = jnp.sum(p * dp, axis=1, keepdims=True)
                ds = p * (dp - delta)
                dbias_ref[kh] += ds
                sacc[kh] += -ps * delta
                dsb = ds.astype(BF16)
                dq = jnp.dot(dsb, k, preferred_element_type=F32) * scale
                for g in range(GQA_GROUP):
                    h = kh * GQA_GROUP + g
                    dq_ref[:, h * HEAD_DIM:(h + 1) * HEAD_DIM] = dq[g * BLOCK:(g + 1) * BLOCK].astype(BF16)
                dk = lax.dot_general(dsb, qs, (((0,), (0,)), ((), ())), preferred_element_type=F32) * scale
                dv = lax.dot_general(p.astype(BF16), dos, (((0,), (0,)), ((), ())), preferred_element_type=F32)
                for j in range(3):
                    slot = (n + 2 + j) % 3
                    acc[slot, :, kh * HEAD_DIM:(kh + 1) * HEAD_DIM] += dk[j * BLOCK:(j + 1) * BLOCK]
                    acc[slot, :, KV_WIDTH + kh * HEAD_DIM: KV_WIDTH + (kh + 1) * HEAD_DIM] += dv[j * BLOCK:(j + 1) * BLOCK]

        dkv_ref[...] = acc[(n + 2) % 3].astype(BF16)

        @pl.when(n == nblk)
        def _():
            lane = lax.broadcasted_iota(jnp.int32, (1, LANE), 1)
            out = jnp.zeros((1, LANE), F32)
            for kh in range(N_KV_HEADS):
                col = sacc[kh]
                for g in range(GQA_GROUP):
                    out = jnp.where(lane == kh * GQA_GROUP + g, jnp.sum(col[g * BLOCK:(g + 1) * BLOCK]), out)
            dsink_ref[...] = out

    qi = lambda n: jnp.minimum(n, nblk - 1)
    in_specs = _col_specs(0, ATTN_WIDTH, BLOCK, qi)
    in_specs += [pl.BlockSpec((BLOCK, COL_TILE), lambda n: (jnp.maximum(qi(n) - 1, 0), kv_col)),
                 pl.BlockSpec((BLOCK, COL_TILE), lambda n: (qi(n), kv_col)),
                 pl.BlockSpec((BLOCK, COL_TILE), lambda n: (jnp.minimum(qi(n) + 1, nblk - 1), kv_col)),
                 pl.BlockSpec((BLOCK, ATTN_WIDTH), lambda n: (qi(n), 0)),
                 _const_spec((N_KV_HEADS, rows, SPAN)),
                 pl.BlockSpec(memory_space=pltpu.SMEM)]
    return _call("attn_backward", body, (nblk + 1,), in_specs, (*([proj] * (nq + 3)), dattn, bias2, sink),
                 (jax.ShapeDtypeStruct((s, ATTN_WIDTH), BF16), jax.ShapeDtypeStruct((s, 2 * KV_WIDTH), BF16),
                  jax.ShapeDtypeStruct((N_KV_HEADS, rows, SPAN), F32), jax.ShapeDtypeStruct((1, LANE), F32)),
                 (pl.BlockSpec((BLOCK, ATTN_WIDTH), lambda n: (qi(n), 0)),
                  pl.BlockSpec((BLOCK, 2 * KV_WIDTH), lambda n: (jnp.clip(n - 1, 0, nblk - 1), 0)),
                  _const_spec((N_KV_HEADS, rows, SPAN)), _const_spec((1, LANE))),
                 [pltpu.VMEM((3, BLOCK, 2 * KV_WIDTH), F32), pltpu.VMEM((N_KV_HEADS, rows, 1), F32)],
                 ("arbitrary",), 24 << 20, comm=comm)


def _pick_tile(n, cap):
    t = cap - cap % LANE
    while n % t:
        t -= LANE
    return t


def _matmul_tn(name, a, b, comm=None):
    s, m = a.shape
    _, n = b.shape
    tk = _row_tile(s, 2048)
    tm, tn = _pick_tile(m, 2048), _pick_tile(n, 1152)
    nk = s // tk

    def body(a_ref, b_ref, o_ref, acc):
        k = pl.program_id(2)

        @pl.when(k == 0)
        def _():
            acc[...] = jnp.zeros_like(acc)

        acc[...] += lax.dot_general(a_ref[...], b_ref[...], (((0,), (0,)), ((), ())), preferred_element_type=F32)

        @pl.when(k == nk - 1)
        def _():
            o_ref[...] = acc[...].astype(BF16)

    (out,), extra = _call(
        name, body, (m // tm, n // tn, nk),
        [pl.BlockSpec((tk, tm), lambda i, j, k: (k, i)), pl.BlockSpec((tk, tn), lambda i, j, k: (k, j))], (a, b),
        [jax.ShapeDtypeStruct((m, n), BF16)], [pl.BlockSpec((tm, tn), lambda i, j, k: (i, j))],
        [pltpu.VMEM((tm, tn), F32)], ("parallel", "parallel", "arbitrary"),
        2 * tk * (tm + tn) * 2 + tm * tn * (4 + 4 + 4), comm=comm)
    return out, extra


def _pool_weight_grad(pooled, dmix, s):
    _, p_w = pooled.shape
    cg = p_w // N_POOL_GROUPS
    tk = _row_tile(s, 2048)
    nk = s // tk

    def body(a_ref, b_ref, o_ref, acc):
        k = pl.program_id(0)

        @pl.when(k == 0)
        def _():
            acc[...] = jnp.zeros_like(acc)

        for gi in range(N_POOL_GROUPS):
            cols = pl.ds(gi * cg, cg)
            acc[gi] += lax.dot_general(a_ref[:, cols], b_ref[:, cols], (((0,), (0,)), ((), ())),
                                       preferred_element_type=F32)

        @pl.when(k == nk - 1)
        def _():
            o_ref[...] = acc[...].astype(BF16)

    return pl.pallas_call(
        body, name="pool_weight_grad", grid=(nk,),
        out_shape=jax.ShapeDtypeStruct((N_POOL_GROUPS, cg, cg), BF16),
        in_specs=[pl.BlockSpec((tk, p_w), lambda k: (k, 0)), pl.BlockSpec((tk, p_w), lambda k: (k, 0))],
        out_specs=_const_spec((N_POOL_GROUPS, cg, cg)),
        scratch_shapes=[pltpu.VMEM((N_POOL_GROUPS, cg, cg), F32)],
        compiler_params=_cparams(("arbitrary",), 4 * tk * p_w * 2 + 3 * N_POOL_GROUPS * cg * cg * 4),
    )(pooled, dmix)


def _bwd_hidden(dproj, dgl, w_in, w_merge, s, d, comm=None):
    tm = _row_tile(s, 1024)
    t_in, t_mg = _pick_tile(dproj.shape[1], 1024), _pick_tile(dgl.shape[1], 1024)
    n_in = dproj.shape[1] // t_in
    n_mg = dgl.shape[1] // t_mg
    nk = n_in + n_mg

    def body(dp_ref, dg_ref, wi_ref, wm_ref, dh_ref, acc):
        k = pl.program_id(1)

        @pl.when(k == 0)
        def _():
            acc[...] = lax.dot_general(dp_ref[...], wi_ref[...], (((1,), (1,)), ((), ())),
                                       preferred_element_type=F32)

        @pl.when(jnp.logical_and(k > 0, k < n_in))
        def _():
            acc[...] += lax.dot_general(dp_ref[...], wi_ref[...], (((1,), (1,)), ((), ())),
                                        preferred_element_type=F32)

        @pl.when(jnp.logical_and(k >= n_in, k < nk - 1))
        def _():
            acc[...] += lax.dot_general(dg_ref[...], wm_ref[...], (((1,), (1,)), ((), ())),
                                        preferred_element_type=F32)

        @pl.when(k == nk - 1)
        def _():
            dh_ref[...] = (acc[...] + lax.dot_general(dg_ref[...], wm_ref[...], (((1,), (1,)), ((), ())),
                                                      preferred_element_type=F32)).astype(BF16)

    in_specs = [pl.BlockSpec((tm, t_in), lambda i, k: (i, jnp.minimum(k, n_in - 1))),
                pl.BlockSpec((tm, t_mg), lambda i, k: (i, jnp.maximum(k - n_in, 0))),
                pl.BlockSpec((d, t_in), lambda i, k: (0, jnp.minimum(k, n_in - 1))),
                pl.BlockSpec((d, t_mg), lambda i, k: (0, jnp.maximum(k - n_in, 0)))]
    est = 2 * (tm + d) * (t_in + t_mg) * 2 + 3 * tm * d * 4
    (dh,), extra = _call("bwd_hidden", body, (s // tm, nk), in_specs, (dproj, dgl, w_in, w_merge),
                         [jax.ShapeDtypeStruct((s, d), BF16)], [pl.BlockSpec((tm, d), lambda i, k: (i, 0))],
                         [pltpu.VMEM((tm, d), F32)], ("parallel", "arbitrary"), est, comm=comm)
    return dh, extra


def _bwd_prenorm(dh, x, dout, mod, pre_g, s, d):
    tm = _row_tile(s, 2 * ROW_TILE)

    def body(dh_ref, x_ref, dout_ref, mod_ref, pg_ref, gx_ref, dsh_ref, dsc_ref, dpg_ref):
        i = pl.program_id(0)

        @pl.when(i == 0)
        def _():
            dsh_ref[...] = jnp.zeros_like(dsh_ref)
            dsc_ref[...] = jnp.zeros_like(dsc_ref)
            dpg_ref[...] = jnp.zeros_like(dpg_ref)

        dh = dh_ref[...].astype(F32)
        xv = x_ref[...]
        r = lax.rsqrt(jnp.mean(xv * xv, axis=1, keepdims=True) + EPS)
        xhat = xv * r
        pg = pg_ref[...]
        one_scale = 1.0 + mod_ref[:, d:2 * d]
        dsh_ref[...] += jnp.sum(dh, axis=0, keepdims=True)
        dsc_ref[...] += jnp.sum(dh * xhat, axis=0, keepdims=True) * pg
        dpg_ref[...] += jnp.sum(dh * xhat, axis=0, keepdims=True) * one_scale
        dxh = dh * (one_scale * pg)
        dx = r * (dxh - xhat * jnp.mean(dxh * xhat, axis=1, keepdims=True))
        gx_ref[...] = dout_ref[...] + dx

    tile = pl.BlockSpec((tm, d), lambda i: (i, 0))
    vec = _const_spec((1, d))
    return pl.pallas_call(
        body, name="bwd_prenorm", grid=(s // tm,),
        out_shape=(jax.ShapeDtypeStruct((s, d), F32),) + (jax.ShapeDtypeStruct((1, d), F32),) * 3,
        in_specs=[tile, tile, tile, _const_spec((1, 3 * d)), vec], out_specs=(tile, vec, vec, vec),
        compiler_params=_cparams(("arbitrary",), 2 * 4 * tm * d * 4 + 6 * tm * d * 4),
    )(dh, x, dout, mod, pre_g)


def _pad_lanes(v, width):
    return jnp.pad(v, ((0, 0), (0, width - v.shape[1])))


def kernel(x, c, rel_bias_table, w_ada, b_ada, pre_norm_g, post_norm_g, w_in, attn_sink, w_pool_group, pool_scale, w_branch_attn, w_branch_pool, w_merge, b_merge, w_out, loss_target, m_rel_bias_table, m_w_ada, m_b_ada, m_pre_norm_g, m_post_norm_g, m_w_in, m_attn_sink, m_w_pool_group, m_pool_scale, m_w_branch_attn, m_w_branch_pool, m_w_merge, m_b_merge, m_w_out, v_rel_bias_table, v_w_ada, v_b_ada, v_pre_norm_g, v_post_norm_g, v_w_in, v_attn_sink, v_w_pool_group, v_pool_scale, v_w_branch_attn, v_w_branch_pool, v_w_merge, v_b_merge, v_w_out):
    _, s, d = x.shape
    p_w = pool_scale.shape[-1]
    cg = p_w // N_POOL_GROUPS
    in_w = 2 * ATTN_WIDTH + 2 * KV_WIDTH + 2 * p_w
    x2, t2 = x[0], loss_target[0]
    chip = 2 * lax.axis_index("x") + lax.axis_index("y")

    specs = [_Sharded("col", (d, in_w)), _Sharded("col", (d, 2 * d)), _Sharded("col", (ATTN_WIDTH, d)),
             _Sharded("col", (p_w, d)), _Sharded("row", (d, d)), _Sharded("grp", (N_POOL_GROUPS, cg, cg))]
    shards32 = [w_in[0], w_merge[0], w_branch_attn[0], w_branch_pool[0], w_out[0],
                w_pool_group[0].reshape(N_POOL_GROUPS * cg // N_CHIPS, cg)]
    names = ["w_in", "w_merge", "w_branch_attn", "w_branch_pool", "w_out", "w_pool_group"]
    shards16 = [_cast_bf16("cast_" + nm, w) for nm, w in zip(names, shards32)]
    shards16[5] = shards16[5].reshape(N_POOL_GROUPS, cg // N_CHIPS, cg)

    sc_all = _all_gather8("gather_cond", c, 1, pre=_silu)
    m_all = _all_gather8("gather_mod", _ada_forward(sc_all, w_ada[0]), N_DEV)
    mod = _mod_finish(m_all, b_ada)

    (h, proj, gates), (wf_in,) = _local_columns(x2, mod, pre_norm_g, shards16[0], shards16[1], b_merge,
                                                comm=_GatherWeights(specs[:1], shards16[:1], middle=0.9))
    proj, (wf_merge,) = _other_columns("proj", h, wf_in, proj, comm=_GatherWeights(specs[1:2], shards16[1:2]))
    gates, (wf_bra, wf_brp, wf_out, wf_grp) = _other_columns("merge_gates", h, wf_merge, gates, bias=b_merge,
                                                             comm=_GatherWeights(specs[2:], shards16[2:]))
    buckets = _t5_buckets()
    bias2 = _bias_table(rel_bias_table, buckets).reshape(N_KV_HEADS, GQA_GROUP * BLOCK, SPAN)
    attn = _attn_forward(proj, bias2, attn_sink, s)
    ya, yp, za, zp, merged, pooled, mixed = _branches(proj, attn, gates, wf_bra, wf_brp, wf_grp, pool_scale, s, d)
    dout, d_o, loss_part, dgate, dpostg = _out_loss(merged, x2, t2, wf_out, post_norm_g, mod, s, d)

    pw_out, _ = _matmul_tn("grad_w_out", merged, d_o)
    (dza, dzp, dgl, dbm), (pc_out,) = _bwd_out(d_o, gates, za, zp, wf_out, s, d,
                                                comm=_ScatterGrads(specs[4:5], [pw_out]))
    pw_bra, _ = _matmul_tn("grad_w_branch_attn", ya, dza)
    pw_brp, _ = _matmul_tn("grad_w_branch_pool", yp, dzp)
    pw_merge, _ = _matmul_tn("grad_w_merge", h, dgl)
    (dattn, dga, dgp, dmix, dpooled, dps), (pc_bra, pc_brp) = _bwd_branches(
        dza, dzp, attn, proj, mixed, wf_bra, wf_brp, wf_grp, pool_scale, s, d,
        comm=_ScatterGrads(specs[2:4], [pw_bra, pw_brp]))
    pw_grp = _pool_weight_grad(pooled, dmix, s)
    du = _pool_backward(dpooled, s)
    near, far = (0, 1, 2, 3, 6), (4, 5, 7)
    (dq, dkv, dbias, dsink), (pc_merge,) = _attn_backward(
        proj, dattn, bias2, attn_sink, s, comm=_ScatterGrads(specs[1:2], [pw_merge], peers=near))
    dproj = jnp.concatenate([dq, dkv, dga, du, dgp], axis=1)
    pw_in, (pc_merge,) = _matmul_tn("grad_w_in", h, dproj,
                                    comm=_ScatterGrads(specs[1:2], [pw_merge], peers=far, into=[pc_merge]))
    dh, (pc_in, pc_grp) = _bwd_hidden(dproj, dgl, wf_in, wf_merge, s, d,
                                      comm=_ScatterGrads([specs[0], specs[5]], [pw_in, pw_grp]))
    gx, dshift, dscale, dpreg = _bwd_prenorm(dh, x2, dout, mod, pre_norm_g, s, d)

    pieces = [pc_in, pc_merge, pc_bra, pc_brp, pc_out]
    weights = [w_in, w_merge, w_branch_attn, w_branch_pool, w_out]
    moms = [m_w_in, m_w_merge, m_w_branch_attn, m_w_branch_pool, m_w_out]
    vars_ = [v_w_in, v_w_merge, v_w_branch_attn, v_w_branch_pool, v_w_out]
    big = {}
    for nm, pc, w, m, v in zip(names, pieces, weights, moms, vars_):
        shape2 = (-1, w.shape[-1])
        res4 = _reduce_adamw("update_" + nm, pc, w.reshape(shape2), m.reshape(shape2), v.reshape(shape2))
        big[nm] = tuple(a.reshape(w.shape) for a in res4)
    hq = cg // N_CHIPS // 2
    g_grp = _reduce16("reduce_w_pool_group", pc_grp.reshape(2, N_DEV, N_POOL_GROUPS * hq, cg))
    g_grp = g_grp.reshape(2, N_POOL_GROUPS, hq, cg).transpose(1, 0, 2, 3).reshape(N_POOL_GROUPS * 2 * hq, cg)
    res3 = _adamw("adamw_w_pool_group", w_pool_group.reshape(-1, cg), g_grp, m_w_pool_group.reshape(-1, cg),
                  v_w_pool_group.reshape(-1, cg))
    big["w_pool_group"] = tuple(a.reshape(w_pool_group.shape) for a in (g_grp,) + tuple(res3))

    dtable = _bias_table_grad(dbias.reshape(N_Q_HEADS, BLOCK, SPAN), buckets)[:, :N_Q_HEADS]
    segs = [("b_ada", jnp.concatenate([dshift, dscale, dgate], axis=1), 3 * d),
            ("pre_norm_g", dpreg, d), ("post_norm_g", dpostg, d), ("attn_sink", dsink, LANE),
            ("pool_scale", dps, p_w), ("b_merge", dbm, 2 * d), ("rel_bias_table", dtable.reshape(1, -1), 2 * LANE)]
    packed = jnp.concatenate([_pad_lanes(v, w) for _, v, w in segs], axis=1)
    rows = _all_gather8("gather_small", packed, 1)[:, 0, :]

    def pack(vals):
        return jnp.concatenate([_pad_lanes(v.reshape(1, -1), w) for v, (_, _, w) in zip(vals, segs)], axis=1)

    small_w = [b_ada, pre_norm_g, post_norm_g, attn_sink, pool_scale, b_merge, rel_bias_table]
    small_m = [m_b_ada, m_pre_norm_g, m_post_norm_g, m_attn_sink, m_pool_scale, m_b_merge, m_rel_bias_table]
    small_v = [v_b_ada, v_pre_norm_g, v_post_norm_g, v_attn_sink, v_pool_scale, v_b_merge, v_rel_bias_table]
    g_small, d_small, nm_small, nv_small = _small_update(rows, pack(small_w), pack(small_m), pack(small_v))
    small = {}
    off = 0
    for (nm, _, w), ref in zip(segs, small_w):
        cut = lambda a: a[:, off:off + ref.size].reshape(ref.shape)
        small[nm] = (cut(g_small), cut(d_small), cut(nm_small), cut(nv_small))
        off += w

    dmod_cols = lax.dynamic_slice_in_dim(rows[:, :3 * d], chip * (3 * d // N_CHIPS), 3 * d // N_CHIPS, axis=1)
    sc_t = sc_all[:, 0, :].T
    g_ada, d_ada, nm_ada, nv_ada = _ada_backward(sc_t, dmod_cols, w_ada[0], m_w_ada[0], v_w_ada[0])
    big["w_ada"] = tuple(a.reshape(w_ada.shape) for a in (g_ada, d_ada, nm_ada, nv_ada))

    loss = lax.psum(loss_part[0, 0], ("x", "y", "c"))
    order = ["rel_bias_table", "w_ada", "b_ada", "pre_norm_g", "post_norm_g", "w_in", "attn_sink", "w_pool_group",
             "pool_scale", "w_branch_attn", "w_branch_pool", "w_merge", "b_merge", "w_out"]
    res = {**big, **small}
    outs = [loss, gx.reshape(x.shape)]
    for part in range(4):
        outs += [res[nm][part] for nm in order]
    return tuple(outs)


def _small_update(rows, w, m, v):
    _, n = rows.shape

    def body(r_ref, w_ref, m_ref, v_ref, g_ref, d_ref, nm_ref, nv_ref):
        g = r_ref[0:1, :]
        for k in range(1, N_DEV):
            g = g + r_ref[k:k + 1, :]
        dl, nm, nv = _adam_math(w_ref[...], g, m_ref[...], v_ref[...])
        g_ref[...] = g
        d_ref[...] = dl
        nm_ref[...] = nm
        nv_ref[...] = nv

    vm = pl.BlockSpec(memory_space=pltpu.VMEM)
    out = jax.ShapeDtypeStruct((1, n), F32)
    return pl.pallas_call(
        body, name="small_update", out_shape=(out,) * 4, in_specs=[vm] * 4, out_specs=(vm,) * 4,
    )(rows, w, m, v)
```

```python
import functools
import math

import jax
import jax.numpy as jnp
from jax import lax
from jax.experimental import pallas as pl
from jax.experimental.pallas import tpu as pltpu

F32 = jnp.float32
BF16 = jnp.bfloat16
MESH = pl.DeviceIdType.MESH

HEAD_DIM = 128
N_Q_HEADS = 8
N_KV_HEADS = 2
GQA_GROUP = N_Q_HEADS // N_KV_HEADS
ATTN_WIDTH = N_Q_HEADS * HEAD_DIM
KV_WIDTH = N_KV_HEADS * HEAD_DIM
WINDOW = 128
BLOCK = 128
SPAN = BLOCK + 2 * WINDOW
N_BUCKETS = 32
MAX_DISTANCE = 128
POOL_SIZES = (2, 4, 8, 16)
N_POOL_GROUPS = len(POOL_SIZES)
HALO = 16
EPS = 1e-6
NEG_INF = -1e30
ADAM_LR = 0.001
ADAM_B1 = 0.9
ADAM_B2 = 0.999
ADAM_EPS = 1e-08
ADAM_WD = 0.01
ADAM_STEP = 10

N_DEV = 8
N_CHIPS = 4
LANE = 128
COL_TILE = 512
VMEM_CAP = 60000 * 1024
ROW_TILE = 256


def _cparams(sem, est_bytes):
    limit = int(min(max(est_bytes * 5 // 4 + (4 << 20), 16 << 20), VMEM_CAP))
    return pltpu.CompilerParams(dimension_semantics=sem, vmem_limit_bytes=limit)


def _sigmoid(x):
    return jax.nn.sigmoid(x)


def _silu(x):
    return x * _sigmoid(x)


def _dsilu(x):
    s = _sigmoid(x)
    return s * (1.0 + x * (1.0 - s))


def _place():
    x, y, c = lax.axis_index("x"), lax.axis_index("y"), lax.axis_index("c")
    return x, y, c


def _flip(v, bit):
    return (1 - v) if bit else v


def _xor_peer(k):
    x, y, c = _place()
    return (_flip(x, (k >> 2) & 1), _flip(y, (k >> 1) & 1), _flip(c, k & 1))


def _resident(shape):
    nd = len(shape)
    return pl.BlockSpec(shape, lambda *_: (0,) * nd, pipeline_mode=pl.Buffered(1))


def _const_spec(shape):
    nd = len(shape)
    return pl.BlockSpec(shape, lambda *_: (0,) * nd)


def _all_gather8(name, x, nrows, pre=None):
    r, n = x.shape

    def body(x_ref, out_ref, stage, send_sems, recv_sems):
        px, py, pc = _place()
        me = 4 * px + 2 * py + pc
        v = x_ref[...]
        if pre is not None:
            v = pre(v)
        stage[...] = v[0:nrows]
        out_ref[me] = v[0:nrows]
        copies = []
        for k in range(1, N_DEV):
            cp = pltpu.make_async_remote_copy(
                src_ref=stage, dst_ref=out_ref.at[me], send_sem=send_sems.at[k - 1], recv_sem=recv_sems.at[k - 1],
                device_id=_xor_peer(k), device_id_type=MESH)
            cp.start()
            copies.append(cp)
        for cp in copies:
            cp.wait()

    return pl.pallas_call(
        body, name=name,
        out_shape=jax.ShapeDtypeStruct((N_DEV, nrows, n), F32),
        in_specs=[pl.BlockSpec(memory_space=pltpu.VMEM)],
        out_specs=pl.BlockSpec(memory_space=pltpu.VMEM),
        scratch_shapes=[pltpu.VMEM((nrows, n), F32), pltpu.SemaphoreType.DMA((N_DEV - 1,)),
                        pltpu.SemaphoreType.DMA((N_DEV - 1,))],
    )(x)


class _Sharded:
    def __init__(self, kind, full_shape):
        self.kind = kind
        self.full_shape = tuple(full_shape)
        if kind == "col":
            r, c = full_shape
            self.shard_shape = (r, c // N_CHIPS)
        elif kind == "row":
            r, c = full_shape
            self.shard_shape = (r // N_CHIPS, c)
        else:
            g, r, c = full_shape
            self.shard_shape = (g, r // N_CHIPS, c)
        self.axis = 1 if kind == "grp" else 0
        s = list(self.shard_shape)
        s[self.axis] //= 2
        self.piece_shape = tuple(s)

    def _rows(self, ref, start, size):
        idx = (slice(None),) * self.axis + (pl.ds(pl.multiple_of(start, 16), size),)
        return ref.at[idx]

    def shard_half(self, ref, hc):
        h = self.piece_shape[self.axis]
        return self._rows(ref, hc * h, h)

    def window(self, ref, chip, hc=None):
        s = self.shard_shape
        if self.kind == "col":
            cols = pl.ds(pl.multiple_of(chip * s[1], LANE), s[1])
            if hc is None:
                return ref.at[:, cols]
            h = s[0] // 2
            return ref.at[pl.ds(pl.multiple_of(hc * h, 16), h), cols]
        n = s[self.axis]
        if hc is None:
            return self._rows(ref, chip * n, n)
        return self._rows(ref, chip * n + hc * (n // 2), n // 2)


def _remote(src, dst, send_sem, recv_sem, to):
    return pltpu.make_async_remote_copy(src_ref=src, dst_ref=dst, send_sem=send_sem, recv_sem=recv_sem,
                                        device_id=to, device_id_type=MESH)


class _GatherWeights:
    def __init__(self, specs, shards, middle=0.7):
        self.specs = specs
        self.middle_at = middle
        self.inputs = list(shards)
        self.out_shapes = [jax.ShapeDtypeStruct(sp.full_shape, BF16) for sp in specs]
        nw = len(specs)
        self.scratch = [pltpu.SemaphoreType.DMA((6 * nw,)), pltpu.SemaphoreType.DMA((6 * nw,)),
                        pltpu.SemaphoreType.DMA((nw,))]

    aliases = staticmethod(lambda n_in, n_out: {})

    def phases(self, nsteps):
        return [(0, self.start), (min(nsteps - 1, max(1, int(self.middle_at * nsteps))), self.middle),
                (nsteps - 1, self.end)]

    def _ctx(self):
        x, y, c = _place()
        return x, y, c, 2 * x + y, (x, y, 1 - c), [(1 - x, y), (x, 1 - y), (1 - x, 1 - y)]

    def start(self, shard_refs, full_refs, sems):
        send_sems, recv_sems, local_sems = sems
        x, y, c, my_chip, sibling, chips = self._ctx()
        for w, sp in enumerate(self.specs):
            pltpu.make_async_copy(shard_refs[w], sp.window(full_refs[w], my_chip), local_sems.at[w]).start()
            for t, (cx, cy) in enumerate(chips):
                _remote(sp.shard_half(shard_refs[w], c), sp.window(full_refs[w], my_chip, c),
                        send_sems.at[6 * w + t], recv_sems.at[6 * w + t], (cx, cy, c)).start()

    def middle(self, shard_refs, full_refs, sems):
        send_sems, recv_sems, local_sems = sems
        x, y, c, my_chip, sibling, chips = self._ctx()
        for w, sp in enumerate(self.specs):
            for t, (cx, cy) in enumerate(chips):
                landed = sp.window(full_refs[w], 2 * cx + cy, c)
                _remote(landed, landed, send_sems.at[6 * w + t], recv_sems.at[6 * w + t], (cx, cy, c)).wait_recv()
                _remote(landed, landed, send_sems.at[6 * w + 3 + t], recv_sems.at[6 * w + 3 + t], sibling).start()

    def end(self, shard_refs, full_refs, sems):
        send_sems, recv_sems, local_sems = sems
        x, y, c, my_chip, sibling, chips = self._ctx()
        for w, sp in enumerate(self.specs):
            for t, (cx, cy) in enumerate(chips):
                other = sp.window(full_refs[w], 2 * cx + cy, 1 - c)
                _remote(other, other, send_sems.at[6 * w + 3 + t], recv_sems.at[6 * w + 3 + t], sibling).wait_recv()
        for w, sp in enumerate(self.specs):
            for t, (cx, cy) in enumerate(chips):
                mine = sp.shard_half(shard_refs[w], c)
                _remote(mine, mine, send_sems.at[6 * w + t], recv_sems.at[6 * w + t], (cx, cy, c)).wait_send()
                landed = sp.window(full_refs[w], 2 * cx + cy, c)
                _remote(landed, landed, send_sems.at[6 * w + 3 + t], recv_sems.at[6 * w + 3 + t], sibling).wait_send()
            pltpu.make_async_copy(shard_refs[w], sp.window(full_refs[w], my_chip), local_sems.at[w]).wait()


class _ScatterGrads:
    def __init__(self, specs, partials, peers=tuple(range(N_DEV)), into=None, middle=0.88):
        self.specs = specs
        self.peers = tuple(peers)
        self.middle_at = middle
        self.n_part = len(partials)
        self.into = into is not None
        self.inputs = list(partials) + (list(into) if self.into else [])
        self.out_shapes = [jax.ShapeDtypeStruct((2, N_DEV) + sp.piece_shape, BF16) for sp in specs]
        nw = len(specs)
        self.scratch = [pltpu.SemaphoreType.DMA((15 * nw,)), pltpu.SemaphoreType.DMA((15 * nw,)),
                        pltpu.SemaphoreType.DMA((nw,))]

    def aliases(self, n_in, n_out):
        return {n_in + self.n_part + w: n_out + w for w in range(len(self.specs))} if self.into else {}

    def phases(self, nsteps):
        return [(0, self.start), (min(nsteps - 1, max(1, int(self.middle_at * nsteps))), self.middle),
                (nsteps - 1, self.end)]

    def _own(self, sp, part_ref, recv_ref, x, y, c, sem):
        return pltpu.make_async_copy(sp.window(part_ref, 2 * x + y, c), recv_ref.at[c, 0], sem)

    def start(self, part_refs, recv_refs, sems):
        send_sems, recv_sems, local_sems = sems
        x, y, c = _place()
        for w, sp in enumerate(self.specs):
            for k in self.peers:
                if k == 0:
                    self._own(sp, part_refs[w], recv_refs[w], x, y, c, local_sems.at[w]).start()
                    continue
                px, py, pc = _xor_peer(k)
                _remote(sp.window(part_refs[w], 2 * px + py, pc), recv_refs[w].at[pc, k],
                        send_sems.at[15 * w + k - 1], recv_sems.at[15 * w + k - 1], (px, py, pc)).start()

    def middle(self, part_refs, recv_refs, sems):
        send_sems, recv_sems, local_sems = sems
        x, y, c = _place()
        sibling = (x, y, 1 - c)
        for w, sp in enumerate(self.specs):
            for k in self.peers:
                landed = recv_refs[w].at[c, k]
                if k:
                    _remote(landed, landed, send_sems.at[15 * w + k - 1], recv_sems.at[15 * w + k - 1],
                            sibling).wait_recv()
                else:
                    self._own(sp, part_refs[w], recv_refs[w], x, y, c, local_sems.at[w]).wait()
                _remote(landed, landed, send_sems.at[15 * w + 7 + k], recv_sems.at[15 * w + 7 + k], sibling).start()

    def end(self, part_refs, recv_refs, sems):
        send_sems, recv_sems, local_sems = sems
        x, y, c = _place()
        sibling = (x, y, 1 - c)
        for w, sp in enumerate(self.specs):
            for k in self.peers:
                other = recv_refs[w].at[1 - c, k]
                _remote(other, other, send_sems.at[15 * w + 7 + k], recv_sems.at[15 * w + 7 + k], sibling).wait_recv()
            for k in self.peers:
                landed = recv_refs[w].at[c, k]
                _remote(landed, landed, send_sems.at[15 * w + 7 + k], recv_sems.at[15 * w + 7 + k],
                        sibling).wait_send()
                if k:
                    px, py, pc = _xor_peer(k)
                    sent = sp.window(part_refs[w], 2 * px + py, pc)
                    _remote(sent, sent, send_sems.at[15 * w + k - 1], recv_sems.at[15 * w + k - 1],
                            (px, py, pc)).wait_send()


def _call(name, body, grid, in_specs, args, out_shape, out_specs, scratch, semantics, est_bytes, comm=None,
          aliases=None):
    out_shape, out_specs = tuple(out_shape), tuple(out_specs)
    if comm is None:
        res = pl.pallas_call(body, name=name, grid=grid, out_shape=out_shape, in_specs=list(in_specs),
                             out_specs=out_specs, scratch_shapes=list(scratch), input_output_aliases=aliases or {},
                             compiler_params=_cparams(semantics, est_bytes))(*args)
        return tuple(res), ()
    n_in, n_out, n_sc = len(in_specs), len(out_shape), len(scratch)
    c_in, c_out = len(comm.inputs), len(comm.out_shapes)
    nsteps = math.prod(grid)
    phases = comm.phases(nsteps)

    def hosted(*refs):
        pos = [0]

        def take(n):
            part = refs[pos[0]:pos[0] + n]
            pos[0] += n
            return part

        ins, cins, outs, couts, scr, sems = take(n_in), take(c_in), take(n_out), take(c_out), take(n_sc), take(3)
        step = 0
        for ax, extent in enumerate(grid):
            step = step * extent + pl.program_id(ax)
        for at, fn in phases:
            if at == 0:
                pl.when(step == 0)(functools.partial(fn, cins, couts, sems))
        body(*ins, *outs, *scr)
        for at, fn in phases:
            if at > 0:
                pl.when(step == at)(functools.partial(fn, cins, couts, sems))

    any_spec = pl.BlockSpec(memory_space=pl.ANY)
    res = pl.pallas_call(
        hosted, name=name, grid=grid, out_shape=out_shape + tuple(comm.out_shapes),
        in_specs=list(in_specs) + [any_spec] * c_in, out_specs=out_specs + (any_spec,) * c_out,
        scratch_shapes=list(scratch) + list(comm.scratch),
        input_output_aliases={**(aliases or {}), **comm.aliases(n_in, n_out)},
        compiler_params=_cparams(("arbitrary",) * len(grid), est_bytes))(*args, *comm.inputs)
    return tuple(res[:n_out]), tuple(res[n_out:])


def _row_tile(rows, cap):
    for t in range(min(rows, cap), 0, -1):
        if rows % t == 0 and (t % 16 == 0 or t == rows):
            return t
    return rows


def _cast_bf16(name, x):
    r, c = x.shape
    tr = _row_tile(r, 512)

    def body(x_ref, o_ref):
        o_ref[...] = x_ref[...].astype(BF16)

    return pl.pallas_call(
        body, name=name, grid=(r // tr,), out_shape=jax.ShapeDtypeStruct((r, c), BF16),
        in_specs=[pl.BlockSpec((tr, c), lambda i: (i, 0))], out_specs=pl.BlockSpec((tr, c), lambda i: (i, 0)),
        compiler_params=_cparams(("parallel",), 2 * tr * c * 6),
    )(x)


def _adam_math(w, g, m, v):
    m = ADAM_B1 * m + (1.0 - ADAM_B1) * g
    v = ADAM_B2 * v + (1.0 - ADAM_B2) * (g * g)
    m_hat = m / (1.0 - ADAM_B1 ** ADAM_STEP)
    v_hat = v / (1.0 - ADAM_B2 ** ADAM_STEP)
    delta = -ADAM_LR * (m_hat / (jnp.sqrt(v_hat) + ADAM_EPS) + ADAM_WD * w)
    return delta, m, v


def _adamw(name, w, g, m, v):
    r, c = w.shape
    tr = _row_tile(r, max(8, (1 << 18) // c))

    def body(w_ref, g_ref, m_ref, v_ref, d_ref, nm_ref, nv_ref):
        d, nm, nv = _adam_math(w_ref[...], g_ref[...], m_ref[...], v_ref[...])
        d_ref[...] = d
        nm_ref[...] = nm
        nv_ref[...] = nv

    spec = pl.BlockSpec((tr, c), lambda i: (i, 0))
    out = jax.ShapeDtypeStruct((r, c), F32)
    return pl.pallas_call(
        body, name=name, grid=(r // tr,), out_shape=(out, out, out), in_specs=[spec] * 4, out_specs=(spec,) * 3,
        compiler_params=_cparams(("parallel",), 2 * 7 * tr * c * 4),
    )(w, g, m, v)


def _sum_pieces(x_ref):
    acc = x_ref[0, 0].astype(F32)
    for k in range(1, N_DEV):
        acc = acc + x_ref[0, k].astype(F32)
    return acc


def _reduce16(name, x):
    _, _, r, c = x.shape
    tr = _row_tile(r, max(16, (1 << 18) // c))
    nt = r // tr

    def body(x_ref, o_ref):
        o_ref[...] = _sum_pieces(x_ref)

    return pl.pallas_call(
        body, name=name, grid=(2, nt), out_shape=jax.ShapeDtypeStruct((2 * r, c), F32),
        in_specs=[pl.BlockSpec((1, N_DEV, tr, c), lambda hf, i: (hf, 0, i, 0))],
        out_specs=pl.BlockSpec((tr, c), lambda hf, i: (hf * nt + i, 0)),
        compiler_params=_cparams(("parallel", "parallel"), 2 * (N_DEV * 2 + 4) * tr * c),
    )(x)


def _reduce_adamw(name, x, w, m, v):
    _, _, r, c = x.shape
    tr = _row_tile(r, max(16, (1 << 18) // c))
    nt = r // tr

    def body(x_ref, w_ref, m_ref, v_ref, g_ref, d_ref, nm_ref, nv_ref):
        g = _sum_pieces(x_ref)
        d, nm, nv = _adam_math(w_ref[...], g, m_ref[...], v_ref[...])
        g_ref[...] = g
        d_ref[...] = d
        nm_ref[...] = nm
        nv_ref[...] = nv

    tile = pl.BlockSpec((tr, c), lambda hf, i: (hf * nt + i, 0))
    out = jax.ShapeDtypeStruct((2 * r, c), F32)
    return pl.pallas_call(
        body, name=name, grid=(2, nt), out_shape=(out,) * 4,
        in_specs=[pl.BlockSpec((1, N_DEV, tr, c), lambda hf, i: (hf, 0, i, 0)), tile, tile, tile],
        out_specs=(tile,) * 4,
        compiler_params=_cparams(("parallel", "parallel"), 2 * (N_DEV * 2 + 7 * 4) * tr * c),
    )(x, w, m, v)


def _t5_buckets():
    rel = jnp.arange(SPAN)[None, :] - WINDOW - jnp.arange(BLOCK)[:, None]
    half = N_BUCKETS // 2
    max_exact = half // 2
    ret = jnp.where(rel > 0, half, 0)
    n = jnp.abs(rel)
    nf = jnp.maximum(n, 1).astype(F32)
    large = max_exact + (jnp.log(nf / max_exact) / math.log(MAX_DISTANCE / max_exact)
                         * (half - max_exact)).astype(jnp.int32)
    large = jnp.minimum(large, half - 1)
    return (ret + jnp.where(n < max_exact, n, large)).astype(jnp.int32)


def _bias_table(table, buckets):
    def body(t_ref, b_ref, o_ref):
        bk = b_ref[...]
        rel = (lax.broadcasted_iota(jnp.int32, (BLOCK, SPAN), 1) - WINDOW
               - lax.broadcasted_iota(jnp.int32, (BLOCK, SPAN), 0))
        band = jnp.abs(rel) <= WINDOW
        for h in range(N_Q_HEADS):
            acc = jnp.zeros((BLOCK, SPAN), F32)
            for b in range(N_BUCKETS):
                acc = jnp.where(bk == b, t_ref[b, h], acc)
            o_ref[h] = jnp.where(band, acc, NEG_INF)

    return pl.pallas_call(
        body, name="bias_table", out_shape=jax.ShapeDtypeStruct((N_Q_HEADS, BLOCK, SPAN), F32),
        in_specs=[pl.BlockSpec(memory_space=pltpu.SMEM), pl.BlockSpec(memory_space=pltpu.VMEM)],
        out_specs=pl.BlockSpec(memory_space=pltpu.VMEM),
    )(table, buckets)


def _bias_table_grad(dbias, buckets):
    def body(d_ref, b_ref, o_ref):
        bk = b_ref[...]
        row = lax.broadcasted_iota(jnp.int32, (N_BUCKETS, LANE), 0)
        lane = lax.broadcasted_iota(jnp.int32, (N_BUCKETS, LANE), 1)
        acc = jnp.zeros((N_BUCKETS, LANE), F32)
        for h in range(N_Q_HEADS):
            d = d_ref[h]
            for b in range(N_BUCKETS):
                s = jnp.sum(jnp.where(bk == b, d, 0.0))
                acc = jnp.where((row == b) & (lane == h), s, acc)
        o_ref[...] = acc

    return pl.pallas_call(
        body, name="bias_table_grad", out_shape=jax.ShapeDtypeStruct((N_BUCKETS, LANE), F32),
        in_specs=[pl.BlockSpec(memory_space=pltpu.VMEM), pl.BlockSpec(memory_space=pltpu.VMEM)],
        out_specs=pl.BlockSpec(memory_space=pltpu.VMEM),
    )(dbias, buckets)


def _ada_forward(sc_all, w_ada):
    d, n = w_ada.shape
    tn = _pick_tile(n, COL_TILE)

    def body(sc_ref, w_ref, o_ref):
        row = lax.broadcasted_iota(jnp.int32, (N_DEV, d), 0)
        sc = jnp.zeros((N_DEV, d), F32)
        for k in range(N_DEV):
            sc = jnp.where(row == k, sc_ref[k], sc)
        o_ref[...] = jnp.dot(sc, w_ref[...], preferred_element_type=F32, precision=lax.Precision.HIGHEST)

    return pl.pallas_call(
        body, name="ada_forward", grid=(n // tn,), out_shape=jax.ShapeDtypeStruct((N_DEV, n), F32),
        in_specs=[_const_spec((N_DEV, 1, d)), pl.BlockSpec((d, tn), lambda j: (0, j))],
        out_specs=pl.BlockSpec((N_DEV, tn), lambda j: (0, j)),
        compiler_params=_cparams(("parallel",), 2 * d * tn * 4 + N_DEV * N_DEV * d * 8),
    )(sc_all, w_ada)


def _mod_finish(m_all, b_ada):
    _, _, n = m_all.shape

    def body(m_ref, b_ref, o_ref):
        x, y, c = _place()
        me = 4 * x + 2 * y + c
        row = lax.broadcasted_iota(jnp.int32, (N_DEV, n), 0)
        for j in range(N_CHIPS):
            blk = m_ref[2 * j]
            mine = jnp.sum(jnp.where(row == me, blk, 0.0), axis=0, keepdims=True)
            o_ref[:, j * n:(j + 1) * n] = mine + b_ref[:, j * n:(j + 1) * n]

    return pl.pallas_call(
        body, name="mod_finish", out_shape=jax.ShapeDtypeStruct((1, N_CHIPS * n), F32),
        in_specs=[pl.BlockSpec(memory_space=pltpu.VMEM), pl.BlockSpec(memory_space=pltpu.VMEM)],
        out_specs=pl.BlockSpec(memory_space=pltpu.VMEM),
    )(m_all, b_ada)


def _ada_backward(sc_t, dmod_cols, w, m, v):
    d, n = w.shape
    tr, tn = _row_tile(d, 512), _pick_tile(n, COL_TILE)

    def body(s_ref, dm_ref, w_ref, m_ref, v_ref, g_ref, d_ref, nm_ref, nv_ref):
        g = jnp.dot(s_ref[...], dm_ref[...], preferred_element_type=F32, precision=lax.Precision.HIGHEST)
        dl, nm, nv = _adam_math(w_ref[...], g, m_ref[...], v_ref[...])
        g_ref[...] = g
        d_ref[...] = dl
        nm_ref[...] = nm
        nv_ref[...] = nv

    tile = pl.BlockSpec((tr, tn), lambda i, j: (i, j))
    out = jax.ShapeDtypeStruct((d, n), F32)
    return pl.pallas_call(
        body, name="ada_backward", grid=(d // tr, n // tn), out_shape=(out,) * 4,
        in_specs=[pl.BlockSpec((tr, N_DEV), lambda i, j: (i, 0)), pl.BlockSpec((N_DEV, tn), lambda i, j: (0, j)),
                  tile, tile, tile],
        out_specs=(tile,) * 4,
        compiler_params=_cparams(("parallel", "parallel"), 2 * 8 * tr * tn * 4),
    )(sc_t, dmod_cols, w, m, v)


def _chip():
    return 2 * lax.axis_index("x") + lax.axis_index("y")


def _local_columns(x, mod, pre_g, w_in_shard, w_merge_shard, b_merge, comm):
    s, d = x.shape
    n1, n2 = w_in_shard.shape[1], w_merge_shard.shape[1]
    tm = _row_tile(s, 512)

    def body(x_ref, mod_ref, pg_ref, wi_ref, wm_ref, b_ref, h_ref, p_ref, g_ref):
        xv = x_ref[...]
        r = lax.rsqrt(jnp.mean(xv * xv, axis=1, keepdims=True) + EPS)
        xn = xv * r * pg_ref[...]
        hv = (xn * (1.0 + mod_ref[:, d:2 * d]) + mod_ref[:, 0:d]).astype(BF16)
        h_ref[...] = hv
        p_ref[...] = jnp.dot(hv, wi_ref[...], preferred_element_type=F32).astype(BF16)
        g_ref[...] = _sigmoid(jnp.dot(hv, wm_ref[...], preferred_element_type=F32) + b_ref[...]).astype(BF16)

    in_specs = [pl.BlockSpec((tm, d), lambda i: (i, 0)), _const_spec((1, 3 * d)), _const_spec((1, d)),
                _resident((d, n1)), _resident((d, n2)), pl.BlockSpec((1, n2), lambda i: (0, _chip()))]
    out_specs = [pl.BlockSpec((tm, d), lambda i: (i, 0)), pl.BlockSpec((tm, n1), lambda i: (i, _chip())),
                 pl.BlockSpec((tm, n2), lambda i: (i, _chip()))]
    est = 2 * tm * (4 * d + 2 * d + 2 * n1 + 2 * n2) + d * (n1 + n2) * 2 + 4 * tm * d * 4 + 2 * tm * (n1 + n2) * 4
    return _call("local_columns", body, (s // tm,), in_specs, (x, mod, pre_g, w_in_shard, w_merge_shard, b_merge),
                 [jax.ShapeDtypeStruct((s, d), BF16), jax.ShapeDtypeStruct((s, N_CHIPS * n1), BF16),
                  jax.ShapeDtypeStruct((s, N_CHIPS * n2), BF16)],
                 out_specs, [], ("arbitrary",), est, comm=comm)


def _other_columns(name, a, b, partial, bias=None, comm=None):
    s, k = a.shape
    _, n = b.shape
    tm, tn = _row_tile(s, 2048), n // N_CHIPS
    col = lambda i, j: (_chip() + 1 + j) % N_CHIPS

    def body(*refs):
        if bias is None:
            a_ref, b_ref, _, o_ref = refs
        else:
            a_ref, b_ref, bias_ref, _, o_ref = refs
        acc = jnp.dot(a_ref[...], b_ref[...], preferred_element_type=F32)
        if bias is not None:
            acc = _sigmoid(acc + bias_ref[...])
        o_ref[...] = acc.astype(BF16)

    in_specs = [pl.BlockSpec((tm, k), lambda i, j: (i, 0)), pl.BlockSpec((k, tn), lambda i, j: (0, col(i, j)))]
    args = [a, b]
    if bias is not None:
        in_specs.append(pl.BlockSpec((1, tn), lambda i, j: (0, col(i, j))))
        args.append(bias)
    in_specs.append(pl.BlockSpec(memory_space=pl.ANY))
    args.append(partial)
    (out,), extra = _call(name, body, (s // tm, N_CHIPS - 1), in_specs, args, [jax.ShapeDtypeStruct((s, n), BF16)],
                          [pl.BlockSpec((tm, tn), lambda i, j: (i, col(i, j)))], [], ("parallel", "arbitrary"),
                          2 * (tm * k + k * tn + tm * tn) * 2 + 2 * tm * tn * 4, comm=comm,
                          aliases={len(args) - 1: 0})
    return out, extra


def _col_specs(off, width, rows, row_index):
    assert off % COL_TILE == 0 and width % COL_TILE == 0
    return [pl.BlockSpec((rows, COL_TILE), functools.partial(lambda p, *ids: (row_index(*ids), p), off // COL_TILE + p))
            for p in range(width // COL_TILE)]


def _cat(refs):
    vals = [r[...] for r in refs]
    return vals[0] if len(vals) == 1 else jnp.concatenate(vals, axis=1)


def _attn_mask(n, s):
    kpos = (n - 1) * BLOCK + lax.broadcasted_iota(jnp.int32, (1, SPAN), 1)
    return (kpos >= 0) & (kpos < s)


def _sink_column(sink_ref, kh):
    rows = GQA_GROUP * BLOCK
    grp = lax.broadcasted_iota(jnp.int32, (rows, 1), 0) // BLOCK
    col = jnp.zeros((rows, 1), F32)
    for g in range(GQA_GROUP):
        col = jnp.where(grp == g, sink_ref[0, kh * GQA_GROUP + g], col)
    return col


def _stack_heads(x, kh):
    base = kh * GQA_GROUP * HEAD_DIM
    return jnp.concatenate([x[:, base + g * HEAD_DIM: base + (g + 1) * HEAD_DIM] for g in range(GQA_GROUP)], axis=0)


def _softmax_terms(qs, k, bias, valid, sink_col):
    sc = lax.dot_general(qs, k, (((1,), (1,)), ((), ())), preferred_element_type=F32)
    sc = sc * (HEAD_DIM ** -0.5) + bias
    sc = jnp.where(valid, sc, NEG_INF)
    mx = jnp.maximum(jnp.max(sc, axis=1, keepdims=True), sink_col)
    e = jnp.exp(sc - mx)
    es = jnp.exp(sink_col - mx)
    return e, es, 1.0 / (jnp.sum(e, axis=1, keepdims=True) + es)


def _softmax_parts(qs, k, bias, valid, sink_col):
    e, es, inv = _softmax_terms(qs, k, bias, valid, sink_col)
    return e * inv, es * inv


def _attn_forward(proj, bias2, sink, s):
    nblk = s // BLOCK
    nq = ATTN_WIDTH // COL_TILE
    kv_col = ATTN_WIDTH // COL_TILE
    assert 2 * KV_WIDTH == COL_TILE

    def body(*refs):
        q_refs = refs[:nq]
        kvp, kvc, kvn, bias_ref, sink_ref, o_ref = refs[nq:]
        n = pl.program_id(0)
        q = _cat(q_refs)
        kv = jnp.concatenate([kvp[...], kvc[...], kvn[...]], axis=0)
        valid = _attn_mask(n, s)
        for kh in range(N_KV_HEADS):
            qs = _stack_heads(q, kh)
            k = kv[:, kh * HEAD_DIM:(kh + 1) * HEAD_DIM]
            v = kv[:, KV_WIDTH + kh * HEAD_DIM: KV_WIDTH + (kh + 1) * HEAD_DIM]
            e, _, inv = _softmax_terms(qs, k, bias_ref[kh], valid, _sink_column(sink_ref, kh))
            o = jnp.dot(e.astype(BF16), v, preferred_element_type=F32) * inv
            for g in range(GQA_GROUP):
                h = kh * GQA_GROUP + g
                o_ref[:, h * HEAD_DIM:(h + 1) * HEAD_DIM] = o[g * BLOCK:(g + 1) * BLOCK].astype(BF16)

    in_specs = _col_specs(0, ATTN_WIDTH, BLOCK, lambda n: n)
    in_specs += [pl.BlockSpec((BLOCK, COL_TILE), lambda n: (jnp.maximum(n - 1, 0), kv_col)),
                 pl.BlockSpec((BLOCK, COL_TILE), lambda n: (n, kv_col)),
                 pl.BlockSpec((BLOCK, COL_TILE), lambda n: (jnp.minimum(n + 1, nblk - 1), kv_col)),
                 _const_spec((N_KV_HEADS, GQA_GROUP * BLOCK, SPAN)),
                 pl.BlockSpec(memory_space=pltpu.SMEM)]
    return pl.pallas_call(
        body, name="attn_forward", grid=(nblk,), out_shape=jax.ShapeDtypeStruct((s, ATTN_WIDTH), BF16),
        in_specs=in_specs, out_specs=pl.BlockSpec((BLOCK, ATTN_WIDTH), lambda n: (n, 0)),
        compiler_params=_cparams(("parallel",), 16 << 20),
    )(*([proj] * (nq + 3)), bias2, sink)


def _pool_positions(i, tm, s, width):
    pos = i * tm - HALO + lax.broadcasted_iota(jnp.int32, (tm + 2 * HALO, width), 0)
    return pos, (pos >= 0) & (pos < s)


def _pool_count(pos, w, s):
    return (jnp.minimum(pos + w // 2, s) - jnp.maximum(pos - w // 2, 0)).astype(F32)


def _halo_specs_cols(off, width, tm, s):
    per = tm // HALO
    last = s // HALO - 1
    prev = _col_specs(off, width, HALO, lambda i: jnp.maximum(i * per - 1, 0))
    nxt = _col_specs(off, width, HALO, lambda i: jnp.minimum((i + 1) * per, last))
    return prev, nxt


def _branches(proj, attn, g, w_bra, w_brp, w_grp, pool_scale, s, d):
    a_w, p_w = ATTN_WIDTH, pool_scale.shape[1]
    cg = p_w // N_POOL_GROUPS
    tm = _row_tile(s, 2 * ROW_TILE)
    off_ga = ATTN_WIDTH + 2 * KV_WIDTH
    off_u = off_ga + a_w
    off_gp = off_u + p_w
    n_ga, n_u, n_gp = a_w // COL_TILE, p_w // COL_TILE, p_w // COL_TILE

    def body(*refs):
        it = iter(refs)
        attn_ref = next(it)
        ga_refs = [next(it) for _ in range(n_ga)]
        u_refs = [next(it) for _ in range(n_u)]
        up_refs = [next(it) for _ in range(n_u)]
        un_refs = [next(it) for _ in range(n_u)]
        gp_refs = [next(it) for _ in range(n_gp)]
        g_ref, wa_ref, wp_ref, wg_ref, ps_ref = (next(it) for _ in range(5))
        ya_ref, yp_ref, za_ref, zp_ref, mg_ref, pooled_ref, mixed_ref = (next(it) for _ in range(7))
        i = pl.program_id(0)
        ya = (attn_ref[...].astype(F32) * _silu(_cat(ga_refs).astype(F32))).astype(BF16)
        ya_ref[...] = ya
        za = jnp.dot(ya, wa_ref[...], preferred_element_type=F32)
        za_ref[...] = za.astype(BF16)

        u = _cat(u_refs).astype(F32)
        ext = jnp.concatenate([_cat(up_refs).astype(F32), u, _cat(un_refs).astype(F32)], axis=0)
        pos, ok = _pool_positions(i, tm, s, cg)
        mixed = []
        for gi, w in enumerate(POOL_SIZES):
            e = jnp.where(ok, ext[:, gi * cg:(gi + 1) * cg], 0.0)
            acc = e[HALO - w // 2: HALO - w // 2 + tm]
            for dd in range(-w // 2 + 1, w // 2):
                acc = acc + e[HALO + dd: HALO + dd + tm]
            cnt = _pool_count(pos[HALO:HALO + tm], w, s)
            pooled = (acc / cnt - u[:, gi * cg:(gi + 1) * cg]).astype(BF16)
            pooled_ref[:, gi * cg:(gi + 1) * cg] = pooled
            mixed.append(jnp.dot(pooled, wg_ref[gi], preferred_element_type=F32))
        mixed = jnp.concatenate(mixed, axis=1)
        mixed_ref[...] = mixed.astype(BF16)
        yp = (mixed * ps_ref[...] * _silu(_cat(gp_refs).astype(F32))).astype(BF16)
        yp_ref[...] = yp
        zp = jnp.dot(yp, wp_ref[...], preferred_element_type=F32)
        zp_ref[...] = zp.astype(BF16)
        gate = g_ref[...].astype(F32)
        mg_ref[...] = (gate[:, :d] * za + gate[:, d:] * zp).astype(BF16)

    row = lambda i: i
    u_prev, u_next = _halo_specs_cols(off_u, p_w, tm, s)
    in_specs = [pl.BlockSpec((tm, a_w), lambda i: (i, 0))]
    in_specs += _col_specs(off_ga, a_w, tm, row) + _col_specs(off_u, p_w, tm, row) + u_prev + u_next
    in_specs += _col_specs(off_gp, p_w, tm, row)
    in_specs += [pl.BlockSpec((tm, 2 * d), lambda i: (i, 0)), _resident((a_w, d)), _resident((p_w, d)),
                 _resident((N_POOL_GROUPS, cg, cg)), _const_spec((1, p_w))]
    n_proj = n_ga + 3 * n_u + n_gp
    tile = lambda w: pl.BlockSpec((tm, w), lambda i: (i, 0))
    out_widths = (a_w, p_w, d, d, d, p_w, p_w)
    est = 2 * tm * (a_w + a_w + 2 * p_w + 2 * d + sum(out_widths)) * 2 + (a_w + p_w) * d * 2 + 6 * tm * d * 4
    return pl.pallas_call(
        body, name="branches", grid=(s // tm,),
        out_shape=tuple(jax.ShapeDtypeStruct((s, w), BF16) for w in out_widths),
        in_specs=in_specs, out_specs=tuple(tile(w) for w in out_widths),
        compiler_params=_cparams(("parallel",), est),
    )(attn, *([proj] * n_proj), g, w_bra, w_brp, w_grp, pool_scale)


def _sub_rows(tm, sub=128):
    sub = min(sub, tm)
    return [pl.ds(r * sub, sub) for r in range(tm // sub)]


def _out_loss(merged, x, target, w_out, post_g, mod, s, d):
    tm = _row_tile(s, ROW_TILE)
    nsteps = s // tm

    def body(mg_ref, x_ref, t_ref, w_ref, pg_ref, mod_ref, dout_ref, do_ref, loss_ref, dgate_ref, dpg_ref, lacc):
        i = pl.program_id(0)

        @pl.when(i == 0)
        def _():
            lacc[...] = jnp.zeros_like(lacc)
            dgate_ref[...] = jnp.zeros_like(dgate_ref)
            dpg_ref[...] = jnp.zeros_like(dpg_ref)

        pg = pg_ref[...]
        gate = mod_ref[:, 2 * d:3 * d]
        for rows in _sub_rows(tm):
            o = jnp.dot(mg_ref[rows, :], w_ref[...], preferred_element_type=F32)
            r = lax.rsqrt(jnp.mean(o * o, axis=1, keepdims=True) + EPS)
            ohat = o * r
            y = ohat * pg
            e = x_ref[rows, :] + gate * y - t_ref[rows, :]
            lacc[...] += jnp.sum(e * e, axis=0, keepdims=True)
            dout = e * (1.0 / d)
            dout_ref[rows, :] = dout
            dgate_ref[...] += jnp.sum(dout * y, axis=0, keepdims=True)
            dy = dout * gate
            dpg_ref[...] += jnp.sum(dy * ohat, axis=0, keepdims=True)
            dohat = dy * pg
            do = r * (dohat - ohat * jnp.mean(dohat * ohat, axis=1, keepdims=True))
            do_ref[rows, :] = do.astype(BF16)

        @pl.when(i == nsteps - 1)
        def _():
            loss_ref[...] = (0.5 / d) * jnp.sum(lacc[...], axis=1, keepdims=True)

    tile = pl.BlockSpec((tm, d), lambda i: (i, 0))
    vec = _const_spec((1, d))
    return pl.pallas_call(
        body, name="out_loss", grid=(nsteps,),
        out_shape=(jax.ShapeDtypeStruct((s, d), F32), jax.ShapeDtypeStruct((s, d), BF16),
                   jax.ShapeDtypeStruct((1, 1), F32), jax.ShapeDtypeStruct((1, d), F32),
                   jax.ShapeDtypeStruct((1, d), F32)),
        in_specs=[tile, tile, tile, _resident((d, d)), vec, _const_spec((1, 3 * d))],
        out_specs=(tile, tile, _const_spec((1, 1)), vec, vec),
        scratch_shapes=[pltpu.VMEM((1, d), F32)],
        compiler_params=_cparams(("arbitrary",), 2 * tm * d * (2 + 4 + 4 + 4 + 2) + d * d * 2 + 8 * tm * d * 4),
    )(merged, x, target, w_out, post_g, mod)


def _bwd_out(d_o, g, za, zp, w_out, s, d, comm=None):
    tm = _row_tile(s, ROW_TILE)

    def body(do_ref, g_ref, za_ref, zp_ref, w_ref, dza_ref, dzp_ref, dgl_ref, dbm_ref):
        i = pl.program_id(0)

        @pl.when(i == 0)
        def _():
            dbm_ref[...] = jnp.zeros_like(dbm_ref)

        dm = lax.dot_general(do_ref[...], w_ref[...], (((1,), (1,)), ((), ())), preferred_element_type=F32)
        gate = g_ref[...].astype(F32)
        ga, gp = gate[:, :d], gate[:, d:]
        dza_ref[...] = (dm * ga).astype(BF16)
        dzp_ref[...] = (dm * gp).astype(BF16)
        dla = dm * za_ref[...].astype(F32) * ga * (1.0 - ga)
        dlp = dm * zp_ref[...].astype(F32) * gp * (1.0 - gp)
        dgl_ref[:, :d] = dla.astype(BF16)
        dgl_ref[:, d:] = dlp.astype(BF16)
        dbm_ref[:, :d] += jnp.sum(dla, axis=0, keepdims=True)
        dbm_ref[:, d:] += jnp.sum(dlp, axis=0, keepdims=True)

    tile = pl.BlockSpec((tm, d), lambda i: (i, 0))
    wide = pl.BlockSpec((tm, 2 * d), lambda i: (i, 0))
    return _call("bwd_out", body, (s // tm,), [tile, wide, tile, tile, _resident((d, d))], (d_o, g, za, zp, w_out),
                 (jax.ShapeDtypeStruct((s, d), BF16), jax.ShapeDtypeStruct((s, d), BF16),
                  jax.ShapeDtypeStruct((s, 2 * d), BF16), jax.ShapeDtypeStruct((1, 2 * d), F32)),
                 (tile, tile, wide, _const_spec((1, 2 * d))), [], ("arbitrary",),
                 2 * tm * d * 2 * 9 + d * d * 2 + 8 * tm * d * 4, comm=comm)


def _bwd_branches(dza, dzp, attn, proj, mixed, w_bra, w_brp, w_grp, pool_scale, s, d, comm=None):
    a_w, p_w = ATTN_WIDTH, pool_scale.shape[1]
    cg = p_w // N_POOL_GROUPS
    tm = _row_tile(s, 2 * ROW_TILE)
    off_ga = ATTN_WIDTH + 2 * KV_WIDTH
    off_gp = off_ga + a_w + p_w
    n_ga, n_gp = a_w // COL_TILE, p_w // COL_TILE

    def body(*refs):
        it = iter(refs)
        dza_ref, dzp_ref, attn_ref = next(it), next(it), next(it)
        ga_refs = [next(it) for _ in range(n_ga)]
        gp_refs = [next(it) for _ in range(n_gp)]
        mixed_ref, wa_ref, wp_ref, wg_ref, ps_ref = (next(it) for _ in range(5))
        dattn_ref, dga_ref, dgp_ref, dmix_ref, dpool_ref, dps_ref = (next(it) for _ in range(6))
        i = pl.program_id(0)

        @pl.when(i == 0)
        def _():
            dps_ref[...] = jnp.zeros_like(dps_ref)

        dya = lax.dot_general(dza_ref[...], wa_ref[...], (((1,), (1,)), ((), ())), preferred_element_type=F32)
        ga = _cat(ga_refs).astype(F32)
        dattn_ref[...] = (dya * _silu(ga)).astype(BF16)
        dga_ref[...] = (dya * attn_ref[...].astype(F32) * _dsilu(ga)).astype(BF16)

        dyp = lax.dot_general(dzp_ref[...], wp_ref[...], (((1,), (1,)), ((), ())), preferred_element_type=F32)
        gp = _cat(gp_refs).astype(F32)
        mixed = mixed_ref[...].astype(F32)
        ps = ps_ref[...]
        sg = _silu(gp)
        dgp_ref[...] = (dyp * mixed * ps * _dsilu(gp)).astype(BF16)
        dps_ref[...] += jnp.sum(dyp * sg * mixed, axis=0, keepdims=True)
        dmix = (dyp * sg * ps).astype(BF16)
        dmix_ref[...] = dmix
        for gi in range(N_POOL_GROUPS):
            dp = lax.dot_general(dmix[:, gi * cg:(gi + 1) * cg], wg_ref[gi], (((1,), (1,)), ((), ())),
                                 preferred_element_type=F32)
            dpool_ref[:, gi * cg:(gi + 1) * cg] = dp.astype(BF16)

    row = lambda i: i
    tile = lambda w: pl.BlockSpec((tm, w), lambda i: (i, 0))
    in_specs = [tile(d), tile(d), tile(a_w)] + _col_specs(off_ga, a_w, tm, row) + _col_specs(off_gp, p_w, tm, row)
    in_specs += [tile(p_w), _resident((a_w, d)), _resident((p_w, d)), _resident((N_POOL_GROUPS, cg, cg)),
                 _const_spec((1, p_w))]
    out_widths = (a_w, a_w, p_w, p_w, p_w)
    est = 2 * tm * (2 * d + 2 * a_w + 2 * p_w + sum(out_widths)) * 2 + (a_w + p_w) * d * 2 + 8 * tm * a_w * 4
    return _call("bwd_branches", body, (s // tm,), in_specs,
                 (dza, dzp, attn, *([proj] * (n_ga + n_gp)), mixed, w_bra, w_brp, w_grp, pool_scale),
                 tuple(jax.ShapeDtypeStruct((s, w), BF16) for w in out_widths) + (jax.ShapeDtypeStruct((1, p_w), F32),),
                 tuple(tile(w) for w in out_widths) + (_const_spec((1, p_w)),), [], ("arbitrary",), est, comm=comm)


def _pool_backward(dpooled, s):
    _, p_w = dpooled.shape
    cg = p_w // N_POOL_GROUPS
    tm = _row_tile(s, ROW_TILE)
    per, last = tm // HALO, s // HALO - 1

    def body(dp_ref, prev_ref, next_ref, du_ref):
        i = pl.program_id(0)
        dp = dp_ref[...].astype(F32)
        ext = jnp.concatenate([prev_ref[...].astype(F32), dp, next_ref[...].astype(F32)], axis=0)
        pos, ok = _pool_positions(i, tm, s, cg)
        for gi, w in enumerate(POOL_SIZES):
            t = jnp.where(ok, ext[:, gi * cg:(gi + 1) * cg], 0.0) / _pool_count(pos, w, s)
            acc = t[HALO - w // 2 + 1: HALO - w // 2 + 1 + tm]
            for dd in range(-w // 2 + 2, w // 2 + 1):
                acc = acc + t[HALO + dd: HALO + dd + tm]
            du_ref[:, gi * cg:(gi + 1) * cg] = (acc - dp[:, gi * cg:(gi + 1) * cg]).astype(BF16)

    return pl.pallas_call(
        body, name="pool_backward", grid=(s // tm,), out_shape=jax.ShapeDtypeStruct((s, p_w), BF16),
        in_specs=[pl.BlockSpec((tm, p_w), lambda i: (i, 0)),
                  pl.BlockSpec((HALO, p_w), lambda i: (jnp.maximum(i * per - 1, 0), 0)),
                  pl.BlockSpec((HALO, p_w), lambda i: (jnp.minimum((i + 1) * per, last), 0))],
        out_specs=pl.BlockSpec((tm, p_w), lambda i: (i, 0)),
        compiler_params=_cparams(("parallel",), 4 * tm * p_w * 2 + 8 * tm * p_w * 4),
    )(dpooled, dpooled, dpooled)


def _attn_backward(proj, dattn, bias2, sink, s, comm=None):
    nblk = s // BLOCK
    nq = ATTN_WIDTH // COL_TILE
    kv_col = ATTN_WIDTH // COL_TILE
    rows = GQA_GROUP * BLOCK
    scale = HEAD_DIM ** -0.5

    def body(*refs):
        q_refs = refs[:nq]
        kvp, kvc, kvn, do_ref, bias_ref, sink_ref, dq_ref, dkv_ref, dbias_ref, dsink_ref, acc, sacc = refs[nq:]
        n = pl.program_id(0)

        @pl.when(n == 0)
        def _():
            acc[...] = jnp.zeros_like(acc)
            sacc[...] = jnp.zeros_like(sacc)
            dbias_ref[...] = jnp.zeros_like(dbias_ref)
            dsink_ref[...] = jnp.zeros_like(dsink_ref)

        @pl.when(jnp.logical_and(n >= 1, n < nblk))
        def _():
            acc[(n + 1) % 3] = jnp.zeros((BLOCK, 2 * KV_WIDTH), F32)

        @pl.when(n < nblk)
        def _():
            q = _cat(q_refs)
            do = do_ref[...]
            kv = jnp.concatenate([kvp[...], kvc[...], kvn[...]], axis=0)
            valid = _attn_mask(n, s)
            for kh in range(N_KV_HEADS):
                qs = _stack_heads(q, kh)
                dos = _stack_heads(do, kh)
                k = kv[:, kh * HEAD_DIM:(kh + 1) * HEAD_DIM]
                v = kv[:, KV_WIDTH + kh * HEAD_DIM: KV_WIDTH + (kh + 1) * HEAD_DIM]
                p, ps = _softmax_parts(qs, k, bias_ref[kh], valid, _sink_column(sink_ref, kh))
                dp = lax.dot_general(dos, v, (((1,), (1,)), ((), ())), preferred_element_type=F32)
                delta = jnp.sum(p * dp, axis=1, keepdims=True)
                ds = p * (dp - delta)
                dbias_ref[kh] += ds
                sacc[kh] += -ps * delta
                dsb = ds.astype(BF16)
                dq = jnp.dot(dsb, k, preferred_element_type=F32) * scale
                for g in range(GQA_GROUP):
                    h = kh * GQA_GROUP + g
                    dq_ref[:, h * HEAD_DIM:(h + 1) * HEAD_DIM] = dq[g * BLOCK:(g + 1) * BLOCK].astype(BF16)
                dk = lax.dot_general(dsb, qs, (((0,), (0,)), ((), ())), preferred_element_type=F32) * scale
                dv = lax.dot_general(p.astype(BF16), dos, (((0,), (0,)), ((), ())), preferred_element_type=F32)
                for j in range(3):
                    slot = (n + 2 + j) % 3
                    acc[slot, :, kh * HEAD_DIM:(kh + 1) * HEAD_DIM] += dk[j * BLOCK:(j + 1) * BLOCK]
                    acc[slot, :, KV_WIDTH + kh * HEAD_DIM: KV_WIDTH + (kh + 1) * HEAD_DIM] += dv[j * BLOCK:(j + 1) * BLOCK]

        dkv_ref[...] = acc[(n + 2) % 3].astype(BF16)

        @pl.when(n == nblk)
        def _():
            lane = lax.broadcasted_iota(jnp.int32, (1, LANE), 1)
            out = jnp.zeros((1, LANE), F32)
            for kh in range(N_KV_HEADS):
                col = sacc[kh]
                for g in range(GQA_GROUP):
                    out = jnp.where(lane == kh * GQA_GROUP + g, jnp.sum(col[g * BLOCK:(g + 1) * BLOCK]), out)
            dsink_ref[...] = out

    qi = lambda n: jnp.minimum(n, nblk - 1)
    in_specs = _col_specs(0, ATTN_WIDTH, BLOCK, qi)
    in_specs += [pl.BlockSpec((BLOCK, COL_TILE), lambda n: (jnp.maximum(qi(n) - 1, 0), kv_col)),
                 pl.BlockSpec((BLOCK, COL_TILE), lambda n: (qi(n), kv_col)),
                 pl.BlockSpec((BLOCK, COL_TILE), lambda n: (jnp.minimum(qi(n) + 1, nblk - 1), kv_col)),
                 pl.BlockSpec((BLOCK, ATTN_WIDTH), lambda n: (qi(n), 0)),
                 _const_spec((N_KV_HEADS, rows, SPAN)),
                 pl.BlockSpec(memory_space=pltpu.SMEM)]
    return _call("attn_backward", body, (nblk + 1,), in_specs, (*([proj] * (nq + 3)), dattn, bias2, sink),
                 (jax.ShapeDtypeStruct((s, ATTN_WIDTH), BF16), jax.ShapeDtypeStruct((s, 2 * KV_WIDTH), BF16),
                  jax.ShapeDtypeStruct((N_KV_HEADS, rows, SPAN), F32), jax.ShapeDtypeStruct((1, LANE), F32)),
                 (pl.BlockSpec((BLOCK, ATTN_WIDTH), lambda n: (qi(n), 0)),
                  pl.BlockSpec((BLOCK, 2 * KV_WIDTH), lambda n: (jnp.clip(n - 1, 0, nblk - 1), 0)),
                  _const_spec((N_KV_HEADS, rows, SPAN)), _const_spec((1, LANE))),
                 [pltpu.VMEM((3, BLOCK, 2 * KV_WIDTH), F32), pltpu.VMEM((N_KV_HEADS, rows, 1), F32)],
                 ("arbitrary",), 24 << 20, comm=comm)


def _pick_tile(n, cap):
    t = cap - cap % LANE
    while n % t:
        t -= LANE
    return t


def _matmul_tn(name, a, b, comm=None):
    s, m = a.shape
    _, n = b.shape
    tk = _row_tile(s, 2048)
    tm, tn = _pick_tile(m, 2048), _pick_tile(n, 1152)
    nk = s // tk

    def body(a_ref, b_ref, o_ref, acc):
        k = pl.program_id(2)

        @pl.when(k == 0)
        def _():
            acc[...] = jnp.zeros_like(acc)

        acc[...] += lax.dot_general(a_ref[...], b_ref[...], (((0,), (0,)), ((), ())), preferred_element_type=F32)

        @pl.when(k == nk - 1)
        def _():
            o_ref[...] = acc[...].astype(BF16)

    (out,), extra = _call(
        name, body, (m // tm, n // tn, nk),
        [pl.BlockSpec((tk, tm), lambda i, j, k: (k, i)), pl.BlockSpec((tk, tn), lambda i, j, k: (k, j))], (a, b),
        [jax.ShapeDtypeStruct((m, n), BF16)], [pl.BlockSpec((tm, tn), lambda i, j, k: (i, j))],
        [pltpu.VMEM((tm, tn), F32)], ("parallel", "parallel", "arbitrary"),
        2 * tk * (tm + tn) * 2 + tm * tn * (4 + 4 + 4), comm=comm)
    return out, extra


def _pool_weight_grad(pooled, dmix, s):
    _, p_w = pooled.shape
    cg = p_w // N_POOL_GROUPS
    tk = _row_tile(s, 2048)
    nk = s // tk

    def body(a_ref, b_ref, o_ref, acc):
        k = pl.program_id(0)

        @pl.when(k == 0)
        def _():
            acc[...] = jnp.zeros_like(acc)

        for gi in range(N_POOL_GROUPS):
            cols = pl.ds(gi * cg, cg)
            acc[gi] += lax.dot_general(a_ref[:, cols], b_ref[:, cols], (((0,), (0,)), ((), ())),
                                       preferred_element_type=F32)

        @pl.when(k == nk - 1)
        def _():
            o_ref[...] = acc[...].astype(BF16)

    return pl.pallas_call(
        body, name="pool_weight_grad", grid=(nk,),
        out_shape=jax.ShapeDtypeStruct((N_POOL_GROUPS, cg, cg), BF16),
        in_specs=[pl.BlockSpec((tk, p_w), lambda k: (k, 0)), pl.BlockSpec((tk, p_w), lambda k: (k, 0))],
        out_specs=_const_spec((N_POOL_GROUPS, cg, cg)),
        scratch_shapes=[pltpu.VMEM((N_POOL_GROUPS, cg, cg), F32)],
        compiler_params=_cparams(("arbitrary",), 4 * tk * p_w * 2 + 3 * N_POOL_GROUPS * cg * cg * 4),
    )(pooled, dmix)


def _bwd_hidden(dproj, dgl, w_in, w_merge, s, d, comm=None):
    tm = _row_tile(s, 1024)
    t_in, t_mg = _pick_tile(dproj.shape[1], 1024), _pick_tile(dgl.shape[1], 1024)
    n_in = dproj.shape[1] // t_in
    n_mg = dgl.shape[1] // t_mg
    nk = n_in + n_mg

    def body(dp_ref, dg_ref, wi_ref, wm_ref, dh_ref, acc):
        k = pl.program_id(1)

        @pl.when(k == 0)
        def _():
            acc[...] = lax.dot_general(dp_ref[...], wi_ref[...], (((1,), (1,)), ((), ())),
                                       preferred_element_type=F32)

        @pl.when(jnp.logical_and(k > 0, k < n_in))
        def _():
            acc[...] += lax.dot_general(dp_ref[...], wi_ref[...], (((1,), (1,)), ((), ())),
                                        preferred_element_type=F32)

        @pl.when(jnp.logical_and(k >= n_in, k < nk - 1))
        def _():
            acc[...] += lax.dot_general(dg_ref[...], wm_ref[...], (((1,), (1,)), ((), ())),
                                        preferred_element_type=F32)

        @pl.when(k == nk - 1)
        def _():
            dh_ref[...] = (acc[...] + lax.dot_general(dg_ref[...], wm_ref[...], (((1,), (1,)), ((), ())),
                                                      preferred_element_type=F32)).astype(BF16)

    in_specs = [pl.BlockSpec((tm, t_in), lambda i, k: (i, jnp.minimum(k, n_in - 1))),
                pl.BlockSpec((tm, t_mg), lambda i, k: (i, jnp.maximum(k - n_in, 0))),
                pl.BlockSpec((d, t_in), lambda i, k: (0, jnp.minimum(k, n_in - 1))),
                pl.BlockSpec((d, t_mg), lambda i, k: (0, jnp.maximum(k - n_in, 0)))]
    est = 2 * (tm + d) * (t_in + t_mg) * 2 + 3 * tm * d * 4
    (dh,), extra = _call("bwd_hidden", body, (s // tm, nk), in_specs, (dproj, dgl, w_in, w_merge),
                         [jax.ShapeDtypeStruct((s, d), BF16)], [pl.BlockSpec((tm, d), lambda i, k: (i, 0))],
                         [pltpu.VMEM((tm, d), F32)], ("parallel", "arbitrary"), est, comm=comm)
    return dh, extra


def _bwd_prenorm(dh, x, dout, mod, pre_g, s, d):
    tm = _row_tile(s, 2 * ROW_TILE)

    def body(dh_ref, x_ref, dout_ref, mod_ref, pg_ref, gx_ref, dsh_ref, dsc_ref, dpg_ref):
        i = pl.program_id(0)

        @pl.when(i == 0)
        def _():
            dsh_ref[...] = jnp.zeros_like(dsh_ref)
            dsc_ref[...] = jnp.zeros_like(dsc_ref)
            dpg_ref[...] = jnp.zeros_like(dpg_ref)

        dh = dh_ref[...].astype(F32)
        xv = x_ref[...]
        r = lax.rsqrt(jnp.mean(xv * xv, axis=1, keepdims=True) + EPS)
        xhat = xv * r
        pg = pg_ref[...]
        one_scale = 1.0 + mod_ref[:, d:2 * d]
        dsh_ref[...] += jnp.sum(dh, axis=0, keepdims=True)
        dsc_ref[...] += jnp.sum(dh * xhat, axis=0, keepdims=True) * pg
        dpg_ref[...] += jnp.sum(dh * xhat, axis=0, keepdims=True) * one_scale
        dxh = dh * (one_scale * pg)
        dx = r * (dxh - xhat * jnp.mean(dxh * xhat, axis=1, keepdims=True))
        gx_ref[...] = dout_ref[...] + dx

    tile = pl.BlockSpec((tm, d), lambda i: (i, 0))
    vec = _const_spec((1, d))
    return pl.pallas_call(
        body, name="bwd_prenorm", grid=(s // tm,),
        out_shape=(jax.ShapeDtypeStruct((s, d), F32),) + (jax.ShapeDtypeStruct((1, d), F32),) * 3,
        in_specs=[tile, tile, tile, _const_spec((1, 3 * d)), vec], out_specs=(tile, vec, vec, vec),
        compiler_params=_cparams(("arbitrary",), 2 * 4 * tm * d * 4 + 6 * tm * d * 4),
    )(dh, x, dout, mod, pre_g)


def _pad_lanes(v, width):
    return jnp.pad(v, ((0, 0), (0, width - v.shape[1])))


def kernel(x, c, rel_bias_table, w_ada, b_ada, pre_norm_g, post_norm_g, w_in, attn_sink, w_pool_group, pool_scale, w_branch_attn, w_branch_pool, w_merge, b_merge, w_out, loss_target, m_rel_bias_table, m_w_ada, m_b_ada, m_pre_norm_g, m_post_norm_g, m_w_in, m_attn_sink, m_w_pool_group, m_pool_scale, m_w_branch_attn, m_w_branch_pool, m_w_merge, m_b_merge, m_w_out, v_rel_bias_table, v_w_ada, v_b_ada, v_pre_norm_g, v_post_norm_g, v_w_in, v_attn_sink, v_w_pool_group, v_pool_scale, v_w_branch_attn, v_w_branch_pool, v_w_merge, v_b_merge, v_w_out):
    _, s, d = x.shape
    p_w = pool_scale.shape[-1]
    cg = p_w // N_POOL_GROUPS
    in_w = 2 * ATTN_WIDTH + 2 * KV_WIDTH + 2 * p_w
    x2, t2 = x[0], loss_target[0]
    chip = 2 * lax.axis_index("x") + lax.axis_index("y")

    specs = [_Sharded("col", (d, in_w)), _Sharded("col", (d, 2 * d)), _Sharded("col", (ATTN_WIDTH, d)),
             _Sharded("col", (p_w, d)), _Sharded("row", (d, d)), _Sharded("grp", (N_POOL_GROUPS, cg, cg))]
    shards32 = [w_in[0], w_merge[0], w_branch_attn[0], w_branch_pool[0], w_out[0],
                w_pool_group[0].reshape(N_POOL_GROUPS * cg // N_CHIPS, cg)]
    names = ["w_in", "w_merge", "w_branch_attn", "w_branch_pool", "w_out", "w_pool_group"]
    shards16 = [_cast_bf16("cast_" + nm, w) for nm, w in zip(names, shards32)]
    shards16[5] = shards16[5].reshape(N_POOL_GROUPS, cg // N_CHIPS, cg)

    sc_all = _all_gather8("gather_cond", c, 1, pre=_silu)
    m_all = _all_gather8("gather_mod", _ada_forward(sc_all, w_ada[0]), N_DEV)
    mod = _mod_finish(m_all, b_ada)

    (h, proj, gates), (wf_in,) = _local_columns(x2, mod, pre_norm_g, shards16[0], shards16[1], b_merge,
                                                comm=_GatherWeights(specs[:1], shards16[:1], middle=0.9))
    proj, (wf_merge,) = _other_columns("proj", h, wf_in, proj, comm=_GatherWeights(specs[1:2], shards16[1:2]))
    gates, (wf_bra, wf_brp, wf_out, wf_grp) = _other_columns("merge_gates", h, wf_merge, gates, bias=b_merge,
                                                             comm=_GatherWeights(specs[2:], shards16[2:]))
    buckets = _t5_buckets()
    bias2 = _bias_table(rel_bias_table, buckets).reshape(N_KV_HEADS, GQA_GROUP * BLOCK, SPAN)
    attn = _attn_forward(proj, bias2, attn_sink, s)
    ya, yp, za, zp, merged, pooled, mixed = _branches(proj, attn, gates, wf_bra, wf_brp, wf_grp, pool_scale, s, d)
    dout, d_o, loss_part, dgate, dpostg = _out_loss(merged, x2, t2, wf_out, post_norm_g, mod, s, d)

    pw_out, _ = _matmul_tn("grad_w_out", merged, d_o)
    (dza, dzp, dgl, dbm), (pc_out,) = _bwd_out(d_o, gates, za, zp, wf_out, s, d,
                                                comm=_ScatterGrads(specs[4:5], [pw_out]))
    pw_bra, _ = _matmul_tn("grad_w_branch_attn", ya, dza)
    pw_brp, _ = _matmul_tn("grad_w_branch_pool", yp, dzp)
    pw_merge, _ = _matmul_tn("grad_w_merge", h, dgl)
    (dattn, dga, dgp, dmix, dpooled, dps), (pc_bra, pc_brp) = _bwd_branches(
        dza, dzp, attn, proj, mixed, wf_bra, wf_brp, wf_grp, pool_scale, s, d,
        comm=_ScatterGrads(specs[2:4], [pw_bra, pw_brp]))
    pw_grp = _pool_weight_grad(pooled, dmix, s)
    du = _pool_backward(dpooled, s)
    near, far = (0, 1, 2, 3, 6), (4, 5, 7)
    (dq, dkv, dbias, dsink), (pc_merge,) = _attn_backward(
        proj, dattn, bias2, attn_sink, s, comm=_ScatterGrads(specs[1:2], [pw_merge], peers=near))
    dproj = jnp.concatenate([dq, dkv, dga, du, dgp], axis=1)
    pw_in, (pc_merge,) = _matmul_tn("grad_w_in", h, dproj,
                                    comm=_ScatterGrads(specs[1:2], [pw_merge], peers=far, into=[pc_merge]))
    dh, (pc_in, pc_grp) = _bwd_hidden(dproj, dgl, wf_in, wf_merge, s, d,
                                      comm=_ScatterGrads([specs[0], specs[5]], [pw_in, pw_grp]))
    gx, dshift, dscale, dpreg = _bwd_prenorm(dh, x2, dout, mod, pre_norm_g, s, d)

    pieces = [pc_in, pc_merge, pc_bra, pc_brp, pc_out]
    weights = [w_in, w_merge, w_branch_attn, w_branch_pool, w_out]
    moms = [m_w_in, m_w_merge, m_w_branch_attn, m_w_branch_pool, m_w_out]
    vars_ = [v_w_in, v_w_merge, v_w_branch_attn, v_w_branch_pool, v_w_out]
    big = {}
    for nm, pc, w, m, v in zip(names, pieces, weights, moms, vars_):
        shape2 = (-1, w.shape[-1])
        res4 = _reduce_adamw("update_" + nm, pc, w.reshape(shape2), m.reshape(shape2), v.reshape(shape2))
        big[nm] = tuple(a.reshape(w.shape) for a in res4)
    hq = cg // N_CHIPS // 2
    g_grp = _reduce16("reduce_w_pool_group", pc_grp.reshape(2, N_DEV, N_POOL_GROUPS * hq, cg))
    g_grp = g_grp.reshape(2, N_POOL_GROUPS, hq, cg).transpose(1, 0, 2, 3).reshape(N_POOL_GROUPS * 2 * hq, cg)
    res3 = _adamw("adamw_w_pool_group", w_pool_group.reshape(-1, cg), g_grp, m_w_pool_group.reshape(-1, cg),
                  v_w_pool_group.reshape(-1, cg))
    big["w_pool_group"] = tuple(a.reshape(w_pool_group.shape) for a in (g_grp,) + tuple(res3))

    dtable = _bias_table_grad(dbias.reshape(N_Q_HEADS, BLOCK, SPAN), buckets)[:, :N_Q_HEADS]
    segs = [("b_ada", jnp.concatenate([dshift, dscale, dgate], axis=1), 3 * d),
            ("pre_norm_g", dpreg, d), ("post_norm_g", dpostg, d), ("attn_sink", dsink, LANE),
            ("pool_scale", dps, p_w), ("b_merge", dbm, 2 * d), ("rel_bias_table", dtable.reshape(1, -1), 2 * LANE)]
    packed = jnp.concatenate([_pad_lanes(v, w) for _, v, w in segs], axis=1)
    rows = _all_gather8("gather_small", packed, 1)[:, 0, :]

    def pack(vals):
        return jnp.concatenate([_pad_lanes(v.reshape(1, -1), w) for v, (_, _, w) in zip(vals, segs)], axis=1)

    small_w = [b_ada, pre_norm_g, post_norm_g, attn_sink, pool_scale, b_merge, rel_bias_table]
    small_m = [m_b_ada, m_pre_norm_g, m_post_norm_g, m_attn_sink, m_pool_scale, m_b_merge, m_rel_bias_table]
    small_v = [v_b_ada, v_pre_norm_g, v_post_norm_g, v_attn_sink, v_pool_scale, v_b_merge, v_rel_bias_table]
    g_small, d_small, nm_small, nv_small = _small_update(rows, pack(small_w), pack(small_m), pack(small_v))
    small = {}
    off = 0
    for (nm, _, w), ref in zip(segs, small_w):
        cut = lambda a: a[:, off:off + ref.size].reshape(ref.shape)
        small[nm] = (cut(g_small), cut(d_small), cut(nm_small), cut(nv_small))
        off += w

    dmod_cols = lax.dynamic_slice_in_dim(rows[:, :3 * d], chip * (3 * d // N_CHIPS), 3 * d // N_CHIPS, axis=1)
    sc_t = sc_all[:, 0, :].T
    g_ada, d_ada, nm_ada, nv_ada = _ada_backward(sc_t, dmod_cols, w_ada[0], m_w_ada[0], v_w_ada[0])
    big["w_ada"] = tuple(a.reshape(w_ada.shape) for a in (g_ada, d_ada, nm_ada, nv_ada))

    loss = lax.psum(loss_part[0, 0], ("x", "y", "c"))
    order = ["rel_bias_table", "w_ada", "b_ada", "pre_norm_g", "post_norm_g", "w_in", "attn_sink", "w_pool_group",
             "pool_scale", "w_branch_attn", "w_branch_pool", "w_merge", "b_merge", "w_out"]
    res = {**big, **small}
    outs = [loss, gx.reshape(x.shape)]
    for part in range(4):
        outs += [res[nm][part] for nm in order]
    return tuple(outs)


def _small_update(rows, w, m, v):
    _, n = rows.shape

    def body(r_ref, w_ref, m_ref, v_ref, g_ref, d_ref, nm_ref, nv_ref):
        g = r_ref[0:1, :]
        for k in range(1, N_DEV):
            g = g + r_ref[k:k + 1, :]
        dl, nm, nv = _adam_math(w_ref[...], g, m_ref[...], v_ref[...])
        g_ref[...] = g
        d_ref[...] = dl
        nm_ref[...] = nm
        nv_ref[...] = nv

    vm = pl.BlockSpec(memory_space=pltpu.VMEM)
    out = jax.ShapeDtypeStruct((1, n), F32)
    return pl.pallas_call(
        body, name="small_update", out_shape=(out,) * 4, in_specs=[vm] * 4, out_specs=(vm,) * 4,
    )(rows, w, m, v)
```
